```python
import math
import jax
import jax.numpy as jnp
from jax import lax
import numpy as np

D_MODEL = 1024
BATCH = 16
SEQ = 256
DEPTH = 1
DEC_BATCH = 2
DEC_SEQ = 1024
PAST_LEN = 512

GRID_W = 64
GLA_HEADS = 4
GLA_DK = 64
GLA_DV = 128
GLA_QK = GLA_HEADS * GLA_DK
GLA_WIDTH = GLA_HEADS * GLA_DV
GLA_GATE_RANK = 16
GLA_GATE_TEMP = 16.0
GLA_CHUNK = 64
DIFF_HEADS = 4
DIFF_DH = 64
DIFF_DV = 2 * DIFF_DH
DIFF_QK = DIFF_HEADS * 2 * DIFF_DH
DIFF_WIDTH = DIFF_HEADS * DIFF_DV
MIX_WIDTH = GLA_WIDTH + DIFF_WIDTH
ROPE_PAIRS = DIFF_DH // 4
ROPE_BASE = 10000.0
Q_BLOCK = 128
EPS = 1e-6
PROJ_SIZES = (GLA_QK, GLA_QK, GLA_WIDTH, GLA_GATE_RANK, GLA_GATE_RANK, DIFF_QK, DIFF_QK, DIFF_WIDTH, MIX_WIDTH)
PROJ_WIDTH = 3616

kernel_name = 'hybrid_gla_diffattn_prefix_dit_step'


def rmsnorm(x, gain):
    x32 = x.astype(jnp.float32)
    y = x32 * lax.rsqrt(jnp.mean(x32 * x32, axis=-1, keepdims=True) + EPS)
    return (y * gain.astype(jnp.float32)).astype(x.dtype)


def adaln_params(cvec, w_mod_l, b_mod_l):
    m = jnp.einsum('...d,de->...e', jax.nn.silu(cvec), w_mod_l) + b_mod_l
    shift, scale, gate = jnp.split(m, 3, axis=-1)
    return shift, scale, gate


def axial_rope_tables(rows, dtype):
    row = jnp.repeat(jnp.arange(rows), GRID_W).astype(jnp.float32)
    col = jnp.tile(jnp.arange(GRID_W), rows).astype(jnp.float32)
    inv_freq = ROPE_BASE ** (-jnp.arange(ROPE_PAIRS, dtype=jnp.float32) / ROPE_PAIRS)
    ang_r = row[:, None] * inv_freq[None, :]
    ang_c = col[:, None] * inv_freq[None, :]
    return (jnp.cos(ang_r).astype(dtype), jnp.sin(ang_r).astype(dtype),
            jnp.cos(ang_c).astype(dtype), jnp.sin(ang_c).astype(dtype))


def rotate_pairs(v, cos, sin):
    v1, v2 = jnp.split(v, 2, axis=-1)
    cos = cos[None, :, None, None, :]
    sin = sin[None, :, None, None, :]
    return jnp.concatenate([v1 * cos - v2 * sin, v1 * sin + v2 * cos], axis=-1)


def apply_axial_rope(x, tables):
    cos_r, sin_r, cos_c, sin_c = tables
    xr, xc = jnp.split(x, 2, axis=-1)
    return jnp.concatenate([rotate_pairs(xr, cos_r, sin_r), rotate_pairs(xc, cos_c, sin_c)], axis=-1)


def project(h, w_in_l, w_alpha_l, b_alpha_l):
    bsz, n, _ = h.shape
    p = jnp.einsum('bld,dp->blp', h, w_in_l)
    idx = [int(i) for i in np.cumsum(PROJ_SIZES)[:-1]]
    gq, gk, gv, lr_f, lr_b, dq, dk, dv, gate = jnp.split(p, idx, axis=-1)
    gq = gq.reshape(bsz, n, GLA_HEADS, GLA_DK) * (GLA_DK ** -0.5)
    gk = gk.reshape(bsz, n, GLA_HEADS, GLA_DK)
    gv = gv.reshape(bsz, n, GLA_HEADS, GLA_DV)
    z_f = jnp.einsum('blr,rk->blk', lr_f, w_alpha_l[0]) + b_alpha_l[0]
    z_b = jnp.einsum('blr,rk->blk', lr_b, w_alpha_l[1]) + b_alpha_l[1]
    g_f = (jax.nn.log_sigmoid(z_f.astype(jnp.float32)) / GLA_GATE_TEMP).reshape(bsz, n, GLA_HEADS, GLA_DK)
    g_b = (jax.nn.log_sigmoid(z_b.astype(jnp.float32)) / GLA_GATE_TEMP).reshape(bsz, n, GLA_HEADS, GLA_DK)
    dq = dq.reshape(bsz, n, DIFF_HEADS, 2, DIFF_DH)
    dk = dk.reshape(bsz, n, DIFF_HEADS, 2, DIFF_DH)
    dv = dv.reshape(bsz, n, DIFF_HEADS, DIFF_DV)
    return gq, gk, gv, g_f, g_b, dq, dk, dv, gate


def gla_chunk_scan(q, k, v, g, s0):
    bsz, n, nh, dkk = q.shape
    dvv = v.shape[-1]
    nc = n // GLA_CHUNK

    def to_chunks(a):
        return a.astype(jnp.float32).reshape(bsz, nc, GLA_CHUNK, nh, a.shape[-1]).swapaxes(0, 1)

    causal = jnp.tril(jnp.ones((GLA_CHUNK, GLA_CHUNK), dtype=bool))[None, :, :, None, None]

    def step(s, inp):
        qc, kc, vc, gc = inp
        b = jnp.cumsum(gc, axis=1)
        o_inter = jnp.einsum('bchk,bhkv->bchv', qc * jnp.exp(b), s)
        diff = b[:, :, None] - b[:, None, :]
        decay = jnp.where(causal, jnp.exp(jnp.minimum(diff, 0.0)), 0.0)
        att = jnp.einsum('bthk,btshk,bshk->bhts', qc, decay, kc)
        o_intra = jnp.einsum('bhts,bshv->bthv', att, vc)
        b_last = b[:, -1]
        s_new = jnp.exp(b_last)[..., None] * s + jnp.einsum(
            'bshk,bshv->bhkv', kc * jnp.exp(b_last[:, None] - b), vc)
        return s_new, o_inter + o_intra

    s_fin, o = lax.scan(step, s0.astype(jnp.float32), (to_chunks(q), to_chunks(k), to_chunks(v), to_chunks(g)))
    o = o.swapaxes(0, 1).reshape(bsz, n, nh, dvv)
    return o, s_fin


def gla_bidirectional(q, k, v, g_f, g_b, s0_f, s0_b):
    o_f, s_f = gla_chunk_scan(q, k, v, g_f, s0_f)
    o_b, s_b = gla_chunk_scan(jnp.flip(q, 1), jnp.flip(k, 1), jnp.flip(v, 1), jnp.flip(g_b, 1), s0_b)
    return o_f + jnp.flip(o_b, 1), s_f, s_b


def diff_lambda_value(lam_params, lam_init):
    lp = lam_params.astype(jnp.float32)
    return jnp.exp(jnp.sum(lp[0] * lp[1])) - jnp.exp(jnp.sum(lp[2] * lp[3])) + lam_init


def diff_attention(q, k, v, lam):
    bsz, nq, nh, _, dh = q.shape
    nb = nq // Q_BLOCK
    qb = jnp.moveaxis(q.reshape(bsz, nb, Q_BLOCK, nh, 2, dh), 1, 0)
    scale = dh ** -0.5

    def block(qi):
        s = jnp.einsum('bqhid,bkhid->bihqk', qi, k).astype(jnp.float32) * scale
        p = jax.nn.softmax(s, axis=-1)
        a = p[:, 0] - lam * p[:, 1]
        return jnp.einsum('bhqk,bkhv->bqhv', a.astype(v.dtype), v)

    o = lax.map(block, qb)
    return jnp.moveaxis(o, 0, 1).reshape(bsz, nq, nh, v.shape[-1])


def merge_heads(o_gla, o_diff, gate, gla_gain, diff_gain, lam_init, w_out_l):
    bsz, n = gate.shape[0], gate.shape[1]
    og = rmsnorm(o_gla.astype(gate.dtype), gla_gain).reshape(bsz, n, GLA_WIDTH)
    od = (rmsnorm(o_diff, diff_gain) * (1.0 - lam_init)).reshape(bsz, n, DIFF_WIDTH)
    o = jnp.concatenate([og, od], axis=-1) * jax.nn.silu(gate)
    return jnp.einsum('blm,md->bld', o, w_out_l)


def setup_inputs(seed: int = 0) -> dict:
    key = jax.random.key(seed)
    ks = jax.random.split(key, 19)

    def nrm(k, shape, s):
        return s * jax.random.normal(k, shape, jnp.float32)

    return {
        'x_prompt': nrm(ks[0], (BATCH, SEQ, D_MODEL), 1.0),
        'x_sample': nrm(ks[1], (DEC_BATCH, DEC_SEQ, D_MODEL), 1.0),
        'cache_diff_k': nrm(ks[2], (DEC_BATCH, DEPTH, PAST_LEN, DIFF_HEADS, 2, DIFF_DH), 1.0),
        'cache_diff_v': nrm(ks[3], (DEC_BATCH, DEPTH, PAST_LEN, DIFF_HEADS, DIFF_DV), 1.0),
        'state_gla_fwd': nrm(ks[4], (DEC_BATCH, DEPTH, GLA_HEADS, GLA_DK, GLA_DV), 1.0),
        'state_gla_bwd': nrm(ks[5], (DEC_BATCH, DEPTH, GLA_HEADS, GLA_DK, GLA_DV), 1.0),
        'c': nrm(ks[6], (DEC_BATCH, D_MODEL), 1.0),
        'c_ctx': nrm(ks[7], (D_MODEL,), 1.0),
        'norm_gain': 1.0 + nrm(ks[8], (DEPTH, D_MODEL), 0.02),
        'w_mod': nrm(ks[9], (DEPTH, D_MODEL, 3 * D_MODEL), D_MODEL ** -0.5),
        'b_mod': nrm(ks[10], (DEPTH, 3 * D_MODEL), 0.02),
        'w_in': nrm(ks[11], (DEPTH, D_MODEL, PROJ_WIDTH), D_MODEL ** -0.5),
        'w_gla_alpha': nrm(ks[12], (DEPTH, 2, GLA_GATE_RANK, GLA_QK), GLA_GATE_RANK ** -0.5),
        'b_gla_alpha': 1.0 + nrm(ks[13], (DEPTH, 2, GLA_QK), 0.5),
        'diff_lambda': nrm(ks[14], (DEPTH, 4, DIFF_DH), 0.1),
        'gla_head_gain': 1.0 + nrm(ks[15], (DEPTH, GLA_DV), 0.02),
        'diff_head_gain': 1.0 + nrm(ks[16], (DEPTH, DIFF_DV), 0.02),
        'w_out': nrm(ks[17], (DEPTH, MIX_WIDTH, D_MODEL), MIX_WIDTH ** -0.5),
        'final_gain': 1.0 + nrm(ks[18], (D_MODEL,), 0.02),
    }


def reference(x_prompt, x_sample, cache_diff_k, cache_diff_v, state_gla_fwd, state_gla_bwd, c, c_ctx,
              norm_gain, w_mod, b_mod, w_in, w_gla_alpha, b_gla_alpha, diff_lambda,
              gla_head_gain, diff_head_gain, w_out, final_gain):
    n_lat = x_sample.shape[1]
    ROWS = n_lat // GRID_W
    rope_tables = axial_rope_tables(ROWS, x_sample.dtype)
    bsz_p = x_prompt.shape[0]
    zero_state = jnp.zeros((bsz_p, GLA_HEADS, GLA_DK, GLA_DV), jnp.float32)

    xp = x_prompt
    xs = x_sample
    new_k, new_v, new_sf, new_sb = [], [], [], []
    for l in range(DEPTH):
        lam_init = 0.8 - 0.6 * math.exp(-0.3 * l)
        lam = diff_lambda_value(diff_lambda[l], lam_init)

        sh, sc, gt = adaln_params(c_ctx, w_mod[l], b_mod[l])
        h = rmsnorm(xp, norm_gain[l]) * (1.0 + sc) + sh
        gq, gk, gv, g_f, g_b, dq, dk, dv, gate = project(h, w_in[l], w_gla_alpha[l], b_gla_alpha[l])
        o_gla, s_f, s_b = gla_bidirectional(gq, gk, gv, g_f, g_b, zero_state, zero_state)
        o_diff = diff_attention(dq, dk, dv, lam)
        xp = xp + gt * merge_heads(o_gla, o_diff, gate, gla_head_gain[l], diff_head_gain[l], lam_init, w_out[l])
        new_k.append(dk)
        new_v.append(dv)
        new_sf.append(s_f.astype(x_prompt.dtype))
        new_sb.append(s_b.astype(x_prompt.dtype))

        sh, sc, gt = adaln_params(c, w_mod[l], b_mod[l])
        h = rmsnorm(xs, norm_gain[l]) * (1.0 + sc[:, None, :]) + sh[:, None, :]
        gq, gk, gv, g_f, g_b, dq, dk, dv, gate = project(h, w_in[l], w_gla_alpha[l], b_gla_alpha[l])
        o_gla, _, _ = gla_bidirectional(gq, gk, gv, g_f, g_b, state_gla_fwd[:, l], state_gla_bwd[:, l])
        dq = apply_axial_rope(dq, rope_tables)
        dk = apply_axial_rope(dk, rope_tables)
        k_all = jnp.concatenate([cache_diff_k[:, l], dk], axis=1)
        v_all = jnp.concatenate([cache_diff_v[:, l], dv], axis=1)
        o_diff = diff_attention(dq, k_all, v_all, lam)
        xs = xs + gt[:, None, :] * merge_heads(o_gla, o_diff, gate, gla_head_gain[l], diff_head_gain[l], lam_init, w_out[l])

    y_prompt = rmsnorm(xp, final_gain)
    y_sample = rmsnorm(xs, final_gain)
    new_diff_k = jnp.stack(new_k, axis=1)
    new_diff_v = jnp.stack(new_v, axis=1)
    new_gla_fwd = jnp.stack(new_sf, axis=1)
    new_gla_bwd = jnp.stack(new_sb, axis=1)
    return (y_prompt, y_sample, new_diff_k, new_diff_v, new_gla_fwd, new_gla_bwd)
```

```python
import functools
import math

import jax
import jax.numpy as jnp
from jax import lax
from jax.experimental import pallas as pl
from jax.experimental.pallas import tpu as pltpu

F32 = jnp.float32
BF16 = jnp.bfloat16

D_MODEL = 1024
GRID_W = 64
GLA_HEADS = 4
GLA_DK = 64
GLA_DV = 128
GLA_QK = GLA_HEADS * GLA_DK
GLA_WIDTH = GLA_HEADS * GLA_DV
GLA_GATE_RANK = 16
GLA_GATE_TEMP = 16.0
GLA_CHUNK = 64
DIFF_HEADS = 4
DIFF_DH = 64
DIFF_DV = 2 * DIFF_DH
DIFF_QK = DIFF_HEADS * 2 * DIFF_DH
DIFF_WIDTH = DIFF_HEADS * DIFF_DV
MIX_WIDTH = GLA_WIDTH + DIFF_WIDTH
ROPE_PAIRS = DIFF_DH // 4
ROPE_BASE = 10000.0
EPS = 1e-6

LANES = 128
MOD_ROWS = 8
LR_PAD = LANES
PROJ_ROWS = 256
DIFF_QB = 128
GLA_SAFE_LOG_DECAY = -40.0
VMEM_LIMIT = 56 * 1024 * 1024

_C_GQ = 0
_C_GK = _C_GQ + GLA_QK
_C_GV = _C_GK + GLA_QK
_C_DQ = _C_GV + GLA_WIDTH
_C_DK = _C_DQ + DIFF_QK
_C_DV = _C_DK + DIFF_QK
_C_GATE = _C_DV + DIFF_WIDTH
_C_LR = _C_GATE + MIX_WIDTH
_C_END = _C_LR + LR_PAD


def _bf(x):
    return x.astype(BF16)


def _dot(a, b):
    return jnp.dot(a, b, preferred_element_type=F32)


def _dot_nt(a, b):
    return lax.dot_general(a, b, (((1,), (1,)), ((), ())), preferred_element_type=F32)


def _dot_tn(a, b):
    return lax.dot_general(a, b, (((0,), (0,)), ((), ())), preferred_element_type=F32)


def _params(n_parallel=0, n_arbitrary=0):
    sem = ("parallel",) * n_parallel + ("arbitrary",) * n_arbitrary
    return pltpu.CompilerParams(dimension_semantics=sem, vmem_limit_bytes=VMEM_LIMIT)


def _adaln_kernel(c_ref, w_ref, b_ref, o_ref):
    c = c_ref[...]
    s = c * jax.nn.sigmoid(c)
    o_ref[...] = _dot(_bf(s), _bf(w_ref[...])) + b_ref[...]


def _adaln(cvecs, w_mod, b_mod):
    n = w_mod.shape[1]
    tn = D_MODEL
    return pl.pallas_call(
        _adaln_kernel,
        grid=(n // tn,),
        in_specs=[
            pl.BlockSpec((MOD_ROWS, D_MODEL), lambda j: (0, 0)),
            pl.BlockSpec((D_MODEL, tn), lambda j: (0, j)),
            pl.BlockSpec((1, tn), lambda j: (0, j)),
        ],
        out_specs=pl.BlockSpec((MOD_ROWS, tn), lambda j: (0, j)),
        out_shape=jax.ShapeDtypeStruct((MOD_ROWS, n), F32),
        compiler_params=_params(1),
        name="adaln",
    )(cvecs, w_mod, b_mod)


def _log_sigmoid(z):
    return jnp.minimum(z, 0.0) - jnp.log1p(jnp.exp(-jnp.abs(z)))


def _rope_slab(x, cos, sin):
    lane = lax.broadcasted_iota(jnp.int32, x.shape, 1)
    up = pltpu.roll(x, ROPE_PAIRS, axis=1)
    dn = pltpu.roll(x, LANES - ROPE_PAIRS, axis=1)
    partner = jnp.where((lane & ROPE_PAIRS) == 0, dn, up)
    return x * cos + partner * sin


def _proj_kernel(*refs, rope):
    if rope:
        (x_ref, gain_ref, shift_ref, scale_ref, w_ref, wa_ref, ba_ref, cos_ref, sin_ref,
         gq_ref, gk_ref, gv_ref, g_ref, dq_ref, dk_ref, dv_ref, gate_ref) = refs
    else:
        (x_ref, gain_ref, shift_ref, scale_ref, w_ref, wa_ref, ba_ref,
         gq_ref, gk_ref, gv_ref, g_ref, dq_ref, dk_ref, dv_ref, gate_ref) = refs
    x = x_ref[...]
    ms = jnp.mean(x * x, axis=-1, keepdims=True)
    h = x * lax.rsqrt(ms + EPS) * gain_ref[...]
    h = h * (1.0 + scale_ref[0]) + shift_ref[0]
    hb = _bf(h)

    def sec(a, b):
        return _dot(hb, w_ref[:, a:b])

    gq_ref[...] = sec(_C_GQ, _C_GK)
    gk_ref[...] = sec(_C_GK, _C_GV)
    gv_ref[...] = sec(_C_GV, _C_DQ)
    dv_ref[...] = sec(_C_DV, _C_GATE)
    gate_ref[...] = sec(_C_GATE, _C_LR)
    lr = sec(_C_LR, _C_END)
    z = _dot(_bf(lr), wa_ref[...]) + ba_ref[...]
    g_ref[...] = _log_sigmoid(z) * (1.0 / GLA_GATE_TEMP)
    dq = sec(_C_DQ, _C_DK)
    dk = sec(_C_DK, _C_DV)
    if rope:
        cos = cos_ref[...]
        sin = sin_ref[...]
        for hd in range(DIFF_HEADS):
            sl = slice(hd * LANES, (hd + 1) * LANES)
            dq_ref[:, sl] = _rope_slab(dq[:, sl], cos, sin)
            dk_ref[:, sl] = _rope_slab(dk[:, sl], cos, sin)
    else:
        dq_ref[...] = dq
        dk_ref[...] = dk


def _proj(x2d, gain, mod3, mod_row_of_tile, w_re, wa, ba, rope_tabs, seq_len):
    t = x2d.shape[0]
    tm = PROJ_ROWS
    nt = t // tm
    rope = rope_tabs is not None
    row = lambda i: (0, 0)
    in_specs = [
        pl.BlockSpec((tm, D_MODEL), lambda i: (i, 0)),
        pl.BlockSpec((1, D_MODEL), row),
        pl.BlockSpec((1, 1, D_MODEL), lambda i: (mod_row_of_tile(i), 0, 0)),
        pl.BlockSpec((1, 1, D_MODEL), lambda i: (mod_row_of_tile(i), 0, 1)),
        pl.BlockSpec((D_MODEL, _C_END), row),
        pl.BlockSpec((LR_PAD, 2 * GLA_QK), row),
        pl.BlockSpec((1, 2 * GLA_QK), row),
    ]
    args = [x2d, gain, mod3, mod3, w_re, wa, ba]
    if rope:
        tiles_per_seq = seq_len // tm
        tab_spec = pl.BlockSpec((tm, LANES), lambda i: (i % tiles_per_seq, 0))
        in_specs += [tab_spec, tab_spec]
        args += list(rope_tabs)
    widths = (GLA_QK, GLA_QK, GLA_WIDTH, 2 * GLA_QK, DIFF_QK, DIFF_QK, DIFF_WIDTH, MIX_WIDTH)
    out_specs = [pl.BlockSpec((tm, w), lambda i: (i, 0)) for w in widths]
    out_shape = [jax.ShapeDtypeStruct((t, w), F32) for w in widths]
    return pl.pallas_call(
        functools.partial(_proj_kernel, rope=rope),
        grid=(nt,),
        in_specs=in_specs,
        out_specs=out_specs,
        out_shape=out_shape,
        compiler_params=_params(1),
        name="proj_rope" if rope else "proj",
    )(*args)


def _head_lane_mask(shape, head, width):
    lane = lax.broadcasted_iota(jnp.int32, shape, 1)
    return (lane >= head * width) & (lane < (head + 1) * width)


def _gla_direction(q_ref, k_ref, v_ref, g_ref, o_ref, st_ref, b_ref, *, g_col, reverse, accumulate, seq_len):
    c_len = GLA_CHUNK
    nc = seq_len // c_len
    r64 = lax.broadcasted_iota(jnp.int32, (c_len, c_len), 0)
    c64 = lax.broadcasted_iota(jnp.int32, (c_len, c_len), 1)
    tri = ((c64 >= r64) if reverse else (c64 <= r64)).astype(F32)
    rt = lax.broadcasted_iota(jnp.int32, (GLA_HEADS * c_len, c_len), 0) & (c_len - 1)
    cs = lax.broadcasted_iota(jnp.int32, (GLA_HEADS * c_len, c_len), 1)
    keep = (cs >= rt) if reverse else (cs <= rt)
    head_masks = [_head_lane_mask((c_len, GLA_QK), h, GLA_DK) for h in range(GLA_HEADS)]
    sr = lax.broadcasted_iota(jnp.int32, (GLA_WIDTH, GLA_QK), 0) // GLA_DV
    sc = lax.broadcasted_iota(jnp.int32, (GLA_WIDTH, GLA_QK), 1) // GLA_DK
    st_mask = sr == sc
    er = lax.broadcasted_iota(jnp.int32, (GLA_QK, GLA_WIDTH), 0) // GLA_DK
    ec = lax.broadcasted_iota(jnp.int32, (GLA_QK, GLA_WIDTH), 1) // GLA_DV
    expand = (er == ec).astype(BF16)
    t_idx = lax.broadcasted_iota(jnp.int32, (c_len, GLA_QK), 0)

    def chunk_rows(ci):
        c = (nc - 1 - ci) if reverse else ci
        return pl.ds(pl.multiple_of(c * c_len, c_len), c_len)

    def load(ci):
        rows = chunk_rows(ci)
        q = q_ref[0, rows, :]
        k = k_ref[0, rows, :]
        v = v_ref[0, rows, :]
        g = g_ref[0, rows, g_col:g_col + GLA_QK]
        b = jnp.dot(tri, g, precision=lax.Precision.HIGHEST, preferred_element_type=F32)
        b_tot = jnp.sum(g, axis=0, keepdims=True)
        return rows, q, k, v, b, b_tot

    def finish(rows, q_in, k_last, v, b_tot, o_intra):
        st = st_ref[...]
        o = o_intra + _dot_nt(_bf(q_in), _bf(st))
        upd = _dot_tn(_bf(v), _bf(k_last))
        st_ref[...] = st * jnp.exp(b_tot) + jnp.where(st_mask, upd, 0.0)
        if accumulate:
            o_ref[0, rows, :] += o
        else:
            o_ref[0, rows, :] = o

    def fast_step(ci, carry):
        rows, q, k, v, b, b_tot = load(ci)
        q_in = q * jnp.exp(b)
        k_out = k * jnp.exp(-b)
        k_last = k_out * jnp.exp(b_tot)
        q_stack = jnp.concatenate([jnp.where(m, q_in, 0.0) for m in head_masks], axis=0)
        att = _dot_nt(_bf(q_stack), _bf(k_out))
        att = jnp.where(keep, att, 0.0)
        r = _dot(_bf(att), _bf(v))
        o_intra = jnp.concatenate(
            [r[h * c_len:(h + 1) * c_len, h * GLA_DV:(h + 1) * GLA_DV] for h in range(GLA_HEADS)], axis=1)
        finish(rows, q_in, k_last, v, b_tot, o_intra)
        return carry

    def slow_step(ci, carry):
        rows, q, k, v, b, b_tot = load(ci)
        b_ref[...] = b
        base = rows.start

        def key_row(s, acc):
            b_s = b_ref[pl.ds(s, 1), :]
            k_s = k_ref[0, pl.ds(base + s, 1), :]
            v_s = v_ref[0, pl.ds(base + s, 1), :]
            visible = (t_idx <= s) if reverse else (t_idx >= s)
            w = jnp.where(visible, q * jnp.exp(jnp.minimum(b - b_s, 0.0)), 0.0) * k_s
            return acc + _dot(_bf(w), expand) * v_s

        o_intra = lax.fori_loop(0, c_len, key_row, jnp.zeros((c_len, GLA_WIDTH), F32))
        finish(rows, q * jnp.exp(b), k * jnp.exp(b_tot - b), v, b_tot, o_intra)
        return carry

    def min_chunk_decay(ci, m):
        rows = chunk_rows(ci)
        g = g_ref[0, rows, g_col:g_col + GLA_QK]
        return jnp.minimum(m, jnp.sum(g, axis=0, keepdims=True))

    worst = jnp.min(lax.fori_loop(0, nc, min_chunk_decay, jnp.zeros((1, GLA_QK), F32)))
    lax.cond(worst >= GLA_SAFE_LOG_DECAY,
             lambda: lax.fori_loop(0, nc, fast_step, 0),
             lambda: lax.fori_loop(0, nc, slow_step, 0))


def _state_to_transposed_blockdiag(s0):
    rows = []
    for h in range(GLA_HEADS):
        blocks = [s0[h] if j == h else jnp.zeros((GLA_DK, GLA_DV), F32) for j in range(GLA_HEADS)]
        rows.append(jnp.concatenate(blocks, axis=1))
    return jnp.concatenate(rows, axis=0).T


def _write_state(st_ref, out_ref):
    s = st_ref[...].T
    for h in range(GLA_HEADS):
        out_ref[0, h] = s[h * GLA_DK:(h + 1) * GLA_DK, h * GLA_DV:(h + 1) * GLA_DV]


def _gla_kernel(*refs, seq_len, has_init):
    if has_init:
        q_ref, k_ref, v_ref, g_ref, s0f_ref, s0b_ref, o_ref, st_ref, b_ref = refs
    else:
        q_ref, k_ref, v_ref, g_ref, o_ref, sf_ref, sb_ref, st_ref, b_ref = refs
    for reverse in (False, True):
        if has_init:
            st_ref[...] = _state_to_transposed_blockdiag((s0b_ref if reverse else s0f_ref)[0])
        else:
            st_ref[...] = jnp.zeros_like(st_ref)
        _gla_direction(q_ref, k_ref, v_ref, g_ref, o_ref, st_ref, b_ref,
                       g_col=GLA_QK if reverse else 0, reverse=reverse, accumulate=reverse, seq_len=seq_len)
        if not has_init:
            _write_state(st_ref, sb_ref if reverse else sf_ref)


def _gla(q, k, v, g, s0f=None, s0b=None):
    bsz, seq_len, _ = q.shape
    has_init = s0f is not None
    seq_spec = lambda w: pl.BlockSpec((1, seq_len, w), lambda b: (b, 0, 0))
    st_spec = pl.BlockSpec((1, GLA_HEADS, GLA_DK, GLA_DV), lambda b: (b, 0, 0, 0))
    in_specs = [seq_spec(GLA_QK), seq_spec(GLA_QK), seq_spec(GLA_WIDTH), seq_spec(2 * GLA_QK)]
    args = [q, k, v, g]
    out_specs = [seq_spec(GLA_WIDTH)]
    out_shape = [jax.ShapeDtypeStruct((bsz, seq_len, GLA_WIDTH), F32)]
    if has_init:
        in_specs += [st_spec, st_spec]
        args += [s0f, s0b]
    else:
        out_specs += [st_spec, st_spec]
        out_shape += [jax.ShapeDtypeStruct((bsz, GLA_HEADS, GLA_DK, GLA_DV), F32)] * 2
    return pl.pallas_call(
        functools.partial(_gla_kernel, seq_len=seq_len, has_init=has_init),
        grid=(bsz,),
        in_specs=in_specs,
        out_specs=out_specs,
        out_shape=out_shape,
        scratch_shapes=[pltpu.VMEM((GLA_WIDTH, GLA_QK), F32), pltpu.VMEM((GLA_CHUNK, GLA_QK), F32)],
        compiler_params=_params(1),
        name="gla_init" if has_init else "gla",
    )(*args)


def _diff_kernel(*refs, n_kv, lam_init):
    lam_ref, q_ref = refs[0], refs[1]
    k_refs = refs[2:2 + n_kv]
    v_refs = refs[2 + n_kv:2 + 2 * n_kv]
    o_ref = refs[2 + 2 * n_kv]
    lp = lam_ref[...]
    lam = (jnp.exp(jnp.sum(lp[0:1] * lp[1:2], axis=-1, keepdims=True))
           - jnp.exp(jnp.sum(lp[2:3] * lp[3:4], axis=-1, keepdims=True)) + lam_init)
    q = q_ref[0]
    tq = q.shape[0]
    lane = lax.broadcasted_iota(jnp.int32, q.shape, 1)
    first = lane < DIFF_DH
    qs = _bf(jnp.concatenate([jnp.where(first, q, 0.0), jnp.where(first, 0.0, q)], axis=0))
    scores = [_dot_nt(qs, _bf(k_ref[0])) for k_ref in k_refs]
    m = functools.reduce(jnp.maximum, [jnp.max(s, axis=-1, keepdims=True) for s in scores])
    es = [jnp.exp(s - m) for s in scores]
    denom = functools.reduce(jnp.add, [jnp.sum(e, axis=-1, keepdims=True) for e in es])
    inv = 1.0 / denom
    w1 = inv[:tq]
    w2 = inv[tq:] * lam
    out = None
    for e, v_ref in zip(es, v_refs):
        a = e[:tq] * w1 - e[tq:] * w2
        part = _dot(_bf(a), _bf(v_ref[0]))
        out = part if out is None else out + part
    o_ref[0] = out


def _diff_attention(diff_lambda, q, ks, vs, lam_init):
    bsz, lq, _ = q.shape
    n_kv = len(ks)
    tq = DIFF_QB
    head_block = lambda rows: pl.BlockSpec((1, rows, LANES), lambda b, h, i: (b, 0, h))
    in_specs = [pl.BlockSpec((4, DIFF_DH), lambda b, h, i: (0, 0)),
                pl.BlockSpec((1, tq, LANES), lambda b, h, i: (b, i, h))]
    in_specs += [head_block(k.shape[1]) for k in ks]
    in_specs += [head_block(v.shape[1]) for v in vs]
    return pl.pallas_call(
        functools.partial(_diff_kernel, n_kv=n_kv, lam_init=lam_init),
        grid=(bsz, DIFF_HEADS, lq // tq),
        in_specs=in_specs,
        out_specs=pl.BlockSpec((1, tq, LANES), lambda b, h, i: (b, i, h)),
        out_shape=jax.ShapeDtypeStruct((bsz, lq, DIFF_WIDTH), F32),
        compiler_params=_params(3),
        name=f"diff{n_kv}",
    )(diff_lambda, q, *ks, *vs)


def _head_rmsnorm(x, gain):
    ms = jnp.mean(x * x, axis=-1, keepdims=True)
    return x * lax.rsqrt(ms + EPS) * gain


def _merge_kernel(og_ref, od_ref, gate_ref, x_ref, gt_ref, gg_ref, dg_ref, w_ref, fg_ref, y_ref, *, diff_scale):
    gg = gg_ref[...]
    dg = dg_ref[...]
    slabs = []
    for h in range(GLA_HEADS):
        slabs.append(_head_rmsnorm(og_ref[:, h * GLA_DV:(h + 1) * GLA_DV], gg))
    for h in range(DIFF_HEADS):
        slabs.append(_head_rmsnorm(od_ref[:, h * DIFF_DV:(h + 1) * DIFF_DV], dg) * diff_scale)
    o = jnp.concatenate(slabs, axis=-1)
    gate = gate_ref[...]
    o = o * (gate * jax.nn.sigmoid(gate))
    delta = _dot(_bf(o), w_ref[...])
    xn = x_ref[...] + gt_ref[0] * delta
    ms = jnp.mean(xn * xn, axis=-1, keepdims=True)
    y_ref[...] = xn * lax.rsqrt(ms + EPS) * fg_ref[...]


def _merge(og, od, gate, x2d, mod3, mod_row_of_tile, gla_gain, diff_gain, w_out, final_gain, lam_init):
    t = x2d.shape[0]
    tm = PROJ_ROWS
    row = lambda i: (0, 0)
    tile = lambda w: pl.BlockSpec((tm, w), lambda i: (i, 0))
    return pl.pallas_call(
        functools.partial(_merge_kernel, diff_scale=1.0 - lam_init),
        grid=(t // tm,),
        in_specs=[
            tile(GLA_WIDTH), tile(DIFF_WIDTH), tile(MIX_WIDTH), tile(D_MODEL),
            pl.BlockSpec((1, 1, D_MODEL), lambda i: (mod_row_of_tile(i), 0, 2)),
            pl.BlockSpec((1, GLA_DV), row),
            pl.BlockSpec((1, DIFF_DV), row),
            pl.BlockSpec((MIX_WIDTH, D_MODEL), row),
            pl.BlockSpec((1, D_MODEL), row),
        ],
        out_specs=tile(D_MODEL),
        out_shape=jax.ShapeDtypeStruct((t, D_MODEL), F32),
        compiler_params=_params(1),
        name="merge",
    )(og, od, gate, x2d, mod3, gla_gain, diff_gain, w_out, final_gain)


def _rope_tables(rows):
    row = jnp.repeat(jnp.arange(rows), GRID_W).astype(F32)
    col = jnp.tile(jnp.arange(GRID_W), rows).astype(F32)
    inv_freq = ROPE_BASE ** (-jnp.arange(ROPE_PAIRS, dtype=F32) / ROPE_PAIRS)
    ang_r = row[:, None] * inv_freq[None, :]
    ang_c = col[:, None] * inv_freq[None, :]
    cr, sr, cc, sc = jnp.cos(ang_r), jnp.sin(ang_r), jnp.cos(ang_c), jnp.sin(ang_c)
    cos = jnp.concatenate([cr, cr, cc, cc] * 2, axis=-1)
    sin = jnp.concatenate([-sr, sr, -sc, sc] * 2, axis=-1)
    return cos, sin


def _reorder_w_in(w):
    o = GLA_QK + GLA_QK + GLA_WIDTH
    lr = w[:, o:o + 2 * GLA_GATE_RANK]
    rest = w[:, o + 2 * GLA_GATE_RANK:]
    dq = rest[:, :DIFF_QK] * (DIFF_DH ** -0.5)
    pad = jnp.zeros((w.shape[0], LR_PAD - 2 * GLA_GATE_RANK), w.dtype)
    return _bf(jnp.concatenate([w[:, :GLA_QK] * (GLA_DK ** -0.5), w[:, GLA_QK:o], dq, rest[:, DIFF_QK:], lr, pad], axis=1))


def _alpha_weights(w_alpha, b_alpha):
    z = jnp.zeros((GLA_GATE_RANK, GLA_QK), w_alpha.dtype)
    top = jnp.concatenate([w_alpha[0], z], axis=1)
    bot = jnp.concatenate([z, w_alpha[1]], axis=1)
    pad = jnp.zeros((LR_PAD - 2 * GLA_GATE_RANK, 2 * GLA_QK), w_alpha.dtype)
    wa = _bf(jnp.concatenate([top, bot, pad], axis=0))
    ba = jnp.concatenate([b_alpha[0], b_alpha[1]])[None, :]
    return wa, ba


def kernel(x_prompt, x_sample, cache_diff_k, cache_diff_v, state_gla_fwd, state_gla_bwd, c, c_ctx,
           norm_gain, w_mod, b_mod, w_in, w_gla_alpha, b_gla_alpha, diff_lambda,
           gla_head_gain, diff_head_gain, w_out, final_gain):
    bp, lp, d = x_prompt.shape
    bs, ls, _ = x_sample.shape
    depth = norm_gain.shape[0]
    assert depth == 1 and d == D_MODEL and bs + 1 <= MOD_ROWS
    l = 0
    lam_init = 0.8 - 0.6 * math.exp(-0.3 * l)

    cvecs = jnp.concatenate([c_ctx[None, :], c, jnp.zeros((MOD_ROWS - 1 - bs, d), F32)], axis=0)
    mod = _adaln(cvecs, w_mod[l], b_mod[l][None, :])
    mod3 = mod.reshape(MOD_ROWS, 1, 3 * d)

    w_re = _reorder_w_in(w_in[l])
    wa, ba = _alpha_weights(w_gla_alpha[l], b_gla_alpha[l])
    w_out_b = _bf(w_out[l])
    gain = norm_gain[l][None, :]
    gla_gain = gla_head_gain[l][None, :]
    diff_gain = diff_head_gain[l][None, :]
    fgain = final_gain[None, :]
    lam_p = diff_lambda[l]

    xp2 = x_prompt.reshape(bp * lp, d)
    ctx_row = lambda i: 0
    gq, gk, gv, g, dq, dk, dv, gate = _proj(xp2, gain, mod3, ctx_row, w_re, wa, ba, None, lp)
    r3 = lambda a, b_, n: a.reshape(b_, n, a.shape[-1])
    o_gla, s_f, s_b = _gla(r3(gq, bp, lp), r3(gk, bp, lp), r3(gv, bp, lp), r3(g, bp, lp))
    dk3, dv3 = r3(dk, bp, lp), r3(dv, bp, lp)
    o_diff = _diff_attention(lam_p, r3(dq, bp, lp), [dk3], [dv3], lam_init)
    y_prompt = _merge(o_gla.reshape(bp * lp, GLA_WIDTH), o_diff.reshape(bp * lp, DIFF_WIDTH), gate, xp2, mod3,
                      ctx_row, gla_gain, diff_gain, w_out_b, fgain, lam_init).reshape(bp, lp, d)
    new_diff_k = dk.reshape(bp, 1, lp, DIFF_HEADS, 2, DIFF_DH)
    new_diff_v = dv.reshape(bp, 1, lp, DIFF_HEADS, DIFF_DV)
    new_gla_fwd = s_f[:, None]
    new_gla_bwd = s_b[:, None]

    xs2 = x_sample.reshape(bs * ls, d)
    tiles_per_seq = ls // PROJ_ROWS
    lat_row = lambda i: 1 + i // tiles_per_seq
    tabs = _rope_tables(ls // GRID_W)
    gq, gk, gv, g, dq, dk, dv, gate = _proj(xs2, gain, mod3, lat_row, w_re, wa, ba, tabs, ls)
    o_gla = _gla(r3(gq, bs, ls), r3(gk, bs, ls), r3(gv, bs, ls), r3(g, bs, ls),
                 state_gla_fwd[:, l], state_gla_bwd[:, l])[0]
    past = cache_diff_k.shape[2]
    ck = cache_diff_k[:, l].reshape(bs, past, DIFF_QK)
    cv = cache_diff_v[:, l].reshape(bs, past, DIFF_WIDTH)
    o_diff = _diff_attention(lam_p, r3(dq, bs, ls), [ck, r3(dk, bs, ls)], [cv, r3(dv, bs, ls)], lam_init)
    y_sample = _merge(o_gla.reshape(bs * ls, GLA_WIDTH), o_diff.reshape(bs * ls, DIFF_WIDTH), gate, xs2, mod3,
                      lat_row, gla_gain, diff_gain, w_out_b, fgain, lam_init).reshape(bs, ls, d)

    return (y_prompt, y_sample, new_diff_k, new_diff_v, new_gla_fwd, new_gla_bwd)
```

```python
import functools
import math

import numpy as np
import jax
import jax.numpy as jnp
from jax import lax
from jax.experimental import pallas as pl
from jax.experimental.pallas import tpu as pltpu

F32 = jnp.float32
BF16 = jnp.bfloat16

D_MODEL = 1024
GRID_W = 64
GLA_HEADS = 4
GLA_DK = 64
GLA_DV = 128
GLA_QK = GLA_HEADS * GLA_DK
GLA_WIDTH = GLA_HEADS * GLA_DV
GLA_GATE_RANK = 16
GLA_GATE_TEMP = 16.0
GLA_CHUNK = 64
DIFF_HEADS = 4
DIFF_DH = 64
DIFF_DV = 2 * DIFF_DH
DIFF_QK = DIFF_HEADS * 2 * DIFF_DH
DIFF_WIDTH = DIFF_HEADS * DIFF_DV
MIX_WIDTH = GLA_WIDTH + DIFF_WIDTH
ROPE_PAIRS = DIFF_DH // 4
ROPE_BASE = 10000.0
EPS = 1e-6

LANES = 128
SUBLANES = 8
MOD_ROWS = 8
PROJ_ROWS = 512
MERGE_ROWS = 256
PREP_STEPS = 8
DIFF_QB = 256
GLA_SUPER = 256
GLA_PAIR_K = 2 * GLA_DK
GLA_PAIR_V = 2 * GLA_DV
GLA_SAFE_LOG_DECAY = -40.0
VMEM_LIMIT = 56 * 1024 * 1024

_R_GQ = 0
_R_GK = _R_GQ + GLA_QK
_R_GV = _R_GK + GLA_QK
_R_LR = _R_GV + GLA_WIDTH
_R_DQ = _R_LR + 2 * GLA_GATE_RANK
_R_DK = _R_DQ + DIFF_QK
_R_DV = _R_DK + DIFF_QK
_R_GATE = _R_DV + DIFF_WIDTH
_R_END = _R_GATE + MIX_WIDTH


def _bf(x):
    return x.astype(BF16)


def _dot(a, b):
    return jnp.dot(a, b, preferred_element_type=F32)


def _dot_nt(a, b):
    return lax.dot_general(a, b, (((1,), (1,)), ((), ())), preferred_element_type=F32)


def _dot_tn(a, b):
    return lax.dot_general(a, b, (((0,), (0,)), ((), ())), preferred_element_type=F32)


def _params(n_parallel=0, n_arbitrary=0):
    sem = ("parallel",) * n_parallel + ("arbitrary",) * n_arbitrary
    return pltpu.CompilerParams(dimension_semantics=sem, vmem_limit_bytes=VMEM_LIMIT)


def _iota(shape, axis):
    return lax.broadcasted_iota(jnp.int32, shape, axis)


def _prep_kernel(wi_ref, wo_ref, wib_ref, wob_ref):
    wib_ref[...] = _bf(wi_ref[...])
    wob_ref[...] = _bf(wo_ref[...])


def _prep(w_in_t, w_out):
    n_in = w_in_t.shape[0]
    ti = pl.cdiv(n_in, PREP_STEPS * 2 * SUBLANES) * 2 * SUBLANES
    to = w_out.shape[0] // PREP_STEPS
    spec = lambda r: pl.BlockSpec((r, D_MODEL), lambda i: (i, 0))
    return pl.pallas_call(
        _prep_kernel,
        grid=(PREP_STEPS,),
        in_specs=[spec(ti), spec(to)],
        out_specs=[spec(ti), spec(to)],
        out_shape=[jax.ShapeDtypeStruct(w_in_t.shape, BF16), jax.ShapeDtypeStruct(w_out.shape, BF16)],
        compiler_params=_params(1),
        name="prep",
    )(w_in_t, w_out)


def _adaln_kernel(c_ref, w_ref, b_ref, o_ref):
    c = c_ref[...]
    s = c * jax.nn.sigmoid(c)
    o_ref[...] = _dot(_bf(s), _bf(w_ref[...])) + b_ref[...]


def _adaln(cvecs, w_mod, b_mod):
    n = w_mod.shape[1]
    tn = D_MODEL
    return pl.pallas_call(
        _adaln_kernel,
        grid=(n // tn,),
        in_specs=[
            pl.BlockSpec((MOD_ROWS, D_MODEL), lambda j: (0, 0)),
            pl.BlockSpec((D_MODEL, tn), lambda j: (0, j)),
            pl.BlockSpec((1, tn), lambda j: (0, j)),
        ],
        out_specs=pl.BlockSpec((MOD_ROWS, tn), lambda j: (0, j)),
        out_shape=jax.ShapeDtypeStruct((MOD_ROWS, n), F32),
        compiler_params=_params(1),
        name="adaln",
    )(cvecs, w_mod, b_mod)


def _log_sigmoid(z):
    return jnp.minimum(z, 0.0) - jnp.log1p(jnp.exp(-jnp.abs(z)))


def _rope_lanes(x, cos, sin):
    lane = _iota(x.shape, 1)
    up = pltpu.roll(x, ROPE_PAIRS, axis=1)
    dn = pltpu.roll(x, LANES - ROPE_PAIRS, axis=1)
    partner = jnp.where((lane & ROPE_PAIRS) == 0, dn, up)
    return x * cos + partner * sin


def _rope_rows(x, cos, sin):
    p = ROPE_PAIRS
    parts = []
    for g in range(x.shape[0] // (2 * p)):
        parts += [x[(2 * g + 1) * p:(2 * g + 2) * p], x[2 * g * p:(2 * g + 1) * p]]
    return x * cos + jnp.concatenate(parts, axis=0) * sin


def _proj_kernel(*refs, rope, tiles_per_seq):
    if rope:
        (x_ref, gain_ref, mod_ref, wt_ref, wa_ref, ba_ref, cos_ref, sin_ref, cost_ref, sint_ref,
         gq_ref, gk_ref, gv_ref, g_ref, dq_ref, dkt_ref, dv_ref, gate_ref) = refs
        mod_row = 1 + pl.program_id(0) // tiles_per_seq
    else:
        (x_ref, gain_ref, mod_ref, wt_ref, wa_ref, ba_ref,
         gq_ref, gk_ref, gv_ref, g_ref, dq_ref, dkt_ref, dv_ref, gate_ref) = refs
        mod_row = 0
    x = x_ref[...]
    d = x.shape[1]
    ms = jnp.mean(x * x, axis=-1, keepdims=True)
    h = x * lax.rsqrt(ms + EPS) * gain_ref[...]
    shift = mod_ref[pl.ds(mod_row, 1), 0:d]
    scale = mod_ref[pl.ds(mod_row, 1), d:2 * d]
    hb = _bf(h * (1.0 + scale) + shift)

    def sec(a, b):
        return _dot_nt(hb, wt_ref[a:b, :])

    gq_ref[...] = sec(_R_GQ, _R_GK) * (GLA_DK ** -0.5)
    gk_ref[...] = sec(_R_GK, _R_GV)
    gv_ref[...] = _bf(sec(_R_GV, _R_LR))
    gate_ref[...] = sec(_R_GATE, _R_END)

    lr = sec(_R_LR, _R_DQ)
    zero = jnp.zeros((GLA_GATE_RANK, GLA_QK), F32)
    wa = jnp.concatenate([jnp.concatenate([wa_ref[0], zero], axis=1),
                          jnp.concatenate([zero, wa_ref[1]], axis=1)], axis=0)
    ba = jnp.concatenate([ba_ref[0:1, :], ba_ref[1:2, :]], axis=1)
    z = _dot(_bf(lr), _bf(wa)) + ba
    g_ref[...] = _log_sigmoid(z) * (1.0 / GLA_GATE_TEMP)

    dq = sec(_R_DQ, _R_DK) * (DIFF_DH ** -0.5)
    dkt = _dot_nt(wt_ref[_R_DK:_R_DV, :], hb)
    dv = sec(_R_DV, _R_GATE)
    if rope:
        cos, sin, cost, sint = cos_ref[...], sin_ref[...], cost_ref[...], sint_ref[...]
        for hd in range(DIFF_HEADS):
            sl = slice(hd * LANES, (hd + 1) * LANES)
            dq_ref[:, sl] = _bf(_rope_lanes(dq[:, sl], cos, sin))
            dkt_ref[0, sl, :] = _bf(_rope_rows(dkt[sl, :], cost, sint))
        dv_ref[...] = _bf(dv)
    else:
        dq_ref[...] = _bf(dq)
        seq = dkt_ref.shape[2]
        for j in range(dkt_ref.shape[0]):
            dkt_ref[j] = dkt[:, j * seq:(j + 1) * seq]
        dv_ref[...] = dv


def _proj(x2d, gain, mod, wt, wa, ba, rope_tabs, seq_len):
    t = x2d.shape[0]
    tm = PROJ_ROWS
    nt = t // tm
    rope = rope_tabs is not None
    whole = lambda shape: pl.BlockSpec(shape, lambda i: (0,) * len(shape))
    in_specs = [
        pl.BlockSpec((tm, D_MODEL), lambda i: (i, 0)),
        whole(gain.shape), whole(mod.shape), whole(wt.shape), whole(wa.shape), whole(ba.shape),
    ]
    args = [x2d, gain, mod, wt, wa, ba]
    tile = lambda w: pl.BlockSpec((tm, w), lambda i: (i, 0))
    if rope:
        tiles_per_seq = seq_len // tm
        in_specs += [pl.BlockSpec((tm, LANES), lambda i: (i % tiles_per_seq, 0))] * 2
        in_specs += [pl.BlockSpec((LANES, tm), lambda i: (0, i % tiles_per_seq))] * 2
        args += list(rope_tabs)
        dkt_spec = pl.BlockSpec((1, DIFF_QK, tm), lambda i: (i // tiles_per_seq, 0, i % tiles_per_seq))
        kv_dtype = BF16
    else:
        tiles_per_seq = 1
        seqs_per_tile = tm // seq_len
        dkt_spec = pl.BlockSpec((seqs_per_tile, DIFF_QK, seq_len), lambda i: (i, 0, 0))
        kv_dtype = F32
    out_specs = [tile(GLA_QK), tile(GLA_QK), tile(GLA_WIDTH), tile(2 * GLA_QK), tile(DIFF_QK),
                 dkt_spec, tile(DIFF_WIDTH), tile(MIX_WIDTH)]
    sds = jax.ShapeDtypeStruct
    out_shape = [sds((t, GLA_QK), F32), sds((t, GLA_QK), F32), sds((t, GLA_WIDTH), BF16), sds((t, 2 * GLA_QK), F32),
                 sds((t, DIFF_QK), BF16), sds((t // seq_len, DIFF_QK, seq_len), kv_dtype),
                 sds((t, DIFF_WIDTH), kv_dtype), sds((t, MIX_WIDTH), F32)]
    return pl.pallas_call(
        functools.partial(_proj_kernel, rope=rope, tiles_per_seq=tiles_per_seq),
        grid=(nt,),
        in_specs=in_specs,
        out_specs=out_specs,
        out_shape=out_shape,
        compiler_params=_params(1),
        name="proj_rope" if rope else "proj",
    )(*args)


def _tile4(x):
    return jnp.concatenate([x, x, x, x], axis=0)


def _gla_fast(q_ref, k_ref, v_ref, g_ref, o_ref, s_ref, qi_ref, kl_ref, et_ref, *, seq_len):
    c_len, sc_len = GLA_CHUNK, GLA_SUPER
    nc = seq_len // c_len
    cps = sc_len // c_len
    r = _iota((sc_len, 2 * sc_len), 0)
    c = _iota((sc_len, 2 * sc_len), 1) & (sc_len - 1)
    same = (r // c_len) == (c // c_len)
    tri2 = [(same & (c <= r)).astype(BF16), (same & (c >= r)).astype(BF16)]
    rt = _iota((c_len, GLA_HEADS * c_len), 0)
    cs = _iota((c_len, GLA_HEADS * c_len), 1) & (c_len - 1)
    keep = [cs <= rt, cs >= rt]
    k_mask = (_iota((GLA_HEADS * c_len, GLA_QK), 0) // c_len) == (_iota((GLA_HEADS * c_len, GLA_QK), 1) // GLA_DK)
    v_mask = (_iota((GLA_HEADS * c_len, GLA_WIDTH), 0) // c_len) == (_iota((GLA_HEADS * c_len, GLA_WIDTH), 1) // GLA_DV)
    p_mask = (_iota((GLA_PAIR_K, GLA_PAIR_V), 0) // GLA_DK) == (_iota((GLA_PAIR_K, GLA_PAIR_V), 1) // GLA_DV)
    zb = jnp.zeros((), BF16)

    for sci in range(seq_len // sc_len):
        rows = slice(sci * sc_len, (sci + 1) * sc_len)
        q = q_ref[0, rows, :]
        k = k_ref[0, rows, :]
        vb = v_ref[0, rows, :]
        bs = []
        for d in range(2):
            g = g_ref[0, rows, d * GLA_QK:(d + 1) * GLA_QK]
            hi = _bf(g)
            lo = _bf(g - hi.astype(F32))
            bs.append(_dot(tri2[d], jnp.concatenate([hi, lo], axis=0)))
        qib, kob = [], []
        for d in range(2):
            b = bs[d]
            qi = _bf(q * jnp.exp(b))
            ko = k * jnp.exp(-b)
            kls = []
            for j in range(cps):
                last = j * c_len + (0 if d else c_len - 1)
                e_tot = jnp.exp(b[last:last + 1, :])
                et_ref[d, sci * cps + j] = jnp.broadcast_to(e_tot, (SUBLANES, GLA_QK))
                kls.append(_bf(ko[j * c_len:(j + 1) * c_len] * e_tot))
            qib.append(qi)
            kob.append(_bf(ko))
            qi_ref[d, rows, :] = qi
            kl_ref[d, rows, :] = jnp.concatenate(kls, axis=0)
        atts = [[_dot_nt(qib[d][j * c_len:(j + 1) * c_len],
                         jnp.where(k_mask, _tile4(kob[d][j * c_len:(j + 1) * c_len]), zb))
                 for d in range(2)] for j in range(cps)]
        for j in range(cps):
            a2 = jnp.concatenate([_bf(jnp.where(keep[d], atts[j][d], 0.0)) for d in range(2)], axis=0)
            v_bd = jnp.where(v_mask, _tile4(vb[j * c_len:(j + 1) * c_len]), zb)
            r2 = _dot(a2, v_bd)
            o_ref[0, sci * sc_len + j * c_len:sci * sc_len + (j + 1) * c_len, :] = r2[:c_len] + r2[c_len:]

    for i in range(nc):
        for d in range(2):
            ci = (nc - 1 - i) if d else i
            rows = slice(ci * c_len, (ci + 1) * c_len)
            qi = qi_ref[d, rows, :]
            kl = kl_ref[d, rows, :]
            vb = v_ref[0, rows, :]
            e_col = et_ref[d, ci].T[:, 0:1]
            inter = []
            for p in range(2):
                ks = slice(p * GLA_PAIR_K, (p + 1) * GLA_PAIR_K)
                s = s_ref[d, p]
                inter.append(_dot(qi[:, ks], _bf(s)))
                upd = _dot_tn(kl[:, ks], vb[:, p * GLA_PAIR_V:(p + 1) * GLA_PAIR_V])
                s_ref[d, p] = s * e_col[ks] + jnp.where(p_mask, upd, 0.0)
            o_ref[0, rows, :] += jnp.concatenate(inter, axis=1)


def _gla_slow(q_ref, k_ref, v_ref, g_ref, o_ref, s_ref, b_ref, vf_ref, *, seq_len):
    c_len = GLA_CHUNK
    nc = seq_len // c_len
    r64 = _iota((c_len, c_len), 0)
    c64 = _iota((c_len, c_len), 1)
    p_mask = (_iota((GLA_PAIR_K, GLA_PAIR_V), 0) // GLA_DK) == (_iota((GLA_PAIR_K, GLA_PAIR_V), 1) // GLA_DV)
    expand = ((_iota((GLA_QK, GLA_WIDTH), 0) // GLA_DK) == (_iota((GLA_QK, GLA_WIDTH), 1) // GLA_DV)).astype(BF16)
    t_idx = _iota((c_len, GLA_QK), 0)
    for d in range(2):
        reverse = bool(d)
        tri = ((c64 >= r64) if reverse else (c64 <= r64)).astype(F32)

        def step(i, carry, d=d, reverse=reverse, tri=tri):
            ci = (nc - 1 - i) if reverse else i
            base = pl.multiple_of(ci * c_len, c_len)
            rows = pl.ds(base, c_len)
            q = q_ref[0, rows, :]
            k = k_ref[0, rows, :]
            vb = v_ref[0, rows, :]
            g = g_ref[0, rows, d * GLA_QK:(d + 1) * GLA_QK]
            b = jnp.dot(tri, g, precision=lax.Precision.HIGHEST, preferred_element_type=F32)
            b_tot = jnp.sum(g, axis=0, keepdims=True)
            b_ref[...] = b
            vf_ref[...] = vb.astype(F32)

            def key_row(s, acc):
                b_s = b_ref[pl.ds(s, 1), :]
                k_s = k_ref[0, pl.ds(base + s, 1), :]
                v_s = vf_ref[pl.ds(s, 1), :]
                visible = (t_idx <= s) if reverse else (t_idx >= s)
                w = jnp.where(visible, q * jnp.exp(jnp.minimum(b - b_s, 0.0)), 0.0) * k_s
                return acc + _dot(_bf(w), expand) * v_s

            o = lax.fori_loop(0, c_len, key_row, jnp.zeros((c_len, GLA_WIDTH), F32))
            qi = _bf(q * jnp.exp(b))
            kl = _bf(k * jnp.exp(b_tot - b))
            e_col = jnp.broadcast_to(jnp.exp(b_tot), (SUBLANES, GLA_QK)).T[:, 0:1]
            inter = []
            for p in range(2):
                ks = slice(p * GLA_PAIR_K, (p + 1) * GLA_PAIR_K)
                s = s_ref[d, p]
                inter.append(_dot(qi[:, ks], _bf(s)))
                upd = _dot_tn(kl[:, ks], vb[:, p * GLA_PAIR_V:(p + 1) * GLA_PAIR_V])
                s_ref[d, p] = s * e_col[ks] + jnp.where(p_mask, upd, 0.0)
            o = o + jnp.concatenate(inter, axis=1)
            if reverse:
                o_ref[0, rows, :] += o
            else:
                o_ref[0, rows, :] = o
            return carry

        lax.fori_loop(0, nc, step, 0)


def _gla_kernel(*refs, seq_len, has_init):
    if has_init:
        q_ref, k_ref, v_ref, g_ref, s0f_ref, s0b_ref, o_ref, s_ref, qi_ref, kl_ref, et_ref, b_ref, vf_ref = refs
    else:
        q_ref, k_ref, v_ref, g_ref, o_ref, sf_ref, sb_ref, s_ref, qi_ref, kl_ref, et_ref, b_ref, vf_ref = refs
    zero = jnp.zeros((GLA_DK, GLA_DV), F32)
    for d in range(2):
        for p in range(2):
            if has_init:
                s0 = (s0b_ref if d else s0f_ref)
                top = jnp.concatenate([s0[0, 2 * p], zero], axis=1)
                bot = jnp.concatenate([zero, s0[0, 2 * p + 1]], axis=1)
                s_ref[d, p] = jnp.concatenate([top, bot], axis=0)
            else:
                s_ref[d, p] = jnp.zeros((GLA_PAIR_K, GLA_PAIR_V), F32)

    nc = seq_len // GLA_CHUNK
    worst = jnp.zeros((1, 2 * GLA_QK), F32)
    for ci in range(nc):
        worst = jnp.minimum(worst, jnp.sum(g_ref[0, ci * GLA_CHUNK:(ci + 1) * GLA_CHUNK, :], axis=0, keepdims=True))
    lax.cond(jnp.min(worst) >= GLA_SAFE_LOG_DECAY,
             functools.partial(_gla_fast, q_ref, k_ref, v_ref, g_ref, o_ref, s_ref, qi_ref, kl_ref, et_ref,
                               seq_len=seq_len),
             functools.partial(_gla_slow, q_ref, k_ref, v_ref, g_ref, o_ref, s_ref, b_ref, vf_ref, seq_len=seq_len))
    if not has_init:
        for d, ref in ((0, sf_ref), (1, sb_ref)):
            for h in range(GLA_HEADS):
                p, j = divmod(h, 2)
                ref[0, h] = s_ref[d, p, j * GLA_DK:(j + 1) * GLA_DK, j * GLA_DV:(j + 1) * GLA_DV]


def _gla(q, k, v, g, s0f=None, s0b=None):
    bsz, seq_len, _ = q.shape
    has_init = s0f is not None
    seq_spec = lambda w: pl.BlockSpec((1, seq_len, w), lambda b: (b, 0, 0))
    st_spec = pl.BlockSpec((1, GLA_HEADS, GLA_DK, GLA_DV), lambda b: (b, 0, 0, 0))
    in_specs = [seq_spec(GLA_QK), seq_spec(GLA_QK), seq_spec(GLA_WIDTH), seq_spec(2 * GLA_QK)]
    args = [q, k, v, g]
    out_specs = [seq_spec(GLA_WIDTH)]
    out_shape = [jax.ShapeDtypeStruct((bsz, seq_len, GLA_WIDTH), F32)]
    if has_init:
        in_specs += [st_spec, st_spec]
        args += [s0f, s0b]
    else:
        out_specs += [st_spec, st_spec]
        out_shape += [jax.ShapeDtypeStruct((bsz, GLA_HEADS, GLA_DK, GLA_DV), F32)] * 2
    scratch = [
        pltpu.VMEM((2, 2, GLA_PAIR_K, GLA_PAIR_V), F32),
        pltpu.VMEM((2, seq_len, GLA_QK), BF16),
        pltpu.VMEM((2, seq_len, GLA_QK), BF16),
        pltpu.VMEM((2, seq_len // GLA_CHUNK, SUBLANES, GLA_QK), F32),
        pltpu.VMEM((GLA_CHUNK, GLA_QK), F32),
        pltpu.VMEM((GLA_CHUNK, GLA_WIDTH), F32),
    ]
    return pl.pallas_call(
        functools.partial(_gla_kernel, seq_len=seq_len, has_init=has_init),
        grid=(bsz,),
        in_specs=in_specs,
        out_specs=out_specs,
        out_shape=out_shape,
        scratch_shapes=scratch,
        compiler_params=_params(1),
        name="gla_init" if has_init else "gla",
    )(*args)


def _diff_kernel(*refs, n_kv, n_heads, lam_init):
    lam_ref, q_ref = refs[0], refs[1]
    kt_refs = refs[2:2 + n_kv]
    v_refs = refs[2 + n_kv:2 + 2 * n_kv]
    o_ref = refs[2 + 2 * n_kv]
    lp = lam_ref[...]
    lam = (jnp.exp(jnp.sum(lp[0:1] * lp[1:2], axis=-1, keepdims=True))
           - jnp.exp(jnp.sum(lp[2:3] * lp[3:4], axis=-1, keepdims=True)) + lam_init)
    lq = q_ref.shape[1]
    tq = min(DIFF_QB, lq)
    first = _iota((tq, LANES), 1) < DIFF_DH
    zb = jnp.zeros((), BF16)
    for hd in range(n_heads):
        sl = slice(hd * LANES, (hd + 1) * LANES)
        kts = [_bf(kt_ref[0, sl, :]) for kt_ref in kt_refs]
        vs = [_bf(v_ref[0, :, sl]) for v_ref in v_refs]
        for qb in range(lq // tq):
            q = q_ref[0, qb * tq:(qb + 1) * tq, sl]
            qs = jnp.concatenate([jnp.where(first, q, zb), jnp.where(first, zb, q)], axis=0)
            scores = [_dot(qs, kt) for kt in kts]
            m = functools.reduce(jnp.maximum, [jnp.max(s, axis=-1, keepdims=True) for s in scores])
            es = [jnp.exp(s - m) for s in scores]
            denom = functools.reduce(jnp.add, [jnp.sum(e, axis=-1, keepdims=True) for e in es])
            inv = 1.0 / denom
            w1 = inv[:tq]
            w2 = inv[tq:] * lam
            out = None
            for e, v in zip(es, vs):
                part = _dot(_bf(e[:tq] * w1 - e[tq:] * w2), v)
                out = part if out is None else out + part
            o_ref[0, qb * tq:(qb + 1) * tq, sl] = out


def _diff_attention(diff_lambda, q, kts, vs, lam_init, heads_per_step):
    bsz, lq, _ = q.shape
    n_kv = len(kts)
    hw = heads_per_step * LANES
    in_specs = [pl.BlockSpec(diff_lambda.shape, lambda b, h: (0, 0)),
                pl.BlockSpec((1, lq, hw), lambda b, h: (b, 0, h))]
    in_specs += [pl.BlockSpec((1, hw, kt.shape[2]), lambda b, h: (b, h, 0)) for kt in kts]
    in_specs += [pl.BlockSpec((1, v.shape[1], hw), lambda b, h: (b, 0, h)) for v in vs]
    return pl.pallas_call(
        functools.partial(_diff_kernel, n_kv=n_kv, n_heads=heads_per_step, lam_init=lam_init),
        grid=(bsz, DIFF_HEADS // heads_per_step),
        in_specs=in_specs,
        out_specs=pl.BlockSpec((1, lq, hw), lambda b, h: (b, 0, h)),
        out_shape=jax.ShapeDtypeStruct((bsz, lq, DIFF_WIDTH), F32),
        compiler_params=_params(2),
        name=f"diff{n_kv}",
    )(diff_lambda, q, *kts, *vs)


def _head_rmsnorm(x, gain):
    ms = jnp.mean(x * x, axis=-1, keepdims=True)
    return x * lax.rsqrt(ms + EPS) * gain


def _merge_kernel(og_ref, od_ref, gate_ref, x_ref, mod_ref, gg_ref, dg_ref, w_ref, fg_ref, y_ref, *,
                  diff_scale, tiles_per_seq):
    mod_row = 0 if tiles_per_seq is None else 1 + pl.program_id(0) // tiles_per_seq
    gg = gg_ref[...]
    dg = dg_ref[...]
    slabs = []
    for h in range(GLA_HEADS):
        slabs.append(_head_rmsnorm(og_ref[:, h * GLA_DV:(h + 1) * GLA_DV], gg))
    for h in range(DIFF_HEADS):
        slabs.append(_head_rmsnorm(od_ref[:, h * DIFF_DV:(h + 1) * DIFF_DV], dg) * diff_scale)
    o = jnp.concatenate(slabs, axis=-1)
    gate = gate_ref[...]
    o = o * (gate * jax.nn.sigmoid(gate))
    delta = _dot(_bf(o), w_ref[...])
    d = delta.shape[1]
    xn = x_ref[...] + mod_ref[pl.ds(mod_row, 1), 2 * d:3 * d] * delta
    ms = jnp.mean(xn * xn, axis=-1, keepdims=True)
    y_ref[...] = xn * lax.rsqrt(ms + EPS) * fg_ref[...]


def _merge(og, od, gate, x2d, mod, tiles_per_seq, gla_gain, diff_gain, w_out, final_gain, lam_init):
    t = x2d.shape[0]
    tm = MERGE_ROWS
    whole = lambda shape: pl.BlockSpec(shape, lambda i: (0,) * len(shape))
    tile = lambda w: pl.BlockSpec((tm, w), lambda i: (i, 0))
    return pl.pallas_call(
        functools.partial(_merge_kernel, diff_scale=1.0 - lam_init, tiles_per_seq=tiles_per_seq),
        grid=(t // tm,),
        in_specs=[tile(GLA_WIDTH), tile(DIFF_WIDTH), tile(MIX_WIDTH), tile(D_MODEL), whole(mod.shape),
                  whole(gla_gain.shape), whole(diff_gain.shape), whole(w_out.shape), whole(final_gain.shape)],
        out_specs=tile(D_MODEL),
        out_shape=jax.ShapeDtypeStruct((t, D_MODEL), F32),
        compiler_params=_params(1),
        name="merge",
    )(og, od, gate, x2d, mod, gla_gain, diff_gain, w_out, final_gain)


def _rope_tables(seq_len):
    t = np.arange(seq_len)
    inv_freq = ROPE_BASE ** (-np.arange(ROPE_PAIRS, dtype=np.float64) / ROPE_PAIRS)
    ang_r = (t // GRID_W)[:, None] * inv_freq[None, :]
    ang_c = (t % GRID_W)[:, None] * inv_freq[None, :]
    cr, sr, cc, sc = np.cos(ang_r), np.sin(ang_r), np.cos(ang_c), np.sin(ang_c)
    cos = np.concatenate([cr, cr, cc, cc] * 2, axis=-1).astype(np.float32)
    sin = np.concatenate([-sr, sr, -sc, sc] * 2, axis=-1).astype(np.float32)
    return (jnp.asarray(cos), jnp.asarray(sin),
            jnp.asarray(np.ascontiguousarray(cos.T)), jnp.asarray(np.ascontiguousarray(sin.T)))


def kernel(x_prompt, x_sample, cache_diff_k, cache_diff_v, state_gla_fwd, state_gla_bwd, c, c_ctx,
           norm_gain, w_mod, b_mod, w_in, w_gla_alpha, b_gla_alpha, diff_lambda,
           gla_head_gain, diff_head_gain, w_out, final_gain):
    bp, lp, d = x_prompt.shape
    bs, ls, _ = x_sample.shape
    depth = norm_gain.shape[0]
    assert depth == 1 and d == D_MODEL and bs + 1 <= MOD_ROWS
    assert PROJ_ROWS % lp == 0 and ls % PROJ_ROWS == 0 and w_in.shape[2] == _R_END
    l = 0
    lam_init = 0.8 - 0.6 * math.exp(-0.3 * l)

    cvecs = jnp.concatenate([c_ctx[None, :], c, jnp.zeros((MOD_ROWS - 1 - bs, d), F32)], axis=0)
    mod = _adaln(cvecs, w_mod[l], b_mod)

    wt, w_out_b = _prep(jnp.swapaxes(w_in[l], 0, 1), w_out[l])
    wa, ba = w_gla_alpha[l], b_gla_alpha[l]
    fgain = final_gain[None, :]
    lam_p = diff_lambda[l]
    r3 = lambda a, b_, n: a.reshape(b_, n, a.shape[-1])

    xp2 = x_prompt.reshape(bp * lp, d)
    gq, gk, gv, g, dq, dkt, dv, gate = _proj(xp2, norm_gain, mod, wt, wa, ba, None, lp)
    o_gla, s_f, s_b = _gla(r3(gq, bp, lp), r3(gk, bp, lp), r3(gv, bp, lp), r3(g, bp, lp))
    o_diff = _diff_attention(lam_p, r3(dq, bp, lp), [dkt], [r3(dv, bp, lp)], lam_init, DIFF_HEADS)
    y_prompt = _merge(o_gla.reshape(bp * lp, GLA_WIDTH), o_diff.reshape(bp * lp, DIFF_WIDTH), gate, xp2, mod,
                      None, gla_head_gain, diff_head_gain, w_out_b, fgain, lam_init).reshape(bp, lp, d)
    new_diff_k = jnp.transpose(dkt.reshape(bp, DIFF_HEADS, 2, DIFF_DH, lp), (0, 4, 1, 2, 3))[:, None]
    new_diff_v = dv.reshape(bp, 1, lp, DIFF_HEADS, DIFF_DV)
    new_gla_fwd = s_f[:, None]
    new_gla_bwd = s_b[:, None]

    xs2 = x_sample.reshape(bs * ls, d)
    gq, gk, gv, g, dq, dkt, dv, gate = _proj(xs2, norm_gain, mod, wt, wa, ba, _rope_tables(ls), ls)
    o_gla = _gla(r3(gq, bs, ls), r3(gk, bs, ls), r3(gv, bs, ls), r3(g, bs, ls),
                 state_gla_fwd[:, l], state_gla_bwd[:, l])[0]
    past = cache_diff_k.shape[2]
    ckt = jnp.transpose(cache_diff_k[:, l], (0, 2, 3, 4, 1)).reshape(bs, DIFF_QK, past)
    cv = cache_diff_v[:, l].reshape(bs, past, DIFF_WIDTH)
    o_diff = _diff_attention(lam_p, r3(dq, bs, ls), [ckt, dkt], [cv, r3(dv, bs, ls)], lam_init, 1)
    y_sample = _merge(o_gla.reshape(bs * ls, GLA_WIDTH), o_diff.reshape(bs * ls, DIFF_WIDTH), gate, xs2, mod,
                      ls // MERGE_ROWS, gla_head_gain, diff_head_gain, w_out_b, fgain, lam_init).reshape(bs, ls, d)

    return (y_prompt, y_sample, new_diff_k, new_diff_v, new_gla_fwd, new_gla_bwd)
```

```python
import functools
import math

import numpy as np
import jax
import jax.numpy as jnp
from jax import lax
from jax.experimental import pallas as pl
from jax.experimental.pallas import tpu as pltpu

F32 = jnp.float32
BF16 = jnp.bfloat16

D_MODEL = 1024
GRID_W = 64
GLA_HEADS = 4
GLA_DK = 64
GLA_DV = 128
GLA_QK = GLA_HEADS * GLA_DK
GLA_WIDTH = GLA_HEADS * GLA_DV
GLA_GATE_RANK = 16
GLA_GATE_TEMP = 16.0
GLA_CHUNK = 64
DIFF_HEADS = 4
DIFF_DH = 64
DIFF_DV = 2 * DIFF_DH
DIFF_QK = DIFF_HEADS * 2 * DIFF_DH
DIFF_WIDTH = DIFF_HEADS * DIFF_DV
MIX_WIDTH = GLA_WIDTH + DIFF_WIDTH
ROPE_PAIRS = DIFF_DH // 4
ROPE_BASE = 10000.0
EPS = 1e-6
LOG2E = math.log2(math.e)

LANES = 128
SUBLANES = 8
MOD_ROWS = 8
ROW_TILE = 512
PREP_STEPS = 8
DIFF_QB = 256
DIFF_GROUP = 2
GLA_SUPER = 256
GLA_PAIR_K = 2 * GLA_DK
GLA_PAIR_V = 2 * GLA_DV
GLA_SAFE_LOG_DECAY = -40.0
VMEM_LIMIT = 58 * 1024 * 1024

_R_GQ = 0
_R_GK = _R_GQ + GLA_QK
_R_GV = _R_GK + GLA_QK
_R_LR = _R_GV + GLA_WIDTH
_R_DQ = _R_LR + 2 * GLA_GATE_RANK
_R_DK = _R_DQ + DIFF_QK
_R_DV = _R_DK + DIFF_QK
_R_GATE = _R_DV + DIFF_WIDTH
_R_END = _R_GATE + MIX_WIDTH


def _bf(x):
    return x.astype(BF16)


def _dot(a, b):
    return jnp.dot(a, b, preferred_element_type=F32)


def _dot_nt(a, b):
    return lax.dot_general(a, b, (((1,), (1,)), ((), ())), preferred_element_type=F32)


def _dot_tn(a, b):
    return lax.dot_general(a, b, (((0,), (0,)), ((), ())), preferred_element_type=F32)


def _params(n_parallel=0, n_arbitrary=0):
    sem = ("parallel",) * n_parallel + ("arbitrary",) * n_arbitrary
    return pltpu.CompilerParams(dimension_semantics=sem, vmem_limit_bytes=VMEM_LIMIT)


def _iota(shape, axis):
    return lax.broadcasted_iota(jnp.int32, shape, axis)


def _rows(ref, start, size):
    return ref.at[pl.ds(start, size)]


def _prep_kernel(wi_ref, wo_ref, wib_ref, wob_ref):
    wib_ref[...] = _bf(wi_ref[...])
    wob_ref[...] = _bf(wo_ref[...])


def _prep(w_in_t, w_out):
    n_in = w_in_t.shape[0]
    ti = pl.cdiv(n_in, PREP_STEPS * 2 * SUBLANES) * 2 * SUBLANES
    to = w_out.shape[0] // PREP_STEPS
    spec = lambda r: pl.BlockSpec((r, D_MODEL), lambda i: (i, 0))
    return pl.pallas_call(
        _prep_kernel,
        grid=(PREP_STEPS,),
        in_specs=[spec(ti), spec(to)],
        out_specs=[spec(ti), spec(to)],
        out_shape=[jax.ShapeDtypeStruct(w_in_t.shape, BF16), jax.ShapeDtypeStruct(w_out.shape, BF16)],
        compiler_params=_params(1),
        name="prep",
    )(w_in_t, w_out)


def _adaln_kernel(c_ref, w_ref, b_ref, o_ref):
    c = c_ref[...]
    s = c * jax.nn.sigmoid(c)
    o_ref[...] = _dot(_bf(s), _bf(w_ref[...])) + b_ref[...]


def _adaln(cvecs, w_mod, b_mod):
    n = w_mod.shape[1]
    tn = D_MODEL
    return pl.pallas_call(
        _adaln_kernel,
        grid=(n // tn,),
        in_specs=[
            pl.BlockSpec((MOD_ROWS, D_MODEL), lambda j: (0, 0)),
            pl.BlockSpec((D_MODEL, tn), lambda j: (0, j)),
            pl.BlockSpec((1, tn), lambda j: (0, j)),
        ],
        out_specs=pl.BlockSpec((MOD_ROWS, tn), lambda j: (0, j)),
        out_shape=jax.ShapeDtypeStruct((MOD_ROWS, n), F32),
        compiler_params=_params(1),
        name="adaln",
    )(cvecs, w_mod, b_mod)


def _log_sigmoid(z):
    return jnp.minimum(z, 0.0) - jnp.log1p(jnp.exp(-jnp.abs(z)))


def _silu(x):
    h = 0.5 * x
    return h + h * jnp.tanh(h)


def _rope_lanes(x, cos, sin):
    lane = _iota(x.shape, 1)
    up = pltpu.roll(x, ROPE_PAIRS, axis=1)
    dn = pltpu.roll(x, LANES - ROPE_PAIRS, axis=1)
    partner = jnp.where((lane & ROPE_PAIRS) == 0, dn, up)
    return x * cos + partner * sin


def _rope_rows(x, cos, sin):
    p = ROPE_PAIRS
    parts = []
    for g in range(x.shape[0] // (2 * p)):
        parts += [x[(2 * g + 1) * p:(2 * g + 2) * p], x[2 * g * p:(2 * g + 1) * p]]
    return x * cos + jnp.concatenate(parts, axis=0) * sin


def _project_rows(x, gain, shift, scale, wt_ref, wa, ba, row0, n_rows, sc, rope_refs, kt_store, dv_store):
    rows = pl.ds(row0, n_rows)
    ms = jnp.mean(x * x, axis=-1, keepdims=True)
    h = x * lax.rsqrt(ms + EPS) * gain
    hb = _bf(h * (1.0 + scale) + shift)

    def sec(a, b):
        return _dot_nt(hb, wt_ref[a:b, :])

    sc["gq"][rows, :] = sec(_R_GQ, _R_GK) * (GLA_DK ** -0.5)
    sc["gk"][rows, :] = sec(_R_GK, _R_GV)
    sc["gv"][rows, :] = _bf(sec(_R_GV, _R_LR))
    sc["gate"][rows, :] = _silu(sec(_R_GATE, _R_END))
    z = _dot(_bf(sec(_R_LR, _R_DQ)), wa) + ba
    sc["g"][rows, :] = _log_sigmoid(z) * (1.0 / GLA_GATE_TEMP)
    dq = sec(_R_DQ, _R_DK) * (DIFF_DH ** -0.5 * LOG2E)
    dkt = _dot_nt(wt_ref[_R_DK:_R_DV, :], hb)
    dv = sec(_R_DV, _R_GATE)
    if rope_refs is not None:
        cos_ref, sin_ref, cost_ref, sint_ref = rope_refs
        cos, sin = cos_ref[rows, :], sin_ref[rows, :]
        cost, sint = cost_ref[:, rows], sint_ref[:, rows]
        for hd in range(DIFF_HEADS):
            sl = slice(hd * LANES, (hd + 1) * LANES)
            sc["dq"][rows, sl] = _bf(_rope_lanes(dq[:, sl], cos, sin))
            kt_store(sl, _rope_rows(dkt[sl, :], cost, sint))
    else:
        sc["dq"][rows, :] = _bf(dq)
        kt_store(slice(0, DIFF_QK), dkt)
    dv_store(dv)


def _tile4(x):
    return jnp.concatenate([x, x, x, x], axis=0)


def _gla_fast(q_ref, k_ref, v_ref, g_ref, o_ref, s_ref, qi_ref, kl_ref, et_ref, *, seq_len):
    c_len, sc_len = GLA_CHUNK, GLA_SUPER
    nc = seq_len // c_len
    cps = sc_len // c_len
    r = _iota((sc_len, 2 * sc_len), 0)
    c = _iota((sc_len, 2 * sc_len), 1) & (sc_len - 1)
    same = (r // c_len) == (c // c_len)
    tri2 = [(same & (c <= r)).astype(BF16), (same & (c >= r)).astype(BF16)]
    rt = _iota((c_len, GLA_HEADS * c_len), 0)
    cs = _iota((c_len, GLA_HEADS * c_len), 1) & (c_len - 1)
    keep = [cs <= rt, cs >= rt]
    k_mask = (_iota((GLA_HEADS * c_len, GLA_QK), 0) // c_len) == (_iota((GLA_HEADS * c_len, GLA_QK), 1) // GLA_DK)
    v_mask = (_iota((GLA_HEADS * c_len, GLA_WIDTH), 0) // c_len) == (_iota((GLA_HEADS * c_len, GLA_WIDTH), 1) // GLA_DV)
    p_mask = (_iota((GLA_PAIR_K, GLA_PAIR_V), 0) // GLA_DK) == (_iota((GLA_PAIR_K, GLA_PAIR_V), 1) // GLA_DV)
    zb = jnp.zeros((), BF16)

    for sci in range(seq_len // sc_len):
        rows = slice(sci * sc_len, (sci + 1) * sc_len)
        q = q_ref[rows, :]
        k = k_ref[rows, :]
        vb = v_ref[rows, :]
        bs = []
        for d in range(2):
            g = g_ref[rows, d * GLA_QK:(d + 1) * GLA_QK]
            hi = _bf(g)
            lo = _bf(g - hi.astype(F32))
            bs.append(_dot(tri2[d], jnp.concatenate([hi, lo], axis=0)))
        qib, kob = [], []
        for d in range(2):
            b = bs[d]
            qi = _bf(q * jnp.exp(b))
            ko = k * jnp.exp(-b)
            kls = []
            for j in range(cps):
                last = j * c_len + (0 if d else c_len - 1)
                e_tot = jnp.exp(b[last:last + 1, :])
                et_ref[d, sci * cps + j] = jnp.broadcast_to(e_tot, (SUBLANES, GLA_QK))
                kls.append(_bf(ko[j * c_len:(j + 1) * c_len] * e_tot))
            qib.append(qi)
            kob.append(_bf(ko))
            qi_ref[d, rows, :] = qi
            kl_ref[d, rows, :] = jnp.concatenate(kls, axis=0)
        atts = [[_dot_nt(qib[d][j * c_len:(j + 1) * c_len],
                         jnp.where(k_mask, _tile4(kob[d][j * c_len:(j + 1) * c_len]), zb))
                 for d in range(2)] for j in range(cps)]
        for j in range(cps):
            a2 = jnp.concatenate([_bf(jnp.where(keep[d], atts[j][d], 0.0)) for d in range(2)], axis=0)
            v_bd = jnp.where(v_mask, _tile4(vb[j * c_len:(j + 1) * c_len]), zb)
            r2 = _dot(a2, v_bd)
            o_ref[sci * sc_len + j * c_len:sci * sc_len + (j + 1) * c_len, :] = r2[:c_len] + r2[c_len:]

    for i in range(nc):
        for d in range(2):
            ci = (nc - 1 - i) if d else i
            rows = slice(ci * c_len, (ci + 1) * c_len)
            qi = qi_ref[d, rows, :]
            kl = kl_ref[d, rows, :]
            vb = v_ref[rows, :]
            e_col = et_ref[d, ci].T[:, 0:1]
            inter = []
            for p in range(2):
                ks = slice(p * GLA_PAIR_K, (p + 1) * GLA_PAIR_K)
                s = s_ref[d, p]
                inter.append(_dot(qi[:, ks], _bf(s)))
                upd = _dot_tn(kl[:, ks], vb[:, p * GLA_PAIR_V:(p + 1) * GLA_PAIR_V])
                s_ref[d, p] = s * e_col[ks] + jnp.where(p_mask, upd, 0.0)
            o_ref[rows, :] += jnp.concatenate(inter, axis=1)


def _gla_slow(q_ref, k_ref, v_ref, g_ref, o_ref, s_ref, b_ref, vf_ref, *, seq_len):
    c_len = GLA_CHUNK
    nc = seq_len // c_len
    r64 = _iota((c_len, c_len), 0)
    c64 = _iota((c_len, c_len), 1)
    p_mask = (_iota((GLA_PAIR_K, GLA_PAIR_V), 0) // GLA_DK) == (_iota((GLA_PAIR_K, GLA_PAIR_V), 1) // GLA_DV)
    expand = ((_iota((GLA_QK, GLA_WIDTH), 0) // GLA_DK) == (_iota((GLA_QK, GLA_WIDTH), 1) // GLA_DV)).astype(BF16)
    t_idx = _iota((c_len, GLA_QK), 0)
    for d in range(2):
        reverse = bool(d)
        tri = ((c64 >= r64) if reverse else (c64 <= r64)).astype(F32)

        def step(i, carry, d=d, reverse=reverse, tri=tri):
            ci = (nc - 1 - i) if reverse else i
            base = pl.multiple_of(ci * c_len, c_len)
            rows = pl.ds(base, c_len)
            q = q_ref[rows, :]
            k = k_ref[rows, :]
            vb = v_ref[rows, :]
            g = g_ref[rows, d * GLA_QK:(d + 1) * GLA_QK]
            b = jnp.dot(tri, g, precision=lax.Precision.HIGHEST, preferred_element_type=F32)
            b_tot = jnp.sum(g, axis=0, keepdims=True)
            b_ref[...] = b
            vf_ref[...] = vb.astype(F32)

            def key_row(s, acc):
                b_s = b_ref[pl.ds(s, 1), :]
                k_s = k_ref[pl.ds(base + s, 1), :]
                v_s = vf_ref[pl.ds(s, 1), :]
                visible = (t_idx <= s) if reverse else (t_idx >= s)
                w = jnp.where(visible, q * jnp.exp(jnp.minimum(b - b_s, 0.0)), 0.0) * k_s
                return acc + _dot(_bf(w), expand) * v_s

            o = lax.fori_loop(0, c_len, key_row, jnp.zeros((c_len, GLA_WIDTH), F32))
            qi = _bf(q * jnp.exp(b))
            kl = _bf(k * jnp.exp(b_tot - b))
            e_col = jnp.broadcast_to(jnp.exp(b_tot), (SUBLANES, GLA_QK)).T[:, 0:1]
            inter = []
            for p in range(2):
                ks = slice(p * GLA_PAIR_K, (p + 1) * GLA_PAIR_K)
                s = s_ref[d, p]
                inter.append(_dot(qi[:, ks], _bf(s)))
                upd = _dot_tn(kl[:, ks], vb[:, p * GLA_PAIR_V:(p + 1) * GLA_PAIR_V])
                s_ref[d, p] = s * e_col[ks] + jnp.where(p_mask, upd, 0.0)
            o = o + jnp.concatenate(inter, axis=1)
            if reverse:
                o_ref[rows, :] += o
            else:
                o_ref[rows, :] = o
            return carry

        lax.fori_loop(0, nc, step, 0)


def _gla_sequence(q_ref, k_ref, v_ref, g_ref, o_ref, gs, init_states, state_out, *, seq_len):
    s_ref = gs["s"]
    zero = jnp.zeros((GLA_DK, GLA_DV), F32)
    for d in range(2):
        for p in range(2):
            if init_states is not None:
                s0 = init_states[d]
                top = jnp.concatenate([s0[2 * p], zero], axis=1)
                bot = jnp.concatenate([zero, s0[2 * p + 1]], axis=1)
                s_ref[d, p] = jnp.concatenate([top, bot], axis=0)
            else:
                s_ref[d, p] = jnp.zeros((GLA_PAIR_K, GLA_PAIR_V), F32)
    nc = seq_len // GLA_CHUNK
    worst = jnp.zeros((1, 2 * GLA_QK), F32)
    for ci in range(nc):
        worst = jnp.minimum(worst, jnp.sum(g_ref[ci * GLA_CHUNK:(ci + 1) * GLA_CHUNK, :], axis=0, keepdims=True))
    lax.cond(jnp.min(worst) >= GLA_SAFE_LOG_DECAY,
             functools.partial(_gla_fast, q_ref, k_ref, v_ref, g_ref, o_ref, s_ref, gs["qi"], gs["kl"], gs["et"],
                               seq_len=seq_len),
             functools.partial(_gla_slow, q_ref, k_ref, v_ref, g_ref, o_ref, s_ref, gs["b"], gs["vf"],
                               seq_len=seq_len))
    if state_out is not None:
        for d in range(2):
            for h in range(GLA_HEADS):
                p, j = divmod(h, 2)
                state_out[d][h] = s_ref[d, p, j * GLA_DK:(j + 1) * GLA_DK, j * GLA_DV:(j + 1) * GLA_DV]


def _diff_blocks(tasks, lam):
    kv_key, kv = None, None
    for g0 in range(0, len(tasks), DIFF_GROUP):
        group = []
        for key, load_q, load_kv, store in tasks[g0:g0 + DIFF_GROUP]:
            if key != kv_key:
                kv_key, kv = key, load_kv()
            group.append((load_q(), kv[0], kv[1], store))
        scores = []
        for q, kts, _, _ in group:
            first = _iota(q.shape, 1) < DIFF_DH
            zb = jnp.zeros((), BF16)
            qs = jnp.concatenate([jnp.where(first, q, zb), jnp.where(first, zb, q)], axis=0)
            scores.append([_dot(qs, kt) for kt in kts])
        maxes = [functools.reduce(jnp.maximum, [jnp.max(s, axis=-1, keepdims=True) for s in ss]) for ss in scores]
        es = [[_bf(jnp.exp2(s - m)) for s in ss] for ss, m in zip(scores, maxes)]
        for (q, _, vas, store), e_list in zip(group, es):
            tq = q.shape[0]
            r = functools.reduce(jnp.add, [_dot(e, va) for e, va in zip(e_list, vas)])
            n = r[:, :DIFF_DV] / r[:, DIFF_DV:]
            store(n[:tq] - lam * n[tq:])


def _diff_lambda(lam_ref, lam_init):
    lp = lam_ref[...]
    return (jnp.exp(jnp.sum(lp[0:1] * lp[1:2], axis=-1, keepdims=True))
            - jnp.exp(jnp.sum(lp[2:3] * lp[3:4], axis=-1, keepdims=True)) + lam_init)


def _with_ones(v):
    return jnp.concatenate([v, jnp.ones(v.shape, v.dtype)], axis=1)


def _head_rmsnorm(x, gain):
    ms = jnp.mean(x * x, axis=-1, keepdims=True)
    return x * lax.rsqrt(ms + EPS) * gain


def _merge_rows(og_ref, od_ref, sg_ref, x, gt, gg, dg, wo_ref, fg, rows):
    slabs = []
    for h in range(GLA_HEADS):
        slabs.append(_head_rmsnorm(og_ref[rows, h * GLA_DV:(h + 1) * GLA_DV], gg))
    for h in range(DIFF_HEADS):
        slabs.append(_head_rmsnorm(od_ref[rows, h * DIFF_DV:(h + 1) * DIFF_DV], dg))
    o = jnp.concatenate(slabs, axis=-1) * sg_ref[rows, :]
    xn = x + gt * _dot(_bf(o), wo_ref[...])
    ms = jnp.mean(xn * xn, axis=-1, keepdims=True)
    return xn * lax.rsqrt(ms + EPS) * fg


def _layer_kernel(*refs, seq_len, n_seq, rope, has_cache, lam_init):
    it = iter(refs)
    x_ref, gain_ref, mod_ref, wt_ref, wo_ref, wa_ref, ba_ref, lam_ref, gg_ref, dg_ref, fg_ref = (next(it) for _ in range(11))
    rope_refs = tuple(next(it) for _ in range(4)) if rope else None
    if has_cache:
        ckt_ref, cv_ref, s0f_ref, s0b_ref = (next(it) for _ in range(4))
    y_ref = next(it)
    if not has_cache:
        kt_out, dv_out, sf_out, sb_out = (next(it) for _ in range(4))
    names = ["gq", "gk", "gv", "g", "dq", "gate", "og", "od"] + (["dkt", "dv"] if has_cache else [])
    sc = {n: next(it) for n in names}
    gs = {n: next(it) for n in ["s", "qi", "kl", "et", "b", "vf"]}

    d = D_MODEL
    n_tok = seq_len * n_seq
    mod_row = (1 + pl.program_id(0)) if has_cache else 0
    shift = mod_ref[pl.ds(mod_row, 1), 0:d]
    scale = mod_ref[pl.ds(mod_row, 1), d:2 * d]
    gt = mod_ref[pl.ds(mod_row, 1), 2 * d:3 * d]
    gain = gain_ref[...]

    zero = jnp.zeros((GLA_GATE_RANK, GLA_QK), F32)
    wa = _bf(jnp.concatenate([jnp.concatenate([wa_ref[0], zero], axis=1),
                              jnp.concatenate([zero, wa_ref[1]], axis=1)], axis=0))
    ba = jnp.concatenate([ba_ref[0:1, :], ba_ref[1:2, :]], axis=1)

    for t0 in range(0, n_tok, ROW_TILE):
        if has_cache:
            def kt_store(sl, val, t0=t0):
                sc["dkt"][sl, t0:t0 + ROW_TILE] = _bf(val)

            def dv_store(val, t0=t0):
                sc["dv"][t0:t0 + ROW_TILE, :] = _bf(val)
        else:
            def kt_store(sl, val, t0=t0):
                for j in range(ROW_TILE // seq_len):
                    kt_out[(t0 // seq_len) + j, sl, :] = val[:, j * seq_len:(j + 1) * seq_len]

            def dv_store(val, t0=t0):
                dv_out[t0:t0 + ROW_TILE, :] = val

        _project_rows(x_ref[t0:t0 + ROW_TILE, :], gain, shift, scale, wt_ref, wa, ba, t0, ROW_TILE, sc,
                      rope_refs, kt_store, dv_store)

    for b in range(n_seq):
        view = lambda name: _rows(sc[name], b * seq_len, seq_len)
        init = (s0f_ref.at[0], s0b_ref.at[0]) if has_cache else None
        out = None if has_cache else (sf_out.at[b], sb_out.at[b])
        _gla_sequence(view("gq"), view("gk"), view("gv"), view("g"), view("og"), gs, init, out, seq_len=seq_len)

    lam = _diff_lambda(lam_ref, lam_init)
    tasks = []
    for b in range(n_seq):
        for hd in range(DIFF_HEADS):
            sl = slice(hd * LANES, (hd + 1) * LANES)

            def load_kv(b=b, sl=sl):
                if has_cache:
                    return ([_bf(ckt_ref[0, sl, :]), sc["dkt"][sl, :]],
                            [_with_ones(_bf(cv_ref[0, :, sl])), _with_ones(sc["dv"][:, sl])])
                return ([_bf(kt_out[b, sl, :])],
                        [_with_ones(_bf(dv_out[b * seq_len:(b + 1) * seq_len, sl]))])

            for q0 in range(0, seq_len, DIFF_QB):
                r0 = b * seq_len + q0

                def load_q(r0=r0, sl=sl):
                    return sc["dq"][r0:r0 + DIFF_QB, sl]

                def store(val, r0=r0, sl=sl):
                    sc["od"][r0:r0 + DIFF_QB, sl] = val

                tasks.append(((b, hd), load_q, load_kv, store))
    _diff_blocks(tasks, lam)

    gg = gg_ref[...]
    dg = dg_ref[...] * (1.0 - lam_init)
    fg = fg_ref[...]
    for t0 in range(0, n_tok, ROW_TILE):
        rows = slice(t0, t0 + ROW_TILE)
        y_ref[rows, :] = _merge_rows(sc["og"], sc["od"], sc["gate"], x_ref[rows, :], gt, gg, dg, wo_ref, fg, rows)


def _layer(x2d, seq_len, n_seq, gain, mod, wt, wo, wa, ba, lam_p, gg, dg, fg, lam_init, rope_tabs=None, cache=None):
    t = x2d.shape[0]
    n_tok = seq_len * n_seq
    n_steps = t // n_tok
    rope = rope_tabs is not None
    has_cache = cache is not None
    assert n_tok % ROW_TILE == 0 and ROW_TILE % seq_len in (0, ROW_TILE) and seq_len % GLA_SUPER == 0

    def whole(a, single=False):
        kw = {"pipeline_mode": pl.Buffered(1)} if single else {}
        return pl.BlockSpec(a.shape, lambda i: (0,) * a.ndim, **kw)

    io_kw = {"pipeline_mode": pl.Buffered(1)} if n_steps <= 2 else {}
    args = [x2d, gain, mod, wt, wo, wa, ba, lam_p, gg, dg, fg]
    in_specs = [pl.BlockSpec((n_tok, D_MODEL), lambda i: (i, 0), **io_kw), whole(gain), whole(mod), whole(wt, True),
                whole(wo, True), whole(wa), whole(ba), whole(lam_p), whole(gg), whole(dg), whole(fg)]
    if rope:
        args += list(rope_tabs)
        in_specs += [whole(a, True) for a in rope_tabs]
    sds = jax.ShapeDtypeStruct
    out_shape = [sds((t, D_MODEL), F32)]
    out_specs = [pl.BlockSpec((n_tok, D_MODEL), lambda i: (i, 0), **io_kw)]
    if has_cache:
        ckt, cv, s0f, s0b = cache
        args += [ckt, cv, s0f, s0b]
        st_spec = pl.BlockSpec((1,) + s0f.shape[1:], lambda i: (i, 0, 0, 0))
        in_specs += [pl.BlockSpec((1,) + ckt.shape[1:], lambda i: (i, 0, 0)),
                     pl.BlockSpec((1,) + cv.shape[1:], lambda i: (i, 0, 0)), st_spec, st_spec]
    else:
        n_b = t // seq_len
        out_shape += [sds((n_b, DIFF_QK, seq_len), F32), sds((t, DIFF_WIDTH), F32),
                      sds((n_b, GLA_HEADS, GLA_DK, GLA_DV), F32), sds((n_b, GLA_HEADS, GLA_DK, GLA_DV), F32)]
        st_spec = pl.BlockSpec((n_seq, GLA_HEADS, GLA_DK, GLA_DV), lambda i: (i, 0, 0, 0))
        out_specs += [pl.BlockSpec((n_seq, DIFF_QK, seq_len), lambda i: (i, 0, 0)),
                      pl.BlockSpec((n_tok, DIFF_WIDTH), lambda i: (i, 0)), st_spec, st_spec]
    vm = pltpu.VMEM
    scratch = [vm((n_tok, GLA_QK), F32), vm((n_tok, GLA_QK), F32), vm((n_tok, GLA_WIDTH), BF16),
               vm((n_tok, 2 * GLA_QK), F32), vm((n_tok, DIFF_QK), BF16), vm((n_tok, MIX_WIDTH), F32),
               vm((n_tok, GLA_WIDTH), F32), vm((n_tok, DIFF_WIDTH), F32)]
    if has_cache:
        scratch += [vm((DIFF_QK, n_tok), BF16), vm((n_tok, DIFF_WIDTH), BF16)]
    scratch += [vm((2, 2, GLA_PAIR_K, GLA_PAIR_V), F32),
                vm((2, seq_len, GLA_QK), BF16),
                vm((2, seq_len, GLA_QK), BF16),
                vm((2, seq_len // GLA_CHUNK, SUBLANES, GLA_QK), F32),
                vm((GLA_CHUNK, GLA_QK), F32),
                vm((GLA_CHUNK, GLA_WIDTH), F32)]
    return pl.pallas_call(
        functools.partial(_layer_kernel, seq_len=seq_len, n_seq=n_seq, rope=rope, has_cache=has_cache,
                          lam_init=lam_init),
        grid=(n_steps,),
        in_specs=in_specs,
        out_specs=out_specs,
        out_shape=out_shape,
        scratch_shapes=scratch,
        compiler_params=_params(1),
        name="layer_lat" if has_cache else "layer_ctx",
    )(*args)


def _rope_tables(seq_len):
    t = np.arange(seq_len)
    inv_freq = ROPE_BASE ** (-np.arange(ROPE_PAIRS, dtype=np.float64) / ROPE_PAIRS)
    ang_r = (t // GRID_W)[:, None] * inv_freq[None, :]
    ang_c = (t % GRID_W)[:, None] * inv_freq[None, :]
    cr, sr, cc, sc = np.cos(ang_r), np.sin(ang_r), np.cos(ang_c), np.sin(ang_c)
    cos = np.concatenate([cr, cr, cc, cc] * 2, axis=-1).astype(np.float32)
    sin = np.concatenate([-sr, sr, -sc, sc] * 2, axis=-1).astype(np.float32)
    return (jnp.asarray(cos), jnp.asarray(sin),
            jnp.asarray(np.ascontiguousarray(cos.T)), jnp.asarray(np.ascontiguousarray(sin.T)))


def kernel(x_prompt, x_sample, cache_diff_k, cache_diff_v, state_gla_fwd, state_gla_bwd, c, c_ctx,
           norm_gain, w_mod, b_mod, w_in, w_gla_alpha, b_gla_alpha, diff_lambda,
           gla_head_gain, diff_head_gain, w_out, final_gain):
    bp, lp, d = x_prompt.shape
    bs, ls, _ = x_sample.shape
    depth = norm_gain.shape[0]
    assert depth == 1 and d == D_MODEL and bs + 1 <= MOD_ROWS and w_in.shape[2] == _R_END
    l = 0
    lam_init = 0.8 - 0.6 * math.exp(-0.3 * l)

    cvecs = jnp.concatenate([c_ctx[None, :], c, jnp.zeros((MOD_ROWS - 1 - bs, d), F32)], axis=0)
    mod = _adaln(cvecs, w_mod[l], b_mod)
    wt, wo = _prep(jnp.swapaxes(w_in[l], 0, 1), w_out[l])
    shared = (norm_gain, mod, wt, wo, w_gla_alpha[l], b_gla_alpha[l], diff_lambda[l],
              gla_head_gain, diff_head_gain, final_gain[None, :], lam_init)

    y_p, dkt, dv, s_f, s_b = _layer(x_prompt.reshape(bp * lp, d), lp, ROW_TILE // lp, *shared)
    y_prompt = y_p.reshape(bp, lp, d)
    new_diff_k = jnp.transpose(dkt.reshape(bp, DIFF_HEADS, 2, DIFF_DH, lp), (0, 4, 1, 2, 3))[:, None]
    new_diff_v = dv.reshape(bp, 1, lp, DIFF_HEADS, DIFF_DV)
    new_gla_fwd = s_f[:, None]
    new_gla_bwd = s_b[:, None]

    past = cache_diff_k.shape[2]
    ckt = jnp.transpose(cache_diff_k[:, l], (0, 2, 3, 4, 1)).reshape(bs, DIFF_QK, past)
    cv = cache_diff_v[:, l].reshape(bs, past, DIFF_WIDTH)
    y_s = _layer(x_sample.reshape(bs * ls, d), ls, 1, *shared, rope_tabs=_rope_tables(ls),
                 cache=(ckt, cv, state_gla_fwd[:, l], state_gla_bwd[:, l]))[0]
    y_sample = y_s.reshape(bs, ls, d)

    return (y_prompt, y_sample, new_diff_k, new_diff_v, new_gla_fwd, new_gla_bwd)
```

```python
import functools
import math

import numpy as np
import jax
import jax.numpy as jnp
from jax import lax
from jax.experimental import pallas as pl
from jax.experimental.pallas import tpu as pltpu

F32 = jnp.float32
BF16 = jnp.bfloat16

D_MODEL = 1024
GRID_W = 64
GLA_HEADS = 4
GLA_DK = 64
GLA_DV = 128
GLA_QK = GLA_HEADS * GLA_DK
GLA_WIDTH = GLA_HEADS * GLA_DV
GLA_GATE_RANK = 16
GLA_GATE_TEMP = 16.0
GLA_CHUNK = 64
DIFF_HEADS = 4
DIFF_DH = 64
DIFF_DV = 2 * DIFF_DH
DIFF_QK = DIFF_HEADS * 2 * DIFF_DH
DIFF_WIDTH = DIFF_HEADS * DIFF_DV
MIX_WIDTH = GLA_WIDTH + DIFF_WIDTH
ROPE_PAIRS = DIFF_DH // 4
ROPE_BASE = 10000.0
EPS = 1e-6
LOG2E = math.log2(math.e)

LANES = 128
SUBLANES = 8
MOD_ROWS = 8
ROW_TILE = 512
SETUP_STEPS = 8
DIFF_QB = 256
DIFF_GROUP = 2
DIFF_MXU_SUM_MAX_KEYS = 1024
GLA_SUPER = 256
GLA_PAIR_K = 2 * GLA_DK
GLA_PAIR_V = 2 * GLA_DV
GLA_SAFE_LOG_DECAY = -40.0
VMEM_LIMIT = 58 * 1024 * 1024

_R_GQ = 0
_R_GK = _R_GQ + GLA_QK
_R_GV = _R_GK + GLA_QK
_R_LR = _R_GV + GLA_WIDTH
_R_DQ = _R_LR + 2 * GLA_GATE_RANK
_R_DK = _R_DQ + DIFF_QK
_R_DV = _R_DK + DIFF_QK
_R_GATE = _R_DV + DIFF_WIDTH
_R_END = _R_GATE + MIX_WIDTH


def _bf(x):
    return x.astype(BF16)


def _dot(a, b):
    return jnp.dot(a, b, preferred_element_type=F32)


def _dot_nt(a, b):
    return lax.dot_general(a, b, (((1,), (1,)), ((), ())), preferred_element_type=F32)


def _dot_tn(a, b):
    return lax.dot_general(a, b, (((0,), (0,)), ((), ())), preferred_element_type=F32)


def _params(n_parallel=0, n_arbitrary=0):
    sem = ("parallel",) * n_parallel + ("arbitrary",) * n_arbitrary
    return pltpu.CompilerParams(dimension_semantics=sem, vmem_limit_bytes=VMEM_LIMIT)


def _iota(shape, axis):
    return lax.broadcasted_iota(jnp.int32, shape, axis)


def _rows(ref, start, size):
    return ref.at[pl.ds(start, size)]


def _setup_kernel(c_ref, wm_ref, bm_ref, wi_ref, wo_ref, mod_ref, wib_ref, wob_ref):
    mod_ref[...] = _dot(_bf(_silu(c_ref[...])), _bf(wm_ref[...])) + bm_ref[...]
    wib_ref[...] = _bf(wi_ref[...])
    wob_ref[...] = _bf(wo_ref[...])


def _setup(cvecs, w_mod, b_mod, w_in_t, w_out):
    n_mod = w_mod.shape[1]
    n_in = w_in_t.shape[0]
    tm = n_mod // SETUP_STEPS
    ti = pl.cdiv(n_in, SETUP_STEPS * 2 * SUBLANES) * 2 * SUBLANES
    to = w_out.shape[0] // SETUP_STEPS
    assert tm % LANES == 0 and tm * SETUP_STEPS == n_mod
    rows = lambda r: pl.BlockSpec((r, D_MODEL), lambda i: (i, 0))
    cols = lambda r: pl.BlockSpec((r, tm), lambda i: (0, i))
    return pl.pallas_call(
        _setup_kernel,
        grid=(SETUP_STEPS,),
        in_specs=[pl.BlockSpec((MOD_ROWS, D_MODEL), lambda i: (0, 0)), cols(D_MODEL), cols(1), rows(ti), rows(to)],
        out_specs=[cols(MOD_ROWS), rows(ti), rows(to)],
        out_shape=[jax.ShapeDtypeStruct((MOD_ROWS, n_mod), F32), jax.ShapeDtypeStruct(w_in_t.shape, BF16),
                   jax.ShapeDtypeStruct(w_out.shape, BF16)],
        compiler_params=_params(1),
        name="setup",
    )(cvecs, w_mod, b_mod, w_in_t, w_out)


def _log_sigmoid(z):
    return jnp.minimum(z, 0.0) - jnp.log1p(jnp.exp(-jnp.abs(z)))


def _silu(x):
    h = 0.5 * x
    return h + h * jnp.tanh(h)


def _rope_lanes(x, cos, sin):
    lane = _iota(x.shape, 1)
    up = pltpu.roll(x, ROPE_PAIRS, axis=1)
    dn = pltpu.roll(x, LANES - ROPE_PAIRS, axis=1)
    partner = jnp.where((lane & ROPE_PAIRS) == 0, dn, up)
    return x * cos + partner * sin


def _rope_rows(x, cos, sin):
    p = ROPE_PAIRS
    parts = []
    for g in range(x.shape[0] // (2 * p)):
        parts += [x[(2 * g + 1) * p:(2 * g + 2) * p], x[2 * g * p:(2 * g + 1) * p]]
    return x * cos + jnp.concatenate(parts, axis=0) * sin


def _project_rows(x, gain, shift, scale, wt_ref, wa, ba, row0, n_rows, sc, rope_refs, kt_store, dv_store):
    rows = pl.ds(row0, n_rows)
    ms = jnp.mean(x * x, axis=-1, keepdims=True)
    h = x * lax.rsqrt(ms + EPS) * gain
    hb = _bf(h * (1.0 + scale) + shift)

    def sec(a, b):
        return _dot_nt(hb, wt_ref[a:b, :])

    sc["gq"][rows, :] = sec(_R_GQ, _R_GK) * (GLA_DK ** -0.5)
    sc["gk"][rows, :] = sec(_R_GK, _R_GV)
    sc["gv"][rows, :] = _bf(sec(_R_GV, _R_LR))
    sc["gate"][rows, :] = _silu(sec(_R_GATE, _R_END))
    z = _dot(_bf(sec(_R_LR, _R_DQ)), wa) + ba
    sc["g"][rows, :] = _log_sigmoid(z) * (1.0 / GLA_GATE_TEMP)
    dq = sec(_R_DQ, _R_DK) * (DIFF_DH ** -0.5 * LOG2E)
    dkt = _dot_nt(wt_ref[_R_DK:_R_DV, :], hb)
    dv = sec(_R_DV, _R_GATE)
    if rope_refs is not None:
        cos_ref, sin_ref, cost_ref, sint_ref = rope_refs
        cos, sin = cos_ref[rows, :], sin_ref[rows, :]
        cost, sint = cost_ref[:, rows], sint_ref[:, rows]
        for hd in range(DIFF_HEADS):
            sl = slice(hd * LANES, (hd + 1) * LANES)
            sc["dq"][rows, sl] = _bf(_rope_lanes(dq[:, sl], cos, sin))
            kt_store(sl, _rope_rows(dkt[sl, :], cost, sint))
    else:
        sc["dq"][rows, :] = _bf(dq)
        kt_store(slice(0, DIFF_QK), dkt)
    dv_store(dv)


def _tile4(x):
    return jnp.concatenate([x, x, x, x], axis=0)


def _gla_fast(q_ref, k_ref, v_ref, g_ref, o_ref, s_ref, qi_ref, kl_ref, et_ref, *, seq_len):
    c_len, sc_len = GLA_CHUNK, GLA_SUPER
    nc = seq_len // c_len
    cps = sc_len // c_len
    r = _iota((sc_len, 2 * sc_len), 0)
    c = _iota((sc_len, 2 * sc_len), 1) & (sc_len - 1)
    same = (r // c_len) == (c // c_len)
    tri2 = [(same & (c <= r)).astype(BF16), (same & (c >= r)).astype(BF16)]
    rt = _iota((c_len, GLA_HEADS * c_len), 0)
    cs = _iota((c_len, GLA_HEADS * c_len), 1) & (c_len - 1)
    keep = [cs <= rt, cs >= rt]
    k_mask = (_iota((GLA_HEADS * c_len, GLA_QK), 0) // c_len) == (_iota((GLA_HEADS * c_len, GLA_QK), 1) // GLA_DK)
    v_mask = (_iota((GLA_HEADS * c_len, GLA_WIDTH), 0) // c_len) == (_iota((GLA_HEADS * c_len, GLA_WIDTH), 1) // GLA_DV)
    p_mask = (_iota((GLA_PAIR_K, GLA_PAIR_V), 0) // GLA_DK) == (_iota((GLA_PAIR_K, GLA_PAIR_V), 1) // GLA_DV)
    zb = jnp.zeros((), BF16)

    for sci in range(seq_len // sc_len):
        rows = slice(sci * sc_len, (sci + 1) * sc_len)
        q = q_ref[rows, :]
        k = k_ref[rows, :]
        vb = v_ref[rows, :]
        bs = []
        for d in range(2):
            g = g_ref[rows, d * GLA_QK:(d + 1) * GLA_QK]
            hi = _bf(g)
            lo = _bf(g - hi.astype(F32))
            bs.append(_dot(tri2[d], jnp.concatenate([hi, lo], axis=0)))
        qib, kob = [], []
        for d in range(2):
            b = bs[d]
            qi = _bf(q * jnp.exp(b))
            ko = k * jnp.exp(-b)
            kls = []
            for j in range(cps):
                last = j * c_len + (0 if d else c_len - 1)
                e_tot = jnp.exp(b[last:last + 1, :])
                et_ref[d, sci * cps + j] = jnp.broadcast_to(e_tot, (SUBLANES, GLA_QK))
                kls.append(_bf(ko[j * c_len:(j + 1) * c_len] * e_tot))
            qib.append(qi)
            kob.append(_bf(ko))
            qi_ref[d, rows, :] = qi
            kl_ref[d, rows, :] = jnp.concatenate(kls, axis=0)
        atts = [[_dot_nt(qib[d][j * c_len:(j + 1) * c_len],
                         jnp.where(k_mask, _tile4(kob[d][j * c_len:(j + 1) * c_len]), zb))
                 for d in range(2)] for j in range(cps)]
        for j in range(cps):
            a2 = jnp.concatenate([_bf(jnp.where(keep[d], atts[j][d], 0.0)) for d in range(2)], axis=0)
            v_bd = jnp.where(v_mask, _tile4(vb[j * c_len:(j + 1) * c_len]), zb)
            r2 = _dot(a2, v_bd)
            o_ref[sci * sc_len + j * c_len:sci * sc_len + (j + 1) * c_len, :] = r2[:c_len] + r2[c_len:]

    for i in range(nc):
        for d in range(2):
            ci = (nc - 1 - i) if d else i
            rows = slice(ci * c_len, (ci + 1) * c_len)
            qi = qi_ref[d, rows, :]
            kl = kl_ref[d, rows, :]
            vb = v_ref[rows, :]
            e_col = et_ref[d, ci].T[:, 0:1]
            inter = []
            for p in range(2):
                ks = slice(p * GLA_PAIR_K, (p + 1) * GLA_PAIR_K)
                s = s_ref[d, p]
                inter.append(_dot(qi[:, ks], _bf(s)))
                upd = _dot_tn(kl[:, ks], vb[:, p * GLA_PAIR_V:(p + 1) * GLA_PAIR_V])
                s_ref[d, p] = s * e_col[ks] + jnp.where(p_mask, upd, 0.0)
            o_ref[rows, :] += jnp.concatenate(inter, axis=1)


def _gla_slow(q_ref, k_ref, v_ref, g_ref, o_ref, s_ref, b_ref, vf_ref, *, seq_len):
    c_len = GLA_CHUNK
    nc = seq_len // c_len
    r64 = _iota((c_len, c_len), 0)
    c64 = _iota((c_len, c_len), 1)
    p_mask = (_iota((GLA_PAIR_K, GLA_PAIR_V), 0) // GLA_DK) == (_iota((GLA_PAIR_K, GLA_PAIR_V), 1) // GLA_DV)
    expand = ((_iota((GLA_QK, GLA_WIDTH), 0) // GLA_DK) == (_iota((GLA_QK, GLA_WIDTH), 1) // GLA_DV)).astype(BF16)
    t_idx = _iota((c_len, GLA_QK), 0)
    for d in range(2):
        reverse = bool(d)
        tri = ((c64 >= r64) if reverse else (c64 <= r64)).astype(F32)

        def step(i, carry, d=d, reverse=reverse, tri=tri):
            ci = (nc - 1 - i) if reverse else i
            base = pl.multiple_of(ci * c_len, c_len)
            rows = pl.ds(base, c_len)
            q = q_ref[rows, :]
            k = k_ref[rows, :]
            vb = v_ref[rows, :]
            g = g_ref[rows, d * GLA_QK:(d + 1) * GLA_QK]
            b = jnp.dot(tri, g, precision=lax.Precision.HIGHEST, preferred_element_type=F32)
            b_tot = jnp.sum(g, axis=0, keepdims=True)
            b_ref[...] = b
            vf_ref[...] = vb.astype(F32)

            def key_row(s, acc):
                b_s = b_ref[pl.ds(s, 1), :]
                k_s = k_ref[pl.ds(base + s, 1), :]
                v_s = vf_ref[pl.ds(s, 1), :]
                visible = (t_idx <= s) if reverse else (t_idx >= s)
                w = jnp.where(visible, q * jnp.exp(jnp.minimum(b - b_s, 0.0)), 0.0) * k_s
                return acc + _dot(_bf(w), expand) * v_s

            o = lax.fori_loop(0, c_len, key_row, jnp.zeros((c_len, GLA_WIDTH), F32))
            qi = _bf(q * jnp.exp(b))
            kl = _bf(k * jnp.exp(b_tot - b))
            e_col = jnp.broadcast_to(jnp.exp(b_tot), (SUBLANES, GLA_QK)).T[:, 0:1]
            inter = []
            for p in range(2):
                ks = slice(p * GLA_PAIR_K, (p + 1) * GLA_PAIR_K)
                s = s_ref[d, p]
                inter.append(_dot(qi[:, ks], _bf(s)))
                upd = _dot_tn(kl[:, ks], vb[:, p * GLA_PAIR_V:(p + 1) * GLA_PAIR_V])
                s_ref[d, p] = s * e_col[ks] + jnp.where(p_mask, upd, 0.0)
            o = o + jnp.concatenate(inter, axis=1)
            if reverse:
                o_ref[rows, :] += o
            else:
                o_ref[rows, :] = o
            return carry

        lax.fori_loop(0, nc, step, 0)


def _gla_sequence(q_ref, k_ref, v_ref, g_ref, o_ref, gs, init_states, state_out, *, seq_len):
    s_ref = gs["s"]
    zero = jnp.zeros((GLA_DK, GLA_DV), F32)
    for d in range(2):
        for p in range(2):
            if init_states is not None:
                s0 = init_states[d]
                top = jnp.concatenate([s0[2 * p], zero], axis=1)
                bot = jnp.concatenate([zero, s0[2 * p + 1]], axis=1)
                s_ref[d, p] = jnp.concatenate([top, bot], axis=0)
            else:
                s_ref[d, p] = jnp.zeros((GLA_PAIR_K, GLA_PAIR_V), F32)
    nc = seq_len // GLA_CHUNK
    worst = jnp.zeros((1, 2 * GLA_QK), F32)
    for ci in range(nc):
        worst = jnp.minimum(worst, jnp.sum(g_ref[ci * GLA_CHUNK:(ci + 1) * GLA_CHUNK, :], axis=0, keepdims=True))
    lax.cond(jnp.min(worst) >= GLA_SAFE_LOG_DECAY,
             functools.partial(_gla_fast, q_ref, k_ref, v_ref, g_ref, o_ref, s_ref, gs["qi"], gs["kl"], gs["et"],
                               seq_len=seq_len),
             functools.partial(_gla_slow, q_ref, k_ref, v_ref, g_ref, o_ref, s_ref, gs["b"], gs["vf"],
                               seq_len=seq_len))
    if state_out is not None:
        for d in range(2):
            for h in range(GLA_HEADS):
                p, j = divmod(h, 2)
                state_out[d][h] = s_ref[d, p, j * GLA_DK:(j + 1) * GLA_DK, j * GLA_DV:(j + 1) * GLA_DV]


def _diff_blocks(tasks, lam, sums_on_mxu):
    kv_key, kv = None, None
    for g0 in range(0, len(tasks), DIFF_GROUP):
        group = []
        for key, load_q, load_kv, store in tasks[g0:g0 + DIFF_GROUP]:
            if key != kv_key:
                kts, vs = load_kv()
                kv_key, kv = key, (kts, [_with_ones(v) for v in vs] if sums_on_mxu else vs)
            group.append((load_q(), kv[0], kv[1], store))
        scores = []
        for q, kts, _, _ in group:
            first = _iota(q.shape, 1) < DIFF_DH
            zb = jnp.zeros((), BF16)
            qs = jnp.concatenate([jnp.where(first, q, zb), jnp.where(first, zb, q)], axis=0)
            scores.append([_dot(qs, kt) for kt in kts])
        maxes = [functools.reduce(jnp.maximum, [jnp.max(s, axis=-1, keepdims=True) for s in ss]) for ss in scores]
        if sums_on_mxu:
            es = [[_bf(jnp.exp2(s - m)) for s in ss] for ss, m in zip(scores, maxes)]
            for (q, _, vas, store), e_list in zip(group, es):
                tq = q.shape[0]
                r = functools.reduce(jnp.add, [_dot(e, va) for e, va in zip(e_list, vas)])
                n = r[:, :DIFF_DV] / r[:, DIFF_DV:]
                store(n[:tq] - lam * n[tq:])
        else:
            es = [[jnp.exp2(s - m) for s in ss] for ss, m in zip(scores, maxes)]
            for (q, _, vs, store), e_list in zip(group, es):
                tq = q.shape[0]
                inv = 1.0 / functools.reduce(jnp.add, [jnp.sum(e, axis=-1, keepdims=True) for e in e_list])
                w1 = inv[:tq]
                w2 = inv[tq:] * lam
                store(functools.reduce(jnp.add, [_dot(_bf(e[:tq] * w1 - e[tq:] * w2), v)
                                                 for e, v in zip(e_list, vs)]))


def _diff_lambda(lam_ref, lam_init):
    lp = lam_ref[...]
    return (jnp.exp(jnp.sum(lp[0:1] * lp[1:2], axis=-1, keepdims=True))
            - jnp.exp(jnp.sum(lp[2:3] * lp[3:4], axis=-1, keepdims=True)) + lam_init)


def _with_ones(v):
    return jnp.concatenate([v, jnp.ones(v.shape, v.dtype)], axis=1)


def _head_rmsnorm(x, gain):
    ms = jnp.mean(x * x, axis=-1, keepdims=True)
    return x * lax.rsqrt(ms + EPS) * gain


def _merge_rows(og_ref, od_ref, sg_ref, x, gt, gg, dg, wo_ref, fg, rows):
    slabs = []
    for h in range(GLA_HEADS):
        slabs.append(_head_rmsnorm(og_ref[rows, h * GLA_DV:(h + 1) * GLA_DV], gg))
    for h in range(DIFF_HEADS):
        slabs.append(_head_rmsnorm(od_ref[rows, h * DIFF_DV:(h + 1) * DIFF_DV], dg))
    o = jnp.concatenate(slabs, axis=-1) * sg_ref[rows, :]
    xn = x + gt * _dot(_bf(o), wo_ref[...])
    ms = jnp.mean(xn * xn, axis=-1, keepdims=True)
    return xn * lax.rsqrt(ms + EPS) * fg


def _layer_kernel(*refs, seq_len, n_seq, rope, has_cache, lam_init):
    it = iter(refs)
    x_ref, gain_ref, mod_ref, wt_ref, wo_ref, wa_ref, ba_ref, lam_ref, gg_ref, dg_ref, fg_ref = (next(it) for _ in range(11))
    rope_refs = tuple(next(it) for _ in range(4)) if rope else None
    if has_cache:
        ckt_ref, cv_ref, s0f_ref, s0b_ref = (next(it) for _ in range(4))
    y_ref = next(it)
    if not has_cache:
        kt_out, dv_out, sf_out, sb_out = (next(it) for _ in range(4))
    names = ["gq", "gk", "gv", "g", "dq", "dv", "gate", "og", "od"] + (["dkt"] if has_cache else [])
    sc = {n: next(it) for n in names}
    gs = {n: next(it) for n in ["s", "qi", "kl", "et", "b", "vf"]}

    d = D_MODEL
    n_tok = seq_len * n_seq
    mod_row = (1 + pl.program_id(0)) if has_cache else 0
    shift = mod_ref[pl.ds(mod_row, 1), 0:d]
    scale = mod_ref[pl.ds(mod_row, 1), d:2 * d]
    gt = mod_ref[pl.ds(mod_row, 1), 2 * d:3 * d]
    gain = gain_ref[...]

    zero = jnp.zeros((GLA_GATE_RANK, GLA_QK), F32)
    wa = _bf(jnp.concatenate([jnp.concatenate([wa_ref[0], zero], axis=1),
                              jnp.concatenate([zero, wa_ref[1]], axis=1)], axis=0))
    ba = jnp.concatenate([ba_ref[0:1, :], ba_ref[1:2, :]], axis=1)

    for t0 in range(0, n_tok, ROW_TILE):
        if has_cache:
            def kt_store(sl, val, t0=t0):
                sc["dkt"][sl, t0:t0 + ROW_TILE] = _bf(val)

            def dv_store(val, t0=t0):
                sc["dv"][t0:t0 + ROW_TILE, :] = _bf(val)
        else:
            def kt_store(sl, val, t0=t0):
                for j in range(ROW_TILE // seq_len):
                    kt_out[(t0 // seq_len) + j, sl, :] = val[:, j * seq_len:(j + 1) * seq_len]

            def dv_store(val, t0=t0):
                sc["dv"][t0:t0 + ROW_TILE, :] = _bf(val)
                for j in range(ROW_TILE // seq_len):
                    for hd in range(DIFF_HEADS):
                        dv_out[(t0 // seq_len) + j, :, hd, :] = val[j * seq_len:(j + 1) * seq_len,
                                                                   hd * DIFF_DV:(hd + 1) * DIFF_DV]

        _project_rows(x_ref[t0:t0 + ROW_TILE, :], gain, shift, scale, wt_ref, wa, ba, t0, ROW_TILE, sc,
                      rope_refs, kt_store, dv_store)

    for b in range(n_seq):
        view = lambda name: _rows(sc[name], b * seq_len, seq_len)
        init = (s0f_ref.at[0], s0b_ref.at[0]) if has_cache else None
        out = None if has_cache else (sf_out.at[b], sb_out.at[b])
        _gla_sequence(view("gq"), view("gk"), view("gv"), view("g"), view("og"), gs, init, out, seq_len=seq_len)

    lam = _diff_lambda(lam_ref, lam_init)
    tasks = []
    for b in range(n_seq):
        for hd in range(DIFF_HEADS):
            sl = slice(hd * LANES, (hd + 1) * LANES)

            def load_kv(b=b, hd=hd, sl=sl):
                if has_cache:
                    return ([_bf(ckt_ref[0, sl, :]), sc["dkt"][sl, :]], [_bf(cv_ref[0, :, hd, :]), sc["dv"][:, sl]])
                return [_bf(kt_out[b, sl, :])], [sc["dv"][b * seq_len:(b + 1) * seq_len, sl]]

            for q0 in range(0, seq_len, DIFF_QB):
                r0 = b * seq_len + q0

                def load_q(r0=r0, sl=sl):
                    return sc["dq"][r0:r0 + DIFF_QB, sl]

                def store(val, r0=r0, sl=sl):
                    sc["od"][r0:r0 + DIFF_QB, sl] = val

                tasks.append(((b, hd), load_q, load_kv, store))
    n_keys = seq_len + (ckt_ref.shape[2] if has_cache else 0)
    _diff_blocks(tasks, lam, sums_on_mxu=n_keys < DIFF_MXU_SUM_MAX_KEYS)

    gg = gg_ref[...]
    dg = dg_ref[...] * (1.0 - lam_init)
    fg = fg_ref[...]
    for t0 in range(0, n_tok, ROW_TILE):
        rows = slice(t0, t0 + ROW_TILE)
        y_ref[rows, :] = _merge_rows(sc["og"], sc["od"], sc["gate"], x_ref[rows, :], gt, gg, dg, wo_ref, fg, rows)


def _layer(x2d, seq_len, n_seq, gain, mod, wt, wo, wa, ba, lam_p, gg, dg, fg, lam_init, rope_tabs=None, cache=None):
    t = x2d.shape[0]
    n_tok = seq_len * n_seq
    n_steps = t // n_tok
    rope = rope_tabs is not None
    has_cache = cache is not None
    assert n_tok % ROW_TILE == 0 and ROW_TILE % seq_len in (0, ROW_TILE) and seq_len % GLA_SUPER == 0

    def whole(a, single=False):
        kw = {"pipeline_mode": pl.Buffered(1)} if single else {}
        return pl.BlockSpec(a.shape, lambda i: (0,) * a.ndim, **kw)

    io_kw = {"pipeline_mode": pl.Buffered(1)} if n_steps <= 2 else {}
    args = [x2d, gain, mod, wt, wo, wa, ba, lam_p, gg, dg, fg]
    in_specs = [pl.BlockSpec((n_tok, D_MODEL), lambda i: (i, 0), **io_kw), whole(gain), whole(mod), whole(wt, True),
                whole(wo, True), whole(wa), whole(ba), whole(lam_p), whole(gg), whole(dg), whole(fg)]
    if rope:
        args += list(rope_tabs)
        in_specs += [whole(a, True) for a in rope_tabs]
    sds = jax.ShapeDtypeStruct
    out_shape = [sds((t, D_MODEL), F32)]
    out_specs = [pl.BlockSpec((n_tok, D_MODEL), lambda i: (i, 0), **io_kw)]
    if has_cache:
        ckt, cv, s0f, s0b = cache
        args += [ckt, cv, s0f, s0b]
        st_spec = pl.BlockSpec((1,) + s0f.shape[1:], lambda i: (i, 0, 0, 0))
        in_specs += [pl.BlockSpec((1,) + ckt.shape[1:], lambda i: (i, 0, 0)),
                     pl.BlockSpec((1,) + cv.shape[1:], lambda i: (i, 0, 0, 0)), st_spec, st_spec]
    else:
        n_b = t // seq_len
        out_shape += [sds((n_b, DIFF_QK, seq_len), F32), sds((n_b, seq_len, DIFF_HEADS, DIFF_DV), F32),
                      sds((n_b, GLA_HEADS, GLA_DK, GLA_DV), F32), sds((n_b, GLA_HEADS, GLA_DK, GLA_DV), F32)]
        st_spec = pl.BlockSpec((n_seq, GLA_HEADS, GLA_DK, GLA_DV), lambda i: (i, 0, 0, 0))
        out_specs += [pl.BlockSpec((n_seq, DIFF_QK, seq_len), lambda i: (i, 0, 0)),
                      pl.BlockSpec((n_seq, seq_len, DIFF_HEADS, DIFF_DV), lambda i: (i, 0, 0, 0)), st_spec, st_spec]
    vm = pltpu.VMEM
    scratch = [vm((n_tok, GLA_QK), F32), vm((n_tok, GLA_QK), F32), vm((n_tok, GLA_WIDTH), BF16),
               vm((n_tok, 2 * GLA_QK), F32), vm((n_tok, DIFF_QK), BF16), vm((n_tok, DIFF_WIDTH), BF16),
               vm((n_tok, MIX_WIDTH), F32), vm((n_tok, GLA_WIDTH), F32), vm((n_tok, DIFF_WIDTH), F32)]
    if has_cache:
        scratch += [vm((DIFF_QK, n_tok), BF16)]
    scratch += [vm((2, 2, GLA_PAIR_K, GLA_PAIR_V), F32),
                vm((2, seq_len, GLA_QK), BF16),
                vm((2, seq_len, GLA_QK), BF16),
                vm((2, seq_len // GLA_CHUNK, SUBLANES, GLA_QK), F32),
                vm((GLA_CHUNK, GLA_QK), F32),
                vm((GLA_CHUNK, GLA_WIDTH), F32)]
    return pl.pallas_call(
        functools.partial(_layer_kernel, seq_len=seq_len, n_seq=n_seq, rope=rope, has_cache=has_cache,
                          lam_init=lam_init),
        grid=(n_steps,),
        in_specs=in_specs,
        out_specs=out_specs,
        out_shape=out_shape,
        scratch_shapes=scratch,
        compiler_params=_params(1),
        name="layer_lat" if has_cache else "layer_ctx",
    )(*args)


def _rope_tables(seq_len):
    t = np.arange(seq_len)
    inv_freq = ROPE_BASE ** (-np.arange(ROPE_PAIRS, dtype=np.float64) / ROPE_PAIRS)
    ang_r = (t // GRID_W)[:, None] * inv_freq[None, :]
    ang_c = (t % GRID_W)[:, None] * inv_freq[None, :]
    cr, sr, cc, sc = np.cos(ang_r), np.sin(ang_r), np.cos(ang_c), np.sin(ang_c)
    cos = np.concatenate([cr, cr, cc, cc] * 2, axis=-1).astype(np.float32)
    sin = np.concatenate([-sr, sr, -sc, sc] * 2, axis=-1).astype(np.float32)
    return (jnp.asarray(cos), jnp.asarray(sin),
            jnp.asarray(np.ascontiguousarray(cos.T)), jnp.asarray(np.ascontiguousarray(sin.T)))


def kernel(x_prompt, x_sample, cache_diff_k, cache_diff_v, state_gla_fwd, state_gla_bwd, c, c_ctx,
           norm_gain, w_mod, b_mod, w_in, w_gla_alpha, b_gla_alpha, diff_lambda,
           gla_head_gain, diff_head_gain, w_out, final_gain):
    bp, lp, d = x_prompt.shape
    bs, ls, _ = x_sample.shape
    depth = norm_gain.shape[0]
    assert depth == 1 and d == D_MODEL and bs + 1 <= MOD_ROWS and w_in.shape[2] == _R_END
    l = 0
    lam_init = 0.8 - 0.6 * math.exp(-0.3 * l)

    cvecs = jnp.concatenate([c_ctx[None, :], c, jnp.zeros((MOD_ROWS - 1 - bs, d), F32)], axis=0)
    mod, wt, wo = _setup(cvecs, w_mod[l], b_mod, jnp.swapaxes(w_in[l], 0, 1), w_out[l])
    shared = (norm_gain, mod, wt, wo, w_gla_alpha[l], b_gla_alpha[l], diff_lambda[l],
              gla_head_gain, diff_head_gain, final_gain[None, :], lam_init)

    y_p, dkt, dv, s_f, s_b = _layer(x_prompt.reshape(bp * lp, d), lp, ROW_TILE // lp, *shared)
    y_prompt = y_p.reshape(bp, lp, d)
    new_diff_k = jnp.transpose(dkt.reshape(bp, DIFF_HEADS, 2, DIFF_DH, lp), (0, 4, 1, 2, 3))[:, None]
    new_diff_v = dv[:, None]
    new_gla_fwd = s_f[:, None]
    new_gla_bwd = s_b[:, None]

    past = cache_diff_k.shape[2]
    ckt = jnp.transpose(cache_diff_k[:, l], (0, 2, 3, 4, 1)).reshape(bs, DIFF_QK, past)
    cv = cache_diff_v[:, l]
    y_s = _layer(x_sample.reshape(bs * ls, d), ls, 1, *shared, rope_tabs=_rope_tables(ls),
                 cache=(ckt, cv, state_gla_fwd[:, l], state_gla_bwd[:, l]))[0]
    y_sample = y_s.reshape(bs, ls, d)

    return (y_prompt, y_sample, new_diff_k, new_diff_v, new_gla_fwd, new_gla_bwd)
```

```python
import functools
import math

import numpy as np
import jax
import jax.numpy as jnp
from jax import lax
from jax.experimental import pallas as pl
from jax.experimental.pallas import tpu as pltpu

F32 = jnp.float32
BF16 = jnp.bfloat16

D_MODEL = 1024
GRID_W = 64
GLA_HEADS = 4
GLA_DK = 64
GLA_DV = 128
GLA_QK = GLA_HEADS * GLA_DK
GLA_WIDTH = GLA_HEADS * GLA_DV
GLA_GATE_RANK = 16
GLA_GATE_TEMP = 16.0
GLA_CHUNK = 64
DIFF_HEADS = 4
DIFF_DH = 64
DIFF_DV = 2 * DIFF_DH
DIFF_QK = DIFF_HEADS * 2 * DIFF_DH
DIFF_WIDTH = DIFF_HEADS * DIFF_DV
MIX_WIDTH = GLA_WIDTH + DIFF_WIDTH
ROPE_PAIRS = DIFF_DH // 4
ROPE_BASE = 10000.0
EPS = 1e-6
LOG2E = math.log2(math.e)

LANES = 128
SUBLANES = 8
MOD_ROWS = 8
ROW_TILE = 512
SETUP_STEPS = 8
DIFF_QB = 256
DIFF_GROUP = 2
GLA_SUPER = 256
GLA_PAIR_K = 2 * GLA_DK
GLA_PAIR_V = 2 * GLA_DV
GLA_SAFE_LOG_DECAY = -40.0
VMEM_LIMIT = 58 * 1024 * 1024

_R_GQ = 0
_R_GK = _R_GQ + GLA_QK
_R_GV = _R_GK + GLA_QK
_R_LR = _R_GV + GLA_WIDTH
_R_DQ = _R_LR + 2 * GLA_GATE_RANK
_R_DK = _R_DQ + DIFF_QK
_R_DV = _R_DK + DIFF_QK
_R_GATE = _R_DV + DIFF_WIDTH
_R_END = _R_GATE + MIX_WIDTH


def _bf(x):
    return x.astype(BF16)


def _dot(a, b):
    return jnp.dot(a, b, preferred_element_type=F32)


def _dot_nt(a, b):
    return lax.dot_general(a, b, (((1,), (1,)), ((), ())), preferred_element_type=F32)


def _dot_tn(a, b):
    return lax.dot_general(a, b, (((0,), (0,)), ((), ())), preferred_element_type=F32)


def _params(n_parallel=0, n_arbitrary=0):
    sem = ("parallel",) * n_parallel + ("arbitrary",) * n_arbitrary
    return pltpu.CompilerParams(dimension_semantics=sem, vmem_limit_bytes=VMEM_LIMIT)


def _iota(shape, axis):
    return lax.broadcasted_iota(jnp.int32, shape, axis)


def _rows(ref, start, size):
    return ref.at[pl.ds(start, size)]


def _setup_kernel(c_ref, wm_ref, bm_ref, wi_ref, wo_ref, mod_ref, wib_ref, wob_ref):
    mod_ref[...] = _dot(_bf(_silu(c_ref[...])), _bf(wm_ref[...])) + bm_ref[...]
    wib_ref[...] = _bf(wi_ref[...])
    wob_ref[...] = _bf(wo_ref[...])


def _setup(cvecs, w_mod, b_mod, w_in_t, w_out):
    n_mod = w_mod.shape[1]
    n_in = w_in_t.shape[0]
    tm = n_mod // SETUP_STEPS
    ti = pl.cdiv(n_in, SETUP_STEPS * 2 * SUBLANES) * 2 * SUBLANES
    to = w_out.shape[0] // SETUP_STEPS
    assert tm % LANES == 0 and tm * SETUP_STEPS == n_mod
    rows = lambda r: pl.BlockSpec((r, D_MODEL), lambda i: (i, 0))
    cols = lambda r: pl.BlockSpec((r, tm), lambda i: (0, i))
    return pl.pallas_call(
        _setup_kernel,
        grid=(SETUP_STEPS,),
        in_specs=[pl.BlockSpec((MOD_ROWS, D_MODEL), lambda i: (0, 0)), cols(D_MODEL), cols(1), rows(ti), rows(to)],
        out_specs=[cols(MOD_ROWS), rows(ti), rows(to)],
        out_shape=[jax.ShapeDtypeStruct((MOD_ROWS, n_mod), F32), jax.ShapeDtypeStruct(w_in_t.shape, BF16),
                   jax.ShapeDtypeStruct(w_out.shape, BF16)],
        compiler_params=_params(1),
        name="setup",
    )(cvecs, w_mod, b_mod, w_in_t, w_out)


def _log_sigmoid(z):
    return jnp.minimum(z, 0.0) - jnp.log1p(jnp.exp(-jnp.abs(z)))


def _silu(x):
    h = 0.5 * x
    return h + h * jnp.tanh(h)


def _rope_lanes(x, cos, sin):
    lane = _iota(x.shape, 1)
    up = pltpu.roll(x, ROPE_PAIRS, axis=1)
    dn = pltpu.roll(x, LANES - ROPE_PAIRS, axis=1)
    partner = jnp.where((lane & ROPE_PAIRS) == 0, dn, up)
    return x * cos + partner * sin


def _rope_rows(x, cos, sin):
    p = ROPE_PAIRS
    parts = []
    for g in range(x.shape[0] // (2 * p)):
        parts += [x[(2 * g + 1) * p:(2 * g + 2) * p], x[2 * g * p:(2 * g + 1) * p]]
    return x * cos + jnp.concatenate(parts, axis=0) * sin


def _project_rows(x, gain, shift, scale, wt_ref, wa, ba, row0, n_rows, sc, rope_refs, kt_store, dv_store):
    rows = pl.ds(row0, n_rows)
    ms = jnp.mean(x * x, axis=-1, keepdims=True)
    h = x * lax.rsqrt(ms + EPS) * gain
    hb = _bf(h * (1.0 + scale) + shift)

    def sec(a, b):
        return _dot_nt(hb, wt_ref[a:b, :])

    sc["gq"][rows, :] = sec(_R_GQ, _R_GK) * (GLA_DK ** -0.5)
    sc["gk"][rows, :] = sec(_R_GK, _R_GV)
    sc["gv"][rows, :] = _bf(sec(_R_GV, _R_LR))
    sc["gate"][rows, :] = _silu(sec(_R_GATE, _R_END))
    z = _dot(_bf(sec(_R_LR, _R_DQ)), wa) + ba
    sc["g"][rows, :] = _log_sigmoid(z) * (1.0 / GLA_GATE_TEMP)
    dq = sec(_R_DQ, _R_DK) * (DIFF_DH ** -0.5 * LOG2E)
    dkt = _dot_nt(wt_ref[_R_DK:_R_DV, :], hb)
    dv = sec(_R_DV, _R_GATE)
    if rope_refs is not None:
        cos_ref, sin_ref, cost_ref, sint_ref = rope_refs
        cos, sin = cos_ref[rows, :], sin_ref[rows, :]
        cost, sint = cost_ref[:, rows], sint_ref[:, rows]
        for hd in range(DIFF_HEADS):
            sl = slice(hd * LANES, (hd + 1) * LANES)
            sc["dq"][rows, sl] = _bf(_rope_lanes(dq[:, sl], cos, sin))
            kt_store(sl, _rope_rows(dkt[sl, :], cost, sint))
    else:
        sc["dq"][rows, :] = _bf(dq)
        kt_store(slice(0, DIFF_QK), dkt)
    dv_store(dv)


def _tile4(x):
    return jnp.concatenate([x, x, x, x], axis=0)


def _gla_fast(q_ref, k_ref, v_ref, g_ref, o_ref, s_ref, qi_ref, kl_ref, et_ref, *, seq_len):
    c_len, sc_len = GLA_CHUNK, GLA_SUPER
    nc = seq_len // c_len
    cps = sc_len // c_len
    r = _iota((sc_len, 2 * sc_len), 0)
    c = _iota((sc_len, 2 * sc_len), 1) & (sc_len - 1)
    same = (r // c_len) == (c // c_len)
    tri2 = [(same & (c <= r)).astype(BF16), (same & (c >= r)).astype(BF16)]
    rt = _iota((c_len, GLA_HEADS * c_len), 0)
    cs = _iota((c_len, GLA_HEADS * c_len), 1) & (c_len - 1)
    keep = [cs <= rt, cs >= rt]
    k_mask = (_iota((GLA_HEADS * c_len, GLA_QK), 0) // c_len) == (_iota((GLA_HEADS * c_len, GLA_QK), 1) // GLA_DK)
    v_mask = (_iota((GLA_HEADS * c_len, GLA_WIDTH), 0) // c_len) == (_iota((GLA_HEADS * c_len, GLA_WIDTH), 1) // GLA_DV)
    p_mask = (_iota((GLA_PAIR_K, GLA_PAIR_V), 0) // GLA_DK) == (_iota((GLA_PAIR_K, GLA_PAIR_V), 1) // GLA_DV)
    zb = jnp.zeros((), BF16)

    for sci in range(seq_len // sc_len):
        rows = slice(sci * sc_len, (sci + 1) * sc_len)
        q = q_ref[rows, :]
        k = k_ref[rows, :]
        vb = v_ref[rows, :]
        bs = []
        for d in range(2):
            g = g_ref[rows, d * GLA_QK:(d + 1) * GLA_QK]
            hi = _bf(g)
            lo = _bf(g - hi.astype(F32))
            bs.append(_dot(tri2[d], jnp.concatenate([hi, lo], axis=0)))
        qib, kob = [], []
        for d in range(2):
            b = bs[d]
            qi = _bf(q * jnp.exp(b))
            ko = k * jnp.exp(-b)
            kls = []
            for j in range(cps):
                last = j * c_len + (0 if d else c_len - 1)
                e_tot = jnp.exp(b[last:last + 1, :])
                et_ref[d, sci * cps + j] = jnp.broadcast_to(e_tot, (SUBLANES, GLA_QK))
                kls.append(_bf(ko[j * c_len:(j + 1) * c_len] * e_tot))
            qib.append(qi)
            kob.append(_bf(ko))
            qi_ref[d, rows, :] = qi
            kl_ref[d, rows, :] = jnp.concatenate(kls, axis=0)
        atts = [[_dot_nt(qib[d][j * c_len:(j + 1) * c_len],
                         jnp.where(k_mask, _tile4(kob[d][j * c_len:(j + 1) * c_len]), zb))
                 for d in range(2)] for j in range(cps)]
        for j in range(cps):
            a2 = jnp.concatenate([_bf(jnp.where(keep[d], atts[j][d], 0.0)) for d in range(2)], axis=0)
            v_bd = jnp.where(v_mask, _tile4(vb[j * c_len:(j + 1) * c_len]), zb)
            r2 = _dot(a2, v_bd)
            o_ref[sci * sc_len + j * c_len:sci * sc_len + (j + 1) * c_len, :] = r2[:c_len] + r2[c_len:]

    for i in range(nc):
        for d in range(2):
            ci = (nc - 1 - i) if d else i
            rows = slice(ci * c_len, (ci + 1) * c_len)
            qi = qi_ref[d, rows, :]
            kl = kl_ref[d, rows, :]
            vb = v_ref[rows, :]
            e_col = et_ref[d, ci].T[:, 0:1]
            inter = []
            for p in range(2):
                ks = slice(p * GLA_PAIR_K, (p + 1) * GLA_PAIR_K)
                s = s_ref[d, p]
                inter.append(_dot(qi[:, ks], _bf(s)))
                upd = _dot_tn(kl[:, ks], vb[:, p * GLA_PAIR_V:(p + 1) * GLA_PAIR_V])
                s_ref[d, p] = s * e_col[ks] + jnp.where(p_mask, upd, 0.0)
            o_ref[rows, :] += jnp.concatenate(inter, axis=1)


def _gla_slow(q_ref, k_ref, v_ref, g_ref, o_ref, s_ref, b_ref, vf_ref, *, seq_len):
    c_len = GLA_CHUNK
    nc = seq_len // c_len
    r64 = _iota((c_len, c_len), 0)
    c64 = _iota((c_len, c_len), 1)
    p_mask = (_iota((GLA_PAIR_K, GLA_PAIR_V), 0) // GLA_DK) == (_iota((GLA_PAIR_K, GLA_PAIR_V), 1) // GLA_DV)
    expand = ((_iota((GLA_QK, GLA_WIDTH), 0) // GLA_DK) == (_iota((GLA_QK, GLA_WIDTH), 1) // GLA_DV)).astype(BF16)
    t_idx = _iota((c_len, GLA_QK), 0)
    for d in range(2):
        reverse = bool(d)
        tri = ((c64 >= r64) if reverse else (c64 <= r64)).astype(F32)

        def step(i, carry, d=d, reverse=reverse, tri=tri):
            ci = (nc - 1 - i) if reverse else i
            base = pl.multiple_of(ci * c_len, c_len)
            rows = pl.ds(base, c_len)
            q = q_ref[rows, :]
            k = k_ref[rows, :]
            vb = v_ref[rows, :]
            g = g_ref[rows, d * GLA_QK:(d + 1) * GLA_QK]
            b = jnp.dot(tri, g, precision=lax.Precision.HIGHEST, preferred_element_type=F32)
            b_tot = jnp.sum(g, axis=0, keepdims=True)
            b_ref[...] = b
            vf_ref[...] = vb.astype(F32)

            def key_row(s, acc):
                b_s = b_ref[pl.ds(s, 1), :]
                k_s = k_ref[pl.ds(base + s, 1), :]
                v_s = vf_ref[pl.ds(s, 1), :]
                visible = (t_idx <= s) if reverse else (t_idx >= s)
                w = jnp.where(visible, q * jnp.exp(jnp.minimum(b - b_s, 0.0)), 0.0) * k_s
                return acc + _dot(_bf(w), expand) * v_s

            o = lax.fori_loop(0, c_len, key_row, jnp.zeros((c_len, GLA_WIDTH), F32))
            qi = _bf(q * jnp.exp(b))
            kl = _bf(k * jnp.exp(b_tot - b))
            e_col = jnp.broadcast_to(jnp.exp(b_tot), (SUBLANES, GLA_QK)).T[:, 0:1]
            inter = []
            for p in range(2):
                ks = slice(p * GLA_PAIR_K, (p + 1) * GLA_PAIR_K)
                s = s_ref[d, p]
                inter.append(_dot(qi[:, ks], _bf(s)))
                upd = _dot_tn(kl[:, ks], vb[:, p * GLA_PAIR_V:(p + 1) * GLA_PAIR_V])
                s_ref[d, p] = s * e_col[ks] + jnp.where(p_mask, upd, 0.0)
            o = o + jnp.concatenate(inter, axis=1)
            if reverse:
                o_ref[rows, :] += o
            else:
                o_ref[rows, :] = o
            return carry

        lax.fori_loop(0, nc, step, 0)


def _gla_sequence(q_ref, k_ref, v_ref, g_ref, o_ref, gs, init_states, state_out, *, seq_len):
    s_ref = gs["s"]
    zero = jnp.zeros((GLA_DK, GLA_DV), F32)
    for d in range(2):
        for p in range(2):
            if init_states is not None:
                s0 = init_states[d]
                top = jnp.concatenate([s0[2 * p], zero], axis=1)
                bot = jnp.concatenate([zero, s0[2 * p + 1]], axis=1)
                s_ref[d, p] = jnp.concatenate([top, bot], axis=0)
            else:
                s_ref[d, p] = jnp.zeros((GLA_PAIR_K, GLA_PAIR_V), F32)
    nc = seq_len // GLA_CHUNK
    worst = jnp.zeros((1, 2 * GLA_QK), F32)
    for ci in range(nc):
        worst = jnp.minimum(worst, jnp.sum(g_ref[ci * GLA_CHUNK:(ci + 1) * GLA_CHUNK, :], axis=0, keepdims=True))
    lax.cond(jnp.min(worst) >= GLA_SAFE_LOG_DECAY,
             functools.partial(_gla_fast, q_ref, k_ref, v_ref, g_ref, o_ref, s_ref, gs["qi"], gs["kl"], gs["et"],
                               seq_len=seq_len),
             functools.partial(_gla_slow, q_ref, k_ref, v_ref, g_ref, o_ref, s_ref, gs["b"], gs["vf"],
                               seq_len=seq_len))
    if state_out is not None:
        for d in range(2):
            for h in range(GLA_HEADS):
                p, j = divmod(h, 2)
                state_out[d][h] = s_ref[d, p, j * GLA_DK:(j + 1) * GLA_DK, j * GLA_DV:(j + 1) * GLA_DV]


def _diff_blocks(tasks, lam):
    kv_key, kv = None, None
    for g0 in range(0, len(tasks), DIFF_GROUP):
        group = []
        for key, load_q, load_kv, store in tasks[g0:g0 + DIFF_GROUP]:
            if key != kv_key:
                kts, vs = load_kv()
                kv_key, kv = key, (kts, [_with_ones(v) for v in vs])
            group.append((load_q(), kv[0], kv[1], store))
        scores = []
        for q, kts, _, _ in group:
            first = _iota(q.shape, 1) < DIFF_DH
            zb = jnp.zeros((), BF16)
            qs = jnp.concatenate([jnp.where(first, q, zb), jnp.where(first, zb, q)], axis=0)
            scores.append([_dot(qs, kt) for kt in kts])
        maxes = [functools.reduce(jnp.maximum, [jnp.max(s, axis=-1, keepdims=True) for s in ss]) for ss in scores]
        es = [[_bf(jnp.exp2(s - m)) for s in ss] for ss, m in zip(scores, maxes)]
        for (q, _, vas, store), e_list in zip(group, es):
            tq = q.shape[0]
            r = functools.reduce(jnp.add, [_dot(e, va) for e, va in zip(e_list, vas)])
            n = r[:, :DIFF_DV] / r[:, DIFF_DV:]
            store(n[:tq] - lam * n[tq:])


def _diff_lambda(lam_ref, lam_init):
    lp = lam_ref[...]
    return (jnp.exp(jnp.sum(lp[0:1] * lp[1:2], axis=-1, keepdims=True))
            - jnp.exp(jnp.sum(lp[2:3] * lp[3:4], axis=-1, keepdims=True)) + lam_init)


def _with_ones(v):
    return jnp.concatenate([v, jnp.ones(v.shape, v.dtype)], axis=1)


def _head_rmsnorm(x, gain):
    ms = jnp.mean(x * x, axis=-1, keepdims=True)
    return x * lax.rsqrt(ms + EPS) * gain


def _merge_rows(og_ref, od_ref, sg_ref, x, gt, gg, dg, wo_ref, fg, rows):
    slabs = []
    for h in range(GLA_HEADS):
        slabs.append(_head_rmsnorm(og_ref[rows, h * GLA_DV:(h + 1) * GLA_DV], gg))
    for h in range(DIFF_HEADS):
        slabs.append(_head_rmsnorm(od_ref[rows, h * DIFF_DV:(h + 1) * DIFF_DV], dg))
    o = jnp.concatenate(slabs, axis=-1) * sg_ref[rows, :]
    xn = x + gt * _dot(_bf(o), wo_ref[...])
    ms = jnp.mean(xn * xn, axis=-1, keepdims=True)
    return xn * lax.rsqrt(ms + EPS) * fg


def _layer_kernel(*refs, seq_len, n_seq, rope, has_cache, lam_init):
    it = iter(refs)
    x_ref, gain_ref, mod_ref, wt_ref, wo_ref, wa_ref, ba_ref, lam_ref, gg_ref, dg_ref, fg_ref = (next(it) for _ in range(11))
    rope_refs = tuple(next(it) for _ in range(4)) if rope else None
    if has_cache:
        ckt_ref, cv_ref, s0f_ref, s0b_ref = (next(it) for _ in range(4))
    y_ref = next(it)
    if not has_cache:
        kt_out, dv_out, sf_out, sb_out = (next(it) for _ in range(4))
    names = ["gq", "gk", "gv", "g", "dq", "dv", "gate", "og", "od"] + (["dkt"] if has_cache else [])
    sc = {n: next(it) for n in names}
    gs = {n: next(it) for n in ["s", "qi", "kl", "et", "b", "vf"]}

    d = D_MODEL
    n_tok = seq_len * n_seq
    mod_row = (1 + pl.program_id(0)) if has_cache else 0
    shift = mod_ref[pl.ds(mod_row, 1), 0:d]
    scale = mod_ref[pl.ds(mod_row, 1), d:2 * d]
    gt = mod_ref[pl.ds(mod_row, 1), 2 * d:3 * d]
    gain = gain_ref[...]

    zero = jnp.zeros((GLA_GATE_RANK, GLA_QK), F32)
    wa = _bf(jnp.concatenate([jnp.concatenate([wa_ref[0], zero], axis=1),
                              jnp.concatenate([zero, wa_ref[1]], axis=1)], axis=0))
    ba = jnp.concatenate([ba_ref[0:1, :], ba_ref[1:2, :]], axis=1)

    for t0 in range(0, n_tok, ROW_TILE):
        if has_cache:
            def kt_store(sl, val, t0=t0):
                sc["dkt"][sl, t0:t0 + ROW_TILE] = _bf(val)

            def dv_store(val, t0=t0):
                sc["dv"][t0:t0 + ROW_TILE, :] = _bf(val)
        else:
            def kt_store(sl, val, t0=t0):
                for j in range(ROW_TILE // seq_len):
                    kt_out[(t0 // seq_len) + j, sl, :] = val[:, j * seq_len:(j + 1) * seq_len]

            def dv_store(val, t0=t0):
                sc["dv"][t0:t0 + ROW_TILE, :] = _bf(val)
                for j in range(ROW_TILE // seq_len):
                    for hd in range(DIFF_HEADS):
                        dv_out[(t0 // seq_len) + j, :, hd, :] = val[j * seq_len:(j + 1) * seq_len,
                                                                   hd * DIFF_DV:(hd + 1) * DIFF_DV]

        _project_rows(x_ref[t0:t0 + ROW_TILE, :], gain, shift, scale, wt_ref, wa, ba, t0, ROW_TILE, sc,
                      rope_refs, kt_store, dv_store)

    for b in range(n_seq):
        view = lambda name: _rows(sc[name], b * seq_len, seq_len)
        init = (s0f_ref.at[0], s0b_ref.at[0]) if has_cache else None
        out = None if has_cache else (sf_out.at[b], sb_out.at[b])
        _gla_sequence(view("gq"), view("gk"), view("gv"), view("g"), view("og"), gs, init, out, seq_len=seq_len)

    lam = _diff_lambda(lam_ref, lam_init)
    tasks = []
    for b in range(n_seq):
        for hd in range(DIFF_HEADS):
            sl = slice(hd * LANES, (hd + 1) * LANES)

            def load_kv(b=b, hd=hd, sl=sl):
                if has_cache:
                    return ([_bf(ckt_ref[0, sl, :]), sc["dkt"][sl, :]], [_bf(cv_ref[0, :, hd, :]), sc["dv"][:, sl]])
                return [_bf(kt_out[b, sl, :])], [sc["dv"][b * seq_len:(b + 1) * seq_len, sl]]

            for q0 in range(0, seq_len, DIFF_QB):
                r0 = b * seq_len + q0

                def load_q(r0=r0, sl=sl):
                    return sc["dq"][r0:r0 + DIFF_QB, sl]

                def store(val, r0=r0, sl=sl):
                    sc["od"][r0:r0 + DIFF_QB, sl] = val

                tasks.append(((b, hd), load_q, load_kv, store))
    _diff_blocks(tasks, lam)

    gg = gg_ref[...]
    dg = dg_ref[...] * (1.0 - lam_init)
    fg = fg_ref[...]
    for t0 in range(0, n_tok, ROW_TILE):
        rows = slice(t0, t0 + ROW_TILE)
        y_ref[rows, :] = _merge_rows(sc["og"], sc["od"], sc["gate"], x_ref[rows, :], gt, gg, dg, wo_ref, fg, rows)


def _layer(x2d, seq_len, n_seq, gain, mod, wt, wo, wa, ba, lam_p, gg, dg, fg, lam_init, rope_tabs=None, cache=None):
    t = x2d.shape[0]
    n_tok = seq_len * n_seq
    n_steps = t // n_tok
    rope = rope_tabs is not None
    has_cache = cache is not None
    assert n_tok % ROW_TILE == 0 and ROW_TILE % seq_len in (0, ROW_TILE) and seq_len % GLA_SUPER == 0

    def whole(a, single=False):
        kw = {"pipeline_mode": pl.Buffered(1)} if single else {}
        return pl.BlockSpec(a.shape, lambda i: (0,) * a.ndim, **kw)

    io_kw = {"pipeline_mode": pl.Buffered(1)} if n_steps <= 2 else {}
    args = [x2d, gain, mod, wt, wo, wa, ba, lam_p, gg, dg, fg]
    in_specs = [pl.BlockSpec((n_tok, D_MODEL), lambda i: (i, 0), **io_kw), whole(gain), whole(mod), whole(wt, True),
                whole(wo, True), whole(wa), whole(ba), whole(lam_p), whole(gg), whole(dg), whole(fg)]
    if rope:
        args += list(rope_tabs)
        in_specs += [whole(a, True) for a in rope_tabs]
    sds = jax.ShapeDtypeStruct
    out_shape = [sds((t, D_MODEL), F32)]
    out_specs = [pl.BlockSpec((n_tok, D_MODEL), lambda i: (i, 0), **io_kw)]
    if has_cache:
        ckt, cv, s0f, s0b = cache
        args += [ckt, cv, s0f, s0b]
        st_spec = pl.BlockSpec((1,) + s0f.shape[1:], lambda i: (i, 0, 0, 0))
        in_specs += [pl.BlockSpec((1,) + ckt.shape[1:], lambda i: (i, 0, 0)),
                     pl.BlockSpec((1,) + cv.shape[1:], lambda i: (i, 0, 0, 0)), st_spec, st_spec]
    else:
        n_b = t // seq_len
        out_shape += [sds((n_b, DIFF_QK, seq_len), F32), sds((n_b, seq_len, DIFF_HEADS, DIFF_DV), F32),
                      sds((n_b, GLA_HEADS, GLA_DK, GLA_DV), F32), sds((n_b, GLA_HEADS, GLA_DK, GLA_DV), F32)]
        st_spec = pl.BlockSpec((n_seq, GLA_HEADS, GLA_DK, GLA_DV), lambda i: (i, 0, 0, 0))
        out_specs += [pl.BlockSpec((n_seq, DIFF_QK, seq_len), lambda i: (i, 0, 0)),
                      pl.BlockSpec((n_seq, seq_len, DIFF_HEADS, DIFF_DV), lambda i: (i, 0, 0, 0)), st_spec, st_spec]
    vm = pltpu.VMEM
    scratch = [vm((n_tok, GLA_QK), F32), vm((n_tok, GLA_QK), F32), vm((n_tok, GLA_WIDTH), BF16),
               vm((n_tok, 2 * GLA_QK), F32), vm((n_tok, DIFF_QK), BF16), vm((n_tok, DIFF_WIDTH), BF16),
               vm((n_tok, MIX_WIDTH), F32), vm((n_tok, GLA_WIDTH), F32), vm((n_tok, DIFF_WIDTH), F32)]
    if has_cache:
        scratch += [vm((DIFF_QK, n_tok), BF16)]
    scratch += [vm((2, 2, GLA_PAIR_K, GLA_PAIR_V), F32),
                vm((2, seq_len, GLA_QK), BF16),
                vm((2, seq_len, GLA_QK), BF16),
                vm((2, seq_len // GLA_CHUNK, SUBLANES, GLA_QK), F32),
                vm((GLA_CHUNK, GLA_QK), F32),
                vm((GLA_CHUNK, GLA_WIDTH), F32)]
    return pl.pallas_call(
        functools.partial(_layer_kernel, seq_len=seq_len, n_seq=n_seq, rope=rope, has_cache=has_cache,
                          lam_init=lam_init),
        grid=(n_steps,),
        in_specs=in_specs,
        out_specs=out_specs,
        out_shape=out_shape,
        scratch_shapes=scratch,
        compiler_params=_params(1),
        name="layer_lat" if has_cache else "layer_ctx",
    )(*args)


def _rope_tables(seq_len):
    t = np.arange(seq_len)
    inv_freq = ROPE_BASE ** (-np.arange(ROPE_PAIRS, dtype=np.float64) / ROPE_PAIRS)
    ang_r = (t // GRID_W)[:, None] * inv_freq[None, :]
    ang_c = (t % GRID_W)[:, None] * inv_freq[None, :]
    cr, sr, cc, sc = np.cos(ang_r), np.sin(ang_r), np.cos(ang_c), np.sin(ang_c)
    cos = np.concatenate([cr, cr, cc, cc] * 2, axis=-1).astype(np.float32)
    sin = np.concatenate([-sr, sr, -sc, sc] * 2, axis=-1).astype(np.float32)
    return (jnp.asarray(cos), jnp.asarray(sin),
            jnp.asarray(np.ascontiguousarray(cos.T)), jnp.asarray(np.ascontiguousarray(sin.T)))


def kernel(x_prompt, x_sample, cache_diff_k, cache_diff_v, state_gla_fwd, state_gla_bwd, c, c_ctx,
           norm_gain, w_mod, b_mod, w_in, w_gla_alpha, b_gla_alpha, diff_lambda,
           gla_head_gain, diff_head_gain, w_out, final_gain):
    bp, lp, d = x_prompt.shape
    bs, ls, _ = x_sample.shape
    depth = norm_gain.shape[0]
    assert depth == 1 and d == D_MODEL and bs + 1 <= MOD_ROWS and w_in.shape[2] == _R_END
    l = 0
    lam_init = 0.8 - 0.6 * math.exp(-0.3 * l)

    cvecs = jnp.concatenate([c_ctx[None, :], c, jnp.zeros((MOD_ROWS - 1 - bs, d), F32)], axis=0)
    mod, wt, wo = _setup(cvecs, w_mod[l], b_mod, jnp.swapaxes(w_in[l], 0, 1), w_out[l])
    shared = (norm_gain, mod, wt, wo, w_gla_alpha[l], b_gla_alpha[l], diff_lambda[l],
              gla_head_gain, diff_head_gain, final_gain[None, :], lam_init)

    y_p, dkt, dv, s_f, s_b = _layer(x_prompt.reshape(bp * lp, d), lp, ROW_TILE // lp, *shared)
    y_prompt = y_p.reshape(bp, lp, d)
    new_diff_k = jnp.transpose(dkt.reshape(bp, DIFF_HEADS, 2, DIFF_DH, lp), (0, 4, 1, 2, 3))[:, None]
    new_diff_v = dv[:, None]
    new_gla_fwd = s_f[:, None]
    new_gla_bwd = s_b[:, None]

    past = cache_diff_k.shape[2]
    ckt = jnp.transpose(cache_diff_k[:, l], (0, 2, 3, 4, 1)).reshape(bs, DIFF_QK, past)
    cv = cache_diff_v[:, l]
    y_s = _layer(x_sample.reshape(bs * ls, d), ls, 1, *shared, rope_tabs=_rope_tables(ls),
                 cache=(ckt, cv, state_gla_fwd[:, l], state_gla_bwd[:, l]))[0]
    y_sample = y_s.reshape(bs, ls, d)

    return (y_prompt, y_sample, new_diff_k, new_diff_v, new_gla_fwd, new_gla_bwd)
```

```python
import functools
import math

import numpy as np
import jax
import jax.numpy as jnp
from jax import lax
from jax.experimental import pallas as pl
from jax.experimental.pallas import tpu as pltpu

F32 = jnp.float32
BF16 = jnp.bfloat16

D_MODEL = 1024
GRID_W = 64
GLA_HEADS = 4
GLA_DK = 64
GLA_DV = 128
GLA_QK = GLA_HEADS * GLA_DK
GLA_WIDTH = GLA_HEADS * GLA_DV
GLA_GATE_RANK = 16
GLA_GATE_TEMP = 16.0
GLA_CHUNK = 64
DIFF_HEADS = 4
DIFF_DH = 64
DIFF_DV = 2 * DIFF_DH
DIFF_QK = DIFF_HEADS * 2 * DIFF_DH
DIFF_WIDTH = DIFF_HEADS * DIFF_DV
MIX_WIDTH = GLA_WIDTH + DIFF_WIDTH
ROPE_PAIRS = DIFF_DH // 4
ROPE_BASE = 10000.0
EPS = 1e-6
LOG2E = math.log2(math.e)

LANES = 128
SUBLANES = 8
MOD_ROWS = 8
ROW_TILE = 512
SETUP_STEPS = 8
DIFF_QB = 256
DIFF_GROUP = 2
GLA_SUPER = 256
GLA_PAIR_K = 2 * GLA_DK
GLA_PAIR_V = 2 * GLA_DV
GLA_SAFE_LOG_DECAY = -40.0
VMEM_LIMIT = 58 * 1024 * 1024

_R_GQ = 0
_R_GK = _R_GQ + GLA_QK
_R_GV = _R_GK + GLA_QK
_R_LR = _R_GV + GLA_WIDTH
_R_DQ = _R_LR + 2 * GLA_GATE_RANK
_R_DK = _R_DQ + DIFF_QK
_R_DV = _R_DK + DIFF_QK
_R_GATE = _R_DV + DIFF_WIDTH
_R_END = _R_GATE + MIX_WIDTH


def _bf(x):
    return x.astype(BF16)


def _dot(a, b):
    return jnp.dot(a, b, preferred_element_type=F32)


def _dot_nt(a, b):
    return lax.dot_general(a, b, (((1,), (1,)), ((), ())), preferred_element_type=F32)


def _dot_tn(a, b):
    return lax.dot_general(a, b, (((0,), (0,)), ((), ())), preferred_element_type=F32)


def _params(n_parallel=0, n_arbitrary=0):
    sem = ("parallel",) * n_parallel + ("arbitrary",) * n_arbitrary
    return pltpu.CompilerParams(dimension_semantics=sem, vmem_limit_bytes=VMEM_LIMIT)


def _iota(shape, axis):
    return lax.broadcasted_iota(jnp.int32, shape, axis)


def _rows(ref, start, size):
    return ref.at[pl.ds(start, size)]


def _setup_kernel(c_ref, wm_ref, bm_ref, wi_ref, wo_ref, mod_ref, wib_ref, wob_ref):
    mod_ref[...] = _dot(_bf(_silu(c_ref[...])), _bf(wm_ref[...])) + bm_ref[...]
    wib_ref[...] = _bf(wi_ref[...])
    wob_ref[...] = _bf(wo_ref[...])


def _setup(cvecs, w_mod, b_mod, w_in_t, w_out):
    n_mod = w_mod.shape[1]
    n_in = w_in_t.shape[0]
    tm = n_mod // SETUP_STEPS
    ti = pl.cdiv(n_in, SETUP_STEPS * 2 * SUBLANES) * 2 * SUBLANES
    to = w_out.shape[0] // SETUP_STEPS
    assert tm % LANES == 0 and tm * SETUP_STEPS == n_mod
    rows = lambda r: pl.BlockSpec((r, D_MODEL), lambda i: (i, 0))
    cols = lambda r: pl.BlockSpec((r, tm), lambda i: (0, i))
    return pl.pallas_call(
        _setup_kernel,
        grid=(SETUP_STEPS,),
        in_specs=[pl.BlockSpec((MOD_ROWS, D_MODEL), lambda i: (0, 0)), cols(D_MODEL), cols(1), rows(ti), rows(to)],
        out_specs=[cols(MOD_ROWS), rows(ti), rows(to)],
        out_shape=[jax.ShapeDtypeStruct((MOD_ROWS, n_mod), F32), jax.ShapeDtypeStruct(w_in_t.shape, BF16),
                   jax.ShapeDtypeStruct(w_out.shape, BF16)],
        compiler_params=_params(1),
        name="setup",
    )(cvecs, w_mod, b_mod, w_in_t, w_out)


def _log_sigmoid(z):
    return jnp.minimum(z, 0.0) - jnp.log1p(jnp.exp(-jnp.abs(z)))


def _silu(x):
    h = 0.5 * x
    return h + h * jnp.tanh(h)


def _rope_lanes(x, cos, sin):
    lane = _iota(x.shape, 1)
    up = pltpu.roll(x, ROPE_PAIRS, axis=1)
    dn = pltpu.roll(x, LANES - ROPE_PAIRS, axis=1)
    partner = jnp.where((lane & ROPE_PAIRS) == 0, dn, up)
    return x * cos + partner * sin


def _rope_rows(x, cos, sin):
    p = ROPE_PAIRS
    parts = []
    for g in range(x.shape[0] // (2 * p)):
        parts += [x[(2 * g + 1) * p:(2 * g + 2) * p], x[2 * g * p:(2 * g + 1) * p]]
    return x * cos + jnp.concatenate(parts, axis=0) * sin


def _project_rows(x, gain, shift, scale, wt_ref, wa, ba, row0, n_rows, sc, rope_refs, kt_store, dv_store):
    rows = pl.ds(row0, n_rows)
    ms = jnp.mean(x * x, axis=-1, keepdims=True)
    h = x * lax.rsqrt(ms + EPS) * gain
    hb = _bf(h * (1.0 + scale) + shift)

    def sec(a, b):
        return _dot_nt(hb, wt_ref[a:b, :])

    sc["gq"][rows, :] = sec(_R_GQ, _R_GK) * (GLA_DK ** -0.5)
    sc["gk"][rows, :] = sec(_R_GK, _R_GV)
    sc["gv"][rows, :] = _bf(sec(_R_GV, _R_LR))
    sc["gate"][rows, :] = _silu(sec(_R_GATE, _R_END))
    z = _dot(_bf(sec(_R_LR, _R_DQ)), wa) + ba
    sc["g"][rows, :] = _log_sigmoid(z) * (1.0 / GLA_GATE_TEMP)
    dq = sec(_R_DQ, _R_DK) * (DIFF_DH ** -0.5 * LOG2E)
    dkt = _dot_nt(wt_ref[_R_DK:_R_DV, :], hb)
    dv = sec(_R_DV, _R_GATE)
    if rope_refs is not None:
        cos_ref, sin_ref, cost_ref, sint_ref = rope_refs
        cos, sin = cos_ref[rows, :], sin_ref[rows, :]
        cost, sint = cost_ref[:, rows], sint_ref[:, rows]
        for hd in range(DIFF_HEADS):
            sl = slice(hd * LANES, (hd + 1) * LANES)
            sc["dq"][rows, sl] = _bf(_rope_lanes(dq[:, sl], cos, sin))
            kt_store(sl, _rope_rows(dkt[sl, :], cost, sint))
    else:
        sc["dq"][rows, :] = _bf(dq)
        kt_store(slice(0, DIFF_QK), dkt)
    dv_store(dv)


def _tile4(x):
    return jnp.concatenate([x, x, x, x], axis=0)


def _gla_fast(q_ref, k_ref, v_ref, g_ref, o_ref, s_ref, qi_ref, kl_ref, et_ref, *, seq_len):
    c_len, sc_len = GLA_CHUNK, GLA_SUPER
    nc = seq_len // c_len
    cps = sc_len // c_len
    r = _iota((sc_len, 2 * sc_len), 0)
    c = _iota((sc_len, 2 * sc_len), 1) & (sc_len - 1)
    same = (r // c_len) == (c // c_len)
    tri2 = [(same & (c <= r)).astype(BF16), (same & (c >= r)).astype(BF16)]
    rt = _iota((c_len, GLA_HEADS * c_len), 0)
    cs = _iota((c_len, GLA_HEADS * c_len), 1) & (c_len - 1)
    keep = [cs <= rt, cs >= rt]
    k_mask = (_iota((GLA_HEADS * c_len, GLA_QK), 0) // c_len) == (_iota((GLA_HEADS * c_len, GLA_QK), 1) // GLA_DK)
    v_mask = (_iota((GLA_HEADS * c_len, GLA_WIDTH), 0) // c_len) == (_iota((GLA_HEADS * c_len, GLA_WIDTH), 1) // GLA_DV)
    p_mask = (_iota((GLA_PAIR_K, GLA_PAIR_V), 0) // GLA_DK) == (_iota((GLA_PAIR_K, GLA_PAIR_V), 1) // GLA_DV)
    zb = jnp.zeros((), BF16)

    for sci in range(seq_len // sc_len):
        rows = slice(sci * sc_len, (sci + 1) * sc_len)
        q = q_ref[rows, :]
        k = k_ref[rows, :]
        vb = v_ref[rows, :]
        bs = []
        for d in range(2):
            g = g_ref[rows, d * GLA_QK:(d + 1) * GLA_QK]
            hi = _bf(g)
            lo = _bf(g - hi.astype(F32))
            bs.append(_dot(tri2[d], jnp.concatenate([hi, lo], axis=0)))
        qib, kob = [], []
        for d in range(2):
            b = bs[d]
            qi = _bf(q * jnp.exp(b))
            ko = k * jnp.exp(-b)
            kls = []
            for j in range(cps):
                last = j * c_len + (0 if d else c_len - 1)
                e_tot = jnp.exp(b[last:last + 1, :])
                et_ref[d, sci * cps + j] = jnp.broadcast_to(e_tot, (SUBLANES, GLA_QK))
                kls.append(_bf(ko[j * c_len:(j + 1) * c_len] * e_tot))
            qib.append(qi)
            kob.append(_bf(ko))
            qi_ref[d, rows, :] = qi
            kl_ref[d, rows, :] = jnp.concatenate(kls, axis=0)
        atts = [[_dot_nt(qib[d][j * c_len:(j + 1) * c_len],
                         jnp.where(k_mask, _tile4(kob[d][j * c_len:(j + 1) * c_len]), zb))
                 for d in range(2)] for j in range(cps)]
        for j in range(cps):
            a2 = jnp.concatenate([_bf(jnp.where(keep[d], atts[j][d], 0.0)) for d in range(2)], axis=0)
            v_bd = jnp.where(v_mask, _tile4(vb[j * c_len:(j + 1) * c_len]), zb)
            r2 = _dot(a2, v_bd)
            o_ref[sci * sc_len + j * c_len:sci * sc_len + (j + 1) * c_len, :] = r2[:c_len] + r2[c_len:]

    for i in range(nc):
        for d in range(2):
            ci = (nc - 1 - i) if d else i
            rows = slice(ci * c_len, (ci + 1) * c_len)
            qi = qi_ref[d, rows, :]
            kl = kl_ref[d, rows, :]
            vb = v_ref[rows, :]
            e_col = et_ref[d, ci].T[:, 0:1]
            inter = []
            for p in range(2):
                ks = slice(p * GLA_PAIR_K, (p + 1) * GLA_PAIR_K)
                s = s_ref[d, p]
                inter.append(_dot(qi[:, ks], _bf(s)))
                upd = _dot_tn(kl[:, ks], vb[:, p * GLA_PAIR_V:(p + 1) * GLA_PAIR_V])
                s_ref[d, p] = s * e_col[ks] + jnp.where(p_mask, upd, 0.0)
            o_ref[rows, :] += jnp.concatenate(inter, axis=1)


def _gla_slow(q_ref, k_ref, v_ref, g_ref, o_ref, s_ref, b_ref, vf_ref, *, seq_len):
    c_len = GLA_CHUNK
    nc = seq_len // c_len
    r64 = _iota((c_len, c_len), 0)
    c64 = _iota((c_len, c_len), 1)
    p_mask = (_iota((GLA_PAIR_K, GLA_PAIR_V), 0) // GLA_DK) == (_iota((GLA_PAIR_K, GLA_PAIR_V), 1) // GLA_DV)
    expand = ((_iota((GLA_QK, GLA_WIDTH), 0) // GLA_DK) == (_iota((GLA_QK, GLA_WIDTH), 1) // GLA_DV)).astype(BF16)
    t_idx = _iota((c_len, GLA_QK), 0)
    for d in range(2):
        reverse = bool(d)
        tri = ((c64 >= r64) if reverse else (c64 <= r64)).astype(F32)

        def step(i, carry, d=d, reverse=reverse, tri=tri):
            ci = (nc - 1 - i) if reverse else i
            base = pl.multiple_of(ci * c_len, c_len)
            rows = pl.ds(base, c_len)
            q = q_ref[rows, :]
            k = k_ref[rows, :]
            vb = v_ref[rows, :]
            g = g_ref[rows, d * GLA_QK:(d + 1) * GLA_QK]
            b = jnp.dot(tri, g, precision=lax.Precision.HIGHEST, preferred_element_type=F32)
            b_tot = jnp.sum(g, axis=0, keepdims=True)
            b_ref[...] = b
            vf_ref[...] = vb.astype(F32)

            def key_row(s, acc):
                b_s = b_ref[pl.ds(s, 1), :]
                k_s = k_ref[pl.ds(base + s, 1), :]
                v_s = vf_ref[pl.ds(s, 1), :]
                visible = (t_idx <= s) if reverse else (t_idx >= s)
                w = jnp.where(visible, q * jnp.exp(jnp.minimum(b - b_s, 0.0)), 0.0) * k_s
                return acc + _dot(_bf(w), expand) * v_s

            o = lax.fori_loop(0, c_len, key_row, jnp.zeros((c_len, GLA_WIDTH), F32))
            qi = _bf(q * jnp.exp(b))
            kl = _bf(k * jnp.exp(b_tot - b))
            e_col = jnp.broadcast_to(jnp.exp(b_tot), (SUBLANES, GLA_QK)).T[:, 0:1]
            inter = []
            for p in range(2):
                ks = slice(p * GLA_PAIR_K, (p + 1) * GLA_PAIR_K)
                s = s_ref[d, p]
                inter.append(_dot(qi[:, ks], _bf(s)))
                upd = _dot_tn(kl[:, ks], vb[:, p * GLA_PAIR_V:(p + 1) * GLA_PAIR_V])
                s_ref[d, p] = s * e_col[ks] + jnp.where(p_mask, upd, 0.0)
            o = o + jnp.concatenate(inter, axis=1)
            if reverse:
                o_ref[rows, :] += o
            else:
                o_ref[rows, :] = o
            return carry

        lax.fori_loop(0, nc, step, 0)


def _gla_sequence(q_ref, k_ref, v_ref, g_ref, o_ref, gs, init_states, state_out, *, seq_len):
    s_ref = gs["s"]
    zero = jnp.zeros((GLA_DK, GLA_DV), F32)
    for d in range(2):
        for p in range(2):
            if init_states is not None:
                s0 = init_states[d]
                top = jnp.concatenate([s0[2 * p], zero], axis=1)
                bot = jnp.concatenate([zero, s0[2 * p + 1]], axis=1)
                s_ref[d, p] = jnp.concatenate([top, bot], axis=0)
            else:
                s_ref[d, p] = jnp.zeros((GLA_PAIR_K, GLA_PAIR_V), F32)
    nc = seq_len // GLA_CHUNK
    worst = jnp.zeros((1, 2 * GLA_QK), F32)
    for ci in range(nc):
        worst = jnp.minimum(worst, jnp.sum(g_ref[ci * GLA_CHUNK:(ci + 1) * GLA_CHUNK, :], axis=0, keepdims=True))
    lax.cond(jnp.min(worst) >= GLA_SAFE_LOG_DECAY,
             functools.partial(_gla_fast, q_ref, k_ref, v_ref, g_ref, o_ref, s_ref, gs["qi"], gs["kl"], gs["et"],
                               seq_len=seq_len),
             functools.partial(_gla_slow, q_ref, k_ref, v_ref, g_ref, o_ref, s_ref, gs["b"], gs["vf"],
                               seq_len=seq_len))
    if state_out is not None:
        for d in range(2):
            for h in range(GLA_HEADS):
                p, j = divmod(h, 2)
                state_out[d][h] = s_ref[d, p, j * GLA_DK:(j + 1) * GLA_DK, j * GLA_DV:(j + 1) * GLA_DV]


def _diff_blocks(tasks, lam):
    kv_key, kv = None, None
    for g0 in range(0, len(tasks), DIFF_GROUP):
        group = []
        for key, load_q, load_kv, store in tasks[g0:g0 + DIFF_GROUP]:
            if key != kv_key:
                kts, vs = load_kv()
                kv_key, kv = key, (kts, [_with_ones(v) for v in vs])
            group.append((load_q(), kv[0], kv[1], store))
        scores = []
        for q, kts, _, _ in group:
            first = _iota(q.shape, 1) < DIFF_DH
            zb = jnp.zeros((), BF16)
            qs = jnp.concatenate([jnp.where(first, q, zb), jnp.where(first, zb, q)], axis=0)
            scores.append([_dot(qs, kt) for kt in kts])
        maxes = [functools.reduce(jnp.maximum, [jnp.max(s, axis=-1, keepdims=True) for s in ss]) for ss in scores]
        es = [[_bf(jnp.exp2(s - m)) for s in ss] for ss, m in zip(scores, maxes)]
        for (q, _, vas, store), e_list in zip(group, es):
            tq = q.shape[0]
            r = functools.reduce(jnp.add, [_dot(e, va) for e, va in zip(e_list, vas)])
            n = r[:, :DIFF_DV] / r[:, DIFF_DV:]
            store(n[:tq] - lam * n[tq:])


def _diff_lambda(lam_ref, lam_init):
    lp = lam_ref[...]
    return (jnp.exp(jnp.sum(lp[0:1] * lp[1:2], axis=-1, keepdims=True))
            - jnp.exp(jnp.sum(lp[2:3] * lp[3:4], axis=-1, keepdims=True)) + lam_init)


def _with_ones(v):
    return jnp.concatenate([v, jnp.ones(v.shape, v.dtype)], axis=1)


def _head_rmsnorm(x, gain):
    ms = jnp.mean(x * x, axis=-1, keepdims=True)
    return x * lax.rsqrt(ms + EPS) * gain


def _merge_rows(og_ref, od_ref, sg_ref, x, gt, gg, dg, wo_ref, fg, rows):
    slabs = []
    for h in range(GLA_HEADS):
        slabs.append(_head_rmsnorm(og_ref[rows, h * GLA_DV:(h + 1) * GLA_DV], gg))
    for h in range(DIFF_HEADS):
        slabs.append(_head_rmsnorm(od_ref[rows, h * DIFF_DV:(h + 1) * DIFF_DV], dg))
    o = jnp.concatenate(slabs, axis=-1) * sg_ref[rows, :]
    xn = x + gt * _dot(_bf(o), wo_ref[...])
    ms = jnp.mean(xn * xn, axis=-1, keepdims=True)
    return xn * lax.rsqrt(ms + EPS) * fg


def _layer_kernel(*refs, seq_len, n_seq, rope, has_cache, lam_init):
    it = iter(refs)
    x_ref, gain_ref, mod_ref, wt_ref, wo_ref, wa_ref, ba_ref, lam_ref, gg_ref, dg_ref, fg_ref = (next(it) for _ in range(11))
    rope_refs = tuple(next(it) for _ in range(4)) if rope else None
    if has_cache:
        ckt_ref, cv_ref, s0f_ref, s0b_ref = (next(it) for _ in range(4))
    y_ref = next(it)
    if not has_cache:
        kt_out, dv_out, sf_out, sb_out = (next(it) for _ in range(4))
    names = ["gq", "gk", "gv", "g", "dq", "dv", "gate", "og", "od"] + (["dkt"] if has_cache else [])
    sc = {n: next(it) for n in names}
    gs = {n: next(it) for n in ["s", "qi", "kl", "et", "b", "vf"]}

    d = D_MODEL
    n_tok = seq_len * n_seq
    mod_row = (1 + pl.program_id(0)) if has_cache else 0
    shift = mod_ref[pl.ds(mod_row, 1), 0:d]
    scale = mod_ref[pl.ds(mod_row, 1), d:2 * d]
    gt = mod_ref[pl.ds(mod_row, 1), 2 * d:3 * d]
    gain = gain_ref[...]

    zero = jnp.zeros((GLA_GATE_RANK, GLA_QK), F32)
    wa = _bf(jnp.concatenate([jnp.concatenate([wa_ref[0], zero], axis=1),
                              jnp.concatenate([zero, wa_ref[1]], axis=1)], axis=0))
    ba = jnp.concatenate([ba_ref[0:1, :], ba_ref[1:2, :]], axis=1)

    for t0 in range(0, n_tok, ROW_TILE):
        if has_cache:
            def kt_store(sl, val, t0=t0):
                sc["dkt"][sl, t0:t0 + ROW_TILE] = _bf(val)

            def dv_store(val, t0=t0):
                sc["dv"][t0:t0 + ROW_TILE, :] = _bf(val)
        else:
            def kt_store(sl, val, t0=t0):
                for j in range(ROW_TILE // seq_len):
                    kt_out[(t0 // seq_len) + j, sl, :] = val[:, j * seq_len:(j + 1) * seq_len]

            def dv_store(val, t0=t0):
                sc["dv"][t0:t0 + ROW_TILE, :] = _bf(val)
                for j in range(ROW_TILE // seq_len):
                    for hd in range(DIFF_HEADS):
                        dv_out[(t0 // seq_len) + j, :, hd, :] = val[j * seq_len:(j + 1) * seq_len,
                                                                   hd * DIFF_DV:(hd + 1) * DIFF_DV]

        _project_rows(x_ref[t0:t0 + ROW_TILE, :], gain, shift, scale, wt_ref, wa, ba, t0, ROW_TILE, sc,
                      rope_refs, kt_store, dv_store)

    for b in range(n_seq):
        view = lambda name: _rows(sc[name], b * seq_len, seq_len)
        init = (s0f_ref.at[0], s0b_ref.at[0]) if has_cache else None
        out = None if has_cache else (sf_out.at[b], sb_out.at[b])
        _gla_sequence(view("gq"), view("gk"), view("gv"), view("g"), view("og"), gs, init, out, seq_len=seq_len)

    lam = _diff_lambda(lam_ref, lam_init)
    tasks = []
    for b in range(n_seq):
        for hd in range(DIFF_HEADS):
            sl = slice(hd * LANES, (hd + 1) * LANES)

            def load_kv(b=b, hd=hd, sl=sl):
                if has_cache:
                    return ([_bf(ckt_ref[0, sl, :]), sc["dkt"][sl, :]], [_bf(cv_ref[0, :, sl]), sc["dv"][:, sl]])
                return [_bf(kt_out[b, sl, :])], [sc["dv"][b * seq_len:(b + 1) * seq_len, sl]]

            for q0 in range(0, seq_len, DIFF_QB):
                r0 = b * seq_len + q0

                def load_q(r0=r0, sl=sl):
                    return sc["dq"][r0:r0 + DIFF_QB, sl]

                def store(val, r0=r0, sl=sl):
                    sc["od"][r0:r0 + DIFF_QB, sl] = val

                tasks.append(((b, hd), load_q, load_kv, store))
    _diff_blocks(tasks, lam)

    gg = gg_ref[...]
    dg = dg_ref[...] * (1.0 - lam_init)
    fg = fg_ref[...]
    for t0 in range(0, n_tok, ROW_TILE):
        rows = slice(t0, t0 + ROW_TILE)
        y_ref[rows, :] = _merge_rows(sc["og"], sc["od"], sc["gate"], x_ref[rows, :], gt, gg, dg, wo_ref, fg, rows)


def _layer(x2d, seq_len, n_seq, gain, mod, wt, wo, wa, ba, lam_p, gg, dg, fg, lam_init, rope_tabs=None, cache=None):
    t = x2d.shape[0]
    n_tok = seq_len * n_seq
    n_steps = t // n_tok
    rope = rope_tabs is not None
    has_cache = cache is not None
    assert n_tok % ROW_TILE == 0 and ROW_TILE % seq_len in (0, ROW_TILE) and seq_len % GLA_SUPER == 0

    def whole(a, single=False):
        kw = {"pipeline_mode": pl.Buffered(1)} if single else {}
        return pl.BlockSpec(a.shape, lambda i: (0,) * a.ndim, **kw)

    io_kw = {"pipeline_mode": pl.Buffered(1)} if n_steps <= 2 else {}
    args = [x2d, gain, mod, wt, wo, wa, ba, lam_p, gg, dg, fg]
    in_specs = [pl.BlockSpec((n_tok, D_MODEL), lambda i: (i, 0), **io_kw), whole(gain), whole(mod), whole(wt, True),
                whole(wo, True), whole(wa), whole(ba), whole(lam_p), whole(gg), whole(dg), whole(fg)]
    if rope:
        args += list(rope_tabs)
        in_specs += [whole(a, True) for a in rope_tabs]
    sds = jax.ShapeDtypeStruct
    out_shape = [sds((t, D_MODEL), F32)]
    out_specs = [pl.BlockSpec((n_tok, D_MODEL), lambda i: (i, 0), **io_kw)]
    if has_cache:
        ckt, cv, s0f, s0b = cache
        args += [ckt, cv, s0f, s0b]
        st_spec = pl.BlockSpec((1,) + s0f.shape[1:], lambda i: (i, 0, 0, 0))
        in_specs += [pl.BlockSpec((1,) + ckt.shape[1:], lambda i: (i, 0, 0)),
                     pl.BlockSpec((1,) + cv.shape[1:], lambda i: (i, 0, 0)), st_spec, st_spec]
    else:
        n_b = t // seq_len
        out_shape += [sds((n_b, DIFF_QK, seq_len), F32), sds((n_b, seq_len, DIFF_HEADS, DIFF_DV), F32),
                      sds((n_b, GLA_HEADS, GLA_DK, GLA_DV), F32), sds((n_b, GLA_HEADS, GLA_DK, GLA_DV), F32)]
        st_spec = pl.BlockSpec((n_seq, GLA_HEADS, GLA_DK, GLA_DV), lambda i: (i, 0, 0, 0))
        out_specs += [pl.BlockSpec((n_seq, DIFF_QK, seq_len), lambda i: (i, 0, 0)),
                      pl.BlockSpec((n_seq, seq_len, DIFF_HEADS, DIFF_DV), lambda i: (i, 0, 0, 0)), st_spec, st_spec]
    vm = pltpu.VMEM
    scratch = [vm((n_tok, GLA_QK), F32), vm((n_tok, GLA_QK), F32), vm((n_tok, GLA_WIDTH), BF16),
               vm((n_tok, 2 * GLA_QK), F32), vm((n_tok, DIFF_QK), BF16), vm((n_tok, DIFF_WIDTH), BF16),
               vm((n_tok, MIX_WIDTH), F32), vm((n_tok, GLA_WIDTH), F32), vm((n_tok, DIFF_WIDTH), F32)]
    if has_cache:
        scratch += [vm((DIFF_QK, n_tok), BF16)]
    scratch += [vm((2, 2, GLA_PAIR_K, GLA_PAIR_V), F32),
                vm((2, seq_len, GLA_QK), BF16),
                vm((2, seq_len, GLA_QK), BF16),
                vm((2, seq_len // GLA_CHUNK, SUBLANES, GLA_QK), F32),
                vm((GLA_CHUNK, GLA_QK), F32),
                vm((GLA_CHUNK, GLA_WIDTH), F32)]
    return pl.pallas_call(
        functools.partial(_layer_kernel, seq_len=seq_len, n_seq=n_seq, rope=rope, has_cache=has_cache,
                          lam_init=lam_init),
        grid=(n_steps,),
        in_specs=in_specs,
        out_specs=out_specs,
        out_shape=out_shape,
        scratch_shapes=scratch,
        compiler_params=_params(1),
        name="layer_lat" if has_cache else "layer_ctx",
    )(*args)


def _rope_tables(seq_len):
    t = np.arange(seq_len)
    inv_freq = ROPE_BASE ** (-np.arange(ROPE_PAIRS, dtype=np.float64) / ROPE_PAIRS)
    ang_r = (t // GRID_W)[:, None] * inv_freq[None, :]
    ang_c = (t % GRID_W)[:, None] * inv_freq[None, :]
    cr, sr, cc, sc = np.cos(ang_r), np.sin(ang_r), np.cos(ang_c), np.sin(ang_c)
    cos = np.concatenate([cr, cr, cc, cc] * 2, axis=-1).astype(np.float32)
    sin = np.concatenate([-sr, sr, -sc, sc] * 2, axis=-1).astype(np.float32)
    return (jnp.asarray(cos), jnp.asarray(sin),
            jnp.asarray(np.ascontiguousarray(cos.T)), jnp.asarray(np.ascontiguousarray(sin.T)))


def kernel(x_prompt, x_sample, cache_diff_k, cache_diff_v, state_gla_fwd, state_gla_bwd, c, c_ctx,
           norm_gain, w_mod, b_mod, w_in, w_gla_alpha, b_gla_alpha, diff_lambda,
           gla_head_gain, diff_head_gain, w_out, final_gain):
    bp, lp, d = x_prompt.shape
    bs, ls, _ = x_sample.shape
    depth = norm_gain.shape[0]
    assert depth == 1 and d == D_MODEL and bs + 1 <= MOD_ROWS and w_in.shape[2] == _R_END
    l = 0
    lam_init = 0.8 - 0.6 * math.exp(-0.3 * l)

    cvecs = jnp.concatenate([c_ctx[None, :], c, jnp.zeros((MOD_ROWS - 1 - bs, d), F32)], axis=0)
    mod, wt, wo = _setup(cvecs, w_mod[l], b_mod, jnp.swapaxes(w_in[l], 0, 1), w_out[l])
    shared = (norm_gain, mod, wt, wo, w_gla_alpha[l], b_gla_alpha[l], diff_lambda[l],
              gla_head_gain, diff_head_gain, final_gain[None, :], lam_init)

    y_p, dkt, dv, s_f, s_b = _layer(x_prompt.reshape(bp * lp, d), lp, ROW_TILE // lp, *shared)
    y_prompt = y_p.reshape(bp, lp, d)
    new_diff_k = jnp.transpose(dkt.reshape(bp, DIFF_HEADS, 2, DIFF_DH, lp), (0, 4, 1, 2, 3))[:, None]
    new_diff_v = dv[:, None]
    new_gla_fwd = s_f[:, None]
    new_gla_bwd = s_b[:, None]

    past = cache_diff_k.shape[2]
    ckt = jnp.transpose(cache_diff_k[:, l], (0, 2, 3, 4, 1)).reshape(bs, DIFF_QK, past)
    cv = cache_diff_v[:, l].reshape(bs, past, DIFF_WIDTH)
    y_s = _layer(x_sample.reshape(bs * ls, d), ls, 1, *shared, rope_tabs=_rope_tables(ls),
                 cache=(ckt, cv, state_gla_fwd[:, l], state_gla_bwd[:, l]))[0]
    y_sample = y_s.reshape(bs, ls, d)

    return (y_prompt, y_sample, new_diff_k, new_diff_v, new_gla_fwd, new_gla_bwd)
```

```python
import functools
import math

import numpy as np
import jax
import jax.numpy as jnp
from jax import lax
from jax.experimental import pallas as pl
from jax.experimental.pallas import tpu as pltpu

F32 = jnp.float32
BF16 = jnp.bfloat16

D_MODEL = 1024
GRID_W = 64
GLA_HEADS = 4
GLA_DK = 64
GLA_DV = 128
GLA_QK = GLA_HEADS * GLA_DK
GLA_WIDTH = GLA_HEADS * GLA_DV
GLA_GATE_RANK = 16
GLA_GATE_TEMP = 16.0
GLA_CHUNK = 64
DIFF_HEADS = 4
DIFF_DH = 64
DIFF_DV = 2 * DIFF_DH
DIFF_QK = DIFF_HEADS * 2 * DIFF_DH
DIFF_WIDTH = DIFF_HEADS * DIFF_DV
MIX_WIDTH = GLA_WIDTH + DIFF_WIDTH
ROPE_PAIRS = DIFF_DH // 4
ROPE_BASE = 10000.0
EPS = 1e-6
LOG2E = math.log2(math.e)

LANES = 128
SUBLANES = 8
MOD_ROWS = 8
ROW_TILE = 512
SETUP_STEPS = 8
DIFF_QB = 256
DIFF_GROUP = 2
GLA_SUPER = 256
GLA_GROUP = 2
GLA_PAIR_K = 2 * GLA_DK
GLA_PAIR_V = 2 * GLA_DV
GLA_SAFE_LOG_DECAY = -40.0
VMEM_LIMIT = 58 * 1024 * 1024

_R_GQ = 0
_R_GK = _R_GQ + GLA_QK
_R_GV = _R_GK + GLA_QK
_R_LR = _R_GV + GLA_WIDTH
_R_DQ = _R_LR + 2 * GLA_GATE_RANK
_R_DK = _R_DQ + DIFF_QK
_R_DV = _R_DK + DIFF_QK
_R_GATE = _R_DV + DIFF_WIDTH
_R_END = _R_GATE + MIX_WIDTH


def _bf(x):
    return x.astype(BF16)


def _dot(a, b):
    return jnp.dot(a, b, preferred_element_type=F32)


def _dot_nt(a, b):
    return lax.dot_general(a, b, (((1,), (1,)), ((), ())), preferred_element_type=F32)


def _dot_tn(a, b):
    return lax.dot_general(a, b, (((0,), (0,)), ((), ())), preferred_element_type=F32)


def _params(n_parallel=0, n_arbitrary=0):
    sem = ("parallel",) * n_parallel + ("arbitrary",) * n_arbitrary
    return pltpu.CompilerParams(dimension_semantics=sem, vmem_limit_bytes=VMEM_LIMIT)


def _iota(shape, axis):
    return lax.broadcasted_iota(jnp.int32, shape, axis)


def _rows(ref, start, size):
    return ref.at[pl.ds(start, size)]


def _setup_kernel(c_ref, wm_ref, bm_ref, wi_ref, wo_ref, mod_ref, wib_ref, wob_ref):
    mod_ref[...] = _dot(_bf(_silu(c_ref[...])), _bf(wm_ref[...])) + bm_ref[...]
    wib_ref[...] = _bf(wi_ref[...])
    wob_ref[...] = _bf(wo_ref[...])


def _setup(cvecs, w_mod, b_mod, w_in_t, w_out):
    n_mod = w_mod.shape[1]
    n_in = w_in_t.shape[0]
    tm = n_mod // SETUP_STEPS
    ti = pl.cdiv(n_in, SETUP_STEPS * 2 * SUBLANES) * 2 * SUBLANES
    to = w_out.shape[0] // SETUP_STEPS
    assert tm % LANES == 0 and tm * SETUP_STEPS == n_mod
    rows = lambda r: pl.BlockSpec((r, D_MODEL), lambda i: (i, 0))
    cols = lambda r: pl.BlockSpec((r, tm), lambda i: (0, i))
    return pl.pallas_call(
        _setup_kernel,
        grid=(SETUP_STEPS,),
        in_specs=[pl.BlockSpec((MOD_ROWS, D_MODEL), lambda i: (0, 0)), cols(D_MODEL), cols(1), rows(ti), rows(to)],
        out_specs=[cols(MOD_ROWS), rows(ti), rows(to)],
        out_shape=[jax.ShapeDtypeStruct((MOD_ROWS, n_mod), F32), jax.ShapeDtypeStruct(w_in_t.shape, BF16),
                   jax.ShapeDtypeStruct(w_out.shape, BF16)],
        compiler_params=_params(1),
        name="setup",
    )(cvecs, w_mod, b_mod, w_in_t, w_out)


def _log_sigmoid(z):
    return jnp.minimum(z, 0.0) - jnp.log1p(jnp.exp(-jnp.abs(z)))


def _silu(x):
    h = 0.5 * x
    return h + h * jnp.tanh(h)


def _rope_lanes(x, cos, sin):
    lane = _iota(x.shape, 1)
    up = pltpu.roll(x, ROPE_PAIRS, axis=1)
    dn = pltpu.roll(x, LANES - ROPE_PAIRS, axis=1)
    partner = jnp.where((lane & ROPE_PAIRS) == 0, dn, up)
    return x * cos + partner * sin


def _rope_rows(x, cos, sin):
    p = ROPE_PAIRS
    parts = []
    for g in range(x.shape[0] // (2 * p)):
        parts += [x[(2 * g + 1) * p:(2 * g + 2) * p], x[2 * g * p:(2 * g + 1) * p]]
    return x * cos + jnp.concatenate(parts, axis=0) * sin


def _project_rows(x, gain, shift, scale, wt_ref, wa, ba, row0, n_rows, sc, rope_refs, kt_store, dv_store):
    rows = pl.ds(row0, n_rows)
    ms = jnp.mean(x * x, axis=-1, keepdims=True)
    h = x * lax.rsqrt(ms + EPS) * gain
    hb = _bf(h * (1.0 + scale) + shift)

    def sec(a, b):
        return _dot_nt(hb, wt_ref[a:b, :])

    sc["gq"][rows, :] = sec(_R_GQ, _R_GK) * (GLA_DK ** -0.5)
    sc["gk"][rows, :] = sec(_R_GK, _R_GV)
    sc["gv"][rows, :] = _bf(sec(_R_GV, _R_LR))
    sc["gate"][rows, :] = _silu(sec(_R_GATE, _R_END))
    w_kl = jnp.concatenate([wt_ref[_R_DK:_R_DV, :], wt_ref[_R_LR:_R_DQ, :]], axis=0)
    klt = _dot_nt(w_kl, hb)
    dkt = klt[:DIFF_QK]
    z = _dot_tn(_bf(klt[DIFF_QK:]), wa) + ba
    sc["g"][rows, :] = _log_sigmoid(z) * (1.0 / GLA_GATE_TEMP)
    dq = sec(_R_DQ, _R_DK) * (DIFF_DH ** -0.5 * LOG2E)
    dv = sec(_R_DV, _R_GATE)
    if rope_refs is not None:
        cos_ref, sin_ref, cost_ref, sint_ref = rope_refs
        cos, sin = cos_ref[rows, :], sin_ref[rows, :]
        cost, sint = cost_ref[:, rows], sint_ref[:, rows]
        for hd in range(DIFF_HEADS):
            sl = slice(hd * LANES, (hd + 1) * LANES)
            sc["dq"][rows, sl] = _bf(_rope_lanes(dq[:, sl], cos, sin))
            kt_store(sl, _rope_rows(dkt[sl, :], cost, sint))
    else:
        sc["dq"][rows, :] = _bf(dq)
        kt_store(slice(0, DIFF_QK), dkt)
    dv_store(dv)


def _tile4(x):
    return jnp.concatenate([x, x, x, x], axis=0)


def _pair_mask():
    return (_iota((GLA_PAIR_K, GLA_PAIR_V), 0) // GLA_DK) == (_iota((GLA_PAIR_K, GLA_PAIR_V), 1) // GLA_DV)


def _gla_recurrence_step(qi, kl, vb, e_col, s_ref, b, d, p_mask):
    inter = []
    for p in range(2):
        ks = slice(p * GLA_PAIR_K, (p + 1) * GLA_PAIR_K)
        s = s_ref[b, d, p]
        inter.append(_dot(qi[:, ks], _bf(s)))
        upd = _dot_tn(kl[:, ks], vb[:, p * GLA_PAIR_V:(p + 1) * GLA_PAIR_V])
        s_ref[b, d, p] = s * e_col[ks] + jnp.where(p_mask, upd, 0.0)
    return jnp.concatenate(inter, axis=1)


def _gla_fast(q_ref, k_ref, v_ref, g_ref, o_ref, s_ref, qi_ref, kl_ref, et_ref, *, seq_len, n_seq):
    c_len, sc_len = GLA_CHUNK, GLA_SUPER
    nc = seq_len // c_len
    cps = sc_len // c_len
    n_super = n_seq * seq_len // sc_len
    r = _iota((sc_len, 2 * sc_len), 0)
    c = _iota((sc_len, 2 * sc_len), 1) & (sc_len - 1)
    same = (r // c_len) == (c // c_len)
    tri2 = [(same & (c <= r)).astype(BF16), (same & (c >= r)).astype(BF16)]
    rt = _iota((c_len, GLA_HEADS * c_len), 0)
    cs = _iota((c_len, GLA_HEADS * c_len), 1) & (c_len - 1)
    keep = [cs <= rt, cs >= rt]
    k_mask = (_iota((GLA_HEADS * c_len, GLA_QK), 0) // c_len) == (_iota((GLA_HEADS * c_len, GLA_QK), 1) // GLA_DK)
    v_mask = (_iota((GLA_HEADS * c_len, GLA_WIDTH), 0) // c_len) == (_iota((GLA_HEADS * c_len, GLA_WIDTH), 1) // GLA_DV)
    p_mask = _pair_mask()
    zb = jnp.zeros((), BF16)
    dirs = range(2)

    for g0 in range(0, n_super, GLA_GROUP):
        group = range(g0, min(g0 + GLA_GROUP, n_super))
        rows = {u: slice(u * sc_len, (u + 1) * sc_len) for u in group}
        bs = {}
        for u in group:
            for d in dirs:
                g = g_ref[rows[u], d * GLA_QK:(d + 1) * GLA_QK]
                hi = _bf(g)
                lo = _bf(g - hi.astype(F32))
                bs[u, d] = _dot(tri2[d], jnp.concatenate([hi, lo], axis=0))
        qib, kob = {}, {}
        for u in group:
            q = q_ref[rows[u], :]
            k = k_ref[rows[u], :]
            for d in dirs:
                b = bs[u, d]
                qi = _bf(q * jnp.exp(b))
                ko = k * jnp.exp(-b)
                kls = []
                for j in range(cps):
                    last = j * c_len + (0 if d else c_len - 1)
                    e_tot = jnp.exp(b[last:last + 1, :])
                    et_ref[d, u * cps + j] = jnp.broadcast_to(e_tot, (SUBLANES, GLA_QK))
                    kls.append(_bf(ko[j * c_len:(j + 1) * c_len] * e_tot))
                qib[u, d], kob[u, d] = qi, _bf(ko)
                qi_ref[d, rows[u], :] = qi
                kl_ref[d, rows[u], :] = jnp.concatenate(kls, axis=0)
        atts = {}
        for u in group:
            for j in range(cps):
                cr = slice(j * c_len, (j + 1) * c_len)
                for d in dirs:
                    atts[u, j, d] = _dot_nt(qib[u, d][cr], jnp.where(k_mask, _tile4(kob[u, d][cr]), zb))
        for u in group:
            vb = v_ref[rows[u], :]
            for j in range(cps):
                a2 = jnp.concatenate([_bf(jnp.where(keep[d], atts[u, j, d], 0.0)) for d in dirs], axis=0)
                v_bd = jnp.where(v_mask, _tile4(vb[j * c_len:(j + 1) * c_len]), zb)
                r2 = _dot(a2, v_bd)
                o_ref[u * sc_len + j * c_len:u * sc_len + (j + 1) * c_len, :] = r2[:c_len] + r2[c_len:]

    for i in range(nc):
        for b in range(n_seq):
            for d in dirs:
                ci = b * nc + ((nc - 1 - i) if d else i)
                rows_c = slice(ci * c_len, (ci + 1) * c_len)
                e_col = et_ref[d, ci].T[:, 0:1]
                o_ref[rows_c, :] += _gla_recurrence_step(qi_ref[d, rows_c, :], kl_ref[d, rows_c, :],
                                                         v_ref[rows_c, :], e_col, s_ref, b, d, p_mask)


def _gla_slow(q_ref, k_ref, v_ref, g_ref, o_ref, s_ref, b_ref, vf_ref, *, seq_len, n_seq):
    c_len = GLA_CHUNK
    nc = seq_len // c_len
    r64 = _iota((c_len, c_len), 0)
    c64 = _iota((c_len, c_len), 1)
    p_mask = _pair_mask()
    expand = ((_iota((GLA_QK, GLA_WIDTH), 0) // GLA_DK) == (_iota((GLA_QK, GLA_WIDTH), 1) // GLA_DV)).astype(BF16)
    t_idx = _iota((c_len, GLA_QK), 0)
    for b in range(n_seq):
        for d in range(2):
            reverse = bool(d)
            tri = ((c64 >= r64) if reverse else (c64 <= r64)).astype(F32)

            def step(i, carry, b=b, d=d, reverse=reverse, tri=tri):
                ci = b * nc + ((nc - 1 - i) if reverse else i)
                base = pl.multiple_of(ci * c_len, c_len)
                rows = pl.ds(base, c_len)
                q = q_ref[rows, :]
                k = k_ref[rows, :]
                vb = v_ref[rows, :]
                g = g_ref[rows, d * GLA_QK:(d + 1) * GLA_QK]
                bcum = jnp.dot(tri, g, precision=lax.Precision.HIGHEST, preferred_element_type=F32)
                b_tot = jnp.sum(g, axis=0, keepdims=True)
                b_ref[...] = bcum
                vf_ref[...] = vb.astype(F32)

                def key_row(s, acc):
                    b_s = b_ref[pl.ds(s, 1), :]
                    k_s = k_ref[pl.ds(base + s, 1), :]
                    v_s = vf_ref[pl.ds(s, 1), :]
                    visible = (t_idx <= s) if reverse else (t_idx >= s)
                    w = jnp.where(visible, q * jnp.exp(jnp.minimum(bcum - b_s, 0.0)), 0.0) * k_s
                    return acc + _dot(_bf(w), expand) * v_s

                o = lax.fori_loop(0, c_len, key_row, jnp.zeros((c_len, GLA_WIDTH), F32))
                e_col = jnp.broadcast_to(jnp.exp(b_tot), (SUBLANES, GLA_QK)).T[:, 0:1]
                o = o + _gla_recurrence_step(_bf(q * jnp.exp(bcum)), _bf(k * jnp.exp(b_tot - bcum)), vb, e_col,
                                             s_ref, b, d, p_mask)
                if reverse:
                    o_ref[rows, :] += o
                else:
                    o_ref[rows, :] = o
                return carry

            lax.fori_loop(0, nc, step, 0)


def _gla_sequences(q_ref, k_ref, v_ref, g_ref, o_ref, gs, init_states, state_out, *, seq_len, n_seq):
    s_ref = gs["s"]
    zero = jnp.zeros((GLA_DK, GLA_DV), F32)
    for b in range(n_seq):
        for d in range(2):
            for p in range(2):
                if init_states is not None:
                    s0 = init_states[b][d]
                    top = jnp.concatenate([s0[2 * p], zero], axis=1)
                    bot = jnp.concatenate([zero, s0[2 * p + 1]], axis=1)
                    s_ref[b, d, p] = jnp.concatenate([top, bot], axis=0)
                else:
                    s_ref[b, d, p] = jnp.zeros((GLA_PAIR_K, GLA_PAIR_V), F32)
    worst = jnp.zeros((1, 2 * GLA_QK), F32)
    for ci in range(n_seq * seq_len // GLA_CHUNK):
        worst = jnp.minimum(worst, jnp.sum(g_ref[ci * GLA_CHUNK:(ci + 1) * GLA_CHUNK, :], axis=0, keepdims=True))
    lax.cond(jnp.min(worst) >= GLA_SAFE_LOG_DECAY,
             functools.partial(_gla_fast, q_ref, k_ref, v_ref, g_ref, o_ref, s_ref, gs["qi"], gs["kl"], gs["et"],
                               seq_len=seq_len, n_seq=n_seq),
             functools.partial(_gla_slow, q_ref, k_ref, v_ref, g_ref, o_ref, s_ref, gs["b"], gs["vf"],
                               seq_len=seq_len, n_seq=n_seq))
    if state_out is not None:
        for b in range(n_seq):
            for d in range(2):
                for h in range(GLA_HEADS):
                    p, j = divmod(h, 2)
                    state_out[b][d][h] = s_ref[b, d, p, j * GLA_DK:(j + 1) * GLA_DK, j * GLA_DV:(j + 1) * GLA_DV]


def _diff_blocks(tasks, lam):
    kv_key, kv = None, None
    for g0 in range(0, len(tasks), DIFF_GROUP):
        group = []
        for key, load_q, load_kv, store in tasks[g0:g0 + DIFF_GROUP]:
            if key != kv_key:
                kts, vs = load_kv()
                kv_key, kv = key, (kts, [_with_ones(v) for v in vs])
            group.append((load_q(), kv[0], kv[1], store))
        scores = []
        for q, kts, _, _ in group:
            first = _iota(q.shape, 1) < DIFF_DH
            zb = jnp.zeros((), BF16)
            qs = jnp.concatenate([jnp.where(first, q, zb), jnp.where(first, zb, q)], axis=0)
            scores.append([_dot(qs, kt) for kt in kts])
        maxes = [functools.reduce(jnp.maximum, [jnp.max(s, axis=-1, keepdims=True) for s in ss]) for ss in scores]
        es = [[_bf(jnp.exp2(s - m)) for s in ss] for ss, m in zip(scores, maxes)]
        for (q, _, vas, store), e_list in zip(group, es):
            tq = q.shape[0]
            r = functools.reduce(jnp.add, [_dot(e, va) for e, va in zip(e_list, vas)])
            n = r[:, :DIFF_DV] / r[:, DIFF_DV:]
            store(n[:tq] - lam * n[tq:])


def _diff_lambda(lam_ref, lam_init):
    lp = lam_ref[...]
    return (jnp.exp(jnp.sum(lp[0:1] * lp[1:2], axis=-1, keepdims=True))
            - jnp.exp(jnp.sum(lp[2:3] * lp[3:4], axis=-1, keepdims=True)) + lam_init)


def _with_ones(v):
    return jnp.concatenate([v, jnp.ones(v.shape, v.dtype)], axis=1)


def _head_rmsnorm(x, gain):
    ms = jnp.mean(x * x, axis=-1, keepdims=True)
    return x * lax.rsqrt(ms + EPS) * gain


def _merge_rows(og_ref, od_ref, sg_ref, x, gt, gg, dg, wo_ref, fg, rows):
    slabs = []
    for h in range(GLA_HEADS):
        slabs.append(_head_rmsnorm(og_ref[rows, h * GLA_DV:(h + 1) * GLA_DV], gg))
    for h in range(DIFF_HEADS):
        slabs.append(_head_rmsnorm(od_ref[rows, h * DIFF_DV:(h + 1) * DIFF_DV], dg))
    o = jnp.concatenate(slabs, axis=-1) * sg_ref[rows, :]
    xn = x + gt * _dot(_bf(o), wo_ref[...])
    ms = jnp.mean(xn * xn, axis=-1, keepdims=True)
    return xn * lax.rsqrt(ms + EPS) * fg


def _layer_kernel(*refs, seq_len, n_seq, rope, has_cache, lam_init):
    it = iter(refs)
    x_ref, gain_ref, mod_ref, wt_ref, wo_ref, wa_ref, ba_ref, lam_ref, gg_ref, dg_ref, fg_ref = (next(it) for _ in range(11))
    rope_refs = tuple(next(it) for _ in range(4)) if rope else None
    if has_cache:
        ckt_ref, cv_ref, s0f_ref, s0b_ref = (next(it) for _ in range(4))
    y_ref = next(it)
    if not has_cache:
        kt_out, dv_out, sf_out, sb_out = (next(it) for _ in range(4))
    names = ["gq", "gk", "gv", "g", "dq", "dv", "gate", "og", "od"] + (["dkt"] if has_cache else [])
    sc = {n: next(it) for n in names}
    gs = {n: next(it) for n in ["s", "qi", "kl", "et", "b", "vf"]}

    d = D_MODEL
    n_tok = seq_len * n_seq
    mod_row = (1 + pl.program_id(0)) if has_cache else 0
    shift = mod_ref[pl.ds(mod_row, 1), 0:d]
    scale = mod_ref[pl.ds(mod_row, 1), d:2 * d]
    gt = mod_ref[pl.ds(mod_row, 1), 2 * d:3 * d]
    gain = gain_ref[...]

    zero = jnp.zeros((GLA_GATE_RANK, GLA_QK), F32)
    wa = _bf(jnp.concatenate([jnp.concatenate([wa_ref[0], zero], axis=1),
                              jnp.concatenate([zero, wa_ref[1]], axis=1)], axis=0))
    ba = jnp.concatenate([ba_ref[0:1, :], ba_ref[1:2, :]], axis=1)

    for t0 in range(0, n_tok, ROW_TILE):
        if has_cache:
            def kt_store(sl, val, t0=t0):
                sc["dkt"][sl, t0:t0 + ROW_TILE] = _bf(val)

            def dv_store(val, t0=t0):
                sc["dv"][t0:t0 + ROW_TILE, :] = _bf(val)
        else:
            def kt_store(sl, val, t0=t0):
                for j in range(ROW_TILE // seq_len):
                    kt_out[(t0 // seq_len) + j, sl, :] = val[:, j * seq_len:(j + 1) * seq_len]

            def dv_store(val, t0=t0):
                sc["dv"][t0:t0 + ROW_TILE, :] = _bf(val)
                for j in range(ROW_TILE // seq_len):
                    for hd in range(DIFF_HEADS):
                        dv_out[(t0 // seq_len) + j, :, hd, :] = val[j * seq_len:(j + 1) * seq_len,
                                                                   hd * DIFF_DV:(hd + 1) * DIFF_DV]

        _project_rows(x_ref[t0:t0 + ROW_TILE, :], gain, shift, scale, wt_ref, wa, ba, t0, ROW_TILE, sc,
                      rope_refs, kt_store, dv_store)

    init = [(s0f_ref.at[0], s0b_ref.at[0])] if has_cache else None
    out = None if has_cache else [(sf_out.at[b], sb_out.at[b]) for b in range(n_seq)]
    _gla_sequences(sc["gq"], sc["gk"], sc["gv"], sc["g"], sc["og"], gs, init, out, seq_len=seq_len, n_seq=n_seq)

    lam = _diff_lambda(lam_ref, lam_init)
    tasks = []
    for b in range(n_seq):
        for hd in range(DIFF_HEADS):
            sl = slice(hd * LANES, (hd + 1) * LANES)

            def load_kv(b=b, hd=hd, sl=sl):
                if has_cache:
                    return ([_bf(ckt_ref[0, sl, :]), sc["dkt"][sl, :]], [_bf(cv_ref[0, :, sl]), sc["dv"][:, sl]])
                return [_bf(kt_out[b, sl, :])], [sc["dv"][b * seq_len:(b + 1) * seq_len, sl]]

            for q0 in range(0, seq_len, DIFF_QB):
                r0 = b * seq_len + q0

                def load_q(r0=r0, sl=sl):
                    return sc["dq"][r0:r0 + DIFF_QB, sl]

                def store(val, r0=r0, sl=sl):
                    sc["od"][r0:r0 + DIFF_QB, sl] = val

                tasks.append(((b, hd), load_q, load_kv, store))
    _diff_blocks(tasks, lam)

    gg = gg_ref[...]
    dg = dg_ref[...] * (1.0 - lam_init)
    fg = fg_ref[...]
    for t0 in range(0, n_tok, ROW_TILE):
        rows = slice(t0, t0 + ROW_TILE)
        y_ref[rows, :] = _merge_rows(sc["og"], sc["od"], sc["gate"], x_ref[rows, :], gt, gg, dg, wo_ref, fg, rows)


def _layer(x2d, seq_len, n_seq, gain, mod, wt, wo, wa, ba, lam_p, gg, dg, fg, lam_init, rope_tabs=None, cache=None):
    t = x2d.shape[0]
    n_tok = seq_len * n_seq
    n_steps = t // n_tok
    rope = rope_tabs is not None
    has_cache = cache is not None
    assert n_tok % ROW_TILE == 0 and ROW_TILE % seq_len in (0, ROW_TILE) and seq_len % GLA_SUPER == 0

    def whole(a, single=False):
        kw = {"pipeline_mode": pl.Buffered(1)} if single else {}
        return pl.BlockSpec(a.shape, lambda i: (0,) * a.ndim, **kw)

    io_kw = {"pipeline_mode": pl.Buffered(1)} if n_steps <= 2 else {}
    args = [x2d, gain, mod, wt, wo, wa, ba, lam_p, gg, dg, fg]
    in_specs = [pl.BlockSpec((n_tok, D_MODEL), lambda i: (i, 0), **io_kw), whole(gain), whole(mod), whole(wt, True),
                whole(wo, True), whole(wa), whole(ba), whole(lam_p), whole(gg), whole(dg), whole(fg)]
    if rope:
        args += list(rope_tabs)
        in_specs += [whole(a, True) for a in rope_tabs]
    sds = jax.ShapeDtypeStruct
    out_shape = [sds((t, D_MODEL), F32)]
    out_specs = [pl.BlockSpec((n_tok, D_MODEL), lambda i: (i, 0), **io_kw)]
    if has_cache:
        ckt, cv, s0f, s0b = cache
        args += [ckt, cv, s0f, s0b]
        st_spec = pl.BlockSpec((1,) + s0f.shape[1:], lambda i: (i, 0, 0, 0))
        in_specs += [pl.BlockSpec((1,) + ckt.shape[1:], lambda i: (i, 0, 0)),
                     pl.BlockSpec((1,) + cv.shape[1:], lambda i: (i, 0, 0)), st_spec, st_spec]
    else:
        n_b = t // seq_len
        out_shape += [sds((n_b, DIFF_QK, seq_len), F32), sds((n_b, seq_len, DIFF_HEADS, DIFF_DV), F32),
                      sds((n_b, GLA_HEADS, GLA_DK, GLA_DV), F32), sds((n_b, GLA_HEADS, GLA_DK, GLA_DV), F32)]
        st_spec = pl.BlockSpec((n_seq, GLA_HEADS, GLA_DK, GLA_DV), lambda i: (i, 0, 0, 0))
        out_specs += [pl.BlockSpec((n_seq, DIFF_QK, seq_len), lambda i: (i, 0, 0)),
                      pl.BlockSpec((n_seq, seq_len, DIFF_HEADS, DIFF_DV), lambda i: (i, 0, 0, 0)), st_spec, st_spec]
    vm = pltpu.VMEM
    scratch = [vm((n_tok, GLA_QK), F32), vm((n_tok, GLA_QK), F32), vm((n_tok, GLA_WIDTH), BF16),
               vm((n_tok, 2 * GLA_QK), F32), vm((n_tok, DIFF_QK), BF16), vm((n_tok, DIFF_WIDTH), BF16),
               vm((n_tok, MIX_WIDTH), F32), vm((n_tok, GLA_WIDTH), F32), vm((n_tok, DIFF_WIDTH), F32)]
    if has_cache:
        scratch += [vm((DIFF_QK, n_tok), BF16)]
    scratch += [vm((n_seq, 2, 2, GLA_PAIR_K, GLA_PAIR_V), F32),
                vm((2, n_tok, GLA_QK), BF16),
                vm((2, n_tok, GLA_QK), BF16),
                vm((2, n_tok // GLA_CHUNK, SUBLANES, GLA_QK), F32),
                vm((GLA_CHUNK, GLA_QK), F32),
                vm((GLA_CHUNK, GLA_WIDTH), F32)]
    return pl.pallas_call(
        functools.partial(_layer_kernel, seq_len=seq_len, n_seq=n_seq, rope=rope, has_cache=has_cache,
                          lam_init=lam_init),
        grid=(n_steps,),
        in_specs=in_specs,
        out_specs=out_specs,
        out_shape=out_shape,
        scratch_shapes=scratch,
        compiler_params=_params(1),
        name="layer_lat" if has_cache else "layer_ctx",
    )(*args)


def _rope_tables(seq_len):
    t = np.arange(seq_len)
    inv_freq = ROPE_BASE ** (-np.arange(ROPE_PAIRS, dtype=np.float64) / ROPE_PAIRS)
    ang_r = (t // GRID_W)[:, None] * inv_freq[None, :]
    ang_c = (t % GRID_W)[:, None] * inv_freq[None, :]
    cr, sr, cc, sc = np.cos(ang_r), np.sin(ang_r), np.cos(ang_c), np.sin(ang_c)
    cos = np.concatenate([cr, cr, cc, cc] * 2, axis=-1).astype(np.float32)
    sin = np.concatenate([-sr, sr, -sc, sc] * 2, axis=-1).astype(np.float32)
    return (jnp.asarray(cos), jnp.asarray(sin),
            jnp.asarray(np.ascontiguousarray(cos.T)), jnp.asarray(np.ascontiguousarray(sin.T)))


def kernel(x_prompt, x_sample, cache_diff_k, cache_diff_v, state_gla_fwd, state_gla_bwd, c, c_ctx,
           norm_gain, w_mod, b_mod, w_in, w_gla_alpha, b_gla_alpha, diff_lambda,
           gla_head_gain, diff_head_gain, w_out, final_gain):
    bp, lp, d = x_prompt.shape
    bs, ls, _ = x_sample.shape
    depth = norm_gain.shape[0]
    assert depth == 1 and d == D_MODEL and bs + 1 <= MOD_ROWS and w_in.shape[2] == _R_END
    l = 0
    lam_init = 0.8 - 0.6 * math.exp(-0.3 * l)

    cvecs = jnp.concatenate([c_ctx[None, :], c, jnp.zeros((MOD_ROWS - 1 - bs, d), F32)], axis=0)
    mod, wt, wo = _setup(cvecs, w_mod[l], b_mod, jnp.swapaxes(w_in[l], 0, 1), w_out[l])
    shared = (norm_gain, mod, wt, wo, w_gla_alpha[l], b_gla_alpha[l], diff_lambda[l],
              gla_head_gain, diff_head_gain, final_gain[None, :], lam_init)

    y_p, dkt, dv, s_f, s_b = _layer(x_prompt.reshape(bp * lp, d), lp, ROW_TILE // lp, *shared)
    y_prompt = y_p.reshape(bp, lp, d)
    new_diff_k = jnp.transpose(dkt.reshape(bp, DIFF_HEADS, 2, DIFF_DH, lp), (0, 4, 1, 2, 3))[:, None]
    new_diff_v = dv[:, None]
    new_gla_fwd = s_f[:, None]
    new_gla_bwd = s_b[:, None]

    past = cache_diff_k.shape[2]
    ckt = jnp.transpose(cache_diff_k[:, l], (0, 2, 3, 4, 1)).reshape(bs, DIFF_QK, past)
    cv = cache_diff_v[:, l].reshape(bs, past, DIFF_WIDTH)
    y_s = _layer(x_sample.reshape(bs * ls, d), ls, 1, *shared, rope_tabs=_rope_tables(ls),
                 cache=(ckt, cv, state_gla_fwd[:, l], state_gla_bwd[:, l]))[0]
    y_sample = y_s.reshape(bs, ls, d)

    return (y_prompt, y_sample, new_diff_k, new_diff_v, new_gla_fwd, new_gla_bwd)
```

```python
import functools
import math

import numpy as np
import jax
import jax.numpy as jnp
from jax import lax
from jax.experimental import pallas as pl
from jax.experimental.pallas import tpu as pltpu

F32 = jnp.float32
BF16 = jnp.bfloat16

D_MODEL = 1024
GRID_W = 64
GLA_HEADS = 4
GLA_DK = 64
GLA_DV = 128
GLA_QK = GLA_HEADS * GLA_DK
GLA_WIDTH = GLA_HEADS * GLA_DV
GLA_GATE_RANK = 16
GLA_GATE_TEMP = 16.0
GLA_CHUNK = 64
DIFF_HEADS = 4
DIFF_DH = 64
DIFF_DV = 2 * DIFF_DH
DIFF_QK = DIFF_HEADS * 2 * DIFF_DH
DIFF_WIDTH = DIFF_HEADS * DIFF_DV
MIX_WIDTH = GLA_WIDTH + DIFF_WIDTH
ROPE_PAIRS = DIFF_DH // 4
ROPE_BASE = 10000.0
EPS = 1e-6
LOG2E = math.log2(math.e)

LANES = 128
SUBLANES = 8
MOD_ROWS = 8
ROW_TILE = 512
SETUP_STEPS = 8
DIFF_QB = 256
DIFF_GROUP_KEYS = 4096
GLA_SUPER = 256
GLA_GROUP = 2
GLA_PAIR_K = 2 * GLA_DK
GLA_PAIR_V = 2 * GLA_DV
GLA_SAFE_LOG_DECAY = -40.0
VMEM_LIMIT = 58 * 1024 * 1024

_R_GQ = 0
_R_GK = _R_GQ + GLA_QK
_R_GV = _R_GK + GLA_QK
_R_LR = _R_GV + GLA_WIDTH
_R_DQ = _R_LR + 2 * GLA_GATE_RANK
_R_DK = _R_DQ + DIFF_QK
_R_DV = _R_DK + DIFF_QK
_R_GATE = _R_DV + DIFF_WIDTH
_R_END = _R_GATE + MIX_WIDTH


def _bf(x):
    return x.astype(BF16)


def _dot(a, b):
    return jnp.dot(a, b, preferred_element_type=F32)


def _dot_nt(a, b):
    return lax.dot_general(a, b, (((1,), (1,)), ((), ())), preferred_element_type=F32)


def _dot_tn(a, b):
    return lax.dot_general(a, b, (((0,), (0,)), ((), ())), preferred_element_type=F32)


def _params(n_parallel=0, n_arbitrary=0):
    sem = ("parallel",) * n_parallel + ("arbitrary",) * n_arbitrary
    return pltpu.CompilerParams(dimension_semantics=sem, vmem_limit_bytes=VMEM_LIMIT)


def _iota(shape, axis):
    return lax.broadcasted_iota(jnp.int32, shape, axis)


def _rows(ref, start, size):
    return ref.at[pl.ds(start, size)]


def _setup_kernel(cc_ref, c_ref, wm_ref, bm_ref, wi_ref, wo_ref, mod_ref, wib_ref, wob_ref):
    row = _iota((MOD_ROWS, D_MODEL), 0)
    cvecs = jnp.where(row == 0, cc_ref[...], 0.0)
    for b in range(c_ref.shape[0]):
        cvecs = jnp.where(row == 1 + b, c_ref[b:b + 1, :], cvecs)
    mod_ref[...] = _dot(_bf(_silu(cvecs)), _bf(wm_ref[...])) + bm_ref[...]
    wib_ref[...] = _bf(wi_ref[...])
    wob_ref[...] = _bf(wo_ref[...])


def _setup(c_ctx, c, w_mod, b_mod, w_in_t, w_out):
    n_mod = w_mod.shape[1]
    n_in = w_in_t.shape[0]
    tm = n_mod // SETUP_STEPS
    ti = pl.cdiv(n_in, SETUP_STEPS * 2 * SUBLANES) * 2 * SUBLANES
    to = w_out.shape[0] // SETUP_STEPS
    assert tm % LANES == 0 and tm * SETUP_STEPS == n_mod
    rows = lambda r: pl.BlockSpec((r, D_MODEL), lambda i: (i, 0))
    cols = lambda r: pl.BlockSpec((r, tm), lambda i: (0, i))
    return pl.pallas_call(
        _setup_kernel,
        grid=(SETUP_STEPS,),
        in_specs=[pl.BlockSpec(c_ctx.shape, lambda i: (0, 0)), pl.BlockSpec(c.shape, lambda i: (0, 0)),
                  cols(D_MODEL), cols(1), rows(ti), rows(to)],
        out_specs=[cols(MOD_ROWS), rows(ti), rows(to)],
        out_shape=[jax.ShapeDtypeStruct((MOD_ROWS, n_mod), F32), jax.ShapeDtypeStruct(w_in_t.shape, BF16),
                   jax.ShapeDtypeStruct(w_out.shape, BF16)],
        compiler_params=_params(1),
        name="setup",
    )(c_ctx, c, w_mod, b_mod, w_in_t, w_out)


def _log_sigmoid(z):
    return jnp.minimum(z, 0.0) - jnp.log1p(jnp.exp(-jnp.abs(z)))


def _silu(x):
    h = 0.5 * x
    return h + h * jnp.tanh(h)


def _rope_lanes(x, cos, sin):
    lane = _iota(x.shape, 1)
    up = pltpu.roll(x, ROPE_PAIRS, axis=1)
    dn = pltpu.roll(x, LANES - ROPE_PAIRS, axis=1)
    partner = jnp.where((lane & ROPE_PAIRS) == 0, dn, up)
    return x * cos + partner * sin


def _rope_rows(x, cos, sin):
    p = ROPE_PAIRS
    parts = []
    for g in range(x.shape[0] // (2 * p)):
        parts += [x[(2 * g + 1) * p:(2 * g + 2) * p], x[2 * g * p:(2 * g + 1) * p]]
    return x * cos + jnp.concatenate(parts, axis=0) * sin


def _project_rows(x, gain, shift, scale, wt_ref, wa, ba, row0, n_rows, sc, rope_refs, kt_store, dv_store):
    rows = pl.ds(row0, n_rows)
    ms = jnp.mean(x * x, axis=-1, keepdims=True)
    h = x * lax.rsqrt(ms + EPS) * gain
    hb = _bf(h * (1.0 + scale) + shift)

    def sec(a, b):
        return _dot_nt(hb, wt_ref[a:b, :])

    sc["gq"][rows, :] = sec(_R_GQ, _R_GK) * (GLA_DK ** -0.5)
    sc["gk"][rows, :] = sec(_R_GK, _R_GV)
    sc["gv"][rows, :] = _bf(sec(_R_GV, _R_LR))
    sc["gate"][rows, :] = _silu(sec(_R_GATE, _R_END))
    w_kl = jnp.concatenate([wt_ref[_R_DK:_R_DV, :], wt_ref[_R_LR:_R_DQ, :]], axis=0)
    klt = _dot_nt(w_kl, hb)
    dkt = klt[:DIFF_QK]
    z = _dot_tn(_bf(klt[DIFF_QK:]), wa) + ba
    sc["g"][rows, :] = _log_sigmoid(z) * (1.0 / GLA_GATE_TEMP)
    dq = sec(_R_DQ, _R_DK) * (DIFF_DH ** -0.5 * LOG2E)
    dv = sec(_R_DV, _R_GATE)
    if rope_refs is not None:
        cos_ref, sin_ref, cost_ref, sint_ref = rope_refs
        cos, sin = cos_ref[rows, :], sin_ref[rows, :]
        cost, sint = cost_ref[:, rows], sint_ref[:, rows]
        for hd in range(DIFF_HEADS):
            sl = slice(hd * LANES, (hd + 1) * LANES)
            sc["dq"][rows, sl] = _bf(_rope_lanes(dq[:, sl], cos, sin))
            kt_store(sl, _rope_rows(dkt[sl, :], cost, sint))
    else:
        sc["dq"][rows, :] = _bf(dq)
        kt_store(slice(0, DIFF_QK), dkt)
    dv_store(dv)


def _tile4(x):
    return jnp.concatenate([x, x, x, x], axis=0)


def _pair_mask():
    return (_iota((GLA_PAIR_K, GLA_PAIR_V), 0) // GLA_DK) == (_iota((GLA_PAIR_K, GLA_PAIR_V), 1) // GLA_DV)


def _gla_recurrence_step(qi, kl, vb, e_col, s_ref, b, d, p_mask):
    inter = []
    for p in range(2):
        ks = slice(p * GLA_PAIR_K, (p + 1) * GLA_PAIR_K)
        s = s_ref[b, d, p]
        inter.append(_dot(qi[:, ks], _bf(s)))
        upd = _dot_tn(kl[:, ks], vb[:, p * GLA_PAIR_V:(p + 1) * GLA_PAIR_V])
        s_ref[b, d, p] = s * e_col[ks] + jnp.where(p_mask, upd, 0.0)
    return jnp.concatenate(inter, axis=1)


def _gla_fast(q_ref, k_ref, v_ref, g_ref, o_ref, s_ref, qi_ref, kl_ref, et_ref, *, seq_len, n_seq):
    c_len, sc_len = GLA_CHUNK, GLA_SUPER
    nc = seq_len // c_len
    cps = sc_len // c_len
    n_super = n_seq * seq_len // sc_len
    r = _iota((sc_len, 2 * sc_len), 0)
    c = _iota((sc_len, 2 * sc_len), 1) & (sc_len - 1)
    same = (r // c_len) == (c // c_len)
    tri2 = [(same & (c <= r)).astype(BF16), (same & (c >= r)).astype(BF16)]
    rt = _iota((c_len, GLA_HEADS * c_len), 0)
    cs = _iota((c_len, GLA_HEADS * c_len), 1) & (c_len - 1)
    keep = [cs <= rt, cs >= rt]
    k_mask = (_iota((GLA_HEADS * c_len, GLA_QK), 0) // c_len) == (_iota((GLA_HEADS * c_len, GLA_QK), 1) // GLA_DK)
    v_mask = (_iota((GLA_HEADS * c_len, GLA_WIDTH), 0) // c_len) == (_iota((GLA_HEADS * c_len, GLA_WIDTH), 1) // GLA_DV)
    p_mask = _pair_mask()
    zb = jnp.zeros((), BF16)
    dirs = range(2)

    for g0 in range(0, n_super, GLA_GROUP):
        group = range(g0, min(g0 + GLA_GROUP, n_super))
        rows = {u: slice(u * sc_len, (u + 1) * sc_len) for u in group}
        bs = {}
        for u in group:
            for d in dirs:
                g = g_ref[rows[u], d * GLA_QK:(d + 1) * GLA_QK]
                hi = _bf(g)
                lo = _bf(g - hi.astype(F32))
                bs[u, d] = _dot(tri2[d], jnp.concatenate([hi, lo], axis=0))
        qib, kob = {}, {}
        for u in group:
            q = q_ref[rows[u], :]
            k = k_ref[rows[u], :]
            for d in dirs:
                b = bs[u, d]
                qi = _bf(q * jnp.exp(b))
                ko = k * jnp.exp(-b)
                kls = []
                for j in range(cps):
                    last = j * c_len + (0 if d else c_len - 1)
                    e_tot = jnp.exp(b[last:last + 1, :])
                    et_ref[d, u * cps + j] = jnp.broadcast_to(e_tot, (SUBLANES, GLA_QK))
                    kls.append(_bf(ko[j * c_len:(j + 1) * c_len] * e_tot))
                qib[u, d], kob[u, d] = qi, _bf(ko)
                qi_ref[d, rows[u], :] = qi
                kl_ref[d, rows[u], :] = jnp.concatenate(kls, axis=0)
        atts = {}
        for u in group:
            for j in range(cps):
                cr = slice(j * c_len, (j + 1) * c_len)
                for d in dirs:
                    atts[u, j, d] = _dot_nt(qib[u, d][cr], jnp.where(k_mask, _tile4(kob[u, d][cr]), zb))
        for u in group:
            vb = v_ref[rows[u], :]
            for j in range(cps):
                a2 = jnp.concatenate([_bf(jnp.where(keep[d], atts[u, j, d], 0.0)) for d in dirs], axis=0)
                v_bd = jnp.where(v_mask, _tile4(vb[j * c_len:(j + 1) * c_len]), zb)
                r2 = _dot(a2, v_bd)
                o_ref[u * sc_len + j * c_len:u * sc_len + (j + 1) * c_len, :] = r2[:c_len] + r2[c_len:]

    for i in range(nc):
        for b in range(n_seq):
            for d in dirs:
                ci = b * nc + ((nc - 1 - i) if d else i)
                rows_c = slice(ci * c_len, (ci + 1) * c_len)
                e_col = et_ref[d, ci].T[:, 0:1]
                o_ref[rows_c, :] += _gla_recurrence_step(qi_ref[d, rows_c, :], kl_ref[d, rows_c, :],
                                                         v_ref[rows_c, :], e_col, s_ref, b, d, p_mask)


def _gla_slow(q_ref, k_ref, v_ref, g_ref, o_ref, s_ref, b_ref, vf_ref, *, seq_len, n_seq):
    c_len = GLA_CHUNK
    nc = seq_len // c_len
    r64 = _iota((c_len, c_len), 0)
    c64 = _iota((c_len, c_len), 1)
    p_mask = _pair_mask()
    expand = ((_iota((GLA_QK, GLA_WIDTH), 0) // GLA_DK) == (_iota((GLA_QK, GLA_WIDTH), 1) // GLA_DV)).astype(BF16)
    t_idx = _iota((c_len, GLA_QK), 0)
    for b in range(n_seq):
        for d in range(2):
            reverse = bool(d)
            tri = ((c64 >= r64) if reverse else (c64 <= r64)).astype(F32)

            def step(i, carry, b=b, d=d, reverse=reverse, tri=tri):
                ci = b * nc + ((nc - 1 - i) if reverse else i)
                base = pl.multiple_of(ci * c_len, c_len)
                rows = pl.ds(base, c_len)
                q = q_ref[rows, :]
                k = k_ref[rows, :]
                vb = v_ref[rows, :]
                g = g_ref[rows, d * GLA_QK:(d + 1) * GLA_QK]
                bcum = jnp.dot(tri, g, precision=lax.Precision.HIGHEST, preferred_element_type=F32)
                b_tot = jnp.sum(g, axis=0, keepdims=True)
                b_ref[...] = bcum
                vf_ref[...] = vb.astype(F32)

                def key_row(s, acc):
                    b_s = b_ref[pl.ds(s, 1), :]
                    k_s = k_ref[pl.ds(base + s, 1), :]
                    v_s = vf_ref[pl.ds(s, 1), :]
                    visible = (t_idx <= s) if reverse else (t_idx >= s)
                    w = jnp.where(visible, q * jnp.exp(jnp.minimum(bcum - b_s, 0.0)), 0.0) * k_s
                    return acc + _dot(_bf(w), expand) * v_s

                o = lax.fori_loop(0, c_len, key_row, jnp.zeros((c_len, GLA_WIDTH), F32))
                e_col = jnp.broadcast_to(jnp.exp(b_tot), (SUBLANES, GLA_QK)).T[:, 0:1]
                o = o + _gla_recurrence_step(_bf(q * jnp.exp(bcum)), _bf(k * jnp.exp(b_tot - bcum)), vb, e_col,
                                             s_ref, b, d, p_mask)
                if reverse:
                    o_ref[rows, :] += o
                else:
                    o_ref[rows, :] = o
                return carry

            lax.fori_loop(0, nc, step, 0)


def _gla_sequences(q_ref, k_ref, v_ref, g_ref, o_ref, gs, init_states, state_out, *, seq_len, n_seq):
    s_ref = gs["s"]
    zero = jnp.zeros((GLA_DK, GLA_DV), F32)
    for b in range(n_seq):
        for d in range(2):
            for p in range(2):
                if init_states is not None:
                    s0 = init_states[b][d]
                    top = jnp.concatenate([s0[2 * p], zero], axis=1)
                    bot = jnp.concatenate([zero, s0[2 * p + 1]], axis=1)
                    s_ref[b, d, p] = jnp.concatenate([top, bot], axis=0)
                else:
                    s_ref[b, d, p] = jnp.zeros((GLA_PAIR_K, GLA_PAIR_V), F32)
    worst = jnp.zeros((1, 2 * GLA_QK), F32)
    for ci in range(n_seq * seq_len // GLA_CHUNK):
        worst = jnp.minimum(worst, jnp.sum(g_ref[ci * GLA_CHUNK:(ci + 1) * GLA_CHUNK, :], axis=0, keepdims=True))
    lax.cond(jnp.min(worst) >= GLA_SAFE_LOG_DECAY,
             functools.partial(_gla_fast, q_ref, k_ref, v_ref, g_ref, o_ref, s_ref, gs["qi"], gs["kl"], gs["et"],
                               seq_len=seq_len, n_seq=n_seq),
             functools.partial(_gla_slow, q_ref, k_ref, v_ref, g_ref, o_ref, s_ref, gs["b"], gs["vf"],
                               seq_len=seq_len, n_seq=n_seq))
    if state_out is not None:
        for b in range(n_seq):
            for d in range(2):
                for h in range(GLA_HEADS):
                    p, j = divmod(h, 2)
                    state_out[b][d][h] = s_ref[b, d, p, j * GLA_DK:(j + 1) * GLA_DK, j * GLA_DV:(j + 1) * GLA_DV]


def _diff_blocks(tasks, lam, n_keys):
    kv_key, kv = None, None
    group_size = max(1, DIFF_GROUP_KEYS // n_keys)
    for g0 in range(0, len(tasks), group_size):
        group = []
        for key, load_q, load_kv, store in tasks[g0:g0 + group_size]:
            if key != kv_key:
                kts, vs = load_kv()
                kv_key, kv = key, (kts, [_with_ones(v) for v in vs])
            group.append((load_q(), kv[0], kv[1], store))
        scores = []
        for q, kts, _, _ in group:
            first = _iota(q.shape, 1) < DIFF_DH
            zb = jnp.zeros((), BF16)
            qs = jnp.concatenate([jnp.where(first, q, zb), jnp.where(first, zb, q)], axis=0)
            scores.append([_dot(qs, kt) for kt in kts])
        maxes = [functools.reduce(jnp.maximum, [jnp.max(s, axis=-1, keepdims=True) for s in ss]) for ss in scores]
        es = [[_bf(jnp.exp2(s - m)) for s in ss] for ss, m in zip(scores, maxes)]
        for (q, _, vas, store), e_list in zip(group, es):
            tq = q.shape[0]
            r = functools.reduce(jnp.add, [_dot(e, va) for e, va in zip(e_list, vas)])
            n = r[:, :DIFF_DV] / r[:, DIFF_DV:]
            store(n[:tq] - lam * n[tq:])


def _diff_lambda(lam_ref, lam_init):
    lp = lam_ref[...]
    return (jnp.exp(jnp.sum(lp[0:1] * lp[1:2], axis=-1, keepdims=True))
            - jnp.exp(jnp.sum(lp[2:3] * lp[3:4], axis=-1, keepdims=True)) + lam_init)


def _with_ones(v):
    return jnp.concatenate([v, jnp.ones(v.shape, v.dtype)], axis=1)


def _head_rmsnorm(x, gain):
    ms = jnp.mean(x * x, axis=-1, keepdims=True)
    return x * lax.rsqrt(ms + EPS) * gain


def _merge_rows(og_ref, od_ref, sg_ref, x, gt, gg, dg, wo_ref, fg, rows):
    slabs = []
    for h in range(GLA_HEADS):
        slabs.append(_head_rmsnorm(og_ref[rows, h * GLA_DV:(h + 1) * GLA_DV], gg))
    for h in range(DIFF_HEADS):
        slabs.append(_head_rmsnorm(od_ref[rows, h * DIFF_DV:(h + 1) * DIFF_DV], dg))
    o = jnp.concatenate(slabs, axis=-1) * sg_ref[rows, :]
    xn = x + gt * _dot(_bf(o), wo_ref[...])
    ms = jnp.mean(xn * xn, axis=-1, keepdims=True)
    return xn * lax.rsqrt(ms + EPS) * fg


def _layer_kernel(*refs, seq_len, n_seq, rope, has_cache, lam_init):
    it = iter(refs)
    x_ref, gain_ref, mod_ref, wt_ref, wo_ref, wa_ref, ba_ref, lam_ref, gg_ref, dg_ref, fg_ref = (next(it) for _ in range(11))
    rope_refs = tuple(next(it) for _ in range(4)) if rope else None
    if has_cache:
        ckt_ref, cv_ref, s0f_ref, s0b_ref = (next(it) for _ in range(4))
    y_ref = next(it)
    if not has_cache:
        kt_out, dv_out, sf_out, sb_out = (next(it) for _ in range(4))
    names = ["gq", "gk", "gv", "g", "dq", "dv", "gate", "og", "od"] + (["dkt"] if has_cache else [])
    sc = {n: next(it) for n in names}
    gs = {n: next(it) for n in ["s", "qi", "kl", "et", "b", "vf"]}

    d = D_MODEL
    n_tok = seq_len * n_seq
    mod_row = (1 + pl.program_id(0)) if has_cache else 0
    shift = mod_ref[pl.ds(mod_row, 1), 0:d]
    scale = mod_ref[pl.ds(mod_row, 1), d:2 * d]
    gt = mod_ref[pl.ds(mod_row, 1), 2 * d:3 * d]
    gain = gain_ref[...]

    zero = jnp.zeros((GLA_GATE_RANK, GLA_QK), F32)
    wa = _bf(jnp.concatenate([jnp.concatenate([wa_ref[0], zero], axis=1),
                              jnp.concatenate([zero, wa_ref[1]], axis=1)], axis=0))
    ba = jnp.concatenate([ba_ref[0:1, :], ba_ref[1:2, :]], axis=1)

    for t0 in range(0, n_tok, ROW_TILE):
        if has_cache:
            def kt_store(sl, val, t0=t0):
                sc["dkt"][sl, t0:t0 + ROW_TILE] = _bf(val)

            def dv_store(val, t0=t0):
                sc["dv"][t0:t0 + ROW_TILE, :] = _bf(val)
        else:
            def kt_store(sl, val, t0=t0):
                for j in range(ROW_TILE // seq_len):
                    kt_out[(t0 // seq_len) + j, sl, :] = val[:, j * seq_len:(j + 1) * seq_len]

            def dv_store(val, t0=t0):
                sc["dv"][t0:t0 + ROW_TILE, :] = _bf(val)
                for j in range(ROW_TILE // seq_len):
                    for hd in range(DIFF_HEADS):
                        dv_out[(t0 // seq_len) + j, :, hd, :] = val[j * seq_len:(j + 1) * seq_len,
                                                                   hd * DIFF_DV:(hd + 1) * DIFF_DV]

        _project_rows(x_ref[t0:t0 + ROW_TILE, :], gain, shift, scale, wt_ref, wa, ba, t0, ROW_TILE, sc,
                      rope_refs, kt_store, dv_store)

    init = [(s0f_ref.at[0], s0b_ref.at[0])] if has_cache else None
    out = None if has_cache else [(sf_out.at[b], sb_out.at[b]) for b in range(n_seq)]
    _gla_sequences(sc["gq"], sc["gk"], sc["gv"], sc["g"], sc["og"], gs, init, out, seq_len=seq_len, n_seq=n_seq)

    lam = _diff_lambda(lam_ref, lam_init)
    tasks = []
    for b in range(n_seq):
        for hd in range(DIFF_HEADS):
            sl = slice(hd * LANES, (hd + 1) * LANES)

            def load_kv(b=b, hd=hd, sl=sl):
                if has_cache:
                    return ([_bf(ckt_ref[0, sl, :]), sc["dkt"][sl, :]], [_bf(cv_ref[0, :, sl]), sc["dv"][:, sl]])
                return [_bf(kt_out[b, sl, :])], [sc["dv"][b * seq_len:(b + 1) * seq_len, sl]]

            for q0 in range(0, seq_len, DIFF_QB):
                r0 = b * seq_len + q0

                def load_q(r0=r0, sl=sl):
                    return sc["dq"][r0:r0 + DIFF_QB, sl]

                def store(val, r0=r0, sl=sl):
                    sc["od"][r0:r0 + DIFF_QB, sl] = val

                tasks.append(((b, hd), load_q, load_kv, store))
    _diff_blocks(tasks, lam, seq_len + (ckt_ref.shape[2] if has_cache else 0))

    gg = gg_ref[...]
    dg = dg_ref[...] * (1.0 - lam_init)
    fg = fg_ref[...]
    for t0 in range(0, n_tok, ROW_TILE):
        rows = slice(t0, t0 + ROW_TILE)
        y_ref[rows, :] = _merge_rows(sc["og"], sc["od"], sc["gate"], x_ref[rows, :], gt, gg, dg, wo_ref, fg, rows)


def _layer(x2d, seq_len, n_seq, gain, mod, wt, wo, wa, ba, lam_p, gg, dg, fg, lam_init, rope_tabs=None, cache=None):
    t = x2d.shape[0]
    n_tok = seq_len * n_seq
    n_steps = t // n_tok
    rope = rope_tabs is not None
    has_cache = cache is not None
    assert n_tok % ROW_TILE == 0 and ROW_TILE % seq_len in (0, ROW_TILE) and seq_len % GLA_SUPER == 0

    def whole(a, single=False):
        kw = {"pipeline_mode": pl.Buffered(1)} if single else {}
        return pl.BlockSpec(a.shape, lambda i: (0,) * a.ndim, **kw)

    io_kw = {"pipeline_mode": pl.Buffered(1)} if n_steps <= 2 else {}
    args = [x2d, gain, mod, wt, wo, wa, ba, lam_p, gg, dg, fg]
    in_specs = [pl.BlockSpec((n_tok, D_MODEL), lambda i: (i, 0), **io_kw), whole(gain), whole(mod), whole(wt, True),
                whole(wo, True), whole(wa), whole(ba), whole(lam_p), whole(gg), whole(dg), whole(fg)]
    if rope:
        args += list(rope_tabs)
        in_specs += [whole(a, True) for a in rope_tabs]
    sds = jax.ShapeDtypeStruct
    out_shape = [sds((t, D_MODEL), F32)]
    out_specs = [pl.BlockSpec((n_tok, D_MODEL), lambda i: (i, 0), **io_kw)]
    if has_cache:
        ckt, cv, s0f, s0b = cache
        args += [ckt, cv, s0f, s0b]
        st_spec = pl.BlockSpec((1,) + s0f.shape[1:], lambda i: (i, 0, 0, 0))
        in_specs += [pl.BlockSpec((1,) + ckt.shape[1:], lambda i: (i, 0, 0)),
                     pl.BlockSpec((1,) + cv.shape[1:], lambda i: (i, 0, 0)), st_spec, st_spec]
    else:
        n_b = t // seq_len
        out_shape += [sds((n_b, DIFF_QK, seq_len), F32), sds((n_b, seq_len, DIFF_HEADS, DIFF_DV), F32),
                      sds((n_b, GLA_HEADS, GLA_DK, GLA_DV), F32), sds((n_b, GLA_HEADS, GLA_DK, GLA_DV), F32)]
        st_spec = pl.BlockSpec((n_seq, GLA_HEADS, GLA_DK, GLA_DV), lambda i: (i, 0, 0, 0))
        out_specs += [pl.BlockSpec((n_seq, DIFF_QK, seq_len), lambda i: (i, 0, 0)),
                      pl.BlockSpec((n_seq, seq_len, DIFF_HEADS, DIFF_DV), lambda i: (i, 0, 0, 0)), st_spec, st_spec]
    vm = pltpu.VMEM
    scratch = [vm((n_tok, GLA_QK), F32), vm((n_tok, GLA_QK), F32), vm((n_tok, GLA_WIDTH), BF16),
               vm((n_tok, 2 * GLA_QK), F32), vm((n_tok, DIFF_QK), BF16), vm((n_tok, DIFF_WIDTH), BF16),
               vm((n_tok, MIX_WIDTH), F32), vm((n_tok, GLA_WIDTH), F32), vm((n_tok, DIFF_WIDTH), F32)]
    if has_cache:
        scratch += [vm((DIFF_QK, n_tok), BF16)]
    scratch += [vm((n_seq, 2, 2, GLA_PAIR_K, GLA_PAIR_V), F32),
                vm((2, n_tok, GLA_QK), BF16),
                vm((2, n_tok, GLA_QK), BF16),
                vm((2, n_tok // GLA_CHUNK, SUBLANES, GLA_QK), F32),
                vm((GLA_CHUNK, GLA_QK), F32),
                vm((GLA_CHUNK, GLA_WIDTH), F32)]
    return pl.pallas_call(
        functools.partial(_layer_kernel, seq_len=seq_len, n_seq=n_seq, rope=rope, has_cache=has_cache,
                          lam_init=lam_init),
        grid=(n_steps,),
        in_specs=in_specs,
        out_specs=out_specs,
        out_shape=out_shape,
        scratch_shapes=scratch,
        compiler_params=_params(1),
        name="layer_lat" if has_cache else "layer_ctx",
    )(*args)


def _rope_tables(seq_len):
    t = np.arange(seq_len)
    inv_freq = ROPE_BASE ** (-np.arange(ROPE_PAIRS, dtype=np.float64) / ROPE_PAIRS)
    ang_r = (t // GRID_W)[:, None] * inv_freq[None, :]
    ang_c = (t % GRID_W)[:, None] * inv_freq[None, :]
    cr, sr, cc, sc = np.cos(ang_r), np.sin(ang_r), np.cos(ang_c), np.sin(ang_c)
    cos = np.concatenate([cr, cr, cc, cc] * 2, axis=-1).astype(np.float32)
    sin = np.concatenate([-sr, sr, -sc, sc] * 2, axis=-1).astype(np.float32)
    return (jnp.asarray(cos), jnp.asarray(sin),
            jnp.asarray(np.ascontiguousarray(cos.T)), jnp.asarray(np.ascontiguousarray(sin.T)))


def kernel(x_prompt, x_sample, cache_diff_k, cache_diff_v, state_gla_fwd, state_gla_bwd, c, c_ctx,
           norm_gain, w_mod, b_mod, w_in, w_gla_alpha, b_gla_alpha, diff_lambda,
           gla_head_gain, diff_head_gain, w_out, final_gain):
    bp, lp, d = x_prompt.shape
    bs, ls, _ = x_sample.shape
    depth = norm_gain.shape[0]
    assert depth == 1 and d == D_MODEL and bs + 1 <= MOD_ROWS and w_in.shape[2] == _R_END
    l = 0
    lam_init = 0.8 - 0.6 * math.exp(-0.3 * l)

    mod, wt, wo = _setup(c_ctx[None, :], c, w_mod[l], b_mod, jnp.swapaxes(w_in[l], 0, 1), w_out[l])
    shared = (norm_gain, mod, wt, wo, w_gla_alpha[l], b_gla_alpha[l], diff_lambda[l],
              gla_head_gain, diff_head_gain, final_gain[None, :], lam_init)

    y_p, dkt, dv, s_f, s_b = _layer(x_prompt.reshape(bp * lp, d), lp, ROW_TILE // lp, *shared)
    y_prompt = y_p.reshape(bp, lp, d)
    new_diff_k = jnp.transpose(dkt.reshape(bp, DIFF_HEADS, 2, DIFF_DH, lp), (0, 4, 1, 2, 3))[:, None]
    new_diff_v = dv[:, None]
    new_gla_fwd = s_f[:, None]
    new_gla_bwd = s_b[:, None]

    past = cache_diff_k.shape[2]
    ckt = jnp.transpose(cache_diff_k[:, l], (0, 2, 3, 4, 1)).reshape(bs, DIFF_QK, past)
    cv = cache_diff_v[:, l].reshape(bs, past, DIFF_WIDTH)
    y_s = _layer(x_sample.reshape(bs * ls, d), ls, 1, *shared, rope_tabs=_rope_tables(ls),
                 cache=(ckt, cv, state_gla_fwd[:, l], state_gla_bwd[:, l]))[0]
    y_sample = y_s.reshape(bs, ls, d)

    return (y_prompt, y_sample, new_diff_k, new_diff_v, new_gla_fwd, new_gla_bwd)
```

```python
import functools
import math

import numpy as np
import jax
import jax.numpy as jnp
from jax import lax
from jax.experimental import pallas as pl
from jax.experimental.pallas import tpu as pltpu

F32 = jnp.float32
BF16 = jnp.bfloat16

D_MODEL = 1024
GRID_W = 64
GLA_HEADS = 4
GLA_DK = 64
GLA_DV = 128
GLA_QK = GLA_HEADS * GLA_DK
GLA_WIDTH = GLA_HEADS * GLA_DV
GLA_GATE_RANK = 16
GLA_GATE_TEMP = 16.0
GLA_CHUNK = 64
DIFF_HEADS = 4
DIFF_DH = 64
DIFF_DV = 2 * DIFF_DH
DIFF_QK = DIFF_HEADS * 2 * DIFF_DH
DIFF_WIDTH = DIFF_HEADS * DIFF_DV
MIX_WIDTH = GLA_WIDTH + DIFF_WIDTH
ROPE_PAIRS = DIFF_DH // 4
ROPE_BASE = 10000.0
EPS = 1e-6
LOG2E = math.log2(math.e)

LANES = 128
SUBLANES = 8
MOD_ROWS = 8
ROW_TILE = 512
SETUP_STEPS = 8
DIFF_QB = 256
DIFF_GROUP_KEYS = 4096
GLA_SUPER = 256
GLA_GROUP = 2
GLA_PAIR_K = 2 * GLA_DK
GLA_PAIR_V = 2 * GLA_DV
GLA_SAFE_LOG_DECAY = -40.0
VMEM_LIMIT = 58 * 1024 * 1024

_R_GQ = 0
_R_GK = _R_GQ + GLA_QK
_R_GV = _R_GK + GLA_QK
_R_LR = _R_GV + GLA_WIDTH
_R_DQ = _R_LR + 2 * GLA_GATE_RANK
_R_DK = _R_DQ + DIFF_QK
_R_DV = _R_DK + DIFF_QK
_R_GATE = _R_DV + DIFF_WIDTH
_R_END = _R_GATE + MIX_WIDTH


def _bf(x):
    return x.astype(BF16)


def _dot(a, b):
    return jnp.dot(a, b, preferred_element_type=F32)


def _dot_nt(a, b):
    return lax.dot_general(a, b, (((1,), (1,)), ((), ())), preferred_element_type=F32)


def _dot_tn(a, b):
    return lax.dot_general(a, b, (((0,), (0,)), ((), ())), preferred_element_type=F32)


def _params(n_parallel=0, n_arbitrary=0):
    sem = ("parallel",) * n_parallel + ("arbitrary",) * n_arbitrary
    return pltpu.CompilerParams(dimension_semantics=sem, vmem_limit_bytes=VMEM_LIMIT)


def _iota(shape, axis):
    return lax.broadcasted_iota(jnp.int32, shape, axis)


def _rows(ref, start, size):
    return ref.at[pl.ds(start, size)]


def _setup_kernel(cc_ref, c_ref, wm_ref, bm_ref, wi_ref, wo_ref, mod_ref, wib_ref, wob_ref):
    row = _iota((MOD_ROWS, D_MODEL), 0)
    cvecs = jnp.where(row == 0, cc_ref[...], 0.0)
    for b in range(c_ref.shape[0]):
        cvecs = jnp.where(row == 1 + b, c_ref[b:b + 1, :], cvecs)
    mod_ref[...] = _dot(_bf(_silu(cvecs)), _bf(wm_ref[...])) + bm_ref[...]
    wib_ref[...] = _bf(wi_ref[...])
    wob_ref[...] = _bf(wo_ref[...])


def _setup(c_ctx, c, w_mod, b_mod, w_in_t, w_out):
    n_mod = w_mod.shape[1]
    n_in = w_in_t.shape[0]
    tm = n_mod // SETUP_STEPS
    ti = pl.cdiv(n_in, SETUP_STEPS * 2 * SUBLANES) * 2 * SUBLANES
    to = w_out.shape[0] // SETUP_STEPS
    assert tm % LANES == 0 and tm * SETUP_STEPS == n_mod
    rows = lambda r: pl.BlockSpec((r, D_MODEL), lambda i: (i, 0))
    cols = lambda r: pl.BlockSpec((r, tm), lambda i: (0, i))
    return pl.pallas_call(
        _setup_kernel,
        grid=(SETUP_STEPS,),
        in_specs=[pl.BlockSpec(c_ctx.shape, lambda i: (0, 0)), pl.BlockSpec(c.shape, lambda i: (0, 0)),
                  cols(D_MODEL), cols(1), rows(ti), rows(to)],
        out_specs=[cols(MOD_ROWS), rows(ti), rows(to)],
        out_shape=[jax.ShapeDtypeStruct((MOD_ROWS, n_mod), F32), jax.ShapeDtypeStruct(w_in_t.shape, BF16),
                   jax.ShapeDtypeStruct(w_out.shape, BF16)],
        compiler_params=_params(1),
        name="setup",
    )(c_ctx, c, w_mod, b_mod, w_in_t, w_out)


def _log_sigmoid(z):
    return jnp.minimum(z, 0.0) - jnp.log1p(jnp.exp(-jnp.abs(z)))


def _silu(x):
    h = 0.5 * x
    return h + h * jnp.tanh(h)


def _rope_lanes(x, cos, sin):
    lane = _iota(x.shape, 1)
    up = pltpu.roll(x, ROPE_PAIRS, axis=1)
    dn = pltpu.roll(x, LANES - ROPE_PAIRS, axis=1)
    partner = jnp.where((lane & ROPE_PAIRS) == 0, dn, up)
    return x * cos + partner * sin


def _rope_rows(x, cos, sin):
    p = ROPE_PAIRS
    parts = []
    for g in range(x.shape[0] // (2 * p)):
        parts += [x[(2 * g + 1) * p:(2 * g + 2) * p], x[2 * g * p:(2 * g + 1) * p]]
    return x * cos + jnp.concatenate(parts, axis=0) * sin


def _projection_phase(tiles, gain, shift, scale, wt_ref, wa, ba, sc, rope_refs):
    hbs = {}

    def normalise(i):
        x = tiles[i][2]()
        ms = jnp.mean(x * x, axis=-1, keepdims=True)
        h = x * lax.rsqrt(ms + EPS) * gain
        hbs[i] = _bf(h * (1.0 + scale) + shift)

    pending = []

    def issue(matmul, finish):
        r = matmul()
        while pending:
            pending.pop()()
        pending.append(lambda: finish(r))

    normalise(0)
    for i, (row0, n_rows, _, kt_store, dv_store) in enumerate(tiles):
        rows = pl.ds(row0, n_rows)
        hb = hbs.pop(i)

        def sec(a, b, hb=hb):
            return lambda: _dot_nt(hb, wt_ref[a:b, :])

        def put(name, f, rows=rows):
            def finish(r):
                sc[name][rows, :] = f(r)
            return finish

        def finish_kl(klt, rows=rows, kt_store=kt_store):
            z = _dot_tn(_bf(klt[DIFF_QK:]), wa) + ba
            sc["g"][rows, :] = _log_sigmoid(z) * (1.0 / GLA_GATE_TEMP)
            if rope_refs is None:
                kt_store(slice(0, DIFF_QK), klt[:DIFF_QK])
            else:
                cost, sint = rope_refs[2][:, rows], rope_refs[3][:, rows]
                for hd in range(DIFF_HEADS):
                    sl = slice(hd * LANES, (hd + 1) * LANES)
                    kt_store(sl, _rope_rows(klt[sl, :], cost, sint))

        def finish_dq(dq, rows=rows):
            dq = dq * (DIFF_DH ** -0.5 * LOG2E)
            if rope_refs is None:
                sc["dq"][rows, :] = _bf(dq)
            else:
                cos, sin = rope_refs[0][rows, :], rope_refs[1][rows, :]
                for hd in range(DIFF_HEADS):
                    sl = slice(hd * LANES, (hd + 1) * LANES)
                    sc["dq"][rows, sl] = _bf(_rope_lanes(dq[:, sl], cos, sin))

        issue(sec(_R_DV, _R_GATE), dv_store)
        issue(sec(_R_GATE, _R_END), put("gate", _silu))
        if i + 1 < len(tiles):
            normalise(i + 1)
        issue(lambda hb=hb: _dot_nt(jnp.concatenate([wt_ref[_R_DK:_R_DV, :], wt_ref[_R_LR:_R_DQ, :]], axis=0), hb),
              finish_kl)
        issue(sec(_R_DQ, _R_DK), finish_dq)
        issue(sec(_R_GV, _R_LR), put("gv", _bf))
        issue(sec(_R_GQ, _R_GK), put("gq", lambda r: r * (GLA_DK ** -0.5)))
        issue(sec(_R_GK, _R_GV), put("gk", lambda r: r))
    while pending:
        pending.pop()()


def _tile4(x):
    return jnp.concatenate([x, x, x, x], axis=0)


def _pair_mask():
    return (_iota((GLA_PAIR_K, GLA_PAIR_V), 0) // GLA_DK) == (_iota((GLA_PAIR_K, GLA_PAIR_V), 1) // GLA_DV)


def _gla_recurrence_step(qi, kl, vb, e_col, s_ref, b, d, p_mask):
    inter = []
    for p in range(2):
        ks = slice(p * GLA_PAIR_K, (p + 1) * GLA_PAIR_K)
        s = s_ref[b, d, p]
        inter.append(_dot(qi[:, ks], _bf(s)))
        upd = _dot_tn(kl[:, ks], vb[:, p * GLA_PAIR_V:(p + 1) * GLA_PAIR_V])
        s_ref[b, d, p] = s * e_col[ks] + jnp.where(p_mask, upd, 0.0)
    return jnp.concatenate(inter, axis=1)


def _gla_fast(q_ref, k_ref, v_ref, g_ref, o_ref, s_ref, qi_ref, kl_ref, et_ref, *, seq_len, n_seq):
    c_len, sc_len = GLA_CHUNK, GLA_SUPER
    nc = seq_len // c_len
    cps = sc_len // c_len
    n_super = n_seq * seq_len // sc_len
    r = _iota((sc_len, 2 * sc_len), 0)
    c = _iota((sc_len, 2 * sc_len), 1) & (sc_len - 1)
    same = (r // c_len) == (c // c_len)
    tri2 = [(same & (c <= r)).astype(BF16), (same & (c >= r)).astype(BF16)]
    rt = _iota((c_len, GLA_HEADS * c_len), 0)
    cs = _iota((c_len, GLA_HEADS * c_len), 1) & (c_len - 1)
    keep = [cs <= rt, cs >= rt]
    k_mask = (_iota((GLA_HEADS * c_len, GLA_QK), 0) // c_len) == (_iota((GLA_HEADS * c_len, GLA_QK), 1) // GLA_DK)
    v_mask = (_iota((GLA_HEADS * c_len, GLA_WIDTH), 0) // c_len) == (_iota((GLA_HEADS * c_len, GLA_WIDTH), 1) // GLA_DV)
    p_mask = _pair_mask()
    zb = jnp.zeros((), BF16)
    dirs = range(2)

    for g0 in range(0, n_super, GLA_GROUP):
        group = range(g0, min(g0 + GLA_GROUP, n_super))
        rows = {u: slice(u * sc_len, (u + 1) * sc_len) for u in group}
        bs = {}
        for u in group:
            for d in dirs:
                g = g_ref[rows[u], d * GLA_QK:(d + 1) * GLA_QK]
                hi = _bf(g)
                lo = _bf(g - hi.astype(F32))
                bs[u, d] = _dot(tri2[d], jnp.concatenate([hi, lo], axis=0))
        qib, kob = {}, {}
        for u in group:
            q = q_ref[rows[u], :]
            k = k_ref[rows[u], :]
            for d in dirs:
                b = bs[u, d]
                qi = _bf(q * jnp.exp(b))
                ko = k * jnp.exp(-b)
                kls = []
                for j in range(cps):
                    last = j * c_len + (0 if d else c_len - 1)
                    e_tot = jnp.exp(b[last:last + 1, :])
                    et_ref[d, u * cps + j] = jnp.broadcast_to(e_tot, (SUBLANES, GLA_QK))
                    kls.append(_bf(ko[j * c_len:(j + 1) * c_len] * e_tot))
                qib[u, d], kob[u, d] = qi, _bf(ko)
                qi_ref[d, rows[u], :] = qi
                kl_ref[d, rows[u], :] = jnp.concatenate(kls, axis=0)
        atts = {}
        for u in group:
            for j in range(cps):
                cr = slice(j * c_len, (j + 1) * c_len)
                for d in dirs:
                    atts[u, j, d] = _dot_nt(qib[u, d][cr], jnp.where(k_mask, _tile4(kob[u, d][cr]), zb))
        for u in group:
            vb = v_ref[rows[u], :]
            for j in range(cps):
                a2 = jnp.concatenate([_bf(jnp.where(keep[d], atts[u, j, d], 0.0)) for d in dirs], axis=0)
                v_bd = jnp.where(v_mask, _tile4(vb[j * c_len:(j + 1) * c_len]), zb)
                r2 = _dot(a2, v_bd)
                o_ref[u * sc_len + j * c_len:u * sc_len + (j + 1) * c_len, :] = r2[:c_len] + r2[c_len:]

    for i in range(nc):
        for b in range(n_seq):
            for d in dirs:
                ci = b * nc + ((nc - 1 - i) if d else i)
                rows_c = slice(ci * c_len, (ci + 1) * c_len)
                e_col = et_ref[d, ci].T[:, 0:1]
                o_ref[rows_c, :] += _gla_recurrence_step(qi_ref[d, rows_c, :], kl_ref[d, rows_c, :],
                                                         v_ref[rows_c, :], e_col, s_ref, b, d, p_mask)


def _gla_slow(q_ref, k_ref, v_ref, g_ref, o_ref, s_ref, b_ref, vf_ref, *, seq_len, n_seq):
    c_len = GLA_CHUNK
    nc = seq_len // c_len
    r64 = _iota((c_len, c_len), 0)
    c64 = _iota((c_len, c_len), 1)
    p_mask = _pair_mask()
    expand = ((_iota((GLA_QK, GLA_WIDTH), 0) // GLA_DK) == (_iota((GLA_QK, GLA_WIDTH), 1) // GLA_DV)).astype(BF16)
    t_idx = _iota((c_len, GLA_QK), 0)
    for b in range(n_seq):
        for d in range(2):
            reverse = bool(d)
            tri = ((c64 >= r64) if reverse else (c64 <= r64)).astype(F32)

            def step(i, carry, b=b, d=d, reverse=reverse, tri=tri):
                ci = b * nc + ((nc - 1 - i) if reverse else i)
                base = pl.multiple_of(ci * c_len, c_len)
                rows = pl.ds(base, c_len)
                q = q_ref[rows, :]
                k = k_ref[rows, :]
                vb = v_ref[rows, :]
                g = g_ref[rows, d * GLA_QK:(d + 1) * GLA_QK]
                bcum = jnp.dot(tri, g, precision=lax.Precision.HIGHEST, preferred_element_type=F32)
                b_tot = jnp.sum(g, axis=0, keepdims=True)
                b_ref[...] = bcum
                vf_ref[...] = vb.astype(F32)

                def key_row(s, acc):
                    b_s = b_ref[pl.ds(s, 1), :]
                    k_s = k_ref[pl.ds(base + s, 1), :]
                    v_s = vf_ref[pl.ds(s, 1), :]
                    visible = (t_idx <= s) if reverse else (t_idx >= s)
                    w = jnp.where(visible, q * jnp.exp(jnp.minimum(bcum - b_s, 0.0)), 0.0) * k_s
                    return acc + _dot(_bf(w), expand) * v_s

                o = lax.fori_loop(0, c_len, key_row, jnp.zeros((c_len, GLA_WIDTH), F32))
                e_col = jnp.broadcast_to(jnp.exp(b_tot), (SUBLANES, GLA_QK)).T[:, 0:1]
                o = o + _gla_recurrence_step(_bf(q * jnp.exp(bcum)), _bf(k * jnp.exp(b_tot - bcum)), vb, e_col,
                                             s_ref, b, d, p_mask)
                if reverse:
                    o_ref[rows, :] += o
                else:
                    o_ref[rows, :] = o
                return carry

            lax.fori_loop(0, nc, step, 0)


def _gla_sequences(q_ref, k_ref, v_ref, g_ref, o_ref, gs, init_states, state_out, *, seq_len, n_seq):
    s_ref = gs["s"]
    zero = jnp.zeros((GLA_DK, GLA_DV), F32)
    for b in range(n_seq):
        for d in range(2):
            for p in range(2):
                if init_states is not None:
                    s0 = init_states[b][d]
                    top = jnp.concatenate([s0[2 * p], zero], axis=1)
                    bot = jnp.concatenate([zero, s0[2 * p + 1]], axis=1)
                    s_ref[b, d, p] = jnp.concatenate([top, bot], axis=0)
                else:
                    s_ref[b, d, p] = jnp.zeros((GLA_PAIR_K, GLA_PAIR_V), F32)
    worst = jnp.zeros((1, 2 * GLA_QK), F32)
    for ci in range(n_seq * seq_len // GLA_CHUNK):
        worst = jnp.minimum(worst, jnp.sum(g_ref[ci * GLA_CHUNK:(ci + 1) * GLA_CHUNK, :], axis=0, keepdims=True))
    lax.cond(jnp.min(worst) >= GLA_SAFE_LOG_DECAY,
             functools.partial(_gla_fast, q_ref, k_ref, v_ref, g_ref, o_ref, s_ref, gs["qi"], gs["kl"], gs["et"],
                               seq_len=seq_len, n_seq=n_seq),
             functools.partial(_gla_slow, q_ref, k_ref, v_ref, g_ref, o_ref, s_ref, gs["b"], gs["vf"],
                               seq_len=seq_len, n_seq=n_seq))
    if state_out is not None:
        for b in range(n_seq):
            for d in range(2):
                for h in range(GLA_HEADS):
                    p, j = divmod(h, 2)
                    state_out[b][d][h] = s_ref[b, d, p, j * GLA_DK:(j + 1) * GLA_DK, j * GLA_DV:(j + 1) * GLA_DV]


def _diff_blocks(tasks, lam, n_keys):
    kv_key, kv = None, None
    group_size = max(1, DIFF_GROUP_KEYS // n_keys)
    for g0 in range(0, len(tasks), group_size):
        group = []
        for key, load_q, load_kv, store in tasks[g0:g0 + group_size]:
            if key != kv_key:
                kts, vs = load_kv()
                kv_key, kv = key, (kts, [_with_ones(v) for v in vs])
            group.append((load_q(), kv[0], kv[1], store))
        scores = []
        for q, kts, _, _ in group:
            first = _iota(q.shape, 1) < DIFF_DH
            zb = jnp.zeros((), BF16)
            qs = jnp.concatenate([jnp.where(first, q, zb), jnp.where(first, zb, q)], axis=0)
            scores.append([_dot(qs, kt) for kt in kts])
        maxes = [functools.reduce(jnp.maximum, [jnp.max(s, axis=-1, keepdims=True) for s in ss]) for ss in scores]
        es = [[_bf(jnp.exp2(s - m)) for s in ss] for ss, m in zip(scores, maxes)]
        for (q, _, vas, store), e_list in zip(group, es):
            tq = q.shape[0]
            r = functools.reduce(jnp.add, [_dot(e, va) for e, va in zip(e_list, vas)])
            n = r[:, :DIFF_DV] / r[:, DIFF_DV:]
            store(n[:tq] - lam * n[tq:])


def _diff_lambda(lam_ref, lam_init):
    lp = lam_ref[...]
    return (jnp.exp(jnp.sum(lp[0:1] * lp[1:2], axis=-1, keepdims=True))
            - jnp.exp(jnp.sum(lp[2:3] * lp[3:4], axis=-1, keepdims=True)) + lam_init)


def _with_ones(v):
    return jnp.concatenate([v, jnp.ones(v.shape, v.dtype)], axis=1)


def _head_rmsnorm(x, gain):
    ms = jnp.mean(x * x, axis=-1, keepdims=True)
    return x * lax.rsqrt(ms + EPS) * gain


def _merge_rows(og_ref, od_ref, sg_ref, x, gt, gg, dg, wo_ref, fg, rows):
    slabs = []
    for h in range(GLA_HEADS):
        slabs.append(_head_rmsnorm(og_ref[rows, h * GLA_DV:(h + 1) * GLA_DV], gg))
    for h in range(DIFF_HEADS):
        slabs.append(_head_rmsnorm(od_ref[rows, h * DIFF_DV:(h + 1) * DIFF_DV], dg))
    o = jnp.concatenate(slabs, axis=-1) * sg_ref[rows, :]
    xn = x + gt * _dot(_bf(o), wo_ref[...])
    ms = jnp.mean(xn * xn, axis=-1, keepdims=True)
    return xn * lax.rsqrt(ms + EPS) * fg


def _layer_kernel(*refs, seq_len, n_seq, rope, has_cache, lam_init):
    it = iter(refs)
    x_ref, gain_ref, mod_ref, wt_ref, wo_ref, wa_ref, ba_ref, lam_ref, gg_ref, dg_ref, fg_ref = (next(it) for _ in range(11))
    rope_refs = tuple(next(it) for _ in range(4)) if rope else None
    if has_cache:
        ckt_ref, cv_ref, s0f_ref, s0b_ref = (next(it) for _ in range(4))
    y_ref = next(it)
    if not has_cache:
        kt_out, dv_out, sf_out, sb_out = (next(it) for _ in range(4))
    names = ["gq", "gk", "gv", "g", "dq", "dv", "gate", "og", "od"] + (["dkt"] if has_cache else [])
    sc = {n: next(it) for n in names}
    gs = {n: next(it) for n in ["s", "qi", "kl", "et", "b", "vf"]}

    d = D_MODEL
    n_tok = seq_len * n_seq
    mod_row = (1 + pl.program_id(0)) if has_cache else 0
    shift = mod_ref[pl.ds(mod_row, 1), 0:d]
    scale = mod_ref[pl.ds(mod_row, 1), d:2 * d]
    gt = mod_ref[pl.ds(mod_row, 1), 2 * d:3 * d]
    gain = gain_ref[...]

    zero = jnp.zeros((GLA_GATE_RANK, GLA_QK), F32)
    wa = _bf(jnp.concatenate([jnp.concatenate([wa_ref[0], zero], axis=1),
                              jnp.concatenate([zero, wa_ref[1]], axis=1)], axis=0))
    ba = jnp.concatenate([ba_ref[0:1, :], ba_ref[1:2, :]], axis=1)

    tiles = []
    for t0 in range(0, n_tok, ROW_TILE):
        if has_cache:
            def kt_store(sl, val, t0=t0):
                sc["dkt"][sl, t0:t0 + ROW_TILE] = _bf(val)

            def dv_store(val, t0=t0):
                sc["dv"][t0:t0 + ROW_TILE, :] = _bf(val)
        else:
            def kt_store(sl, val, t0=t0):
                for j in range(ROW_TILE // seq_len):
                    kt_out[(t0 // seq_len) + j, sl, :] = val[:, j * seq_len:(j + 1) * seq_len]

            def dv_store(val, t0=t0):
                sc["dv"][t0:t0 + ROW_TILE, :] = _bf(val)
                for j in range(ROW_TILE // seq_len):
                    for hd in range(DIFF_HEADS):
                        dv_out[(t0 // seq_len) + j, :, hd, :] = val[j * seq_len:(j + 1) * seq_len,
                                                                   hd * DIFF_DV:(hd + 1) * DIFF_DV]

        tiles.append((t0, ROW_TILE, lambda t0=t0: x_ref[t0:t0 + ROW_TILE, :], kt_store, dv_store))
    _projection_phase(tiles, gain, shift, scale, wt_ref, wa, ba, sc, rope_refs)

    init = [(s0f_ref.at[0], s0b_ref.at[0])] if has_cache else None
    out = None if has_cache else [(sf_out.at[b], sb_out.at[b]) for b in range(n_seq)]
    _gla_sequences(sc["gq"], sc["gk"], sc["gv"], sc["g"], sc["og"], gs, init, out, seq_len=seq_len, n_seq=n_seq)

    lam = _diff_lambda(lam_ref, lam_init)
    tasks = []
    for b in range(n_seq):
        for hd in range(DIFF_HEADS):
            sl = slice(hd * LANES, (hd + 1) * LANES)

            def load_kv(b=b, hd=hd, sl=sl):
                if has_cache:
                    return ([_bf(ckt_ref[0, sl, :]), sc["dkt"][sl, :]], [_bf(cv_ref[0, :, sl]), sc["dv"][:, sl]])
                return [_bf(kt_out[b, sl, :])], [sc["dv"][b * seq_len:(b + 1) * seq_len, sl]]

            for q0 in range(0, seq_len, DIFF_QB):
                r0 = b * seq_len + q0

                def load_q(r0=r0, sl=sl):
                    return sc["dq"][r0:r0 + DIFF_QB, sl]

                def store(val, r0=r0, sl=sl):
                    sc["od"][r0:r0 + DIFF_QB, sl] = val

                tasks.append(((b, hd), load_q, load_kv, store))
    _diff_blocks(tasks, lam, seq_len + (ckt_ref.shape[2] if has_cache else 0))

    gg = gg_ref[...]
    dg = dg_ref[...] * (1.0 - lam_init)
    fg = fg_ref[...]
    for t0 in range(0, n_tok, ROW_TILE):
        rows = slice(t0, t0 + ROW_TILE)
        y_ref[rows, :] = _merge_rows(sc["og"], sc["od"], sc["gate"], x_ref[rows, :], gt, gg, dg, wo_ref, fg, rows)


def _layer(x2d, seq_len, n_seq, gain, mod, wt, wo, wa, ba, lam_p, gg, dg, fg, lam_init, rope_tabs=None, cache=None):
    t = x2d.shape[0]
    n_tok = seq_len * n_seq
    n_steps = t // n_tok
    rope = rope_tabs is not None
    has_cache = cache is not None
    assert n_tok % ROW_TILE == 0 and ROW_TILE % seq_len in (0, ROW_TILE) and seq_len % GLA_SUPER == 0

    def whole(a, single=False):
        kw = {"pipeline_mode": pl.Buffered(1)} if single else {}
        return pl.BlockSpec(a.shape, lambda i: (0,) * a.ndim, **kw)

    io_kw = {"pipeline_mode": pl.Buffered(1)} if n_steps <= 2 else {}
    args = [x2d, gain, mod, wt, wo, wa, ba, lam_p, gg, dg, fg]
    in_specs = [pl.BlockSpec((n_tok, D_MODEL), lambda i: (i, 0), **io_kw), whole(gain), whole(mod), whole(wt, True),
                whole(wo, True), whole(wa), whole(ba), whole(lam_p), whole(gg), whole(dg), whole(fg)]
    if rope:
        args += list(rope_tabs)
        in_specs += [whole(a, True) for a in rope_tabs]
    sds = jax.ShapeDtypeStruct
    out_shape = [sds((t, D_MODEL), F32)]
    out_specs = [pl.BlockSpec((n_tok, D_MODEL), lambda i: (i, 0), **io_kw)]
    if has_cache:
        ckt, cv, s0f, s0b = cache
        args += [ckt, cv, s0f, s0b]
        st_spec = pl.BlockSpec((1,) + s0f.shape[1:], lambda i: (i, 0, 0, 0))
        in_specs += [pl.BlockSpec((1,) + ckt.shape[1:], lambda i: (i, 0, 0)),
                     pl.BlockSpec((1,) + cv.shape[1:], lambda i: (i, 0, 0)), st_spec, st_spec]
    else:
        n_b = t // seq_len
        out_shape += [sds((n_b, DIFF_QK, seq_len), F32), sds((n_b, seq_len, DIFF_HEADS, DIFF_DV), F32),
                      sds((n_b, GLA_HEADS, GLA_DK, GLA_DV), F32), sds((n_b, GLA_HEADS, GLA_DK, GLA_DV), F32)]
        st_spec = pl.BlockSpec((n_seq, GLA_HEADS, GLA_DK, GLA_DV), lambda i: (i, 0, 0, 0))
        out_specs += [pl.BlockSpec((n_seq, DIFF_QK, seq_len), lambda i: (i, 0, 0)),
                      pl.BlockSpec((n_seq, seq_len, DIFF_HEADS, DIFF_DV), lambda i: (i, 0, 0, 0)), st_spec, st_spec]
    vm = pltpu.VMEM
    scratch = [vm((n_tok, GLA_QK), F32), vm((n_tok, GLA_QK), F32), vm((n_tok, GLA_WIDTH), BF16),
               vm((n_tok, 2 * GLA_QK), F32), vm((n_tok, DIFF_QK), BF16), vm((n_tok, DIFF_WIDTH), BF16),
               vm((n_tok, MIX_WIDTH), F32), vm((n_tok, GLA_WIDTH), F32), vm((n_tok, DIFF_WIDTH), F32)]
    if has_cache:
        scratch += [vm((DIFF_QK, n_tok), BF16)]
    scratch += [vm((n_seq, 2, 2, GLA_PAIR_K, GLA_PAIR_V), F32),
                vm((2, n_tok, GLA_QK), BF16),
                vm((2, n_tok, GLA_QK), BF16),
                vm((2, n_tok // GLA_CHUNK, SUBLANES, GLA_QK), F32),
                vm((GLA_CHUNK, GLA_QK), F32),
                vm((GLA_CHUNK, GLA_WIDTH), F32)]
    return pl.pallas_call(
        functools.partial(_layer_kernel, seq_len=seq_len, n_seq=n_seq, rope=rope, has_cache=has_cache,
                          lam_init=lam_init),
        grid=(n_steps,),
        in_specs=in_specs,
        out_specs=out_specs,
        out_shape=out_shape,
        scratch_shapes=scratch,
        compiler_params=_params(1),
        name="layer_lat" if has_cache else "layer_ctx",
    )(*args)


def _rope_tables(seq_len):
    t = np.arange(seq_len)
    inv_freq = ROPE_BASE ** (-np.arange(ROPE_PAIRS, dtype=np.float64) / ROPE_PAIRS)
    ang_r = (t // GRID_W)[:, None] * inv_freq[None, :]
    ang_c = (t % GRID_W)[:, None] * inv_freq[None, :]
    cr, sr, cc, sc = np.cos(ang_r), np.sin(ang_r), np.cos(ang_c), np.sin(ang_c)
    cos = np.concatenate([cr, cr, cc, cc] * 2, axis=-1).astype(np.float32)
    sin = np.concatenate([-sr, sr, -sc, sc] * 2, axis=-1).astype(np.float32)
    return (jnp.asarray(cos), jnp.asarray(sin),
            jnp.asarray(np.ascontiguousarray(cos.T)), jnp.asarray(np.ascontiguousarray(sin.T)))


def kernel(x_prompt, x_sample, cache_diff_k, cache_diff_v, state_gla_fwd, state_gla_bwd, c, c_ctx,
           norm_gain, w_mod, b_mod, w_in, w_gla_alpha, b_gla_alpha, diff_lambda,
           gla_head_gain, diff_head_gain, w_out, final_gain):
    bp, lp, d = x_prompt.shape
    bs, ls, _ = x_sample.shape
    depth = norm_gain.shape[0]
    assert depth == 1 and d == D_MODEL and bs + 1 <= MOD_ROWS and w_in.shape[2] == _R_END
    l = 0
    lam_init = 0.8 - 0.6 * math.exp(-0.3 * l)

    mod, wt, wo = _setup(c_ctx[None, :], c, w_mod[l], b_mod, jnp.swapaxes(w_in[l], 0, 1), w_out[l])
    shared = (norm_gain, mod, wt, wo, w_gla_alpha[l], b_gla_alpha[l], diff_lambda[l],
              gla_head_gain, diff_head_gain, final_gain[None, :], lam_init)

    y_p, dkt, dv, s_f, s_b = _layer(x_prompt.reshape(bp * lp, d), lp, ROW_TILE // lp, *shared)
    y_prompt = y_p.reshape(bp, lp, d)
    new_diff_k = jnp.transpose(dkt.reshape(bp, DIFF_HEADS, 2, DIFF_DH, lp), (0, 4, 1, 2, 3))[:, None]
    new_diff_v = dv[:, None]
    new_gla_fwd = s_f[:, None]
    new_gla_bwd = s_b[:, None]

    past = cache_diff_k.shape[2]
    ckt = jnp.transpose(cache_diff_k[:, l], (0, 2, 3, 4, 1)).reshape(bs, DIFF_QK, past)
    cv = cache_diff_v[:, l].reshape(bs, past, DIFF_WIDTH)
    y_s = _layer(x_sample.reshape(bs * ls, d), ls, 1, *shared, rope_tabs=_rope_tables(ls),
                 cache=(ckt, cv, state_gla_fwd[:, l], state_gla_bwd[:, l]))[0]
    y_sample = y_s.reshape(bs, ls, d)

    return (y_prompt, y_sample, new_diff_k, new_diff_v, new_gla_fwd, new_gla_bwd)
```

```python
import functools
import math

import numpy as np
import jax
import jax.numpy as jnp
from jax import lax
from jax.experimental import pallas as pl
from jax.experimental.pallas import tpu as pltpu

F32 = jnp.float32
BF16 = jnp.bfloat16

D_MODEL = 1024
GRID_W = 64
GLA_HEADS = 4
GLA_DK = 64
GLA_DV = 128
GLA_QK = GLA_HEADS * GLA_DK
GLA_WIDTH = GLA_HEADS * GLA_DV
GLA_GATE_RANK = 16
GLA_GATE_TEMP = 16.0
GLA_CHUNK = 64
DIFF_HEADS = 4
DIFF_DH = 64
DIFF_DV = 2 * DIFF_DH
DIFF_QK = DIFF_HEADS * 2 * DIFF_DH
DIFF_WIDTH = DIFF_HEADS * DIFF_DV
MIX_WIDTH = GLA_WIDTH + DIFF_WIDTH
ROPE_PAIRS = DIFF_DH // 4
ROPE_BASE = 10000.0
EPS = 1e-6
LOG2E = math.log2(math.e)

LANES = 128
SUBLANES = 8
MOD_ROWS = 8
ROW_TILE = 512
SETUP_STEPS = 8
DIFF_QB = 256
DIFF_GROUP_KEYS = 4096
GLA_SUPER = 256
GLA_GROUP = 2
GLA_PAIR_K = 2 * GLA_DK
GLA_PAIR_V = 2 * GLA_DV
GLA_SAFE_LOG_DECAY = -40.0
VMEM_LIMIT = 58 * 1024 * 1024

_R_GQ = 0
_R_GK = _R_GQ + GLA_QK
_R_GV = _R_GK + GLA_QK
_R_LR = _R_GV + GLA_WIDTH
_R_DQ = _R_LR + 2 * GLA_GATE_RANK
_R_DK = _R_DQ + DIFF_QK
_R_DV = _R_DK + DIFF_QK
_R_GATE = _R_DV + DIFF_WIDTH
_R_END = _R_GATE + MIX_WIDTH


def _bf(x):
    return x.astype(BF16)


def _dot(a, b):
    return jnp.dot(a, b, preferred_element_type=F32)


def _dot_nt(a, b):
    return lax.dot_general(a, b, (((1,), (1,)), ((), ())), preferred_element_type=F32)


def _dot_tn(a, b):
    return lax.dot_general(a, b, (((0,), (0,)), ((), ())), preferred_element_type=F32)


def _params(n_parallel=0, n_arbitrary=0):
    sem = ("parallel",) * n_parallel + ("arbitrary",) * n_arbitrary
    return pltpu.CompilerParams(dimension_semantics=sem, vmem_limit_bytes=VMEM_LIMIT)


def _iota(shape, axis):
    return lax.broadcasted_iota(jnp.int32, shape, axis)


def _rows(ref, start, size):
    return ref.at[pl.ds(start, size)]


def _setup_kernel(cc_ref, c_ref, wm_ref, bm_ref, wi_ref, wo_ref, mod_ref, wib_ref, wob_ref):
    row = _iota((MOD_ROWS, D_MODEL), 0)
    cvecs = jnp.where(row == 0, cc_ref[...], 0.0)
    for b in range(c_ref.shape[0]):
        cvecs = jnp.where(row == 1 + b, c_ref[b:b + 1, :], cvecs)
    mod_ref[...] = _dot(_bf(_silu(cvecs)), _bf(wm_ref[...])) + bm_ref[...]
    wib_ref[...] = _bf(wi_ref[...])
    wob_ref[...] = _bf(wo_ref[...])


def _setup(c_ctx, c, w_mod, b_mod, w_in_t, w_out):
    n_mod = w_mod.shape[1]
    n_in = w_in_t.shape[0]
    tm = n_mod // SETUP_STEPS
    ti = pl.cdiv(n_in, SETUP_STEPS * 2 * SUBLANES) * 2 * SUBLANES
    to = w_out.shape[0] // SETUP_STEPS
    assert tm % LANES == 0 and tm * SETUP_STEPS == n_mod
    rows = lambda r: pl.BlockSpec((r, D_MODEL), lambda i: (i, 0))
    cols = lambda r: pl.BlockSpec((r, tm), lambda i: (0, i))
    return pl.pallas_call(
        _setup_kernel,
        grid=(SETUP_STEPS,),
        in_specs=[pl.BlockSpec(c_ctx.shape, lambda i: (0, 0)), pl.BlockSpec(c.shape, lambda i: (0, 0)),
                  cols(D_MODEL), cols(1), rows(ti), rows(to)],
        out_specs=[cols(MOD_ROWS), rows(ti), rows(to)],
        out_shape=[jax.ShapeDtypeStruct((MOD_ROWS, n_mod), F32), jax.ShapeDtypeStruct(w_in_t.shape, BF16),
                   jax.ShapeDtypeStruct(w_out.shape, BF16)],
        compiler_params=_params(1),
        name="setup",
    )(c_ctx, c, w_mod, b_mod, w_in_t, w_out)


def _log_sigmoid(z):
    return jnp.minimum(z, 0.0) - jnp.log1p(jnp.exp(-jnp.abs(z)))


def _silu(x):
    h = 0.5 * x
    return h + h * jnp.tanh(h)


def _rope_lanes(x, cos, sin):
    lane = _iota(x.shape, 1)
    up = pltpu.roll(x, ROPE_PAIRS, axis=1)
    dn = pltpu.roll(x, LANES - ROPE_PAIRS, axis=1)
    partner = jnp.where((lane & ROPE_PAIRS) == 0, dn, up)
    return x * cos + partner * sin


def _rope_rows(x, cos, sin):
    p = ROPE_PAIRS
    parts = []
    for g in range(x.shape[0] // (2 * p)):
        parts += [x[(2 * g + 1) * p:(2 * g + 2) * p], x[2 * g * p:(2 * g + 1) * p]]
    return x * cos + jnp.concatenate(parts, axis=0) * sin


def _projection_phase(tiles, gain, shift, scale, wt_ref, wa, ba, sc, rope_refs):
    hbs = {}

    def normalise(i):
        x = tiles[i][2]()
        ms = jnp.mean(x * x, axis=-1, keepdims=True)
        h = x * lax.rsqrt(ms + EPS) * gain
        hbs[i] = _bf(h * (1.0 + scale) + shift)

    pending = []

    def issue(matmul, finish):
        r = matmul()
        while pending:
            pending.pop()()
        pending.append(lambda: finish(r))

    normalise(0)
    for i, (row0, n_rows, _, kt_store, dv_store) in enumerate(tiles):
        rows = pl.ds(row0, n_rows)
        hb = hbs.pop(i)

        def sec(a, b, hb=hb):
            return lambda: _dot_nt(hb, wt_ref[a:b, :])

        def put(name, f, rows=rows):
            def finish(r):
                sc[name][rows, :] = f(r)
            return finish

        def finish_kl(klt, rows=rows, kt_store=kt_store):
            z = _dot_tn(_bf(klt[DIFF_QK:]), wa) + ba
            sc["g"][rows, :] = _log_sigmoid(z) * (1.0 / GLA_GATE_TEMP)
            if rope_refs is None:
                kt_store(slice(0, DIFF_QK), klt[:DIFF_QK])
            else:
                cost, sint = rope_refs[2][:, rows], rope_refs[3][:, rows]
                for hd in range(DIFF_HEADS):
                    sl = slice(hd * LANES, (hd + 1) * LANES)
                    kt_store(sl, _rope_rows(klt[sl, :], cost, sint))

        def finish_dq(dq, rows=rows):
            dq = dq * (DIFF_DH ** -0.5 * LOG2E)
            if rope_refs is None:
                sc["dq"][rows, :] = _bf(dq)
            else:
                cos, sin = rope_refs[0][rows, :], rope_refs[1][rows, :]
                for hd in range(DIFF_HEADS):
                    sl = slice(hd * LANES, (hd + 1) * LANES)
                    sc["dq"][rows, sl] = _bf(_rope_lanes(dq[:, sl], cos, sin))

        issue(sec(_R_DV, _R_GATE), dv_store)
        issue(sec(_R_GATE, _R_END), put("gate", _silu))
        if i + 1 < len(tiles):
            normalise(i + 1)
        issue(lambda hb=hb: _dot_nt(jnp.concatenate([wt_ref[_R_DK:_R_DV, :], wt_ref[_R_LR:_R_DQ, :]], axis=0), hb),
              finish_kl)
        issue(sec(_R_DQ, _R_DK), finish_dq)
        issue(sec(_R_GV, _R_LR), put("gv", _bf))
        issue(sec(_R_GQ, _R_GK), put("gq", lambda r: r * (GLA_DK ** -0.5)))
        issue(sec(_R_GK, _R_GV), put("gk", lambda r: r))
    while pending:
        pending.pop()()


def _tile4(x):
    return jnp.concatenate([x, x, x, x], axis=0)


def _pair_mask():
    return (_iota((GLA_PAIR_K, GLA_PAIR_V), 0) // GLA_DK) == (_iota((GLA_PAIR_K, GLA_PAIR_V), 1) // GLA_DV)


def _gla_recurrence_step(qi, kl, vb, e_col, s_ref, b, d, p_mask):
    inter = []
    for p in range(2):
        ks = slice(p * GLA_PAIR_K, (p + 1) * GLA_PAIR_K)
        s = s_ref[b, d, p]
        inter.append(_dot(qi[:, ks], _bf(s)))
        upd = _dot_tn(kl[:, ks], vb[:, p * GLA_PAIR_V:(p + 1) * GLA_PAIR_V])
        s_ref[b, d, p] = s * e_col[ks] + jnp.where(p_mask, upd, 0.0)
    return jnp.concatenate(inter, axis=1)


def _gla_fast(q_ref, k_ref, v_ref, g_ref, o_ref, s_ref, qi_ref, kl_ref, et_ref, *, seq_len, n_seq):
    c_len, sc_len = GLA_CHUNK, GLA_SUPER
    nc = seq_len // c_len
    cps = sc_len // c_len
    n_super = n_seq * seq_len // sc_len
    r = _iota((sc_len, 2 * sc_len), 0)
    c = _iota((sc_len, 2 * sc_len), 1) & (sc_len - 1)
    same = (r // c_len) == (c // c_len)
    tri2 = [(same & (c <= r)).astype(BF16), (same & (c >= r)).astype(BF16)]
    rt = _iota((c_len, GLA_HEADS * c_len), 0)
    cs = _iota((c_len, GLA_HEADS * c_len), 1) & (c_len - 1)
    keep = [cs <= rt, cs >= rt]
    k_mask = (_iota((GLA_HEADS * c_len, GLA_QK), 0) // c_len) == (_iota((GLA_HEADS * c_len, GLA_QK), 1) // GLA_DK)
    v_mask = (_iota((GLA_HEADS * c_len, GLA_WIDTH), 0) // c_len) == (_iota((GLA_HEADS * c_len, GLA_WIDTH), 1) // GLA_DV)
    p_mask = _pair_mask()
    zb = jnp.zeros((), BF16)
    dirs = range(2)

    for g0 in range(0, n_super, GLA_GROUP):
        group = range(g0, min(g0 + GLA_GROUP, n_super))
        rows = {u: slice(u * sc_len, (u + 1) * sc_len) for u in group}
        bs = {}
        for u in group:
            for d in dirs:
                g = g_ref[rows[u], d * GLA_QK:(d + 1) * GLA_QK]
                hi = _bf(g)
                lo = _bf(g - hi.astype(F32))
                bs[u, d] = _dot(tri2[d], jnp.concatenate([hi, lo], axis=0))
        qib, kob = {}, {}
        for u in group:
            q = q_ref[rows[u], :]
            k = k_ref[rows[u], :]
            for d in dirs:
                b = bs[u, d]
                qi = _bf(q * jnp.exp(b))
                ko = k * jnp.exp(-b)
                kls = []
                for j in range(cps):
                    last = j * c_len + (0 if d else c_len - 1)
                    e_tot = jnp.exp(b[last:last + 1, :])
                    et_ref[d, u * cps + j] = jnp.broadcast_to(e_tot, (SUBLANES, GLA_QK))
                    kls.append(_bf(ko[j * c_len:(j + 1) * c_len] * e_tot))
                qib[u, d], kob[u, d] = qi, _bf(ko)
                qi_ref[d, rows[u], :] = qi
                kl_ref[d, rows[u], :] = jnp.concatenate(kls, axis=0)
        atts = {}
        for u in group:
            for j in range(cps):
                cr = slice(j * c_len, (j + 1) * c_len)
                for d in dirs:
                    atts[u, j, d] = _dot_nt(qib[u, d][cr], jnp.where(k_mask, _tile4(kob[u, d][cr]), zb))
        for u in group:
            vb = v_ref[rows[u], :]
            for j in range(cps):
                a2 = jnp.concatenate([_bf(jnp.where(keep[d], atts[u, j, d], 0.0)) for d in dirs], axis=0)
                v_bd = jnp.where(v_mask, _tile4(vb[j * c_len:(j + 1) * c_len]), zb)
                r2 = _dot(a2, v_bd)
                o_ref[u * sc_len + j * c_len:u * sc_len + (j + 1) * c_len, :] = r2[:c_len] + r2[c_len:]

    for i in range(nc):
        for b in range(n_seq):
            for d in dirs:
                ci = b * nc + ((nc - 1 - i) if d else i)
                rows_c = slice(ci * c_len, (ci + 1) * c_len)
                e_col = et_ref[d, ci].T[:, 0:1]
                o_ref[rows_c, :] += _gla_recurrence_step(qi_ref[d, rows_c, :], kl_ref[d, rows_c, :],
                                                         v_ref[rows_c, :], e_col, s_ref, b, d, p_mask)


def _gla_slow(q_ref, k_ref, v_ref, g_ref, o_ref, s_ref, b_ref, vf_ref, *, seq_len, n_seq):
    c_len = GLA_CHUNK
    nc = seq_len // c_len
    r64 = _iota((c_len, c_len), 0)
    c64 = _iota((c_len, c_len), 1)
    p_mask = _pair_mask()
    expand = ((_iota((GLA_QK, GLA_WIDTH), 0) // GLA_DK) == (_iota((GLA_QK, GLA_WIDTH), 1) // GLA_DV)).astype(BF16)
    t_idx = _iota((c_len, GLA_QK), 0)
    for b in range(n_seq):
        for d in range(2):
            reverse = bool(d)
            tri = ((c64 >= r64) if reverse else (c64 <= r64)).astype(F32)

            def step(i, carry, b=b, d=d, reverse=reverse, tri=tri):
                ci = b * nc + ((nc - 1 - i) if reverse else i)
                base = pl.multiple_of(ci * c_len, c_len)
                rows = pl.ds(base, c_len)
                q = q_ref[rows, :]
                k = k_ref[rows, :]
                vb = v_ref[rows, :]
                g = g_ref[rows, d * GLA_QK:(d + 1) * GLA_QK]
                bcum = jnp.dot(tri, g, precision=lax.Precision.HIGHEST, preferred_element_type=F32)
                b_tot = jnp.sum(g, axis=0, keepdims=True)
                b_ref[...] = bcum
                vf_ref[...] = vb.astype(F32)

                def key_row(s, acc):
                    b_s = b_ref[pl.ds(s, 1), :]
                    k_s = k_ref[pl.ds(base + s, 1), :]
                    v_s = vf_ref[pl.ds(s, 1), :]
                    visible = (t_idx <= s) if reverse else (t_idx >= s)
                    w = jnp.where(visible, q * jnp.exp(jnp.minimum(bcum - b_s, 0.0)), 0.0) * k_s
                    return acc + _dot(_bf(w), expand) * v_s

                o = lax.fori_loop(0, c_len, key_row, jnp.zeros((c_len, GLA_WIDTH), F32))
                e_col = jnp.broadcast_to(jnp.exp(b_tot), (SUBLANES, GLA_QK)).T[:, 0:1]
                o = o + _gla_recurrence_step(_bf(q * jnp.exp(bcum)), _bf(k * jnp.exp(b_tot - bcum)), vb, e_col,
                                             s_ref, b, d, p_mask)
                if reverse:
                    o_ref[rows, :] += o
                else:
                    o_ref[rows, :] = o
                return carry

            lax.fori_loop(0, nc, step, 0)


def _gla_sequences(q_ref, k_ref, v_ref, g_ref, o_ref, gs, init_states, state_out, *, seq_len, n_seq):
    s_ref = gs["s"]
    zero = jnp.zeros((GLA_DK, GLA_DV), F32)
    for b in range(n_seq):
        for d in range(2):
            for p in range(2):
                if init_states is not None:
                    s0 = init_states[b][d]
                    top = jnp.concatenate([s0[2 * p], zero], axis=1)
                    bot = jnp.concatenate([zero, s0[2 * p + 1]], axis=1)
                    s_ref[b, d, p] = jnp.concatenate([top, bot], axis=0)
                else:
                    s_ref[b, d, p] = jnp.zeros((GLA_PAIR_K, GLA_PAIR_V), F32)
    worst = jnp.zeros((1, 2 * GLA_QK), F32)
    for ci in range(n_seq * seq_len // GLA_CHUNK):
        worst = jnp.minimum(worst, jnp.sum(g_ref[ci * GLA_CHUNK:(ci + 1) * GLA_CHUNK, :], axis=0, keepdims=True))
    lax.cond(jnp.min(worst) >= GLA_SAFE_LOG_DECAY,
             functools.partial(_gla_fast, q_ref, k_ref, v_ref, g_ref, o_ref, s_ref, gs["qi"], gs["kl"], gs["et"],
                               seq_len=seq_len, n_seq=n_seq),
             functools.partial(_gla_slow, q_ref, k_ref, v_ref, g_ref, o_ref, s_ref, gs["b"], gs["vf"],
                               seq_len=seq_len, n_seq=n_seq))
    if state_out is not None:
        for b in range(n_seq):
            for d in range(2):
                for h in range(GLA_HEADS):
                    p, j = divmod(h, 2)
                    state_out[b][d][h] = s_ref[b, d, p, j * GLA_DK:(j + 1) * GLA_DK, j * GLA_DV:(j + 1) * GLA_DV]


def _diff_blocks(tasks, lam, n_keys):
    kv_key, kv = None, None
    group_size = max(1, DIFF_GROUP_KEYS // n_keys)
    for g0 in range(0, len(tasks), group_size):
        group = []
        for key, load_q, load_kv, store in tasks[g0:g0 + group_size]:
            if key != kv_key:
                kts, vs = load_kv()
                kv_key, kv = key, (kts, [_with_ones(v) for v in vs])
            group.append((load_q(), kv[0], kv[1], store))
        scores = []
        for q, kts, _, _ in group:
            first = _iota(q.shape, 1) < DIFF_DH
            zb = jnp.zeros((), BF16)
            qs = jnp.concatenate([jnp.where(first, q, zb), jnp.where(first, zb, q)], axis=0)
            scores.append([_dot(qs, kt) for kt in kts])
        maxes = [functools.reduce(jnp.maximum, [jnp.max(s, axis=-1, keepdims=True) for s in ss]) for ss in scores]
        es = [[_bf(jnp.exp2(s - m)) for s in ss] for ss, m in zip(scores, maxes)]
        for (q, _, vas, store), e_list in zip(group, es):
            tq = q.shape[0]
            r = functools.reduce(jnp.add, [_dot(e, va) for e, va in zip(e_list, vas)])
            n = r[:, :DIFF_DV] / r[:, DIFF_DV:]
            store(n[:tq] - lam * n[tq:])


def _diff_lambda(lam_ref, lam_init):
    lp = lam_ref[...]
    return (jnp.exp(jnp.sum(lp[0:1] * lp[1:2], axis=-1, keepdims=True))
            - jnp.exp(jnp.sum(lp[2:3] * lp[3:4], axis=-1, keepdims=True)) + lam_init)


def _with_ones(v):
    return jnp.concatenate([v, jnp.ones(v.shape, v.dtype)], axis=1)


def _head_rmsnorm(x, gain):
    ms = jnp.mean(x * x, axis=-1, keepdims=True)
    return x * lax.rsqrt(ms + EPS) * gain


def _merge_rows(og_ref, od_ref, sg_ref, x, gt, gg, dg, wo_ref, fg, rows):
    slabs = []
    for h in range(GLA_HEADS):
        slabs.append(_head_rmsnorm(og_ref[rows, h * GLA_DV:(h + 1) * GLA_DV], gg))
    for h in range(DIFF_HEADS):
        slabs.append(_head_rmsnorm(od_ref[rows, h * DIFF_DV:(h + 1) * DIFF_DV], dg))
    o = jnp.concatenate(slabs, axis=-1) * sg_ref[rows, :]
    xn = x + gt * _dot(_bf(o), wo_ref[...])
    ms = jnp.mean(xn * xn, axis=-1, keepdims=True)
    return xn * lax.rsqrt(ms + EPS) * fg


def _layer_kernel(*refs, seq_len, n_seq, rope, has_cache, lam_init):
    it = iter(refs)
    x_ref, gain_ref, mod_ref, wt_ref, wo_ref, wa_ref, ba_ref, lam_ref, gg_ref, dg_ref, fg_ref = (next(it) for _ in range(11))
    rope_refs = tuple(next(it) for _ in range(4)) if rope else None
    if has_cache:
        ckt_ref, cv_ref, s0f_ref, s0b_ref = (next(it) for _ in range(4))
    y_ref = next(it)
    if not has_cache:
        kt_out, dv_out, sf_out, sb_out = (next(it) for _ in range(4))
    names = ["gq", "gk", "gv", "g", "dq", "dv", "gate", "og", "od"] + (["dkt"] if has_cache else [])
    sc = {n: next(it) for n in names}
    gs = {n: next(it) for n in ["s", "qi", "kl", "et", "b", "vf"]}

    d = D_MODEL
    n_tok = seq_len * n_seq
    mod_row = (1 + pl.program_id(0)) if has_cache else 0
    shift = mod_ref[pl.ds(mod_row, 1), 0:d]
    scale = mod_ref[pl.ds(mod_row, 1), d:2 * d]
    gt = mod_ref[pl.ds(mod_row, 1), 2 * d:3 * d]
    gain = gain_ref[...]

    zero = jnp.zeros((GLA_GATE_RANK, GLA_QK), F32)
    wa = _bf(jnp.concatenate([jnp.concatenate([wa_ref[0], zero], axis=1),
                              jnp.concatenate([zero, wa_ref[1]], axis=1)], axis=0))
    ba = jnp.concatenate([ba_ref[0:1, :], ba_ref[1:2, :]], axis=1)

    tiles = []
    for t0 in range(0, n_tok, ROW_TILE):
        if has_cache:
            def kt_store(sl, val, t0=t0):
                sc["dkt"][sl, t0:t0 + ROW_TILE] = _bf(val)

            def dv_store(val, t0=t0):
                sc["dv"][t0:t0 + ROW_TILE, :] = _bf(val)
        else:
            def kt_store(sl, val, t0=t0):
                for j in range(ROW_TILE // seq_len):
                    kt_out[(t0 // seq_len) + j, sl, :] = val[:, j * seq_len:(j + 1) * seq_len]

            def dv_store(val, t0=t0):
                sc["dv"][t0:t0 + ROW_TILE, :] = _bf(val)
                for j in range(ROW_TILE // seq_len):
                    dv_out[(t0 // seq_len) + j] = val[j * seq_len:(j + 1) * seq_len, :].reshape(
                        seq_len, DIFF_HEADS, DIFF_DV)

        tiles.append((t0, ROW_TILE, lambda t0=t0: x_ref[t0:t0 + ROW_TILE, :], kt_store, dv_store))
    _projection_phase(tiles, gain, shift, scale, wt_ref, wa, ba, sc, rope_refs)

    init = [(s0f_ref.at[0], s0b_ref.at[0])] if has_cache else None
    out = None if has_cache else [(sf_out.at[b], sb_out.at[b]) for b in range(n_seq)]
    _gla_sequences(sc["gq"], sc["gk"], sc["gv"], sc["g"], sc["og"], gs, init, out, seq_len=seq_len, n_seq=n_seq)

    lam = _diff_lambda(lam_ref, lam_init)
    tasks = []
    for b in range(n_seq):
        for hd in range(DIFF_HEADS):
            sl = slice(hd * LANES, (hd + 1) * LANES)

            def load_kv(b=b, hd=hd, sl=sl):
                if has_cache:
                    return ([_bf(ckt_ref[0, sl, :]), sc["dkt"][sl, :]], [_bf(cv_ref[0, :, sl]), sc["dv"][:, sl]])
                return [_bf(kt_out[b, sl, :])], [sc["dv"][b * seq_len:(b + 1) * seq_len, sl]]

            for q0 in range(0, seq_len, DIFF_QB):
                r0 = b * seq_len + q0

                def load_q(r0=r0, sl=sl):
                    return sc["dq"][r0:r0 + DIFF_QB, sl]

                def store(val, r0=r0, sl=sl):
                    sc["od"][r0:r0 + DIFF_QB, sl] = val

                tasks.append(((b, hd), load_q, load_kv, store))
    _diff_blocks(tasks, lam, seq_len + (ckt_ref.shape[2] if has_cache else 0))

    gg = gg_ref[...]
    dg = dg_ref[...] * (1.0 - lam_init)
    fg = fg_ref[...]
    for t0 in range(0, n_tok, ROW_TILE):
        rows = slice(t0, t0 + ROW_TILE)
        y_ref[rows, :] = _merge_rows(sc["og"], sc["od"], sc["gate"], x_ref[rows, :], gt, gg, dg, wo_ref, fg, rows)


def _layer(x2d, seq_len, n_seq, gain, mod, wt, wo, wa, ba, lam_p, gg, dg, fg, lam_init, rope_tabs=None, cache=None):
    t = x2d.shape[0]
    n_tok = seq_len * n_seq
    n_steps = t // n_tok
    rope = rope_tabs is not None
    has_cache = cache is not None
    assert n_tok % ROW_TILE == 0 and ROW_TILE % seq_len in (0, ROW_TILE) and seq_len % GLA_SUPER == 0

    def whole(a, single=False):
        kw = {"pipeline_mode": pl.Buffered(1)} if single else {}
        return pl.BlockSpec(a.shape, lambda i: (0,) * a.ndim, **kw)

    io_kw = {"pipeline_mode": pl.Buffered(1)} if n_steps <= 2 else {}
    args = [x2d, gain, mod, wt, wo, wa, ba, lam_p, gg, dg, fg]
    in_specs = [pl.BlockSpec((n_tok, D_MODEL), lambda i: (i, 0), **io_kw), whole(gain), whole(mod), whole(wt, True),
                whole(wo, True), whole(wa), whole(ba), whole(lam_p), whole(gg), whole(dg), whole(fg)]
    if rope:
        args += list(rope_tabs)
        in_specs += [whole(a, True) for a in rope_tabs]
    sds = jax.ShapeDtypeStruct
    out_shape = [sds((t, D_MODEL), F32)]
    out_specs = [pl.BlockSpec((n_tok, D_MODEL), lambda i: (i, 0), **io_kw)]
    if has_cache:
        ckt, cv, s0f, s0b = cache
        args += [ckt, cv, s0f, s0b]
        st_spec = pl.BlockSpec((1,) + s0f.shape[1:], lambda i: (i, 0, 0, 0))
        in_specs += [pl.BlockSpec((1,) + ckt.shape[1:], lambda i: (i, 0, 0)),
                     pl.BlockSpec((1,) + cv.shape[1:], lambda i: (i, 0, 0)), st_spec, st_spec]
    else:
        n_b = t // seq_len
        out_shape += [sds((n_b, DIFF_QK, seq_len), F32), sds((n_b, seq_len, DIFF_HEADS, DIFF_DV), F32),
                      sds((n_b, GLA_HEADS, GLA_DK, GLA_DV), F32), sds((n_b, GLA_HEADS, GLA_DK, GLA_DV), F32)]
        st_spec = pl.BlockSpec((n_seq, GLA_HEADS, GLA_DK, GLA_DV), lambda i: (i, 0, 0, 0))
        out_specs += [pl.BlockSpec((n_seq, DIFF_QK, seq_len), lambda i: (i, 0, 0)),
                      pl.BlockSpec((n_seq, seq_len, DIFF_HEADS, DIFF_DV), lambda i: (i, 0, 0, 0)), st_spec, st_spec]
    vm = pltpu.VMEM
    scratch = [vm((n_tok, GLA_QK), F32), vm((n_tok, GLA_QK), F32), vm((n_tok, GLA_WIDTH), BF16),
               vm((n_tok, 2 * GLA_QK), F32), vm((n_tok, DIFF_QK), BF16), vm((n_tok, DIFF_WIDTH), BF16),
               vm((n_tok, MIX_WIDTH), F32), vm((n_tok, GLA_WIDTH), F32), vm((n_tok, DIFF_WIDTH), F32)]
    if has_cache:
        scratch += [vm((DIFF_QK, n_tok), BF16)]
    scratch += [vm((n_seq, 2, 2, GLA_PAIR_K, GLA_PAIR_V), F32),
                vm((2, n_tok, GLA_QK), BF16),
                vm((2, n_tok, GLA_QK), BF16),
                vm((2, n_tok // GLA_CHUNK, SUBLANES, GLA_QK), F32),
                vm((GLA_CHUNK, GLA_QK), F32),
                vm((GLA_CHUNK, GLA_WIDTH), F32)]
    return pl.pallas_call(
        functools.partial(_layer_kernel, seq_len=seq_len, n_seq=n_seq, rope=rope, has_cache=has_cache,
                          lam_init=lam_init),
        grid=(n_steps,),
        in_specs=in_specs,
        out_specs=out_specs,
        out_shape=out_shape,
        scratch_shapes=scratch,
        compiler_params=_params(1),
        name="layer_lat" if has_cache else "layer_ctx",
    )(*args)


def _rope_tables(seq_len):
    t = np.arange(seq_len)
    inv_freq = ROPE_BASE ** (-np.arange(ROPE_PAIRS, dtype=np.float64) / ROPE_PAIRS)
    ang_r = (t // GRID_W)[:, None] * inv_freq[None, :]
    ang_c = (t % GRID_W)[:, None] * inv_freq[None, :]
    cr, sr, cc, sc = np.cos(ang_r), np.sin(ang_r), np.cos(ang_c), np.sin(ang_c)
    cos = np.concatenate([cr, cr, cc, cc] * 2, axis=-1).astype(np.float32)
    sin = np.concatenate([-sr, sr, -sc, sc] * 2, axis=-1).astype(np.float32)
    return (jnp.asarray(cos), jnp.asarray(sin),
            jnp.asarray(np.ascontiguousarray(cos.T)), jnp.asarray(np.ascontiguousarray(sin.T)))


def kernel(x_prompt, x_sample, cache_diff_k, cache_diff_v, state_gla_fwd, state_gla_bwd, c, c_ctx,
           norm_gain, w_mod, b_mod, w_in, w_gla_alpha, b_gla_alpha, diff_lambda,
           gla_head_gain, diff_head_gain, w_out, final_gain):
    bp, lp, d = x_prompt.shape
    bs, ls, _ = x_sample.shape
    depth = norm_gain.shape[0]
    assert depth == 1 and d == D_MODEL and bs + 1 <= MOD_ROWS and w_in.shape[2] == _R_END
    l = 0
    lam_init = 0.8 - 0.6 * math.exp(-0.3 * l)

    mod, wt, wo = _setup(c_ctx[None, :], c, w_mod[l], b_mod, jnp.swapaxes(w_in[l], 0, 1), w_out[l])
    shared = (norm_gain, mod, wt, wo, w_gla_alpha[l], b_gla_alpha[l], diff_lambda[l],
              gla_head_gain, diff_head_gain, final_gain[None, :], lam_init)

    y_p, dkt, dv, s_f, s_b = _layer(x_prompt.reshape(bp * lp, d), lp, ROW_TILE // lp, *shared)
    y_prompt = y_p.reshape(bp, lp, d)
    new_diff_k = jnp.transpose(dkt.reshape(bp, DIFF_HEADS, 2, DIFF_DH, lp), (0, 4, 1, 2, 3))[:, None]
    new_diff_v = dv[:, None]
    new_gla_fwd = s_f[:, None]
    new_gla_bwd = s_b[:, None]

    past = cache_diff_k.shape[2]
    ckt = jnp.transpose(cache_diff_k[:, l], (0, 2, 3, 4, 1)).reshape(bs, DIFF_QK, past)
    cv = cache_diff_v[:, l].reshape(bs, past, DIFF_WIDTH)
    y_s = _layer(x_sample.reshape(bs * ls, d), ls, 1, *shared, rope_tabs=_rope_tables(ls),
                 cache=(ckt, cv, state_gla_fwd[:, l], state_gla_bwd[:, l]))[0]
    y_sample = y_s.reshape(bs, ls, d)

    return (y_prompt, y_sample, new_diff_k, new_diff_v, new_gla_fwd, new_gla_bwd)
```

```python
import functools
import math

import numpy as np
import jax
import jax.numpy as jnp
from jax import lax
from jax.experimental import pallas as pl
from jax.experimental.pallas import tpu as pltpu

F32 = jnp.float32
BF16 = jnp.bfloat16

D_MODEL = 1024
GRID_W = 64
GLA_HEADS = 4
GLA_DK = 64
GLA_DV = 128
GLA_QK = GLA_HEADS * GLA_DK
GLA_WIDTH = GLA_HEADS * GLA_DV
GLA_GATE_RANK = 16
GLA_GATE_TEMP = 16.0
GLA_CHUNK = 64
DIFF_HEADS = 4
DIFF_DH = 64
DIFF_DV = 2 * DIFF_DH
DIFF_QK = DIFF_HEADS * 2 * DIFF_DH
DIFF_WIDTH = DIFF_HEADS * DIFF_DV
MIX_WIDTH = GLA_WIDTH + DIFF_WIDTH
ROPE_PAIRS = DIFF_DH // 4
ROPE_BASE = 10000.0
EPS = 1e-6
LOG2E = math.log2(math.e)

LANES = 128
SUBLANES = 8
MOD_ROWS = 8
ROW_TILE = 512
SETUP_STEPS = 8
DIFF_QB = 256
DIFF_GROUP_KEYS = 4096
GLA_SUPER = 256
GLA_GROUP = 2
GLA_PAIR_K = 2 * GLA_DK
GLA_PAIR_V = 2 * GLA_DV
GLA_SAFE_LOG_DECAY = -40.0
VMEM_LIMIT = 58 * 1024 * 1024

_R_GQ = 0
_R_GK = _R_GQ + GLA_QK
_R_GV = _R_GK + GLA_QK
_R_LR = _R_GV + GLA_WIDTH
_R_DQ = _R_LR + 2 * GLA_GATE_RANK
_R_DK = _R_DQ + DIFF_QK
_R_DV = _R_DK + DIFF_QK
_R_GATE = _R_DV + DIFF_WIDTH
_R_END = _R_GATE + MIX_WIDTH


def _bf(x):
    return x.astype(BF16)


def _dot(a, b):
    return jnp.dot(a, b, preferred_element_type=F32)


def _dot_nt(a, b):
    return lax.dot_general(a, b, (((1,), (1,)), ((), ())), preferred_element_type=F32)


def _dot_tn(a, b):
    return lax.dot_general(a, b, (((0,), (0,)), ((), ())), preferred_element_type=F32)


def _params(n_parallel=0, n_arbitrary=0):
    sem = ("parallel",) * n_parallel + ("arbitrary",) * n_arbitrary
    return pltpu.CompilerParams(dimension_semantics=sem, vmem_limit_bytes=VMEM_LIMIT)


def _iota(shape, axis):
    return lax.broadcasted_iota(jnp.int32, shape, axis)


def _rows(ref, start, size):
    return ref.at[pl.ds(start, size)]


def _setup_kernel(cc_ref, c_ref, wm_ref, bm_ref, wi_ref, wo_ref, mod_ref, wib_ref, wob_ref):
    row = _iota((MOD_ROWS, D_MODEL), 0)
    cvecs = jnp.where(row == 0, cc_ref[...], 0.0)
    for b in range(c_ref.shape[0]):
        cvecs = jnp.where(row == 1 + b, c_ref[b:b + 1, :], cvecs)
    mod_ref[...] = _dot(_bf(_silu(cvecs)), _bf(wm_ref[...])) + bm_ref[...]
    wib_ref[...] = _bf(wi_ref[...])
    wob_ref[...] = _bf(wo_ref[...])


def _setup(c_ctx, c, w_mod, b_mod, w_in_t, w_out):
    n_mod = w_mod.shape[1]
    n_in = w_in_t.shape[0]
    tm = n_mod // SETUP_STEPS
    ti = pl.cdiv(n_in, SETUP_STEPS * 2 * SUBLANES) * 2 * SUBLANES
    to = w_out.shape[0] // SETUP_STEPS
    assert tm % LANES == 0 and tm * SETUP_STEPS == n_mod
    rows = lambda r: pl.BlockSpec((r, D_MODEL), lambda i: (i, 0))
    cols = lambda r: pl.BlockSpec((r, tm), lambda i: (0, i))
    return pl.pallas_call(
        _setup_kernel,
        grid=(SETUP_STEPS,),
        in_specs=[pl.BlockSpec(c_ctx.shape, lambda i: (0, 0)), pl.BlockSpec(c.shape, lambda i: (0, 0)),
                  cols(D_MODEL), cols(1), rows(ti), rows(to)],
        out_specs=[cols(MOD_ROWS), rows(ti), rows(to)],
        out_shape=[jax.ShapeDtypeStruct((MOD_ROWS, n_mod), F32), jax.ShapeDtypeStruct(w_in_t.shape, BF16),
                   jax.ShapeDtypeStruct(w_out.shape, BF16)],
        compiler_params=_params(1),
        name="setup",
    )(c_ctx, c, w_mod, b_mod, w_in_t, w_out)


def _log_sigmoid(z):
    return jnp.minimum(z, 0.0) - jnp.log1p(jnp.exp(-jnp.abs(z)))


def _silu(x):
    h = 0.5 * x
    return h + h * jnp.tanh(h)


def _rope_lanes(x, cos, sin):
    lane = _iota(x.shape, 1)
    up = pltpu.roll(x, ROPE_PAIRS, axis=1)
    dn = pltpu.roll(x, LANES - ROPE_PAIRS, axis=1)
    partner = jnp.where((lane & ROPE_PAIRS) == 0, dn, up)
    return x * cos + partner * sin


def _rope_rows(x, cos, sin):
    p = ROPE_PAIRS
    parts = []
    for g in range(x.shape[0] // (2 * p)):
        parts += [x[(2 * g + 1) * p:(2 * g + 2) * p], x[2 * g * p:(2 * g + 1) * p]]
    return x * cos + jnp.concatenate(parts, axis=0) * sin


def _projection_phase(tiles, gain, shift, scale, wt_ref, wa, ba, sc, rope_refs):
    hbs = {}

    def normalise(i):
        x = tiles[i][2]()
        ms = jnp.mean(x * x, axis=-1, keepdims=True)
        h = x * lax.rsqrt(ms + EPS) * gain
        hbs[i] = _bf(h * (1.0 + scale) + shift)

    pending = []

    def issue(matmul, finish):
        r = matmul()
        while pending:
            pending.pop()()
        pending.append(lambda: finish(r))

    normalise(0)
    for i, (row0, n_rows, _, kt_store, dv_store) in enumerate(tiles):
        rows = pl.ds(row0, n_rows)
        hb = hbs.pop(i)

        def sec(a, b, hb=hb):
            return lambda: _dot_nt(hb, wt_ref[a:b, :])

        def put(name, f, rows=rows):
            def finish(r):
                sc[name][rows, :] = f(r)
            return finish

        def finish_kl(klt, rows=rows, kt_store=kt_store):
            z = _dot_tn(_bf(klt[DIFF_QK:]), wa) + ba
            sc["g"][rows, :] = _log_sigmoid(z) * (1.0 / GLA_GATE_TEMP)
            if rope_refs is None:
                kt_store(slice(0, DIFF_QK), klt[:DIFF_QK])
            else:
                cost, sint = rope_refs[2][:, rows], rope_refs[3][:, rows]
                for hd in range(DIFF_HEADS):
                    sl = slice(hd * LANES, (hd + 1) * LANES)
                    kt_store(sl, _rope_rows(klt[sl, :], cost, sint))

        def finish_dq(dq, rows=rows):
            dq = dq * (DIFF_DH ** -0.5 * LOG2E)
            if rope_refs is None:
                sc["dq"][rows, :] = _bf(dq)
            else:
                cos, sin = rope_refs[0][rows, :], rope_refs[1][rows, :]
                for hd in range(DIFF_HEADS):
                    sl = slice(hd * LANES, (hd + 1) * LANES)
                    sc["dq"][rows, sl] = _bf(_rope_lanes(dq[:, sl], cos, sin))

        issue(sec(_R_DV, _R_GATE), dv_store)
        issue(sec(_R_GATE, _R_END), put("gate", _silu))
        if i + 1 < len(tiles):
            normalise(i + 1)
        issue(lambda hb=hb: _dot_nt(jnp.concatenate([wt_ref[_R_DK:_R_DV, :], wt_ref[_R_LR:_R_DQ, :]], axis=0), hb),
              finish_kl)
        issue(sec(_R_DQ, _R_DK), finish_dq)
        issue(sec(_R_GV, _R_LR), put("gv", _bf))
        issue(sec(_R_GQ, _R_GK), put("gq", lambda r: r * (GLA_DK ** -0.5)))
        issue(sec(_R_GK, _R_GV), put("gk", lambda r: r))
    while pending:
        pending.pop()()


def _tile4(x):
    return jnp.concatenate([x, x, x, x], axis=0)


def _pair_mask():
    return (_iota((GLA_PAIR_K, GLA_PAIR_V), 0) // GLA_DK) == (_iota((GLA_PAIR_K, GLA_PAIR_V), 1) // GLA_DV)


def _gla_recurrence_step(qi, kl, vb, e_col, s_ref, b, d, p_mask):
    inter = []
    for p in range(2):
        ks = slice(p * GLA_PAIR_K, (p + 1) * GLA_PAIR_K)
        s = s_ref[b, d, p]
        inter.append(_dot(qi[:, ks], _bf(s)))
        upd = _dot_tn(kl[:, ks], vb[:, p * GLA_PAIR_V:(p + 1) * GLA_PAIR_V])
        s_ref[b, d, p] = s * e_col[ks] + jnp.where(p_mask, upd, 0.0)
    return jnp.concatenate(inter, axis=1)


def _gla_fast(q_ref, k_ref, v_ref, g_ref, o_ref, s_ref, qi_ref, kl_ref, et_ref, *, seq_len, n_seq):
    c_len, sc_len = GLA_CHUNK, GLA_SUPER
    nc = seq_len // c_len
    cps = sc_len // c_len
    n_super = n_seq * seq_len // sc_len
    r = _iota((sc_len, 2 * sc_len), 0)
    c = _iota((sc_len, 2 * sc_len), 1) & (sc_len - 1)
    same = (r // c_len) == (c // c_len)
    tri2 = [(same & (c <= r)).astype(BF16), (same & (c >= r)).astype(BF16)]
    rt = _iota((c_len, GLA_HEADS * c_len), 0)
    cs = _iota((c_len, GLA_HEADS * c_len), 1) & (c_len - 1)
    keep = [cs <= rt, cs >= rt]
    k_mask = (_iota((GLA_HEADS * c_len, GLA_QK), 0) // c_len) == (_iota((GLA_HEADS * c_len, GLA_QK), 1) // GLA_DK)
    v_mask = (_iota((GLA_HEADS * c_len, GLA_WIDTH), 0) // c_len) == (_iota((GLA_HEADS * c_len, GLA_WIDTH), 1) // GLA_DV)
    p_mask = _pair_mask()
    zb = jnp.zeros((), BF16)
    dirs = range(2)

    for g0 in range(0, n_super, GLA_GROUP):
        group = range(g0, min(g0 + GLA_GROUP, n_super))
        rows = {u: slice(u * sc_len, (u + 1) * sc_len) for u in group}
        bs = {}
        for u in group:
            for d in dirs:
                g = g_ref[rows[u], d * GLA_QK:(d + 1) * GLA_QK]
                hi = _bf(g)
                lo = _bf(g - hi.astype(F32))
                bs[u, d] = _dot(tri2[d], jnp.concatenate([hi, lo], axis=0))
        yield
        qib, kob = {}, {}
        for u in group:
            q = q_ref[rows[u], :]
            k = k_ref[rows[u], :]
            for d in dirs:
                b = bs[u, d]
                qi = _bf(q * jnp.exp(b))
                ko = k * jnp.exp(-b)
                kls = []
                for j in range(cps):
                    last = j * c_len + (0 if d else c_len - 1)
                    e_tot = jnp.exp(b[last:last + 1, :])
                    et_ref[d, u * cps + j] = jnp.broadcast_to(e_tot, (SUBLANES, GLA_QK))
                    kls.append(_bf(ko[j * c_len:(j + 1) * c_len] * e_tot))
                qib[u, d], kob[u, d] = qi, _bf(ko)
                qi_ref[d, rows[u], :] = qi
                kl_ref[d, rows[u], :] = jnp.concatenate(kls, axis=0)
        yield
        atts = {}
        for u in group:
            for j in range(cps):
                cr = slice(j * c_len, (j + 1) * c_len)
                for d in dirs:
                    atts[u, j, d] = _dot_nt(qib[u, d][cr], jnp.where(k_mask, _tile4(kob[u, d][cr]), zb))
        yield
        for u in group:
            vb = v_ref[rows[u], :]
            for j in range(cps):
                a2 = jnp.concatenate([_bf(jnp.where(keep[d], atts[u, j, d], 0.0)) for d in dirs], axis=0)
                v_bd = jnp.where(v_mask, _tile4(vb[j * c_len:(j + 1) * c_len]), zb)
                r2 = _dot(a2, v_bd)
                o_ref[u * sc_len + j * c_len:u * sc_len + (j + 1) * c_len, :] = r2[:c_len] + r2[c_len:]
        yield

    for i in range(nc):
        for b in range(n_seq):
            for d in dirs:
                ci = b * nc + ((nc - 1 - i) if d else i)
                rows_c = slice(ci * c_len, (ci + 1) * c_len)
                e_col = et_ref[d, ci].T[:, 0:1]
                o_ref[rows_c, :] += _gla_recurrence_step(qi_ref[d, rows_c, :], kl_ref[d, rows_c, :],
                                                         v_ref[rows_c, :], e_col, s_ref, b, d, p_mask)
        yield


def _gla_slow(q_ref, k_ref, v_ref, g_ref, o_ref, s_ref, b_ref, vf_ref, *, seq_len, n_seq):
    c_len = GLA_CHUNK
    nc = seq_len // c_len
    r64 = _iota((c_len, c_len), 0)
    c64 = _iota((c_len, c_len), 1)
    p_mask = _pair_mask()
    expand = ((_iota((GLA_QK, GLA_WIDTH), 0) // GLA_DK) == (_iota((GLA_QK, GLA_WIDTH), 1) // GLA_DV)).astype(BF16)
    t_idx = _iota((c_len, GLA_QK), 0)
    for b in range(n_seq):
        for d in range(2):
            reverse = bool(d)
            tri = ((c64 >= r64) if reverse else (c64 <= r64)).astype(F32)

            def step(i, carry, b=b, d=d, reverse=reverse, tri=tri):
                ci = b * nc + ((nc - 1 - i) if reverse else i)
                base = pl.multiple_of(ci * c_len, c_len)
                rows = pl.ds(base, c_len)
                q = q_ref[rows, :]
                k = k_ref[rows, :]
                vb = v_ref[rows, :]
                g = g_ref[rows, d * GLA_QK:(d + 1) * GLA_QK]
                bcum = jnp.dot(tri, g, precision=lax.Precision.HIGHEST, preferred_element_type=F32)
                b_tot = jnp.sum(g, axis=0, keepdims=True)
                b_ref[...] = bcum
                vf_ref[...] = vb.astype(F32)

                def key_row(s, acc):
                    b_s = b_ref[pl.ds(s, 1), :]
                    k_s = k_ref[pl.ds(base + s, 1), :]
                    v_s = vf_ref[pl.ds(s, 1), :]
                    visible = (t_idx <= s) if reverse else (t_idx >= s)
                    w = jnp.where(visible, q * jnp.exp(jnp.minimum(bcum - b_s, 0.0)), 0.0) * k_s
                    return acc + _dot(_bf(w), expand) * v_s

                o = lax.fori_loop(0, c_len, key_row, jnp.zeros((c_len, GLA_WIDTH), F32))
                e_col = jnp.broadcast_to(jnp.exp(b_tot), (SUBLANES, GLA_QK)).T[:, 0:1]
                o = o + _gla_recurrence_step(_bf(q * jnp.exp(bcum)), _bf(k * jnp.exp(b_tot - bcum)), vb, e_col,
                                             s_ref, b, d, p_mask)
                if reverse:
                    o_ref[rows, :] += o
                else:
                    o_ref[rows, :] = o
                return carry

            lax.fori_loop(0, nc, step, 0)


def _run_interleaved(*gens):
    live = list(gens)
    while live:
        for gen in list(live):
            try:
                next(gen)
            except StopIteration:
                live.remove(gen)


def _gla_sequences(q_ref, k_ref, v_ref, g_ref, o_ref, gs, init_states, state_out, side_work, *, seq_len, n_seq):
    s_ref = gs["s"]
    zero = jnp.zeros((GLA_DK, GLA_DV), F32)
    for b in range(n_seq):
        for d in range(2):
            for p in range(2):
                if init_states is not None:
                    s0 = init_states[b][d]
                    top = jnp.concatenate([s0[2 * p], zero], axis=1)
                    bot = jnp.concatenate([zero, s0[2 * p + 1]], axis=1)
                    s_ref[b, d, p] = jnp.concatenate([top, bot], axis=0)
                else:
                    s_ref[b, d, p] = jnp.zeros((GLA_PAIR_K, GLA_PAIR_V), F32)
    worst = jnp.zeros((1, 2 * GLA_QK), F32)
    for ci in range(n_seq * seq_len // GLA_CHUNK):
        worst = jnp.minimum(worst, jnp.sum(g_ref[ci * GLA_CHUNK:(ci + 1) * GLA_CHUNK, :], axis=0, keepdims=True))
    def fast():
        _run_interleaved(_gla_fast(q_ref, k_ref, v_ref, g_ref, o_ref, s_ref, gs["qi"], gs["kl"], gs["et"],
                                   seq_len=seq_len, n_seq=n_seq), side_work())

    def slow():
        _gla_slow(q_ref, k_ref, v_ref, g_ref, o_ref, s_ref, gs["b"], gs["vf"], seq_len=seq_len, n_seq=n_seq)
        _run_interleaved(side_work())

    lax.cond(jnp.min(worst) >= GLA_SAFE_LOG_DECAY, fast, slow)
    if state_out is not None:
        for b in range(n_seq):
            for d in range(2):
                for h in range(GLA_HEADS):
                    p, j = divmod(h, 2)
                    state_out[b][d][h] = s_ref[b, d, p, j * GLA_DK:(j + 1) * GLA_DK, j * GLA_DV:(j + 1) * GLA_DV]


def _diff_blocks(tasks, lam, n_keys):
    kv_key, kv = None, None
    group_size = max(1, DIFF_GROUP_KEYS // n_keys)
    for g0 in range(0, len(tasks), group_size):
        group = []
        for key, load_q, load_kv, store in tasks[g0:g0 + group_size]:
            if key != kv_key:
                kts, vs = load_kv()
                kv_key, kv = key, (kts, [_with_ones(v) for v in vs])
            group.append((load_q(), kv[0], kv[1], store))
        scores = []
        for q, kts, _, _ in group:
            first = _iota(q.shape, 1) < DIFF_DH
            zb = jnp.zeros((), BF16)
            qs = jnp.concatenate([jnp.where(first, q, zb), jnp.where(first, zb, q)], axis=0)
            scores.append([_dot(qs, kt) for kt in kts])
        yield
        maxes = [functools.reduce(jnp.maximum, [jnp.max(s, axis=-1, keepdims=True) for s in ss]) for ss in scores]
        es = [[_bf(jnp.exp2(s - m)) for s in ss] for ss, m in zip(scores, maxes)]
        yield
        for (q, _, vas, store), e_list in zip(group, es):
            tq = q.shape[0]
            r = functools.reduce(jnp.add, [_dot(e, va) for e, va in zip(e_list, vas)])
            n = r[:, :DIFF_DV] / r[:, DIFF_DV:]
            store(n[:tq] - lam * n[tq:])
        yield


def _diff_lambda(lam_ref, lam_init):
    lp = lam_ref[...]
    return (jnp.exp(jnp.sum(lp[0:1] * lp[1:2], axis=-1, keepdims=True))
            - jnp.exp(jnp.sum(lp[2:3] * lp[3:4], axis=-1, keepdims=True)) + lam_init)


def _with_ones(v):
    return jnp.concatenate([v, jnp.ones(v.shape, v.dtype)], axis=1)


def _head_rmsnorm(x, gain):
    ms = jnp.mean(x * x, axis=-1, keepdims=True)
    return x * lax.rsqrt(ms + EPS) * gain


def _merge_rows(og_ref, od_ref, sg_ref, x, gt, gg, dg, wo_ref, fg, rows):
    slabs = []
    for h in range(GLA_HEADS):
        slabs.append(_head_rmsnorm(og_ref[rows, h * GLA_DV:(h + 1) * GLA_DV], gg))
    for h in range(DIFF_HEADS):
        slabs.append(_head_rmsnorm(od_ref[rows, h * DIFF_DV:(h + 1) * DIFF_DV], dg))
    o = jnp.concatenate(slabs, axis=-1) * sg_ref[rows, :]
    xn = x + gt * _dot(_bf(o), wo_ref[...])
    ms = jnp.mean(xn * xn, axis=-1, keepdims=True)
    return xn * lax.rsqrt(ms + EPS) * fg


def _layer_kernel(*refs, seq_len, n_seq, rope, has_cache, lam_init):
    it = iter(refs)
    x_ref, gain_ref, mod_ref, wt_ref, wo_ref, wa_ref, ba_ref, lam_ref, gg_ref, dg_ref, fg_ref = (next(it) for _ in range(11))
    rope_refs = tuple(next(it) for _ in range(4)) if rope else None
    if has_cache:
        ckt_ref, cv_ref, s0f_ref, s0b_ref = (next(it) for _ in range(4))
    y_ref = next(it)
    if not has_cache:
        kt_out, dv_out, sf_out, sb_out = (next(it) for _ in range(4))
    names = ["gq", "gk", "gv", "g", "dq", "dv", "gate", "og", "od"] + (["dkt"] if has_cache else [])
    sc = {n: next(it) for n in names}
    gs = {n: next(it) for n in ["s", "qi", "kl", "et", "b", "vf"]}

    d = D_MODEL
    n_tok = seq_len * n_seq
    mod_row = (1 + pl.program_id(0)) if has_cache else 0
    shift = mod_ref[pl.ds(mod_row, 1), 0:d]
    scale = mod_ref[pl.ds(mod_row, 1), d:2 * d]
    gt = mod_ref[pl.ds(mod_row, 1), 2 * d:3 * d]
    gain = gain_ref[...]

    zero = jnp.zeros((GLA_GATE_RANK, GLA_QK), F32)
    wa = _bf(jnp.concatenate([jnp.concatenate([wa_ref[0], zero], axis=1),
                              jnp.concatenate([zero, wa_ref[1]], axis=1)], axis=0))
    ba = jnp.concatenate([ba_ref[0:1, :], ba_ref[1:2, :]], axis=1)

    tiles = []
    for t0 in range(0, n_tok, ROW_TILE):
        if has_cache:
            def kt_store(sl, val, t0=t0):
                sc["dkt"][sl, t0:t0 + ROW_TILE] = _bf(val)

            def dv_store(val, t0=t0):
                sc["dv"][t0:t0 + ROW_TILE, :] = _bf(val)
        else:
            def kt_store(sl, val, t0=t0):
                for j in range(ROW_TILE // seq_len):
                    kt_out[(t0 // seq_len) + j, sl, :] = val[:, j * seq_len:(j + 1) * seq_len]

            def dv_store(val, t0=t0):
                sc["dv"][t0:t0 + ROW_TILE, :] = _bf(val)
                for j in range(ROW_TILE // seq_len):
                    dv_out[(t0 // seq_len) + j] = val[j * seq_len:(j + 1) * seq_len, :].reshape(
                        seq_len, DIFF_HEADS, DIFF_DV)

        tiles.append((t0, ROW_TILE, lambda t0=t0: x_ref[t0:t0 + ROW_TILE, :], kt_store, dv_store))
    _projection_phase(tiles, gain, shift, scale, wt_ref, wa, ba, sc, rope_refs)

    lam = _diff_lambda(lam_ref, lam_init)
    tasks = []
    for b in range(n_seq):
        for hd in range(DIFF_HEADS):
            sl = slice(hd * LANES, (hd + 1) * LANES)

            def load_kv(b=b, hd=hd, sl=sl):
                if has_cache:
                    return ([_bf(ckt_ref[0, sl, :]), sc["dkt"][sl, :]], [_bf(cv_ref[0, :, sl]), sc["dv"][:, sl]])
                return [_bf(kt_out[b, sl, :])], [sc["dv"][b * seq_len:(b + 1) * seq_len, sl]]

            for q0 in range(0, seq_len, DIFF_QB):
                r0 = b * seq_len + q0

                def load_q(r0=r0, sl=sl):
                    return sc["dq"][r0:r0 + DIFF_QB, sl]

                def store(val, r0=r0, sl=sl):
                    sc["od"][r0:r0 + DIFF_QB, sl] = val

                tasks.append(((b, hd), load_q, load_kv, store))
    n_keys = seq_len + (ckt_ref.shape[2] if has_cache else 0)
    init = [(s0f_ref.at[0], s0b_ref.at[0])] if has_cache else None
    out = None if has_cache else [(sf_out.at[b], sb_out.at[b]) for b in range(n_seq)]
    _gla_sequences(sc["gq"], sc["gk"], sc["gv"], sc["g"], sc["og"], gs, init, out,
                   lambda: _diff_blocks(tasks, lam, n_keys), seq_len=seq_len, n_seq=n_seq)

    gg = gg_ref[...]
    dg = dg_ref[...] * (1.0 - lam_init)
    fg = fg_ref[...]
    for t0 in range(0, n_tok, ROW_TILE):
        rows = slice(t0, t0 + ROW_TILE)
        y_ref[rows, :] = _merge_rows(sc["og"], sc["od"], sc["gate"], x_ref[rows, :], gt, gg, dg, wo_ref, fg, rows)


def _layer(x2d, seq_len, n_seq, gain, mod, wt, wo, wa, ba, lam_p, gg, dg, fg, lam_init, rope_tabs=None, cache=None):
    t = x2d.shape[0]
    n_tok = seq_len * n_seq
    n_steps = t // n_tok
    rope = rope_tabs is not None
    has_cache = cache is not None
    assert n_tok % ROW_TILE == 0 and ROW_TILE % seq_len in (0, ROW_TILE) and seq_len % GLA_SUPER == 0

    def whole(a, single=False):
        kw = {"pipeline_mode": pl.Buffered(1)} if single else {}
        return pl.BlockSpec(a.shape, lambda i: (0,) * a.ndim, **kw)

    io_kw = {"pipeline_mode": pl.Buffered(1)} if n_steps <= 2 else {}
    args = [x2d, gain, mod, wt, wo, wa, ba, lam_p, gg, dg, fg]
    in_specs = [pl.BlockSpec((n_tok, D_MODEL), lambda i: (i, 0), **io_kw), whole(gain), whole(mod), whole(wt, True),
                whole(wo, True), whole(wa), whole(ba), whole(lam_p), whole(gg), whole(dg), whole(fg)]
    if rope:
        args += list(rope_tabs)
        in_specs += [whole(a, True) for a in rope_tabs]
    sds = jax.ShapeDtypeStruct
    out_shape = [sds((t, D_MODEL), F32)]
    out_specs = [pl.BlockSpec((n_tok, D_MODEL), lambda i: (i, 0), **io_kw)]
    if has_cache:
        ckt, cv, s0f, s0b = cache
        args += [ckt, cv, s0f, s0b]
        st_spec = pl.BlockSpec((1,) + s0f.shape[1:], lambda i: (i, 0, 0, 0))
        in_specs += [pl.BlockSpec((1,) + ckt.shape[1:], lambda i: (i, 0, 0)),
                     pl.BlockSpec((1,) + cv.shape[1:], lambda i: (i, 0, 0)), st_spec, st_spec]
    else:
        n_b = t // seq_len
        out_shape += [sds((n_b, DIFF_QK, seq_len), F32), sds((n_b, seq_len, DIFF_HEADS, DIFF_DV), F32),
                      sds((n_b, GLA_HEADS, GLA_DK, GLA_DV), F32), sds((n_b, GLA_HEADS, GLA_DK, GLA_DV), F32)]
        st_spec = pl.BlockSpec((n_seq, GLA_HEADS, GLA_DK, GLA_DV), lambda i: (i, 0, 0, 0))
        out_specs += [pl.BlockSpec((n_seq, DIFF_QK, seq_len), lambda i: (i, 0, 0)),
                      pl.BlockSpec((n_seq, seq_len, DIFF_HEADS, DIFF_DV), lambda i: (i, 0, 0, 0)), st_spec, st_spec]
    vm = pltpu.VMEM
    scratch = [vm((n_tok, GLA_QK), F32), vm((n_tok, GLA_QK), F32), vm((n_tok, GLA_WIDTH), BF16),
               vm((n_tok, 2 * GLA_QK), F32), vm((n_tok, DIFF_QK), BF16), vm((n_tok, DIFF_WIDTH), BF16),
               vm((n_tok, MIX_WIDTH), F32), vm((n_tok, GLA_WIDTH), F32), vm((n_tok, DIFF_WIDTH), F32)]
    if has_cache:
        scratch += [vm((DIFF_QK, n_tok), BF16)]
    scratch += [vm((n_seq, 2, 2, GLA_PAIR_K, GLA_PAIR_V), F32),
                vm((2, n_tok, GLA_QK), BF16),
                vm((2, n_tok, GLA_QK), BF16),
                vm((2, n_tok // GLA_CHUNK, SUBLANES, GLA_QK), F32),
                vm((GLA_CHUNK, GLA_QK), F32),
                vm((GLA_CHUNK, GLA_WIDTH), F32)]
    return pl.pallas_call(
        functools.partial(_layer_kernel, seq_len=seq_len, n_seq=n_seq, rope=rope, has_cache=has_cache,
                          lam_init=lam_init),
        grid=(n_steps,),
        in_specs=in_specs,
        out_specs=out_specs,
        out_shape=out_shape,
        scratch_shapes=scratch,
        compiler_params=_params(1),
        name="layer_lat" if has_cache else "layer_ctx",
    )(*args)


def _rope_tables(seq_len):
    t = np.arange(seq_len)
    inv_freq = ROPE_BASE ** (-np.arange(ROPE_PAIRS, dtype=np.float64) / ROPE_PAIRS)
    ang_r = (t // GRID_W)[:, None] * inv_freq[None, :]
    ang_c = (t % GRID_W)[:, None] * inv_freq[None, :]
    cr, sr, cc, sc = np.cos(ang_r), np.sin(ang_r), np.cos(ang_c), np.sin(ang_c)
    cos = np.concatenate([cr, cr, cc, cc] * 2, axis=-1).astype(np.float32)
    sin = np.concatenate([-sr, sr, -sc, sc] * 2, axis=-1).astype(np.float32)
    return (jnp.asarray(cos), jnp.asarray(sin),
            jnp.asarray(np.ascontiguousarray(cos.T)), jnp.asarray(np.ascontiguousarray(sin.T)))


def kernel(x_prompt, x_sample, cache_diff_k, cache_diff_v, state_gla_fwd, state_gla_bwd, c, c_ctx,
           norm_gain, w_mod, b_mod, w_in, w_gla_alpha, b_gla_alpha, diff_lambda,
           gla_head_gain, diff_head_gain, w_out, final_gain):
    bp, lp, d = x_prompt.shape
    bs, ls, _ = x_sample.shape
    depth = norm_gain.shape[0]
    assert depth == 1 and d == D_MODEL and bs + 1 <= MOD_ROWS and w_in.shape[2] == _R_END
    l = 0
    lam_init = 0.8 - 0.6 * math.exp(-0.3 * l)

    mod, wt, wo = _setup(c_ctx[None, :], c, w_mod[l], b_mod, jnp.swapaxes(w_in[l], 0, 1), w_out[l])
    shared = (norm_gain, mod, wt, wo, w_gla_alpha[l], b_gla_alpha[l], diff_lambda[l],
              gla_head_gain, diff_head_gain, final_gain[None, :], lam_init)

    y_p, dkt, dv, s_f, s_b = _layer(x_prompt.reshape(bp * lp, d), lp, ROW_TILE // lp, *shared)
    y_prompt = y_p.reshape(bp, lp, d)
    new_diff_k = jnp.transpose(dkt.reshape(bp, DIFF_HEADS, 2, DIFF_DH, lp), (0, 4, 1, 2, 3))[:, None]
    new_diff_v = dv[:, None]
    new_gla_fwd = s_f[:, None]
    new_gla_bwd = s_b[:, None]

    past = cache_diff_k.shape[2]
    ckt = jnp.transpose(cache_diff_k[:, l], (0, 2, 3, 4, 1)).reshape(bs, DIFF_QK, past)
    cv = cache_diff_v[:, l].reshape(bs, past, DIFF_WIDTH)
    y_s = _layer(x_sample.reshape(bs * ls, d), ls, 1, *shared, rope_tabs=_rope_tables(ls),
                 cache=(ckt, cv, state_gla_fwd[:, l], state_gla_bwd[:, l]))[0]
    y_sample = y_s.reshape(bs, ls, d)

    return (y_prompt, y_sample, new_diff_k, new_diff_v, new_gla_fwd, new_gla_bwd)
```

```python
import functools
import math

import numpy as np
import jax
import jax.numpy as jnp
from jax import lax
from jax.experimental import pallas as pl
from jax.experimental.pallas import tpu as pltpu

F32 = jnp.float32
BF16 = jnp.bfloat16

D_MODEL = 1024
GRID_W = 64
GLA_HEADS = 4
GLA_DK = 64
GLA_DV = 128
GLA_QK = GLA_HEADS * GLA_DK
GLA_WIDTH = GLA_HEADS * GLA_DV
GLA_GATE_RANK = 16
GLA_GATE_TEMP = 16.0
GLA_CHUNK = 64
DIFF_HEADS = 4
DIFF_DH = 64
DIFF_DV = 2 * DIFF_DH
DIFF_QK = DIFF_HEADS * 2 * DIFF_DH
DIFF_WIDTH = DIFF_HEADS * DIFF_DV
MIX_WIDTH = GLA_WIDTH + DIFF_WIDTH
ROPE_PAIRS = DIFF_DH // 4
ROPE_BASE = 10000.0
EPS = 1e-6
LOG2E = math.log2(math.e)

LANES = 128
SUBLANES = 8
MOD_ROWS = 8
ROW_TILE = 512
SETUP_STEPS = 8
DIFF_QB = 256
DIFF_GROUP_KEYS = 4096
GLA_SUPER = 256
GLA_GROUP = 2
GLA_PAIR_K = 2 * GLA_DK
GLA_PAIR_V = 2 * GLA_DV
GLA_SAFE_LOG_DECAY = -40.0
VMEM_LIMIT = 58 * 1024 * 1024

_R_GQ = 0
_R_GK = _R_GQ + GLA_QK
_R_GV = _R_GK + GLA_QK
_R_LR = _R_GV + GLA_WIDTH
_R_DQ = _R_LR + 2 * GLA_GATE_RANK
_R_DK = _R_DQ + DIFF_QK
_R_DV = _R_DK + DIFF_QK
_R_GATE = _R_DV + DIFF_WIDTH
_R_END = _R_GATE + MIX_WIDTH


def _bf(x):
    return x.astype(BF16)


def _dot(a, b):
    return jnp.dot(a, b, preferred_element_type=F32)


def _dot_nt(a, b):
    return lax.dot_general(a, b, (((1,), (1,)), ((), ())), preferred_element_type=F32)


def _dot_tn(a, b):
    return lax.dot_general(a, b, (((0,), (0,)), ((), ())), preferred_element_type=F32)


def _params(n_parallel=0, n_arbitrary=0):
    sem = ("parallel",) * n_parallel + ("arbitrary",) * n_arbitrary
    return pltpu.CompilerParams(dimension_semantics=sem, vmem_limit_bytes=VMEM_LIMIT)


def _iota(shape, axis):
    return lax.broadcasted_iota(jnp.int32, shape, axis)


def _rows(ref, start, size):
    return ref.at[pl.ds(start, size)]


def _setup_kernel(cc_ref, c_ref, wm_ref, bm_ref, wi_ref, wo_ref, mod_ref, wib_ref, wob_ref):
    row = _iota((MOD_ROWS, D_MODEL), 0)
    cvecs = jnp.where(row == 0, cc_ref[...], 0.0)
    for b in range(c_ref.shape[0]):
        cvecs = jnp.where(row == 1 + b, c_ref[b:b + 1, :], cvecs)
    mod_ref[...] = _dot(_bf(_silu(cvecs)), _bf(wm_ref[...])) + bm_ref[...]
    wib_ref[...] = _bf(wi_ref[...])
    wob_ref[...] = _bf(wo_ref[...])


def _setup(c_ctx, c, w_mod, b_mod, w_in_t, w_out):
    n_mod = w_mod.shape[1]
    n_in = w_in_t.shape[0]
    tm = n_mod // SETUP_STEPS
    ti = pl.cdiv(n_in, SETUP_STEPS * 2 * SUBLANES) * 2 * SUBLANES
    to = w_out.shape[0] // SETUP_STEPS
    assert tm % LANES == 0 and tm * SETUP_STEPS == n_mod
    rows = lambda r: pl.BlockSpec((r, D_MODEL), lambda i: (i, 0))
    cols = lambda r: pl.BlockSpec((r, tm), lambda i: (0, i))
    return pl.pallas_call(
        _setup_kernel,
        grid=(SETUP_STEPS,),
        in_specs=[pl.BlockSpec(c_ctx.shape, lambda i: (0, 0)), pl.BlockSpec(c.shape, lambda i: (0, 0)),
                  cols(D_MODEL), cols(1), rows(ti), rows(to)],
        out_specs=[cols(MOD_ROWS), rows(ti), rows(to)],
        out_shape=[jax.ShapeDtypeStruct((MOD_ROWS, n_mod), F32), jax.ShapeDtypeStruct(w_in_t.shape, BF16),
                   jax.ShapeDtypeStruct(w_out.shape, BF16)],
        compiler_params=_params(1),
        name="setup",
    )(c_ctx, c, w_mod, b_mod, w_in_t, w_out)


def _log_sigmoid(z):
    return jnp.minimum(z, 0.0) - jnp.log1p(jnp.exp(-jnp.abs(z)))


def _silu(x):
    h = 0.5 * x
    return h + h * jnp.tanh(h)


def _rope_lanes(x, cos, sin):
    lane = _iota(x.shape, 1)
    up = pltpu.roll(x, ROPE_PAIRS, axis=1)
    dn = pltpu.roll(x, LANES - ROPE_PAIRS, axis=1)
    partner = jnp.where((lane & ROPE_PAIRS) == 0, dn, up)
    return x * cos + partner * sin


def _rope_rows(x, cos, sin):
    p = ROPE_PAIRS
    parts = []
    for g in range(x.shape[0] // (2 * p)):
        parts += [x[(2 * g + 1) * p:(2 * g + 2) * p], x[2 * g * p:(2 * g + 1) * p]]
    return x * cos + jnp.concatenate(parts, axis=0) * sin


def _projection_phase(tiles, gain, shift, scale, wt_ref, wa, ba, sc, rope_refs):
    hbs = {}

    def normalise(i):
        x = tiles[i][2]()
        ms = jnp.mean(x * x, axis=-1, keepdims=True)
        h = x * lax.rsqrt(ms + EPS) * gain
        hbs[i] = _bf(h * (1.0 + scale) + shift)

    pending = []

    def issue(matmul, finish):
        r = matmul()
        while pending:
            pending.pop()()
        pending.append(lambda: finish(r))

    normalise(0)
    for i, (row0, n_rows, _, kt_store, dv_store) in enumerate(tiles):
        rows = pl.ds(row0, n_rows)
        hb = hbs.pop(i)

        def sec(a, b, hb=hb):
            return lambda: _dot_nt(hb, wt_ref[a:b, :])

        def put(name, f, rows=rows):
            def finish(r):
                sc[name][rows, :] = f(r)
            return finish

        def finish_kl(klt, rows=rows, kt_store=kt_store):
            z = _dot_tn(_bf(klt[DIFF_QK:]), wa) + ba
            sc["g"][rows, :] = _log_sigmoid(z) * (1.0 / GLA_GATE_TEMP)
            if rope_refs is None:
                kt_store(slice(0, DIFF_QK), klt[:DIFF_QK])
            else:
                cost, sint = rope_refs[2][:, rows], rope_refs[3][:, rows]
                for hd in range(DIFF_HEADS):
                    sl = slice(hd * LANES, (hd + 1) * LANES)
                    kt_store(sl, _rope_rows(klt[sl, :], cost, sint))

        def finish_dq(dq, rows=rows):
            dq = dq * (DIFF_DH ** -0.5 * LOG2E)
            if rope_refs is None:
                sc["dq"][rows, :] = _bf(dq)
            else:
                cos, sin = rope_refs[0][rows, :], rope_refs[1][rows, :]
                for hd in range(DIFF_HEADS):
                    sl = slice(hd * LANES, (hd + 1) * LANES)
                    sc["dq"][rows, sl] = _bf(_rope_lanes(dq[:, sl], cos, sin))

        issue(sec(_R_DV, _R_GATE), dv_store)
        issue(sec(_R_GATE, _R_END), put("gate", _silu))
        if i + 1 < len(tiles):
            normalise(i + 1)
        issue(lambda hb=hb: _dot_nt(jnp.concatenate([wt_ref[_R_DK:_R_DV, :], wt_ref[_R_LR:_R_DQ, :]], axis=0), hb),
              finish_kl)
        issue(sec(_R_DQ, _R_DK), finish_dq)
        issue(sec(_R_GV, _R_LR), put("gv", _bf))
        issue(sec(_R_GQ, _R_GK), put("gq", lambda r: r * (GLA_DK ** -0.5)))
        issue(sec(_R_GK, _R_GV), put("gk", lambda r: r))
    while pending:
        pending.pop()()


def _tile4(x):
    return jnp.concatenate([x, x, x, x], axis=0)


def _pair_mask():
    return (_iota((GLA_PAIR_K, GLA_PAIR_V), 0) // GLA_DK) == (_iota((GLA_PAIR_K, GLA_PAIR_V), 1) // GLA_DV)


def _gla_recurrence_step(qi, kl, vb, e_col, s_ref, b, d, p_mask):
    inter = []
    for p in range(2):
        ks = slice(p * GLA_PAIR_K, (p + 1) * GLA_PAIR_K)
        s = s_ref[b, d, p]
        inter.append(_dot(qi[:, ks], _bf(s)))
        upd = _dot_tn(kl[:, ks], vb[:, p * GLA_PAIR_V:(p + 1) * GLA_PAIR_V])
        s_ref[b, d, p] = s * e_col[ks] + jnp.where(p_mask, upd, 0.0)
    return jnp.concatenate(inter, axis=1)


def _gla_fast(q_ref, k_ref, v_ref, g_ref, o_ref, s_ref, qi_ref, kl_ref, et_ref, *, seq_len, n_seq):
    c_len, sc_len = GLA_CHUNK, GLA_SUPER
    nc = seq_len // c_len
    cps = sc_len // c_len
    n_super = n_seq * seq_len // sc_len
    r = _iota((sc_len, 2 * sc_len), 0)
    c = _iota((sc_len, 2 * sc_len), 1) & (sc_len - 1)
    same = (r // c_len) == (c // c_len)
    tri2 = [(same & (c <= r)).astype(BF16), (same & (c >= r)).astype(BF16)]
    rt = _iota((c_len, GLA_HEADS * c_len), 0)
    cs = _iota((c_len, GLA_HEADS * c_len), 1) & (c_len - 1)
    keep = [cs <= rt, cs >= rt]
    k_mask = (_iota((GLA_HEADS * c_len, GLA_QK), 0) // c_len) == (_iota((GLA_HEADS * c_len, GLA_QK), 1) // GLA_DK)
    v_mask = (_iota((GLA_HEADS * c_len, GLA_WIDTH), 0) // c_len) == (_iota((GLA_HEADS * c_len, GLA_WIDTH), 1) // GLA_DV)
    p_mask = _pair_mask()
    zb = jnp.zeros((), BF16)
    dirs = range(2)

    for g0 in range(0, n_super, GLA_GROUP):
        group = range(g0, min(g0 + GLA_GROUP, n_super))
        rows = {u: slice(u * sc_len, (u + 1) * sc_len) for u in group}
        bs = {}
        for u in group:
            for d in dirs:
                g = g_ref[rows[u], d * GLA_QK:(d + 1) * GLA_QK]
                hi = _bf(g)
                lo = _bf(g - hi.astype(F32))
                bs[u, d] = _dot(tri2[d], jnp.concatenate([hi, lo], axis=0))
        qib, kob = {}, {}
        for u in group:
            q = q_ref[rows[u], :]
            k = k_ref[rows[u], :]
            for d in dirs:
                b = bs[u, d]
                qi = _bf(q * jnp.exp(b))
                ko = k * jnp.exp(-b)
                kls = []
                for j in range(cps):
                    last = j * c_len + (0 if d else c_len - 1)
                    e_tot = jnp.exp(b[last:last + 1, :])
                    et_ref[d, u * cps + j] = jnp.broadcast_to(e_tot, (SUBLANES, GLA_QK))
                    kls.append(_bf(ko[j * c_len:(j + 1) * c_len] * e_tot))
                qib[u, d], kob[u, d] = qi, _bf(ko)
                qi_ref[d, rows[u], :] = qi
                kl_ref[d, rows[u], :] = jnp.concatenate(kls, axis=0)
        atts = {}
        for u in group:
            for j in range(cps):
                cr = slice(j * c_len, (j + 1) * c_len)
                for d in dirs:
                    atts[u, j, d] = _dot_nt(qib[u, d][cr], jnp.where(k_mask, _tile4(kob[u, d][cr]), zb))
        for u in group:
            vb = v_ref[rows[u], :]
            for j in range(cps):
                a2 = jnp.concatenate([_bf(jnp.where(keep[d], atts[u, j, d], 0.0)) for d in dirs], axis=0)
                v_bd = jnp.where(v_mask, _tile4(vb[j * c_len:(j + 1) * c_len]), zb)
                r2 = _dot(a2, v_bd)
                o_ref[u * sc_len + j * c_len:u * sc_len + (j + 1) * c_len, :] = r2[:c_len] + r2[c_len:]

    for i in range(nc):
        for b in range(n_seq):
            for d in dirs:
                ci = b * nc + ((nc - 1 - i) if d else i)
                rows_c = slice(ci * c_len, (ci + 1) * c_len)
                e_col = et_ref[d, ci].T[:, 0:1]
                o_ref[rows_c, :] += _gla_recurrence_step(qi_ref[d, rows_c, :], kl_ref[d, rows_c, :],
                                                         v_ref[rows_c, :], e_col, s_ref, b, d, p_mask)


def _gla_slow(q_ref, k_ref, v_ref, g_ref, o_ref, s_ref, b_ref, vf_ref, *, seq_len, n_seq):
    c_len = GLA_CHUNK
    nc = seq_len // c_len
    r64 = _iota((c_len, c_len), 0)
    c64 = _iota((c_len, c_len), 1)
    p_mask = _pair_mask()
    expand = ((_iota((GLA_QK, GLA_WIDTH), 0) // GLA_DK) == (_iota((GLA_QK, GLA_WIDTH), 1) // GLA_DV)).astype(BF16)
    t_idx = _iota((c_len, GLA_QK), 0)
    for b in range(n_seq):
        for d in range(2):
            reverse = bool(d)
            tri = ((c64 >= r64) if reverse else (c64 <= r64)).astype(F32)

            def step(i, carry, b=b, d=d, reverse=reverse, tri=tri):
                ci = b * nc + ((nc - 1 - i) if reverse else i)
                base = pl.multiple_of(ci * c_len, c_len)
                rows = pl.ds(base, c_len)
                q = q_ref[rows, :]
                k = k_ref[rows, :]
                vb = v_ref[rows, :]
                g = g_ref[rows, d * GLA_QK:(d + 1) * GLA_QK]
                bcum = jnp.dot(tri, g, precision=lax.Precision.HIGHEST, preferred_element_type=F32)
                b_tot = jnp.sum(g, axis=0, keepdims=True)
                b_ref[...] = bcum
                vf_ref[...] = vb.astype(F32)

                def key_row(s, acc):
                    b_s = b_ref[pl.ds(s, 1), :]
                    k_s = k_ref[pl.ds(base + s, 1), :]
                    v_s = vf_ref[pl.ds(s, 1), :]
                    visible = (t_idx <= s) if reverse else (t_idx >= s)
                    w = jnp.where(visible, q * jnp.exp(jnp.minimum(bcum - b_s, 0.0)), 0.0) * k_s
                    return acc + _dot(_bf(w), expand) * v_s

                o = lax.fori_loop(0, c_len, key_row, jnp.zeros((c_len, GLA_WIDTH), F32))
                e_col = jnp.broadcast_to(jnp.exp(b_tot), (SUBLANES, GLA_QK)).T[:, 0:1]
                o = o + _gla_recurrence_step(_bf(q * jnp.exp(bcum)), _bf(k * jnp.exp(b_tot - bcum)), vb, e_col,
                                             s_ref, b, d, p_mask)
                if reverse:
                    o_ref[rows, :] += o
                else:
                    o_ref[rows, :] = o
                return carry

            lax.fori_loop(0, nc, step, 0)


def _gla_sequences(q_ref, k_ref, v_ref, g_ref, o_ref, gs, init_states, state_out, *, seq_len, n_seq):
    s_ref = gs["s"]
    zero = jnp.zeros((GLA_DK, GLA_DV), F32)
    for b in range(n_seq):
        for d in range(2):
            for p in range(2):
                if init_states is not None:
                    s0 = init_states[b][d]
                    top = jnp.concatenate([s0[2 * p], zero], axis=1)
                    bot = jnp.concatenate([zero, s0[2 * p + 1]], axis=1)
                    s_ref[b, d, p] = jnp.concatenate([top, bot], axis=0)
                else:
                    s_ref[b, d, p] = jnp.zeros((GLA_PAIR_K, GLA_PAIR_V), F32)
    worst = jnp.zeros((1, 2 * GLA_QK), F32)
    for ci in range(n_seq * seq_len // GLA_CHUNK):
        worst = jnp.minimum(worst, jnp.sum(g_ref[ci * GLA_CHUNK:(ci + 1) * GLA_CHUNK, :], axis=0, keepdims=True))
    lax.cond(jnp.min(worst) >= GLA_SAFE_LOG_DECAY,
             functools.partial(_gla_fast, q_ref, k_ref, v_ref, g_ref, o_ref, s_ref, gs["qi"], gs["kl"], gs["et"],
                               seq_len=seq_len, n_seq=n_seq),
             functools.partial(_gla_slow, q_ref, k_ref, v_ref, g_ref, o_ref, s_ref, gs["b"], gs["vf"],
                               seq_len=seq_len, n_seq=n_seq))
    if state_out is not None:
        for b in range(n_seq):
            for d in range(2):
                for h in range(GLA_HEADS):
                    p, j = divmod(h, 2)
                    state_out[b][d][h] = s_ref[b, d, p, j * GLA_DK:(j + 1) * GLA_DK, j * GLA_DV:(j + 1) * GLA_DV]


def _diff_group(group, lam):
    scores = []
    for q, kts, _, _ in group:
        first = _iota(q.shape, 1) < DIFF_DH
        zb = jnp.zeros((), BF16)
        qs = jnp.concatenate([jnp.where(first, q, zb), jnp.where(first, zb, q)], axis=0)
        scores.append([_dot(qs, kt) for kt in kts])
    maxes = [functools.reduce(jnp.maximum, [jnp.max(s, axis=-1, keepdims=True) for s in ss]) for ss in scores]
    es = [[_bf(jnp.exp2(s - m)) for s in ss] for ss, m in zip(scores, maxes)]
    for (q, _, vas, store), e_list in zip(group, es):
        tq = q.shape[0]
        r = functools.reduce(jnp.add, [_dot(e, va) for e, va in zip(e_list, vas)])
        n = r[:, :DIFF_DV] / r[:, DIFF_DV:]
        store(n[:tq] - lam * n[tq:])


def _diff_phase(heads, lam, n_keys, seq_len):
    group_size = max(1, DIFF_GROUP_KEYS // n_keys)
    blocks = seq_len // DIFF_QB

    def load_kv1(load_kv):
        kts, vs = load_kv()
        return kts, [_with_ones(v) for v in vs]

    if blocks == 1:
        for g0 in range(0, len(heads), group_size):
            _diff_group([(load_q(row0), *load_kv1(load_kv), functools.partial(store, row0))
                         for row0, load_kv, load_q, store in heads[g0:g0 + group_size]], lam)
        return
    group_size = min(group_size, blocks)
    assert blocks % group_size == 0
    for row0, load_kv, load_q, store in heads:
        kts, vas = load_kv1(load_kv)

        def body(gi, carry, row0=row0, load_q=load_q, store=store, kts=kts, vas=vas):
            base = pl.multiple_of(row0 + gi * (group_size * DIFF_QB), DIFF_QB)
            rs = [base + t * DIFF_QB for t in range(group_size)]
            _diff_group([(load_q(r), kts, vas, functools.partial(store, r)) for r in rs], lam)
            return carry

        lax.fori_loop(0, blocks // group_size, body, 0)


def _diff_lambda(lam_ref, lam_init):
    lp = lam_ref[...]
    return (jnp.exp(jnp.sum(lp[0:1] * lp[1:2], axis=-1, keepdims=True))
            - jnp.exp(jnp.sum(lp[2:3] * lp[3:4], axis=-1, keepdims=True)) + lam_init)


def _with_ones(v):
    return jnp.concatenate([v, jnp.ones(v.shape, v.dtype)], axis=1)


def _head_rmsnorm(x, gain):
    ms = jnp.mean(x * x, axis=-1, keepdims=True)
    return x * lax.rsqrt(ms + EPS) * gain


def _merge_rows(og_ref, od_ref, sg_ref, x, gt, gg, dg, wo_ref, fg, rows):
    slabs = []
    for h in range(GLA_HEADS):
        slabs.append(_head_rmsnorm(og_ref[rows, h * GLA_DV:(h + 1) * GLA_DV], gg))
    for h in range(DIFF_HEADS):
        slabs.append(_head_rmsnorm(od_ref[rows, h * DIFF_DV:(h + 1) * DIFF_DV], dg))
    o = jnp.concatenate(slabs, axis=-1) * sg_ref[rows, :]
    xn = x + gt * _dot(_bf(o), wo_ref[...])
    ms = jnp.mean(xn * xn, axis=-1, keepdims=True)
    return xn * lax.rsqrt(ms + EPS) * fg


def _layer_kernel(*refs, seq_len, n_seq, rope, has_cache, lam_init):
    it = iter(refs)
    x_ref, gain_ref, mod_ref, wt_ref, wo_ref, wa_ref, ba_ref, lam_ref, gg_ref, dg_ref, fg_ref = (next(it) for _ in range(11))
    rope_refs = tuple(next(it) for _ in range(4)) if rope else None
    if has_cache:
        ckt_ref, cv_ref, s0f_ref, s0b_ref = (next(it) for _ in range(4))
    y_ref = next(it)
    if not has_cache:
        kt_out, dv_out, sf_out, sb_out = (next(it) for _ in range(4))
    names = ["gq", "gk", "gv", "g", "dq", "dv", "gate", "og", "od"] + (["dkt"] if has_cache else [])
    sc = {n: next(it) for n in names}
    gs = {n: next(it) for n in ["s", "qi", "kl", "et", "b", "vf"]}

    d = D_MODEL
    n_tok = seq_len * n_seq
    mod_row = (1 + pl.program_id(0)) if has_cache else 0
    shift = mod_ref[pl.ds(mod_row, 1), 0:d]
    scale = mod_ref[pl.ds(mod_row, 1), d:2 * d]
    gt = mod_ref[pl.ds(mod_row, 1), 2 * d:3 * d]
    gain = gain_ref[...]

    zero = jnp.zeros((GLA_GATE_RANK, GLA_QK), F32)
    wa = _bf(jnp.concatenate([jnp.concatenate([wa_ref[0], zero], axis=1),
                              jnp.concatenate([zero, wa_ref[1]], axis=1)], axis=0))
    ba = jnp.concatenate([ba_ref[0:1, :], ba_ref[1:2, :]], axis=1)

    tiles = []
    for t0 in range(0, n_tok, ROW_TILE):
        if has_cache:
            def kt_store(sl, val, t0=t0):
                sc["dkt"][sl, t0:t0 + ROW_TILE] = _bf(val)

            def dv_store(val, t0=t0):
                sc["dv"][t0:t0 + ROW_TILE, :] = _bf(val)
        else:
            def kt_store(sl, val, t0=t0):
                for j in range(ROW_TILE // seq_len):
                    kt_out[(t0 // seq_len) + j, sl, :] = val[:, j * seq_len:(j + 1) * seq_len]

            def dv_store(val, t0=t0):
                sc["dv"][t0:t0 + ROW_TILE, :] = _bf(val)
                for j in range(ROW_TILE // seq_len):
                    dv_out[(t0 // seq_len) + j] = val[j * seq_len:(j + 1) * seq_len, :].reshape(
                        seq_len, DIFF_HEADS, DIFF_DV)

        tiles.append((t0, ROW_TILE, lambda t0=t0: x_ref[t0:t0 + ROW_TILE, :], kt_store, dv_store))
    _projection_phase(tiles, gain, shift, scale, wt_ref, wa, ba, sc, rope_refs)

    init = [(s0f_ref.at[0], s0b_ref.at[0])] if has_cache else None
    out = None if has_cache else [(sf_out.at[b], sb_out.at[b]) for b in range(n_seq)]
    _gla_sequences(sc["gq"], sc["gk"], sc["gv"], sc["g"], sc["og"], gs, init, out, seq_len=seq_len, n_seq=n_seq)

    lam = _diff_lambda(lam_ref, lam_init)
    heads = []
    for b in range(n_seq):
        for hd in range(DIFF_HEADS):
            sl = slice(hd * LANES, (hd + 1) * LANES)

            def load_kv(b=b, sl=sl):
                if has_cache:
                    return ([_bf(ckt_ref[0, sl, :]), sc["dkt"][sl, :]], [_bf(cv_ref[0, :, sl]), sc["dv"][:, sl]])
                return [_bf(kt_out[b, sl, :])], [sc["dv"][b * seq_len:(b + 1) * seq_len, sl]]

            def load_q(r, sl=sl):
                return sc["dq"][pl.ds(r, DIFF_QB), sl]

            def store(r, val, sl=sl):
                sc["od"][pl.ds(r, DIFF_QB), sl] = val

            heads.append((b * seq_len, load_kv, load_q, store))
    _diff_phase(heads, lam, seq_len + (ckt_ref.shape[2] if has_cache else 0), seq_len)

    gg = gg_ref[...]
    dg = dg_ref[...] * (1.0 - lam_init)
    fg = fg_ref[...]
    for t0 in range(0, n_tok, ROW_TILE):
        rows = slice(t0, t0 + ROW_TILE)
        y_ref[rows, :] = _merge_rows(sc["og"], sc["od"], sc["gate"], x_ref[rows, :], gt, gg, dg, wo_ref, fg, rows)


def _layer(x2d, seq_len, n_seq, gain, mod, wt, wo, wa, ba, lam_p, gg, dg, fg, lam_init, rope_tabs=None, cache=None):
    t = x2d.shape[0]
    n_tok = seq_len * n_seq
    n_steps = t // n_tok
    rope = rope_tabs is not None
    has_cache = cache is not None
    assert n_tok % ROW_TILE == 0 and ROW_TILE % seq_len in (0, ROW_TILE) and seq_len % GLA_SUPER == 0

    def whole(a, single=False):
        kw = {"pipeline_mode": pl.Buffered(1)} if single else {}
        return pl.BlockSpec(a.shape, lambda i: (0,) * a.ndim, **kw)

    io_kw = {"pipeline_mode": pl.Buffered(1)} if n_steps <= 2 else {}
    args = [x2d, gain, mod, wt, wo, wa, ba, lam_p, gg, dg, fg]
    in_specs = [pl.BlockSpec((n_tok, D_MODEL), lambda i: (i, 0), **io_kw), whole(gain), whole(mod), whole(wt, True),
                whole(wo, True), whole(wa), whole(ba), whole(lam_p), whole(gg), whole(dg), whole(fg)]
    if rope:
        args += list(rope_tabs)
        in_specs += [whole(a, True) for a in rope_tabs]
    sds = jax.ShapeDtypeStruct
    out_shape = [sds((t, D_MODEL), F32)]
    out_specs = [pl.BlockSpec((n_tok, D_MODEL), lambda i: (i, 0), **io_kw)]
    if has_cache:
        ckt, cv, s0f, s0b = cache
        args += [ckt, cv, s0f, s0b]
        st_spec = pl.BlockSpec((1,) + s0f.shape[1:], lambda i: (i, 0, 0, 0))
        in_specs += [pl.BlockSpec((1,) + ckt.shape[1:], lambda i: (i, 0, 0)),
                     pl.BlockSpec((1,) + cv.shape[1:], lambda i: (i, 0, 0)), st_spec, st_spec]
    else:
        n_b = t // seq_len
        out_shape += [sds((n_b, DIFF_QK, seq_len), F32), sds((n_b, seq_len, DIFF_HEADS, DIFF_DV), F32),
                      sds((n_b, GLA_HEADS, GLA_DK, GLA_DV), F32), sds((n_b, GLA_HEADS, GLA_DK, GLA_DV), F32)]
        st_spec = pl.BlockSpec((n_seq, GLA_HEADS, GLA_DK, GLA_DV), lambda i: (i, 0, 0, 0))
        out_specs += [pl.BlockSpec((n_seq, DIFF_QK, seq_len), lambda i: (i, 0, 0)),
                      pl.BlockSpec((n_seq, seq_len, DIFF_HEADS, DIFF_DV), lambda i: (i, 0, 0, 0)), st_spec, st_spec]
    vm = pltpu.VMEM
    scratch = [vm((n_tok, GLA_QK), F32), vm((n_tok, GLA_QK), F32), vm((n_tok, GLA_WIDTH), BF16),
               vm((n_tok, 2 * GLA_QK), F32), vm((n_tok, DIFF_QK), BF16), vm((n_tok, DIFF_WIDTH), BF16),
               vm((n_tok, MIX_WIDTH), F32), vm((n_tok, GLA_WIDTH), F32), vm((n_tok, DIFF_WIDTH), F32)]
    if has_cache:
        scratch += [vm((DIFF_QK, n_tok), BF16)]
    scratch += [vm((n_seq, 2, 2, GLA_PAIR_K, GLA_PAIR_V), F32),
                vm((2, n_tok, GLA_QK), BF16),
                vm((2, n_tok, GLA_QK), BF16),
                vm((2, n_tok // GLA_CHUNK, SUBLANES, GLA_QK), F32),
                vm((GLA_CHUNK, GLA_QK), F32),
                vm((GLA_CHUNK, GLA_WIDTH), F32)]
    return pl.pallas_call(
        functools.partial(_layer_kernel, seq_len=seq_len, n_seq=n_seq, rope=rope, has_cache=has_cache,
                          lam_init=lam_init),
        grid=(n_steps,),
        in_specs=in_specs,
        out_specs=out_specs,
        out_shape=out_shape,
        scratch_shapes=scratch,
        compiler_params=_params(1),
        name="layer_lat" if has_cache else "layer_ctx",
    )(*args)


def _rope_tables(seq_len):
    t = np.arange(seq_len)
    inv_freq = ROPE_BASE ** (-np.arange(ROPE_PAIRS, dtype=np.float64) / ROPE_PAIRS)
    ang_r = (t // GRID_W)[:, None] * inv_freq[None, :]
    ang_c = (t % GRID_W)[:, None] * inv_freq[None, :]
    cr, sr, cc, sc = np.cos(ang_r), np.sin(ang_r), np.cos(ang_c), np.sin(ang_c)
    cos = np.concatenate([cr, cr, cc, cc] * 2, axis=-1).astype(np.float32)
    sin = np.concatenate([-sr, sr, -sc, sc] * 2, axis=-1).astype(np.float32)
    return (jnp.asarray(cos), jnp.asarray(sin),
            jnp.asarray(np.ascontiguousarray(cos.T)), jnp.asarray(np.ascontiguousarray(sin.T)))


def kernel(x_prompt, x_sample, cache_diff_k, cache_diff_v, state_gla_fwd, state_gla_bwd, c, c_ctx,
           norm_gain, w_mod, b_mod, w_in, w_gla_alpha, b_gla_alpha, diff_lambda,
           gla_head_gain, diff_head_gain, w_out, final_gain):
    bp, lp, d = x_prompt.shape
    bs, ls, _ = x_sample.shape
    depth = norm_gain.shape[0]
    assert depth == 1 and d == D_MODEL and bs + 1 <= MOD_ROWS and w_in.shape[2] == _R_END
    l = 0
    lam_init = 0.8 - 0.6 * math.exp(-0.3 * l)

    mod, wt, wo = _setup(c_ctx[None, :], c, w_mod[l], b_mod, jnp.swapaxes(w_in[l], 0, 1), w_out[l])
    shared = (norm_gain, mod, wt, wo, w_gla_alpha[l], b_gla_alpha[l], diff_lambda[l],
              gla_head_gain, diff_head_gain, final_gain[None, :], lam_init)

    y_p, dkt, dv, s_f, s_b = _layer(x_prompt.reshape(bp * lp, d), lp, ROW_TILE // lp, *shared)
    y_prompt = y_p.reshape(bp, lp, d)
    new_diff_k = jnp.transpose(dkt.reshape(bp, DIFF_HEADS, 2, DIFF_DH, lp), (0, 4, 1, 2, 3))[:, None]
    new_diff_v = dv[:, None]
    new_gla_fwd = s_f[:, None]
    new_gla_bwd = s_b[:, None]

    past = cache_diff_k.shape[2]
    ckt = jnp.transpose(cache_diff_k[:, l], (0, 2, 3, 4, 1)).reshape(bs, DIFF_QK, past)
    cv = cache_diff_v[:, l].reshape(bs, past, DIFF_WIDTH)
    y_s = _layer(x_sample.reshape(bs * ls, d), ls, 1, *shared, rope_tabs=_rope_tables(ls),
                 cache=(ckt, cv, state_gla_fwd[:, l], state_gla_bwd[:, l]))[0]
    y_sample = y_s.reshape(bs, ls, d)

    return (y_prompt, y_sample, new_diff_k, new_diff_v, new_gla_fwd, new_gla_bwd)
```

```python
import functools
import math

import numpy as np
import jax
import jax.numpy as jnp
from jax import lax
from jax.experimental import pallas as pl
from jax.experimental.pallas import tpu as pltpu

F32 = jnp.float32
BF16 = jnp.bfloat16

D_MODEL = 1024
GRID_W = 64
GLA_HEADS = 4
GLA_DK = 64
GLA_DV = 128
GLA_QK = GLA_HEADS * GLA_DK
GLA_WIDTH = GLA_HEADS * GLA_DV
GLA_GATE_RANK = 16
GLA_GATE_TEMP = 16.0
GLA_CHUNK = 64
DIFF_HEADS = 4
DIFF_DH = 64
DIFF_DV = 2 * DIFF_DH
DIFF_QK = DIFF_HEADS * 2 * DIFF_DH
DIFF_WIDTH = DIFF_HEADS * DIFF_DV
MIX_WIDTH = GLA_WIDTH + DIFF_WIDTH
ROPE_PAIRS = DIFF_DH // 4
ROPE_BASE = 10000.0
EPS = 1e-6
LOG2E = math.log2(math.e)

LANES = 128
SUBLANES = 8
MOD_ROWS = 8
ROW_TILE = 512
SETUP_STEPS = 8
DIFF_QB = 256
DIFF_GROUP_KEYS = 4096
GLA_SUPER = 256
GLA_GROUP = 2
GLA_UNROLLED_CHUNKS = 4
GLA_PAIR_K = 2 * GLA_DK
GLA_PAIR_V = 2 * GLA_DV
GLA_SAFE_LOG_DECAY = -40.0
VMEM_LIMIT = 58 * 1024 * 1024

_R_GQ = 0
_R_GK = _R_GQ + GLA_QK
_R_GV = _R_GK + GLA_QK
_R_LR = _R_GV + GLA_WIDTH
_R_DQ = _R_LR + 2 * GLA_GATE_RANK
_R_DK = _R_DQ + DIFF_QK
_R_DV = _R_DK + DIFF_QK
_R_GATE = _R_DV + DIFF_WIDTH
_R_END = _R_GATE + MIX_WIDTH


def _bf(x):
    return x.astype(BF16)


def _dot(a, b):
    return jnp.dot(a, b, preferred_element_type=F32)


def _dot_nt(a, b):
    return lax.dot_general(a, b, (((1,), (1,)), ((), ())), preferred_element_type=F32)


def _dot_tn(a, b):
    return lax.dot_general(a, b, (((0,), (0,)), ((), ())), preferred_element_type=F32)


def _params(n_parallel=0, n_arbitrary=0):
    sem = ("parallel",) * n_parallel + ("arbitrary",) * n_arbitrary
    return pltpu.CompilerParams(dimension_semantics=sem, vmem_limit_bytes=VMEM_LIMIT)


def _iota(shape, axis):
    return lax.broadcasted_iota(jnp.int32, shape, axis)


def _rows(ref, start, size):
    return ref.at[pl.ds(start, size)]


def _setup_kernel(cc_ref, c_ref, wm_ref, bm_ref, wi_ref, wo_ref, mod_ref, wib_ref, wob_ref):
    row = _iota((MOD_ROWS, D_MODEL), 0)
    cvecs = jnp.where(row == 0, cc_ref[...], 0.0)
    for b in range(c_ref.shape[0]):
        cvecs = jnp.where(row == 1 + b, c_ref[b:b + 1, :], cvecs)
    mod_ref[...] = _dot(_bf(_silu(cvecs)), _bf(wm_ref[...])) + bm_ref[...]
    wib_ref[...] = _bf(wi_ref[...])
    wob_ref[...] = _bf(wo_ref[...])


def _setup(c_ctx, c, w_mod, b_mod, w_in_t, w_out):
    n_mod = w_mod.shape[1]
    n_in = w_in_t.shape[0]
    tm = n_mod // SETUP_STEPS
    ti = pl.cdiv(n_in, SETUP_STEPS * 2 * SUBLANES) * 2 * SUBLANES
    to = w_out.shape[0] // SETUP_STEPS
    assert tm % LANES == 0 and tm * SETUP_STEPS == n_mod
    rows = lambda r: pl.BlockSpec((r, D_MODEL), lambda i: (i, 0))
    cols = lambda r: pl.BlockSpec((r, tm), lambda i: (0, i))
    return pl.pallas_call(
        _setup_kernel,
        grid=(SETUP_STEPS,),
        in_specs=[pl.BlockSpec(c_ctx.shape, lambda i: (0, 0)), pl.BlockSpec(c.shape, lambda i: (0, 0)),
                  cols(D_MODEL), cols(1), rows(ti), rows(to)],
        out_specs=[cols(MOD_ROWS), rows(ti), rows(to)],
        out_shape=[jax.ShapeDtypeStruct((MOD_ROWS, n_mod), F32), jax.ShapeDtypeStruct(w_in_t.shape, BF16),
                   jax.ShapeDtypeStruct(w_out.shape, BF16)],
        compiler_params=_params(1),
        name="setup",
    )(c_ctx, c, w_mod, b_mod, w_in_t, w_out)


def _log_sigmoid(z):
    return jnp.minimum(z, 0.0) - jnp.log1p(jnp.exp(-jnp.abs(z)))


def _silu(x):
    h = 0.5 * x
    return h + h * jnp.tanh(h)


def _rope_lanes(x, cos, sin):
    lane = _iota(x.shape, 1)
    up = pltpu.roll(x, ROPE_PAIRS, axis=1)
    dn = pltpu.roll(x, LANES - ROPE_PAIRS, axis=1)
    partner = jnp.where((lane & ROPE_PAIRS) == 0, dn, up)
    return x * cos + partner * sin


def _rope_rows(x, cos, sin):
    p = ROPE_PAIRS
    parts = []
    for g in range(x.shape[0] // (2 * p)):
        parts += [x[(2 * g + 1) * p:(2 * g + 2) * p], x[2 * g * p:(2 * g + 1) * p]]
    return x * cos + jnp.concatenate(parts, axis=0) * sin


def _projection_phase(tiles, gain, shift, scale, wt_ref, wa, ba, sc, rope_refs):
    hbs = {}

    def normalise(i):
        x = tiles[i][2]()
        ms = jnp.mean(x * x, axis=-1, keepdims=True)
        h = x * lax.rsqrt(ms + EPS) * gain
        hbs[i] = _bf(h * (1.0 + scale) + shift)

    pending = []

    def issue(matmul, finish):
        r = matmul()
        while pending:
            pending.pop()()
        pending.append(lambda: finish(r))

    normalise(0)
    for i, (row0, n_rows, _, kt_store, dv_store) in enumerate(tiles):
        rows = pl.ds(row0, n_rows)
        hb = hbs.pop(i)

        def sec(a, b, hb=hb):
            return lambda: _dot_nt(hb, wt_ref[a:b, :])

        def put(name, f, rows=rows):
            def finish(r):
                sc[name][rows, :] = f(r)
            return finish

        def finish_kl(klt, rows=rows, kt_store=kt_store):
            z = _dot_tn(_bf(klt[DIFF_QK:]), wa) + ba
            sc["g"][rows, :] = _log_sigmoid(z) * (1.0 / GLA_GATE_TEMP)
            if rope_refs is None:
                kt_store(slice(0, DIFF_QK), klt[:DIFF_QK])
            else:
                cost, sint = rope_refs[2][:, rows], rope_refs[3][:, rows]
                for hd in range(DIFF_HEADS):
                    sl = slice(hd * LANES, (hd + 1) * LANES)
                    kt_store(sl, _rope_rows(klt[sl, :], cost, sint))

        def finish_dq(dq, rows=rows):
            dq = dq * (DIFF_DH ** -0.5 * LOG2E)
            if rope_refs is None:
                sc["dq"][rows, :] = _bf(dq)
            else:
                cos, sin = rope_refs[0][rows, :], rope_refs[1][rows, :]
                for hd in range(DIFF_HEADS):
                    sl = slice(hd * LANES, (hd + 1) * LANES)
                    sc["dq"][rows, sl] = _bf(_rope_lanes(dq[:, sl], cos, sin))

        issue(sec(_R_DV, _R_GATE), dv_store)
        issue(sec(_R_GATE, _R_END), put("gate", _silu))
        if i + 1 < len(tiles):
            normalise(i + 1)
        issue(lambda hb=hb: _dot_nt(jnp.concatenate([wt_ref[_R_DK:_R_DV, :], wt_ref[_R_LR:_R_DQ, :]], axis=0), hb),
              finish_kl)
        issue(sec(_R_DQ, _R_DK), finish_dq)
        issue(sec(_R_GV, _R_LR), put("gv", _bf))
        issue(sec(_R_GQ, _R_GK), put("gq", lambda r: r * (GLA_DK ** -0.5)))
        issue(sec(_R_GK, _R_GV), put("gk", lambda r: r))
    while pending:
        pending.pop()()


def _tile4(x):
    return jnp.concatenate([x, x, x, x], axis=0)


def _pair_mask():
    return (_iota((GLA_PAIR_K, GLA_PAIR_V), 0) // GLA_DK) == (_iota((GLA_PAIR_K, GLA_PAIR_V), 1) // GLA_DV)


def _gla_recurrence_step(qi, kl, vb, e_col, s_ref, b, d, p_mask):
    inter = []
    for p in range(2):
        ks = slice(p * GLA_PAIR_K, (p + 1) * GLA_PAIR_K)
        s = s_ref[b, d, p]
        inter.append(_dot(qi[:, ks], _bf(s)))
        upd = _dot_tn(kl[:, ks], vb[:, p * GLA_PAIR_V:(p + 1) * GLA_PAIR_V])
        s_ref[b, d, p] = s * e_col[ks] + jnp.where(p_mask, upd, 0.0)
    return jnp.concatenate(inter, axis=1)


def _gla_fast(q_ref, k_ref, v_ref, g_ref, o_ref, s_ref, qi_ref, kl_ref, et_ref, *, seq_len, n_seq):
    c_len, sc_len = GLA_CHUNK, GLA_SUPER
    nc = seq_len // c_len
    cps = sc_len // c_len
    n_super = n_seq * seq_len // sc_len
    r = _iota((sc_len, 2 * sc_len), 0)
    c = _iota((sc_len, 2 * sc_len), 1) & (sc_len - 1)
    same = (r // c_len) == (c // c_len)
    tri2 = [(same & (c <= r)).astype(BF16), (same & (c >= r)).astype(BF16)]
    rt = _iota((c_len, GLA_HEADS * c_len), 0)
    cs = _iota((c_len, GLA_HEADS * c_len), 1) & (c_len - 1)
    keep = [cs <= rt, cs >= rt]
    k_mask = (_iota((GLA_HEADS * c_len, GLA_QK), 0) // c_len) == (_iota((GLA_HEADS * c_len, GLA_QK), 1) // GLA_DK)
    v_mask = (_iota((GLA_HEADS * c_len, GLA_WIDTH), 0) // c_len) == (_iota((GLA_HEADS * c_len, GLA_WIDTH), 1) // GLA_DV)
    p_mask = _pair_mask()
    zb = jnp.zeros((), BF16)
    dirs = range(2)

    for g0 in range(0, n_super, GLA_GROUP):
        group = range(g0, min(g0 + GLA_GROUP, n_super))
        rows = {u: slice(u * sc_len, (u + 1) * sc_len) for u in group}
        bs = {}
        for u in group:
            for d in dirs:
                g = g_ref[rows[u], d * GLA_QK:(d + 1) * GLA_QK]
                hi = _bf(g)
                lo = _bf(g - hi.astype(F32))
                bs[u, d] = _dot(tri2[d], jnp.concatenate([hi, lo], axis=0))
        qib, kob = {}, {}
        for u in group:
            q = q_ref[rows[u], :]
            k = k_ref[rows[u], :]
            for d in dirs:
                b = bs[u, d]
                qi = _bf(q * jnp.exp(b))
                ko = k * jnp.exp(-b)
                kls = []
                for j in range(cps):
                    last = j * c_len + (0 if d else c_len - 1)
                    e_tot = jnp.exp(b[last:last + 1, :])
                    et_ref[d, u * cps + j] = jnp.broadcast_to(e_tot, (SUBLANES, GLA_QK))
                    kls.append(_bf(ko[j * c_len:(j + 1) * c_len] * e_tot))
                qib[u, d], kob[u, d] = qi, _bf(ko)
                qi_ref[d, rows[u], :] = qi
                kl_ref[d, rows[u], :] = jnp.concatenate(kls, axis=0)
        atts = {}
        for u in group:
            for j in range(cps):
                cr = slice(j * c_len, (j + 1) * c_len)
                for d in dirs:
                    atts[u, j, d] = _dot_nt(qib[u, d][cr], jnp.where(k_mask, _tile4(kob[u, d][cr]), zb))
        for u in group:
            vb = v_ref[rows[u], :]
            for j in range(cps):
                a2 = jnp.concatenate([_bf(jnp.where(keep[d], atts[u, j, d], 0.0)) for d in dirs], axis=0)
                v_bd = jnp.where(v_mask, _tile4(vb[j * c_len:(j + 1) * c_len]), zb)
                r2 = _dot(a2, v_bd)
                o_ref[u * sc_len + j * c_len:u * sc_len + (j + 1) * c_len, :] = r2[:c_len] + r2[c_len:]

    def recurrences(i, carry=None):
        for b in range(n_seq):
            for d in dirs:
                ci = b * nc + ((nc - 1 - i) if d else i)
                start = ci * c_len
                rows_c = pl.ds(start if isinstance(start, int) else pl.multiple_of(start, c_len), c_len)
                e_col = et_ref[d, ci].T[:, 0:1]
                o_ref[rows_c, :] += _gla_recurrence_step(qi_ref[d, rows_c, :], kl_ref[d, rows_c, :],
                                                         v_ref[rows_c, :], e_col, s_ref, b, d, p_mask)
        return carry

    if nc <= GLA_UNROLLED_CHUNKS:
        for i in range(nc):
            recurrences(i)
    else:
        lax.fori_loop(0, nc, recurrences, None)


def _gla_slow(q_ref, k_ref, v_ref, g_ref, o_ref, s_ref, b_ref, vf_ref, *, seq_len, n_seq):
    c_len = GLA_CHUNK
    nc = seq_len // c_len
    r64 = _iota((c_len, c_len), 0)
    c64 = _iota((c_len, c_len), 1)
    p_mask = _pair_mask()
    expand = ((_iota((GLA_QK, GLA_WIDTH), 0) // GLA_DK) == (_iota((GLA_QK, GLA_WIDTH), 1) // GLA_DV)).astype(BF16)
    t_idx = _iota((c_len, GLA_QK), 0)
    for b in range(n_seq):
        for d in range(2):
            reverse = bool(d)
            tri = ((c64 >= r64) if reverse else (c64 <= r64)).astype(F32)

            def step(i, carry, b=b, d=d, reverse=reverse, tri=tri):
                ci = b * nc + ((nc - 1 - i) if reverse else i)
                base = pl.multiple_of(ci * c_len, c_len)
                rows = pl.ds(base, c_len)
                q = q_ref[rows, :]
                k = k_ref[rows, :]
                vb = v_ref[rows, :]
                g = g_ref[rows, d * GLA_QK:(d + 1) * GLA_QK]
                bcum = jnp.dot(tri, g, precision=lax.Precision.HIGHEST, preferred_element_type=F32)
                b_tot = jnp.sum(g, axis=0, keepdims=True)
                b_ref[...] = bcum
                vf_ref[...] = vb.astype(F32)

                def key_row(s, acc):
                    b_s = b_ref[pl.ds(s, 1), :]
                    k_s = k_ref[pl.ds(base + s, 1), :]
                    v_s = vf_ref[pl.ds(s, 1), :]
                    visible = (t_idx <= s) if reverse else (t_idx >= s)
                    w = jnp.where(visible, q * jnp.exp(jnp.minimum(bcum - b_s, 0.0)), 0.0) * k_s
                    return acc + _dot(_bf(w), expand) * v_s

                o = lax.fori_loop(0, c_len, key_row, jnp.zeros((c_len, GLA_WIDTH), F32))
                e_col = jnp.broadcast_to(jnp.exp(b_tot), (SUBLANES, GLA_QK)).T[:, 0:1]
                o = o + _gla_recurrence_step(_bf(q * jnp.exp(bcum)), _bf(k * jnp.exp(b_tot - bcum)), vb, e_col,
                                             s_ref, b, d, p_mask)
                if reverse:
                    o_ref[rows, :] += o
                else:
                    o_ref[rows, :] = o
                return carry

            lax.fori_loop(0, nc, step, 0)


def _gla_sequences(q_ref, k_ref, v_ref, g_ref, o_ref, gs, init_states, state_out, *, seq_len, n_seq):
    s_ref = gs["s"]
    zero = jnp.zeros((GLA_DK, GLA_DV), F32)
    for b in range(n_seq):
        for d in range(2):
            for p in range(2):
                if init_states is not None:
                    s0 = init_states[b][d]
                    top = jnp.concatenate([s0[2 * p], zero], axis=1)
                    bot = jnp.concatenate([zero, s0[2 * p + 1]], axis=1)
                    s_ref[b, d, p] = jnp.concatenate([top, bot], axis=0)
                else:
                    s_ref[b, d, p] = jnp.zeros((GLA_PAIR_K, GLA_PAIR_V), F32)
    worst = jnp.zeros((1, 2 * GLA_QK), F32)
    for ci in range(n_seq * seq_len // GLA_CHUNK):
        worst = jnp.minimum(worst, jnp.sum(g_ref[ci * GLA_CHUNK:(ci + 1) * GLA_CHUNK, :], axis=0, keepdims=True))
    lax.cond(jnp.min(worst) >= GLA_SAFE_LOG_DECAY,
             functools.partial(_gla_fast, q_ref, k_ref, v_ref, g_ref, o_ref, s_ref, gs["qi"], gs["kl"], gs["et"],
                               seq_len=seq_len, n_seq=n_seq),
             functools.partial(_gla_slow, q_ref, k_ref, v_ref, g_ref, o_ref, s_ref, gs["b"], gs["vf"],
                               seq_len=seq_len, n_seq=n_seq))
    if state_out is not None:
        for b in range(n_seq):
            for d in range(2):
                for h in range(GLA_HEADS):
                    p, j = divmod(h, 2)
                    state_out[b][d][h] = s_ref[b, d, p, j * GLA_DK:(j + 1) * GLA_DK, j * GLA_DV:(j + 1) * GLA_DV]


def _diff_group(group, lam):
    scores = []
    for q, kts, _, _ in group:
        first = _iota(q.shape, 1) < DIFF_DH
        zb = jnp.zeros((), BF16)
        qs = jnp.concatenate([jnp.where(first, q, zb), jnp.where(first, zb, q)], axis=0)
        scores.append([_dot(qs, kt) for kt in kts])
    maxes = [functools.reduce(jnp.maximum, [jnp.max(s, axis=-1, keepdims=True) for s in ss]) for ss in scores]
    es = [[_bf(jnp.exp2(s - m)) for s in ss] for ss, m in zip(scores, maxes)]
    for (q, _, vas, store), e_list in zip(group, es):
        tq = q.shape[0]
        r = functools.reduce(jnp.add, [_dot(e, va) for e, va in zip(e_list, vas)])
        n = r[:, :DIFF_DV] / r[:, DIFF_DV:]
        store(n[:tq] - lam * n[tq:])


def _diff_phase(heads, lam, n_keys, seq_len):
    group_size = max(1, DIFF_GROUP_KEYS // n_keys)
    blocks = seq_len // DIFF_QB

    def load_kv1(load_kv):
        kts, vs = load_kv()
        return kts, [_with_ones(v) for v in vs]

    if blocks == 1:
        for g0 in range(0, len(heads), group_size):
            _diff_group([(load_q(row0), *load_kv1(load_kv), functools.partial(store, row0))
                         for row0, load_kv, load_q, store in heads[g0:g0 + group_size]], lam)
        return
    group_size = min(group_size, blocks)
    assert blocks % group_size == 0
    for row0, load_kv, load_q, store in heads:
        kts, vas = load_kv1(load_kv)

        def body(gi, carry, row0=row0, load_q=load_q, store=store, kts=kts, vas=vas):
            base = pl.multiple_of(row0 + gi * (group_size * DIFF_QB), DIFF_QB)
            rs = [base + t * DIFF_QB for t in range(group_size)]
            _diff_group([(load_q(r), kts, vas, functools.partial(store, r)) for r in rs], lam)
            return carry

        lax.fori_loop(0, blocks // group_size, body, 0)


def _diff_lambda(lam_ref, lam_init):
    lp = lam_ref[...]
    return (jnp.exp(jnp.sum(lp[0:1] * lp[1:2], axis=-1, keepdims=True))
            - jnp.exp(jnp.sum(lp[2:3] * lp[3:4], axis=-1, keepdims=True)) + lam_init)


def _with_ones(v):
    return jnp.concatenate([v, jnp.ones(v.shape, v.dtype)], axis=1)


def _head_rmsnorm(x, gain):
    ms = jnp.mean(x * x, axis=-1, keepdims=True)
    return x * lax.rsqrt(ms + EPS) * gain


def _merge_rows(og_ref, od_ref, sg_ref, x, gt, gg, dg, wo_ref, fg, rows):
    slabs = []
    for h in range(GLA_HEADS):
        slabs.append(_head_rmsnorm(og_ref[rows, h * GLA_DV:(h + 1) * GLA_DV], gg))
    for h in range(DIFF_HEADS):
        slabs.append(_head_rmsnorm(od_ref[rows, h * DIFF_DV:(h + 1) * DIFF_DV], dg))
    o = jnp.concatenate(slabs, axis=-1) * sg_ref[rows, :]
    xn = x + gt * _dot(_bf(o), wo_ref[...])
    ms = jnp.mean(xn * xn, axis=-1, keepdims=True)
    return xn * lax.rsqrt(ms + EPS) * fg


def _layer_kernel(*refs, seq_len, n_seq, rope, has_cache, lam_init):
    it = iter(refs)
    x_ref, gain_ref, mod_ref, wt_ref, wo_ref, wa_ref, ba_ref, lam_ref, gg_ref, dg_ref, fg_ref = (next(it) for _ in range(11))
    rope_refs = tuple(next(it) for _ in range(4)) if rope else None
    if has_cache:
        ckt_ref, cv_ref, s0f_ref, s0b_ref = (next(it) for _ in range(4))
    y_ref = next(it)
    if not has_cache:
        kt_out, dv_out, sf_out, sb_out = (next(it) for _ in range(4))
    names = ["gq", "gk", "gv", "g", "dq", "dv", "gate", "og", "od"] + (["dkt"] if has_cache else [])
    sc = {n: next(it) for n in names}
    gs = {n: next(it) for n in ["s", "qi", "kl", "et", "b", "vf"]}

    d = D_MODEL
    n_tok = seq_len * n_seq
    mod_row = (1 + pl.program_id(0)) if has_cache else 0
    shift = mod_ref[pl.ds(mod_row, 1), 0:d]
    scale = mod_ref[pl.ds(mod_row, 1), d:2 * d]
    gt = mod_ref[pl.ds(mod_row, 1), 2 * d:3 * d]
    gain = gain_ref[...]

    zero = jnp.zeros((GLA_GATE_RANK, GLA_QK), F32)
    wa = _bf(jnp.concatenate([jnp.concatenate([wa_ref[0], zero], axis=1),
                              jnp.concatenate([zero, wa_ref[1]], axis=1)], axis=0))
    ba = jnp.concatenate([ba_ref[0:1, :], ba_ref[1:2, :]], axis=1)

    tiles = []
    for t0 in range(0, n_tok, ROW_TILE):
        if has_cache:
            def kt_store(sl, val, t0=t0):
                sc["dkt"][sl, t0:t0 + ROW_TILE] = _bf(val)

            def dv_store(val, t0=t0):
                sc["dv"][t0:t0 + ROW_TILE, :] = _bf(val)
        else:
            def kt_store(sl, val, t0=t0):
                for j in range(ROW_TILE // seq_len):
                    kt_out[(t0 // seq_len) + j, sl, :] = val[:, j * seq_len:(j + 1) * seq_len]

            def dv_store(val, t0=t0):
                sc["dv"][t0:t0 + ROW_TILE, :] = _bf(val)
                for j in range(ROW_TILE // seq_len):
                    dv_out[(t0 // seq_len) + j] = val[j * seq_len:(j + 1) * seq_len, :].reshape(
                        seq_len, DIFF_HEADS, DIFF_DV)

        tiles.append((t0, ROW_TILE, lambda t0=t0: x_ref[t0:t0 + ROW_TILE, :], kt_store, dv_store))
    _projection_phase(tiles, gain, shift, scale, wt_ref, wa, ba, sc, rope_refs)

    init = [(s0f_ref.at[0], s0b_ref.at[0])] if has_cache else None
    out = None if has_cache else [(sf_out.at[b], sb_out.at[b]) for b in range(n_seq)]
    _gla_sequences(sc["gq"], sc["gk"], sc["gv"], sc["g"], sc["og"], gs, init, out, seq_len=seq_len, n_seq=n_seq)

    lam = _diff_lambda(lam_ref, lam_init)
    heads = []
    for b in range(n_seq):
        for hd in range(DIFF_HEADS):
            sl = slice(hd * LANES, (hd + 1) * LANES)

            def load_kv(b=b, sl=sl):
                if has_cache:
                    return ([_bf(ckt_ref[0, sl, :]), sc["dkt"][sl, :]], [_bf(cv_ref[0, :, sl]), sc["dv"][:, sl]])
                return [_bf(kt_out[b, sl, :])], [sc["dv"][b * seq_len:(b + 1) * seq_len, sl]]

            def load_q(r, sl=sl):
                return sc["dq"][pl.ds(r, DIFF_QB), sl]

            def store(r, val, sl=sl):
                sc["od"][pl.ds(r, DIFF_QB), sl] = val

            heads.append((b * seq_len, load_kv, load_q, store))
    _diff_phase(heads, lam, seq_len + (ckt_ref.shape[2] if has_cache else 0), seq_len)

    gg = gg_ref[...]
    dg = dg_ref[...] * (1.0 - lam_init)
    fg = fg_ref[...]
    def merge_tile(t, carry=None):
        start = t * ROW_TILE
        rows = pl.ds(start if isinstance(start, int) else pl.multiple_of(start, ROW_TILE), ROW_TILE)
        y_ref[rows, :] = _merge_rows(sc["og"], sc["od"], sc["gate"], x_ref[rows, :], gt, gg, dg, wo_ref, fg, rows)
        return carry

    if n_tok == ROW_TILE:
        merge_tile(0)
    else:
        lax.fori_loop(0, n_tok // ROW_TILE, merge_tile, None)


def _layer(x2d, seq_len, n_seq, gain, mod, wt, wo, wa, ba, lam_p, gg, dg, fg, lam_init, rope_tabs=None, cache=None):
    t = x2d.shape[0]
    n_tok = seq_len * n_seq
    n_steps = t // n_tok
    rope = rope_tabs is not None
    has_cache = cache is not None
    assert n_tok % ROW_TILE == 0 and ROW_TILE % seq_len in (0, ROW_TILE) and seq_len % GLA_SUPER == 0

    def whole(a, single=False):
        kw = {"pipeline_mode": pl.Buffered(1)} if single else {}
        return pl.BlockSpec(a.shape, lambda i: (0,) * a.ndim, **kw)

    io_kw = {"pipeline_mode": pl.Buffered(1)} if n_steps <= 2 else {}
    args = [x2d, gain, mod, wt, wo, wa, ba, lam_p, gg, dg, fg]
    in_specs = [pl.BlockSpec((n_tok, D_MODEL), lambda i: (i, 0), **io_kw), whole(gain), whole(mod), whole(wt, True),
                whole(wo, True), whole(wa), whole(ba), whole(lam_p), whole(gg), whole(dg), whole(fg)]
    if rope:
        args += list(rope_tabs)
        in_specs += [whole(a, True) for a in rope_tabs]
    sds = jax.ShapeDtypeStruct
    out_shape = [sds((t, D_MODEL), F32)]
    out_specs = [pl.BlockSpec((n_tok, D_MODEL), lambda i: (i, 0), **io_kw)]
    if has_cache:
        ckt, cv, s0f, s0b = cache
        args += [ckt, cv, s0f, s0b]
        st_spec = pl.BlockSpec((1,) + s0f.shape[1:], lambda i: (i, 0, 0, 0))
        in_specs += [pl.BlockSpec((1,) + ckt.shape[1:], lambda i: (i, 0, 0)),
                     pl.BlockSpec((1,) + cv.shape[1:], lambda i: (i, 0, 0)), st_spec, st_spec]
    else:
        n_b = t // seq_len
        out_shape += [sds((n_b, DIFF_QK, seq_len), F32), sds((n_b, seq_len, DIFF_HEADS, DIFF_DV), F32),
                      sds((n_b, GLA_HEADS, GLA_DK, GLA_DV), F32), sds((n_b, GLA_HEADS, GLA_DK, GLA_DV), F32)]
        st_spec = pl.BlockSpec((n_seq, GLA_HEADS, GLA_DK, GLA_DV), lambda i: (i, 0, 0, 0))
        out_specs += [pl.BlockSpec((n_seq, DIFF_QK, seq_len), lambda i: (i, 0, 0)),
                      pl.BlockSpec((n_seq, seq_len, DIFF_HEADS, DIFF_DV), lambda i: (i, 0, 0, 0)), st_spec, st_spec]
    vm = pltpu.VMEM
    scratch = [vm((n_tok, GLA_QK), F32), vm((n_tok, GLA_QK), F32), vm((n_tok, GLA_WIDTH), BF16),
               vm((n_tok, 2 * GLA_QK), F32), vm((n_tok, DIFF_QK), BF16), vm((n_tok, DIFF_WIDTH), BF16),
               vm((n_tok, MIX_WIDTH), F32), vm((n_tok, GLA_WIDTH), F32), vm((n_tok, DIFF_WIDTH), F32)]
    if has_cache:
        scratch += [vm((DIFF_QK, n_tok), BF16)]
    scratch += [vm((n_seq, 2, 2, GLA_PAIR_K, GLA_PAIR_V), F32),
                vm((2, n_tok, GLA_QK), BF16),
                vm((2, n_tok, GLA_QK), BF16),
                vm((2, n_tok // GLA_CHUNK, SUBLANES, GLA_QK), F32),
                vm((GLA_CHUNK, GLA_QK), F32),
                vm((GLA_CHUNK, GLA_WIDTH), F32)]
    return pl.pallas_call(
        functools.partial(_layer_kernel, seq_len=seq_len, n_seq=n_seq, rope=rope, has_cache=has_cache,
                          lam_init=lam_init),
        grid=(n_steps,),
        in_specs=in_specs,
        out_specs=out_specs,
        out_shape=out_shape,
        scratch_shapes=scratch,
        compiler_params=_params(1),
        name="layer_lat" if has_cache else "layer_ctx",
    )(*args)


def _rope_tables(seq_len):
    t = np.arange(seq_len)
    inv_freq = ROPE_BASE ** (-np.arange(ROPE_PAIRS, dtype=np.float64) / ROPE_PAIRS)
    ang_r = (t // GRID_W)[:, None] * inv_freq[None, :]
    ang_c = (t % GRID_W)[:, None] * inv_freq[None, :]
    cr, sr, cc, sc = np.cos(ang_r), np.sin(ang_r), np.cos(ang_c), np.sin(ang_c)
    cos = np.concatenate([cr, cr, cc, cc] * 2, axis=-1).astype(np.float32)
    sin = np.concatenate([-sr, sr, -sc, sc] * 2, axis=-1).astype(np.float32)
    return (jnp.asarray(cos), jnp.asarray(sin),
            jnp.asarray(np.ascontiguousarray(cos.T)), jnp.asarray(np.ascontiguousarray(sin.T)))


def kernel(x_prompt, x_sample, cache_diff_k, cache_diff_v, state_gla_fwd, state_gla_bwd, c, c_ctx,
           norm_gain, w_mod, b_mod, w_in, w_gla_alpha, b_gla_alpha, diff_lambda,
           gla_head_gain, diff_head_gain, w_out, final_gain):
    bp, lp, d = x_prompt.shape
    bs, ls, _ = x_sample.shape
    depth = norm_gain.shape[0]
    assert depth == 1 and d == D_MODEL and bs + 1 <= MOD_ROWS and w_in.shape[2] == _R_END
    l = 0
    lam_init = 0.8 - 0.6 * math.exp(-0.3 * l)

    mod, wt, wo = _setup(c_ctx[None, :], c, w_mod[l], b_mod, jnp.swapaxes(w_in[l], 0, 1), w_out[l])
    shared = (norm_gain, mod, wt, wo, w_gla_alpha[l], b_gla_alpha[l], diff_lambda[l],
              gla_head_gain, diff_head_gain, final_gain[None, :], lam_init)

    y_p, dkt, dv, s_f, s_b = _layer(x_prompt.reshape(bp * lp, d), lp, ROW_TILE // lp, *shared)
    y_prompt = y_p.reshape(bp, lp, d)
    new_diff_k = jnp.transpose(dkt.reshape(bp, DIFF_HEADS, 2, DIFF_DH, lp), (0, 4, 1, 2, 3))[:, None]
    new_diff_v = dv[:, None]
    new_gla_fwd = s_f[:, None]
    new_gla_bwd = s_b[:, None]

    past = cache_diff_k.shape[2]
    ckt = jnp.transpose(cache_diff_k[:, l], (0, 2, 3, 4, 1)).reshape(bs, DIFF_QK, past)
    cv = cache_diff_v[:, l].reshape(bs, past, DIFF_WIDTH)
    y_s = _layer(x_sample.reshape(bs * ls, d), ls, 1, *shared, rope_tabs=_rope_tables(ls),
                 cache=(ckt, cv, state_gla_fwd[:, l], state_gla_bwd[:, l]))[0]
    y_sample = y_s.reshape(bs, ls, d)

    return (y_prompt, y_sample, new_diff_k, new_diff_v, new_gla_fwd, new_gla_bwd)
```

```python
import functools
import math

import numpy as np
import jax
import jax.numpy as jnp
from jax import lax
from jax.experimental import pallas as pl
from jax.experimental.pallas import tpu as pltpu

F32 = jnp.float32
BF16 = jnp.bfloat16

D_MODEL = 1024
GRID_W = 64
GLA_HEADS = 4
GLA_DK = 64
GLA_DV = 128
GLA_QK = GLA_HEADS * GLA_DK
GLA_WIDTH = GLA_HEADS * GLA_DV
GLA_GATE_RANK = 16
GLA_GATE_TEMP = 16.0
GLA_CHUNK = 64
DIFF_HEADS = 4
DIFF_DH = 64
DIFF_DV = 2 * DIFF_DH
DIFF_QK = DIFF_HEADS * 2 * DIFF_DH
DIFF_WIDTH = DIFF_HEADS * DIFF_DV
MIX_WIDTH = GLA_WIDTH + DIFF_WIDTH
ROPE_PAIRS = DIFF_DH // 4
ROPE_BASE = 10000.0
EPS = 1e-6
LOG2E = math.log2(math.e)

LANES = 128
SUBLANES = 8
MOD_ROWS = 8
ROW_TILE = 512
SETUP_STEPS = 8
DIFF_QB = 256
DIFF_GROUP_KEYS = 4096
GLA_SUPER = 256
GLA_GROUP = 2
GLA_PAIR_K = 2 * GLA_DK
GLA_PAIR_V = 2 * GLA_DV
GLA_SAFE_LOG_DECAY = -40.0
VMEM_LIMIT = 58 * 1024 * 1024

_R_GQ = 0
_R_GK = _R_GQ + GLA_QK
_R_GV = _R_GK + GLA_QK
_R_LR = _R_GV + GLA_WIDTH
_R_DQ = _R_LR + 2 * GLA_GATE_RANK
_R_DK = _R_DQ + DIFF_QK
_R_DV = _R_DK + DIFF_QK
_R_GATE = _R_DV + DIFF_WIDTH
_R_END = _R_GATE + MIX_WIDTH


def _bf(x):
    return x.astype(BF16)


def _dot(a, b):
    return jnp.dot(a, b, preferred_element_type=F32)


def _dot_nt(a, b):
    return lax.dot_general(a, b, (((1,), (1,)), ((), ())), preferred_element_type=F32)


def _dot_tn(a, b):
    return lax.dot_general(a, b, (((0,), (0,)), ((), ())), preferred_element_type=F32)


def _params(n_parallel=0, n_arbitrary=0):
    sem = ("parallel",) * n_parallel + ("arbitrary",) * n_arbitrary
    return pltpu.CompilerParams(dimension_semantics=sem, vmem_limit_bytes=VMEM_LIMIT)


def _iota(shape, axis):
    return lax.broadcasted_iota(jnp.int32, shape, axis)


def _rows(ref, start, size):
    return ref.at[pl.ds(start, size)]


def _setup_kernel(cc_ref, c_ref, wm_ref, bm_ref, wi_ref, wo_ref, mod_ref, wib_ref, wob_ref):
    row = _iota((MOD_ROWS, D_MODEL), 0)
    cvecs = jnp.where(row == 0, cc_ref[...], 0.0)
    for b in range(c_ref.shape[0]):
        cvecs = jnp.where(row == 1 + b, c_ref[b:b + 1, :], cvecs)
    mod_ref[...] = _dot(_bf(_silu(cvecs)), _bf(wm_ref[...])) + bm_ref[...]
    wib_ref[...] = _bf(wi_ref[...])
    wob_ref[...] = _bf(wo_ref[...])


def _setup(c_ctx, c, w_mod, b_mod, w_in_t, w_out):
    n_mod = w_mod.shape[1]
    n_in = w_in_t.shape[0]
    tm = n_mod // SETUP_STEPS
    ti = pl.cdiv(n_in, SETUP_STEPS * 2 * SUBLANES) * 2 * SUBLANES
    to = w_out.shape[0] // SETUP_STEPS
    assert tm % LANES == 0 and tm * SETUP_STEPS == n_mod
    rows = lambda r: pl.BlockSpec((r, D_MODEL), lambda i: (i, 0))
    cols = lambda r: pl.BlockSpec((r, tm), lambda i: (0, i))
    return pl.pallas_call(
        _setup_kernel,
        grid=(SETUP_STEPS,),
        in_specs=[pl.BlockSpec(c_ctx.shape, lambda i: (0, 0)), pl.BlockSpec(c.shape, lambda i: (0, 0)),
                  cols(D_MODEL), cols(1), rows(ti), rows(to)],
        out_specs=[cols(MOD_ROWS), rows(ti), rows(to)],
        out_shape=[jax.ShapeDtypeStruct((MOD_ROWS, n_mod), F32), jax.ShapeDtypeStruct(w_in_t.shape, BF16),
                   jax.ShapeDtypeStruct(w_out.shape, BF16)],
        compiler_params=_params(1),
        name="setup",
    )(c_ctx, c, w_mod, b_mod, w_in_t, w_out)


def _log_sigmoid(z):
    return jnp.minimum(z, 0.0) - jnp.log1p(jnp.exp(-jnp.abs(z)))


def _silu(x):
    h = 0.5 * x
    return h + h * jnp.tanh(h)


def _rope_lanes(x, cos, sin):
    lane = _iota(x.shape, 1)
    up = pltpu.roll(x, ROPE_PAIRS, axis=1)
    dn = pltpu.roll(x, LANES - ROPE_PAIRS, axis=1)
    partner = jnp.where((lane & ROPE_PAIRS) == 0, dn, up)
    return x * cos + partner * sin


def _rope_rows(x, cos, sin):
    p = ROPE_PAIRS
    parts = []
    for g in range(x.shape[0] // (2 * p)):
        parts += [x[(2 * g + 1) * p:(2 * g + 2) * p], x[2 * g * p:(2 * g + 1) * p]]
    return x * cos + jnp.concatenate(parts, axis=0) * sin


def _projection_phase(tiles, gain, shift, scale, wt_ref, wa, ba, sc, rope_refs):
    hbs = {}

    def normalise(i):
        x = tiles[i][2]()
        ms = jnp.mean(x * x, axis=-1, keepdims=True)
        h = x * lax.rsqrt(ms + EPS) * gain
        hbs[i] = _bf(h * (1.0 + scale) + shift)

    pending = []

    def issue(matmul, finish):
        r = matmul()
        while pending:
            pending.pop()()
        pending.append(lambda: finish(r))

    normalise(0)
    for i, (row0, n_rows, _, kt_store, dv_store) in enumerate(tiles):
        rows = pl.ds(row0, n_rows)
        hb = hbs.pop(i)

        def sec(a, b, hb=hb):
            return lambda: _dot_nt(hb, wt_ref[a:b, :])

        def put(name, f, rows=rows):
            def finish(r):
                sc[name][rows, :] = f(r)
            return finish

        def finish_kl(klt, rows=rows, kt_store=kt_store):
            z = _dot_tn(_bf(klt[DIFF_QK:]), wa) + ba
            sc["g"][rows, :] = _log_sigmoid(z) * (1.0 / GLA_GATE_TEMP)
            if rope_refs is None:
                kt_store(slice(0, DIFF_QK), klt[:DIFF_QK])
            else:
                cost, sint = rope_refs[2][:, rows], rope_refs[3][:, rows]
                for hd in range(DIFF_HEADS):
                    sl = slice(hd * LANES, (hd + 1) * LANES)
                    kt_store(sl, _rope_rows(klt[sl, :], cost, sint))

        def finish_dq(dq, rows=rows):
            dq = dq * (DIFF_DH ** -0.5 * LOG2E)
            if rope_refs is None:
                sc["dq"][rows, :] = _bf(dq)
            else:
                cos, sin = rope_refs[0][rows, :], rope_refs[1][rows, :]
                for hd in range(DIFF_HEADS):
                    sl = slice(hd * LANES, (hd + 1) * LANES)
                    sc["dq"][rows, sl] = _bf(_rope_lanes(dq[:, sl], cos, sin))

        issue(sec(_R_DV, _R_GATE), dv_store)
        issue(sec(_R_GATE, _R_END), put("gate", _silu))
        if i + 1 < len(tiles):
            normalise(i + 1)
        issue(lambda hb=hb: _dot_nt(jnp.concatenate([wt_ref[_R_DK:_R_DV, :], wt_ref[_R_LR:_R_DQ, :]], axis=0), hb),
              finish_kl)
        issue(sec(_R_DQ, _R_DK), finish_dq)
        issue(sec(_R_GV, _R_LR), put("gv", _bf))
        issue(sec(_R_GQ, _R_GK), put("gq", lambda r: r * (GLA_DK ** -0.5)))
        issue(sec(_R_GK, _R_GV), put("gk", lambda r: r))
    while pending:
        pending.pop()()


def _tile4(x):
    return jnp.concatenate([x, x, x, x], axis=0)


def _pair_mask():
    return (_iota((GLA_PAIR_K, GLA_PAIR_V), 0) // GLA_DK) == (_iota((GLA_PAIR_K, GLA_PAIR_V), 1) // GLA_DV)


def _gla_recurrence_step(qi, kl, vb, e_col, s_ref, b, d, p_mask):
    inter = []
    for p in range(2):
        ks = slice(p * GLA_PAIR_K, (p + 1) * GLA_PAIR_K)
        s = s_ref[b, d, p]
        inter.append(_dot(qi[:, ks], _bf(s)))
        upd = _dot_tn(kl[:, ks], vb[:, p * GLA_PAIR_V:(p + 1) * GLA_PAIR_V])
        s_ref[b, d, p] = s * e_col[ks] + jnp.where(p_mask, upd, 0.0)
    return jnp.concatenate(inter, axis=1)


def _gla_fast(q_ref, k_ref, v_ref, g_ref, o_ref, s_ref, qi_ref, kl_ref, et_ref, *, seq_len, n_seq):
    c_len, sc_len = GLA_CHUNK, GLA_SUPER
    nc = seq_len // c_len
    cps = sc_len // c_len
    n_super = n_seq * seq_len // sc_len
    r = _iota((sc_len, 2 * sc_len), 0)
    c = _iota((sc_len, 2 * sc_len), 1) & (sc_len - 1)
    same = (r // c_len) == (c // c_len)
    tri2 = [(same & (c <= r)).astype(BF16), (same & (c >= r)).astype(BF16)]
    rt = _iota((c_len, GLA_HEADS * c_len), 0)
    cs = _iota((c_len, GLA_HEADS * c_len), 1) & (c_len - 1)
    keep = [cs <= rt, cs >= rt]
    k_mask = (_iota((GLA_HEADS * c_len, GLA_QK), 0) // c_len) == (_iota((GLA_HEADS * c_len, GLA_QK), 1) // GLA_DK)
    v_mask = (_iota((GLA_HEADS * c_len, GLA_WIDTH), 0) // c_len) == (_iota((GLA_HEADS * c_len, GLA_WIDTH), 1) // GLA_DV)
    p_mask = _pair_mask()
    zb = jnp.zeros((), BF16)
    dirs = range(2)

    for g0 in range(0, n_super, GLA_GROUP):
        group = range(g0, min(g0 + GLA_GROUP, n_super))
        rows = {u: slice(u * sc_len, (u + 1) * sc_len) for u in group}
        bs = {}
        for u in group:
            for d in dirs:
                g = g_ref[rows[u], d * GLA_QK:(d + 1) * GLA_QK]
                hi = _bf(g)
                lo = _bf(g - hi.astype(F32))
                bs[u, d] = _dot(tri2[d], jnp.concatenate([hi, lo], axis=0))
        qib, kob = {}, {}
        for u in group:
            q = q_ref[rows[u], :]
            k = k_ref[rows[u], :]
            for d in dirs:
                b = bs[u, d]
                qi = _bf(q * jnp.exp(b))
                ko = k * jnp.exp(-b)
                kls = []
                for j in range(cps):
                    last = j * c_len + (0 if d else c_len - 1)
                    e_tot = jnp.exp(b[last:last + 1, :])
                    et_ref[d, u * cps + j] = jnp.broadcast_to(e_tot, (SUBLANES, GLA_QK))
                    kls.append(_bf(ko[j * c_len:(j + 1) * c_len] * e_tot))
                qib[u, d], kob[u, d] = qi, _bf(ko)
                qi_ref[d, rows[u], :] = qi
                kl_ref[d, rows[u], :] = jnp.concatenate(kls, axis=0)
        atts = {}
        for u in group:
            for j in range(cps):
                cr = slice(j * c_len, (j + 1) * c_len)
                for d in dirs:
                    atts[u, j, d] = _dot_nt(qib[u, d][cr], jnp.where(k_mask, _tile4(kob[u, d][cr]), zb))
        for u in group:
            vb = v_ref[rows[u], :]
            for j in range(cps):
                a2 = jnp.concatenate([_bf(jnp.where(keep[d], atts[u, j, d], 0.0)) for d in dirs], axis=0)
                v_bd = jnp.where(v_mask, _tile4(vb[j * c_len:(j + 1) * c_len]), zb)
                r2 = _dot(a2, v_bd)
                o_ref[u * sc_len + j * c_len:u * sc_len + (j + 1) * c_len, :] = r2[:c_len] + r2[c_len:]

    for i in range(nc):
        for b in range(n_seq):
            for d in dirs:
                ci = b * nc + ((nc - 1 - i) if d else i)
                rows_c = slice(ci * c_len, (ci + 1) * c_len)
                e_col = et_ref[d, ci].T[:, 0:1]
                o_ref[rows_c, :] += _gla_recurrence_step(qi_ref[d, rows_c, :], kl_ref[d, rows_c, :],
                                                         v_ref[rows_c, :], e_col, s_ref, b, d, p_mask)


def _gla_slow(q_ref, k_ref, v_ref, g_ref, o_ref, s_ref, b_ref, vf_ref, *, seq_len, n_seq):
    c_len = GLA_CHUNK
    nc = seq_len // c_len
    r64 = _iota((c_len, c_len), 0)
    c64 = _iota((c_len, c_len), 1)
    p_mask = _pair_mask()
    expand = ((_iota((GLA_QK, GLA_WIDTH), 0) // GLA_DK) == (_iota((GLA_QK, GLA_WIDTH), 1) // GLA_DV)).astype(BF16)
    t_idx = _iota((c_len, GLA_QK), 0)
    for b in range(n_seq):
        for d in range(2):
            reverse = bool(d)
            tri = ((c64 >= r64) if reverse else (c64 <= r64)).astype(F32)

            def step(i, carry, b=b, d=d, reverse=reverse, tri=tri):
                ci = b * nc + ((nc - 1 - i) if reverse else i)
                base = pl.multiple_of(ci * c_len, c_len)
                rows = pl.ds(base, c_len)
                q = q_ref[rows, :]
                k = k_ref[rows, :]
                vb = v_ref[rows, :]
                g = g_ref[rows, d * GLA_QK:(d + 1) * GLA_QK]
                bcum = jnp.dot(tri, g, precision=lax.Precision.HIGHEST, preferred_element_type=F32)
                b_tot = jnp.sum(g, axis=0, keepdims=True)
                b_ref[...] = bcum
                vf_ref[...] = vb.astype(F32)

                def key_row(s, acc):
                    b_s = b_ref[pl.ds(s, 1), :]
                    k_s = k_ref[pl.ds(base + s, 1), :]
                    v_s = vf_ref[pl.ds(s, 1), :]
                    visible = (t_idx <= s) if reverse else (t_idx >= s)
                    w = jnp.where(visible, q * jnp.exp(jnp.minimum(bcum - b_s, 0.0)), 0.0) * k_s
                    return acc + _dot(_bf(w), expand) * v_s

                o = lax.fori_loop(0, c_len, key_row, jnp.zeros((c_len, GLA_WIDTH), F32))
                e_col = jnp.broadcast_to(jnp.exp(b_tot), (SUBLANES, GLA_QK)).T[:, 0:1]
                o = o + _gla_recurrence_step(_bf(q * jnp.exp(bcum)), _bf(k * jnp.exp(b_tot - bcum)), vb, e_col,
                                             s_ref, b, d, p_mask)
                if reverse:
                    o_ref[rows, :] += o
                else:
                    o_ref[rows, :] = o
                return carry

            lax.fori_loop(0, nc, step, 0)


def _gla_sequences(q_ref, k_ref, v_ref, g_ref, o_ref, gs, init_states, state_out, *, seq_len, n_seq):
    s_ref = gs["s"]
    zero = jnp.zeros((GLA_DK, GLA_DV), F32)
    for b in range(n_seq):
        for d in range(2):
            for p in range(2):
                if init_states is not None:
                    s0 = init_states[b][d]
                    top = jnp.concatenate([s0[2 * p], zero], axis=1)
                    bot = jnp.concatenate([zero, s0[2 * p + 1]], axis=1)
                    s_ref[b, d, p] = jnp.concatenate([top, bot], axis=0)
                else:
                    s_ref[b, d, p] = jnp.zeros((GLA_PAIR_K, GLA_PAIR_V), F32)
    worst = jnp.zeros((1, 2 * GLA_QK), F32)
    for ci in range(n_seq * seq_len // GLA_CHUNK):
        worst = jnp.minimum(worst, jnp.sum(g_ref[ci * GLA_CHUNK:(ci + 1) * GLA_CHUNK, :], axis=0, keepdims=True))
    lax.cond(jnp.min(worst) >= GLA_SAFE_LOG_DECAY,
             functools.partial(_gla_fast, q_ref, k_ref, v_ref, g_ref, o_ref, s_ref, gs["qi"], gs["kl"], gs["et"],
                               seq_len=seq_len, n_seq=n_seq),
             functools.partial(_gla_slow, q_ref, k_ref, v_ref, g_ref, o_ref, s_ref, gs["b"], gs["vf"],
                               seq_len=seq_len, n_seq=n_seq))
    if state_out is not None:
        for b in range(n_seq):
            for d in range(2):
                for h in range(GLA_HEADS):
                    p, j = divmod(h, 2)
                    state_out[b][d][h] = s_ref[b, d, p, j * GLA_DK:(j + 1) * GLA_DK, j * GLA_DV:(j + 1) * GLA_DV]


def _diff_group(group, lam):
    scores = []
    for q, kts, _, _ in group:
        first = _iota(q.shape, 1) < DIFF_DH
        zb = jnp.zeros((), BF16)
        qs = jnp.concatenate([jnp.where(first, q, zb), jnp.where(first, zb, q)], axis=0)
        scores.append([_dot(qs, kt) for kt in kts])
    maxes = [functools.reduce(jnp.maximum, [jnp.max(s, axis=-1, keepdims=True) for s in ss]) for ss in scores]
    es = [[_bf(jnp.exp2(s - m)) for s in ss] for ss, m in zip(scores, maxes)]
    for (q, _, vas, store), e_list in zip(group, es):
        tq = q.shape[0]
        r = functools.reduce(jnp.add, [_dot(e, va) for e, va in zip(e_list, vas)])
        n = r[:, :DIFF_DV] / r[:, DIFF_DV:]
        store(n[:tq] - lam * n[tq:])


def _diff_phase(heads, lam, n_keys, seq_len):
    group_size = max(1, DIFF_GROUP_KEYS // n_keys)
    blocks = seq_len // DIFF_QB

    def load_kv1(load_kv):
        kts, vs = load_kv()
        return kts, [_with_ones(v) for v in vs]

    if blocks == 1:
        for g0 in range(0, len(heads), group_size):
            _diff_group([(load_q(row0), *load_kv1(load_kv), functools.partial(store, row0))
                         for row0, load_kv, load_q, store in heads[g0:g0 + group_size]], lam)
        return
    group_size = min(group_size, blocks)
    assert blocks % group_size == 0
    for row0, load_kv, load_q, store in heads:
        kts, vas = load_kv1(load_kv)

        def body(gi, carry, row0=row0, load_q=load_q, store=store, kts=kts, vas=vas):
            base = pl.multiple_of(row0 + gi * (group_size * DIFF_QB), DIFF_QB)
            rs = [base + t * DIFF_QB for t in range(group_size)]
            _diff_group([(load_q(r), kts, vas, functools.partial(store, r)) for r in rs], lam)
            return carry

        lax.fori_loop(0, blocks // group_size, body, 0)


def _diff_lambda(lam_ref, lam_init):
    lp = lam_ref[...]
    return (jnp.exp(jnp.sum(lp[0:1] * lp[1:2], axis=-1, keepdims=True))
            - jnp.exp(jnp.sum(lp[2:3] * lp[3:4], axis=-1, keepdims=True)) + lam_init)


def _with_ones(v):
    return jnp.concatenate([v, jnp.ones(v.shape, v.dtype)], axis=1)


def _head_rmsnorm(x, gain):
    ms = jnp.mean(x * x, axis=-1, keepdims=True)
    return x * lax.rsqrt(ms + EPS) * gain


def _merge_rows(og_ref, od_ref, sg_ref, x, gt, gg, dg, wo_ref, fg, rows):
    slabs = []
    for h in range(GLA_HEADS):
        slabs.append(_head_rmsnorm(og_ref[rows, h * GLA_DV:(h + 1) * GLA_DV], gg))
    for h in range(DIFF_HEADS):
        slabs.append(_head_rmsnorm(od_ref[rows, h * DIFF_DV:(h + 1) * DIFF_DV], dg))
    o = jnp.concatenate(slabs, axis=-1) * sg_ref[rows, :]
    xn = x + gt * _dot(_bf(o), wo_ref[...])
    ms = jnp.mean(xn * xn, axis=-1, keepdims=True)
    return xn * lax.rsqrt(ms + EPS) * fg


def _layer_kernel(*refs, seq_len, n_seq, rope, has_cache, lam_init):
    it = iter(refs)
    x_ref, gain_ref, mod_ref, wt_ref, wo_ref, wa_ref, ba_ref, lam_ref, gg_ref, dg_ref, fg_ref = (next(it) for _ in range(11))
    rope_refs = tuple(next(it) for _ in range(4)) if rope else None
    if has_cache:
        ckt_ref, cv_ref, s0f_ref, s0b_ref = (next(it) for _ in range(4))
    y_ref = next(it)
    if not has_cache:
        kt_out, dv_out, sf_out, sb_out = (next(it) for _ in range(4))
    names = ["gq", "gk", "gv", "g", "dq", "dv", "gate", "og", "od"] + (["dkt"] if has_cache else [])
    sc = {n: next(it) for n in names}
    gs = {n: next(it) for n in ["s", "qi", "kl", "et", "b", "vf"]}

    d = D_MODEL
    n_tok = seq_len * n_seq
    mod_row = (1 + pl.program_id(0)) if has_cache else 0
    shift = mod_ref[pl.ds(mod_row, 1), 0:d]
    scale = mod_ref[pl.ds(mod_row, 1), d:2 * d]
    gt = mod_ref[pl.ds(mod_row, 1), 2 * d:3 * d]
    gain = gain_ref[...]

    zero = jnp.zeros((GLA_GATE_RANK, GLA_QK), F32)
    wa = _bf(jnp.concatenate([jnp.concatenate([wa_ref[0], zero], axis=1),
                              jnp.concatenate([zero, wa_ref[1]], axis=1)], axis=0))
    ba = jnp.concatenate([ba_ref[0:1, :], ba_ref[1:2, :]], axis=1)

    tiles = []
    for t0 in range(0, n_tok, ROW_TILE):
        if has_cache:
            def kt_store(sl, val, t0=t0):
                sc["dkt"][sl, t0:t0 + ROW_TILE] = _bf(val)

            def dv_store(val, t0=t0):
                sc["dv"][t0:t0 + ROW_TILE, :] = _bf(val)
        else:
            def kt_store(sl, val, t0=t0):
                for j in range(ROW_TILE // seq_len):
                    kt_out[(t0 // seq_len) + j, sl, :] = val[:, j * seq_len:(j + 1) * seq_len]

            def dv_store(val, t0=t0):
                sc["dv"][t0:t0 + ROW_TILE, :] = _bf(val)
                for j in range(ROW_TILE // seq_len):
                    dv_out[(t0 // seq_len) + j] = val[j * seq_len:(j + 1) * seq_len, :].reshape(
                        seq_len, DIFF_HEADS, DIFF_DV)

        tiles.append((t0, ROW_TILE, lambda t0=t0: x_ref[t0:t0 + ROW_TILE, :], kt_store, dv_store))
    _projection_phase(tiles, gain, shift, scale, wt_ref, wa, ba, sc, rope_refs)

    init = [(s0f_ref.at[0], s0b_ref.at[0])] if has_cache else None
    out = None if has_cache else [(sf_out.at[b], sb_out.at[b]) for b in range(n_seq)]
    _gla_sequences(sc["gq"], sc["gk"], sc["gv"], sc["g"], sc["og"], gs, init, out, seq_len=seq_len, n_seq=n_seq)

    lam = _diff_lambda(lam_ref, lam_init)
    heads = []
    for b in range(n_seq):
        for hd in range(DIFF_HEADS):
            sl = slice(hd * LANES, (hd + 1) * LANES)

            def load_kv(b=b, sl=sl):
                if has_cache:
                    return ([_bf(ckt_ref[0, sl, :]), sc["dkt"][sl, :]], [_bf(cv_ref[0, :, sl]), sc["dv"][:, sl]])
                return [_bf(kt_out[b, sl, :])], [sc["dv"][b * seq_len:(b + 1) * seq_len, sl]]

            def load_q(r, sl=sl):
                return sc["dq"][pl.ds(r, DIFF_QB), sl]

            def store(r, val, sl=sl):
                sc["od"][pl.ds(r, DIFF_QB), sl] = val

            heads.append((b * seq_len, load_kv, load_q, store))
    _diff_phase(heads, lam, seq_len + (ckt_ref.shape[2] if has_cache else 0), seq_len)

    gg = gg_ref[...]
    dg = dg_ref[...] * (1.0 - lam_init)
    fg = fg_ref[...]
    def merge_tile(t, carry=None):
        start = t * ROW_TILE
        rows = pl.ds(start if isinstance(start, int) else pl.multiple_of(start, ROW_TILE), ROW_TILE)
        y_ref[rows, :] = _merge_rows(sc["og"], sc["od"], sc["gate"], x_ref[rows, :], gt, gg, dg, wo_ref, fg, rows)
        return carry

    if n_tok == ROW_TILE:
        merge_tile(0)
    else:
        lax.fori_loop(0, n_tok // ROW_TILE, merge_tile, None)


def _layer(x2d, seq_len, n_seq, gain, mod, wt, wo, wa, ba, lam_p, gg, dg, fg, lam_init, rope_tabs=None, cache=None):
    t = x2d.shape[0]
    n_tok = seq_len * n_seq
    n_steps = t // n_tok
    rope = rope_tabs is not None
    has_cache = cache is not None
    assert n_tok % ROW_TILE == 0 and ROW_TILE % seq_len in (0, ROW_TILE) and seq_len % GLA_SUPER == 0

    def whole(a, single=False):
        kw = {"pipeline_mode": pl.Buffered(1)} if single else {}
        return pl.BlockSpec(a.shape, lambda i: (0,) * a.ndim, **kw)

    io_kw = {"pipeline_mode": pl.Buffered(1)} if n_steps <= 2 else {}
    args = [x2d, gain, mod, wt, wo, wa, ba, lam_p, gg, dg, fg]
    in_specs = [pl.BlockSpec((n_tok, D_MODEL), lambda i: (i, 0), **io_kw), whole(gain), whole(mod), whole(wt, True),
                whole(wo, True), whole(wa), whole(ba), whole(lam_p), whole(gg), whole(dg), whole(fg)]
    if rope:
        args += list(rope_tabs)
        in_specs += [whole(a, True) for a in rope_tabs]
    sds = jax.ShapeDtypeStruct
    out_shape = [sds((t, D_MODEL), F32)]
    out_specs = [pl.BlockSpec((n_tok, D_MODEL), lambda i: (i, 0), **io_kw)]
    if has_cache:
        ckt, cv, s0f, s0b = cache
        args += [ckt, cv, s0f, s0b]
        st_spec = pl.BlockSpec((1,) + s0f.shape[1:], lambda i: (i, 0, 0, 0))
        in_specs += [pl.BlockSpec((1,) + ckt.shape[1:], lambda i: (i, 0, 0)),
                     pl.BlockSpec((1,) + cv.shape[1:], lambda i: (i, 0, 0)), st_spec, st_spec]
    else:
        n_b = t // seq_len
        out_shape += [sds((n_b, DIFF_QK, seq_len), F32), sds((n_b, seq_len, DIFF_HEADS, DIFF_DV), F32),
                      sds((n_b, GLA_HEADS, GLA_DK, GLA_DV), F32), sds((n_b, GLA_HEADS, GLA_DK, GLA_DV), F32)]
        st_spec = pl.BlockSpec((n_seq, GLA_HEADS, GLA_DK, GLA_DV), lambda i: (i, 0, 0, 0))
        out_specs += [pl.BlockSpec((n_seq, DIFF_QK, seq_len), lambda i: (i, 0, 0)),
                      pl.BlockSpec((n_seq, seq_len, DIFF_HEADS, DIFF_DV), lambda i: (i, 0, 0, 0)), st_spec, st_spec]
    vm = pltpu.VMEM
    scratch = [vm((n_tok, GLA_QK), F32), vm((n_tok, GLA_QK), F32), vm((n_tok, GLA_WIDTH), BF16),
               vm((n_tok, 2 * GLA_QK), F32), vm((n_tok, DIFF_QK), BF16), vm((n_tok, DIFF_WIDTH), BF16),
               vm((n_tok, MIX_WIDTH), F32), vm((n_tok, GLA_WIDTH), F32), vm((n_tok, DIFF_WIDTH), F32)]
    if has_cache:
        scratch += [vm((DIFF_QK, n_tok), BF16)]
    scratch += [vm((n_seq, 2, 2, GLA_PAIR_K, GLA_PAIR_V), F32),
                vm((2, n_tok, GLA_QK), BF16),
                vm((2, n_tok, GLA_QK), BF16),
                vm((2, n_tok // GLA_CHUNK, SUBLANES, GLA_QK), F32),
                vm((GLA_CHUNK, GLA_QK), F32),
                vm((GLA_CHUNK, GLA_WIDTH), F32)]
    return pl.pallas_call(
        functools.partial(_layer_kernel, seq_len=seq_len, n_seq=n_seq, rope=rope, has_cache=has_cache,
                          lam_init=lam_init),
        grid=(n_steps,),
        in_specs=in_specs,
        out_specs=out_specs,
        out_shape=out_shape,
        scratch_shapes=scratch,
        compiler_params=_params(1),
        name="layer_lat" if has_cache else "layer_ctx",
    )(*args)


def _rope_tables(seq_len):
    t = np.arange(seq_len)
    inv_freq = ROPE_BASE ** (-np.arange(ROPE_PAIRS, dtype=np.float64) / ROPE_PAIRS)
    ang_r = (t // GRID_W)[:, None] * inv_freq[None, :]
    ang_c = (t % GRID_W)[:, None] * inv_freq[None, :]
    cr, sr, cc, sc = np.cos(ang_r), np.sin(ang_r), np.cos(ang_c), np.sin(ang_c)
    cos = np.concatenate([cr, cr, cc, cc] * 2, axis=-1).astype(np.float32)
    sin = np.concatenate([-sr, sr, -sc, sc] * 2, axis=-1).astype(np.float32)
    return (jnp.asarray(cos), jnp.asarray(sin),
            jnp.asarray(np.ascontiguousarray(cos.T)), jnp.asarray(np.ascontiguousarray(sin.T)))


def kernel(x_prompt, x_sample, cache_diff_k, cache_diff_v, state_gla_fwd, state_gla_bwd, c, c_ctx,
           norm_gain, w_mod, b_mod, w_in, w_gla_alpha, b_gla_alpha, diff_lambda,
           gla_head_gain, diff_head_gain, w_out, final_gain):
    bp, lp, d = x_prompt.shape
    bs, ls, _ = x_sample.shape
    depth = norm_gain.shape[0]
    assert depth == 1 and d == D_MODEL and bs + 1 <= MOD_ROWS and w_in.shape[2] == _R_END
    l = 0
    lam_init = 0.8 - 0.6 * math.exp(-0.3 * l)

    mod, wt, wo = _setup(c_ctx[None, :], c, w_mod[l], b_mod, jnp.swapaxes(w_in[l], 0, 1), w_out[l])
    shared = (norm_gain, mod, wt, wo, w_gla_alpha[l], b_gla_alpha[l], diff_lambda[l],
              gla_head_gain, diff_head_gain, final_gain[None, :], lam_init)

    y_p, dkt, dv, s_f, s_b = _layer(x_prompt.reshape(bp * lp, d), lp, ROW_TILE // lp, *shared)
    y_prompt = y_p.reshape(bp, lp, d)
    new_diff_k = jnp.transpose(dkt.reshape(bp, DIFF_HEADS, 2, DIFF_DH, lp), (0, 4, 1, 2, 3))[:, None]
    new_diff_v = dv[:, None]
    new_gla_fwd = s_f[:, None]
    new_gla_bwd = s_b[:, None]

    past = cache_diff_k.shape[2]
    ckt = jnp.transpose(cache_diff_k[:, l], (0, 2, 3, 4, 1)).reshape(bs, DIFF_QK, past)
    cv = cache_diff_v[:, l].reshape(bs, past, DIFF_WIDTH)
    y_s = _layer(x_sample.reshape(bs * ls, d), ls, 1, *shared, rope_tabs=_rope_tables(ls),
                 cache=(ckt, cv, state_gla_fwd[:, l], state_gla_bwd[:, l]))[0]
    y_sample = y_s.reshape(bs, ls, d)

    return (y_prompt, y_sample, new_diff_k, new_diff_v, new_gla_fwd, new_gla_bwd)
```

```python
import functools
import math

import numpy as np
import jax
import jax.numpy as jnp
from jax import lax
from jax.experimental import pallas as pl
from jax.experimental.pallas import tpu as pltpu

F32 = jnp.float32
BF16 = jnp.bfloat16

D_MODEL = 1024
GRID_W = 64
GLA_HEADS = 4
GLA_DK = 64
GLA_DV = 128
GLA_QK = GLA_HEADS * GLA_DK
GLA_WIDTH = GLA_HEADS * GLA_DV
GLA_GATE_RANK = 16
GLA_GATE_TEMP = 16.0
GLA_CHUNK = 64
DIFF_HEADS = 4
DIFF_DH = 64
DIFF_DV = 2 * DIFF_DH
DIFF_QK = DIFF_HEADS * 2 * DIFF_DH
DIFF_WIDTH = DIFF_HEADS * DIFF_DV
MIX_WIDTH = GLA_WIDTH + DIFF_WIDTH
ROPE_PAIRS = DIFF_DH // 4
ROPE_BASE = 10000.0
EPS = 1e-6
LOG2E = math.log2(math.e)

LANES = 128
SUBLANES = 8
MOD_ROWS = 8
ROW_TILE = 512
SETUP_STEPS = 8
DIFF_QB = 256
DIFF_GROUP_KEYS = 4096
GLA_SUPER = 256
GLA_GROUP = 2
GLA_PAIR_K = 2 * GLA_DK
GLA_PAIR_V = 2 * GLA_DV
GLA_SAFE_LOG_DECAY = -40.0
VMEM_LIMIT = 58 * 1024 * 1024

_R_GQ = 0
_R_GK = _R_GQ + GLA_QK
_R_GV = _R_GK + GLA_QK
_R_LR = _R_GV + GLA_WIDTH
_R_DQ = _R_LR + 2 * GLA_GATE_RANK
_R_DK = _R_DQ + DIFF_QK
_R_DV = _R_DK + DIFF_QK
_R_GATE = _R_DV + DIFF_WIDTH
_R_END = _R_GATE + MIX_WIDTH


def _bf(x):
    return x.astype(BF16)


def _dot(a, b):
    return jnp.dot(a, b, preferred_element_type=F32)


def _dot_nt(a, b):
    return lax.dot_general(a, b, (((1,), (1,)), ((), ())), preferred_element_type=F32)


def _dot_tn(a, b):
    return lax.dot_general(a, b, (((0,), (0,)), ((), ())), preferred_element_type=F32)


def _params(n_parallel=0, n_arbitrary=0):
    sem = ("parallel",) * n_parallel + ("arbitrary",) * n_arbitrary
    return pltpu.CompilerParams(dimension_semantics=sem, vmem_limit_bytes=VMEM_LIMIT)


def _iota(shape, axis):
    return lax.broadcasted_iota(jnp.int32, shape, axis)


def _rows(ref, start, size):
    return ref.at[pl.ds(start, size)]


def _setup_kernel(cc_ref, c_ref, wm_ref, bm_ref, wi_ref, wo_ref, mod_ref, wib_ref, wob_ref):
    row = _iota((MOD_ROWS, D_MODEL), 0)
    cvecs = jnp.where(row == 0, cc_ref[...], 0.0)
    for b in range(c_ref.shape[0]):
        cvecs = jnp.where(row == 1 + b, c_ref[b:b + 1, :], cvecs)
    mod_ref[...] = _dot(_bf(_silu(cvecs)), _bf(wm_ref[...])) + bm_ref[...]
    wib_ref[...] = _bf(wi_ref[...])
    wob_ref[...] = _bf(wo_ref[...])


def _setup(c_ctx, c, w_mod, b_mod, w_in_t, w_out):
    n_mod = w_mod.shape[1]
    n_in = w_in_t.shape[0]
    tm = n_mod // SETUP_STEPS
    ti = pl.cdiv(n_in, SETUP_STEPS * 2 * SUBLANES) * 2 * SUBLANES
    to = w_out.shape[0] // SETUP_STEPS
    assert tm % LANES == 0 and tm * SETUP_STEPS == n_mod
    rows = lambda r: pl.BlockSpec((r, D_MODEL), lambda i: (i, 0))
    cols = lambda r: pl.BlockSpec((r, tm), lambda i: (0, i))
    return pl.pallas_call(
        _setup_kernel,
        grid=(SETUP_STEPS,),
        in_specs=[pl.BlockSpec(c_ctx.shape, lambda i: (0, 0)), pl.BlockSpec(c.shape, lambda i: (0, 0)),
                  cols(D_MODEL), cols(1), rows(ti), rows(to)],
        out_specs=[cols(MOD_ROWS), rows(ti), rows(to)],
        out_shape=[jax.ShapeDtypeStruct((MOD_ROWS, n_mod), F32), jax.ShapeDtypeStruct(w_in_t.shape, BF16),
                   jax.ShapeDtypeStruct(w_out.shape, BF16)],
        compiler_params=_params(1),
        name="setup",
    )(c_ctx, c, w_mod, b_mod, w_in_t, w_out)


def _log_sigmoid_scaled(z, scale):
    soft = jnp.log2(1.0 + jnp.exp2(jnp.abs(z) * (-LOG2E)))
    return jnp.minimum(z, 0.0) * scale - soft * (scale * math.log(2.0))


def _silu(x):
    h = 0.5 * x
    return h + h * jnp.tanh(h)


def _rope_lanes(x, cos, sin):
    lane = _iota(x.shape, 1)
    up = pltpu.roll(x, ROPE_PAIRS, axis=1)
    dn = pltpu.roll(x, LANES - ROPE_PAIRS, axis=1)
    partner = jnp.where((lane & ROPE_PAIRS) == 0, dn, up)
    return x * cos + partner * sin


def _rope_rows(x, cos, sin):
    p = ROPE_PAIRS
    parts = []
    for g in range(x.shape[0] // (2 * p)):
        parts += [x[(2 * g + 1) * p:(2 * g + 2) * p], x[2 * g * p:(2 * g + 1) * p]]
    return x * cos + jnp.concatenate(parts, axis=0) * sin


def _projection_phase(tiles, gain, shift, scale, wt_ref, wa, ba, sc, rope_refs):
    hbs = {}

    def normalise(i):
        x = tiles[i][2]()
        ms = jnp.mean(x * x, axis=-1, keepdims=True)
        h = x * lax.rsqrt(ms + EPS) * gain
        hbs[i] = _bf(h * (1.0 + scale) + shift)

    pending = []

    def issue(matmul, finish):
        r = matmul()
        while pending:
            pending.pop()()
        pending.append(lambda: finish(r))

    normalise(0)
    for i, (row0, n_rows, _, kt_store, dv_store) in enumerate(tiles):
        rows = pl.ds(row0, n_rows)
        hb = hbs.pop(i)

        def sec(a, b, hb=hb):
            return lambda: _dot_nt(hb, wt_ref[a:b, :])

        def put(name, f, rows=rows):
            def finish(r):
                sc[name][rows, :] = f(r)
            return finish

        def finish_kl(klt, rows=rows, kt_store=kt_store):
            z = _dot_tn(_bf(klt[DIFF_QK:]), wa) + ba
            sc["g"][rows, :] = _log_sigmoid_scaled(z, 1.0 / GLA_GATE_TEMP)
            if rope_refs is None:
                kt_store(slice(0, DIFF_QK), klt[:DIFF_QK])
            else:
                cost, sint = rope_refs[2][:, rows], rope_refs[3][:, rows]
                for hd in range(DIFF_HEADS):
                    sl = slice(hd * LANES, (hd + 1) * LANES)
                    kt_store(sl, _rope_rows(klt[sl, :], cost, sint))

        def finish_dq(dq, rows=rows):
            dq = dq * (DIFF_DH ** -0.5 * LOG2E)
            if rope_refs is None:
                sc["dq"][rows, :] = _bf(dq)
            else:
                cos, sin = rope_refs[0][rows, :], rope_refs[1][rows, :]
                for hd in range(DIFF_HEADS):
                    sl = slice(hd * LANES, (hd + 1) * LANES)
                    sc["dq"][rows, sl] = _bf(_rope_lanes(dq[:, sl], cos, sin))

        issue(sec(_R_DV, _R_GATE), dv_store)
        issue(sec(_R_GATE, _R_END), put("gate", _silu))
        if i + 1 < len(tiles):
            normalise(i + 1)
        issue(lambda hb=hb: _dot_nt(jnp.concatenate([wt_ref[_R_DK:_R_DV, :], wt_ref[_R_LR:_R_DQ, :]], axis=0), hb),
              finish_kl)
        issue(sec(_R_DQ, _R_DK), finish_dq)
        issue(sec(_R_GV, _R_LR), put("gv", _bf))
        issue(sec(_R_GQ, _R_GK), put("gq", lambda r: r * (GLA_DK ** -0.5)))
        issue(sec(_R_GK, _R_GV), put("gk", lambda r: r))
    while pending:
        pending.pop()()


def _tile4(x):
    return jnp.concatenate([x, x, x, x], axis=0)


def _pair_mask():
    return (_iota((GLA_PAIR_K, GLA_PAIR_V), 0) // GLA_DK) == (_iota((GLA_PAIR_K, GLA_PAIR_V), 1) // GLA_DV)


def _gla_recurrence_step(qi, kl, vb, e_col, s_ref, b, d, p_mask):
    inter = []
    for p in range(2):
        ks = slice(p * GLA_PAIR_K, (p + 1) * GLA_PAIR_K)
        s = s_ref[b, d, p]
        inter.append(_dot(qi[:, ks], _bf(s)))
        upd = _dot_tn(kl[:, ks], vb[:, p * GLA_PAIR_V:(p + 1) * GLA_PAIR_V])
        s_ref[b, d, p] = s * e_col[ks] + jnp.where(p_mask, upd, 0.0)
    return jnp.concatenate(inter, axis=1)


def _gla_fast(q_ref, k_ref, v_ref, g_ref, o_ref, s_ref, qi_ref, kl_ref, et_ref, *, seq_len, n_seq):
    c_len, sc_len = GLA_CHUNK, GLA_SUPER
    nc = seq_len // c_len
    cps = sc_len // c_len
    n_super = n_seq * seq_len // sc_len
    r = _iota((sc_len, 2 * sc_len), 0)
    c = _iota((sc_len, 2 * sc_len), 1) & (sc_len - 1)
    same = (r // c_len) == (c // c_len)
    tri2 = [(same & (c <= r)).astype(BF16), (same & (c >= r)).astype(BF16)]
    rt = _iota((c_len, GLA_HEADS * c_len), 0)
    cs = _iota((c_len, GLA_HEADS * c_len), 1) & (c_len - 1)
    keep = [cs <= rt, cs >= rt]
    k_mask = (_iota((GLA_HEADS * c_len, GLA_QK), 0) // c_len) == (_iota((GLA_HEADS * c_len, GLA_QK), 1) // GLA_DK)
    v_mask = (_iota((GLA_HEADS * c_len, GLA_WIDTH), 0) // c_len) == (_iota((GLA_HEADS * c_len, GLA_WIDTH), 1) // GLA_DV)
    p_mask = _pair_mask()
    zb = jnp.zeros((), BF16)
    dirs = range(2)

    for g0 in range(0, n_super, GLA_GROUP):
        group = range(g0, min(g0 + GLA_GROUP, n_super))
        rows = {u: slice(u * sc_len, (u + 1) * sc_len) for u in group}
        bs = {}
        for u in group:
            for d in dirs:
                g = g_ref[rows[u], d * GLA_QK:(d + 1) * GLA_QK]
                hi = _bf(g)
                lo = _bf(g - hi.astype(F32))
                bs[u, d] = _dot(tri2[d], jnp.concatenate([hi, lo], axis=0))
        qib, kob = {}, {}
        for u in group:
            q = q_ref[rows[u], :]
            k = k_ref[rows[u], :]
            for d in dirs:
                b = bs[u, d]
                qi = _bf(q * jnp.exp(b))
                ko = k * jnp.exp(-b)
                kls = []
                for j in range(cps):
                    last = j * c_len + (0 if d else c_len - 1)
                    e_tot = jnp.exp(b[last:last + 1, :])
                    et_ref[d, u * cps + j] = jnp.broadcast_to(e_tot, (SUBLANES, GLA_QK))
                    kls.append(_bf(ko[j * c_len:(j + 1) * c_len] * e_tot))
                qib[u, d], kob[u, d] = qi, _bf(ko)
                qi_ref[d, rows[u], :] = qi
                kl_ref[d, rows[u], :] = jnp.concatenate(kls, axis=0)
        atts = {}
        for u in group:
            for j in range(cps):
                cr = slice(j * c_len, (j + 1) * c_len)
                for d in dirs:
                    atts[u, j, d] = _dot_nt(qib[u, d][cr], jnp.where(k_mask, _tile4(kob[u, d][cr]), zb))
        for u in group:
            vb = v_ref[rows[u], :]
            for j in range(cps):
                a2 = jnp.concatenate([_bf(jnp.where(keep[d], atts[u, j, d], 0.0)) for d in dirs], axis=0)
                v_bd = jnp.where(v_mask, _tile4(vb[j * c_len:(j + 1) * c_len]), zb)
                r2 = _dot(a2, v_bd)
                o_ref[u * sc_len + j * c_len:u * sc_len + (j + 1) * c_len, :] = r2[:c_len] + r2[c_len:]

    for i in range(nc):
        for b in range(n_seq):
            for d in dirs:
                ci = b * nc + ((nc - 1 - i) if d else i)
                rows_c = slice(ci * c_len, (ci + 1) * c_len)
                e_col = et_ref[d, ci].T[:, 0:1]
                o_ref[rows_c, :] += _gla_recurrence_step(qi_ref[d, rows_c, :], kl_ref[d, rows_c, :],
                                                         v_ref[rows_c, :], e_col, s_ref, b, d, p_mask)


def _gla_slow(q_ref, k_ref, v_ref, g_ref, o_ref, s_ref, b_ref, vf_ref, *, seq_len, n_seq):
    c_len = GLA_CHUNK
    nc = seq_len // c_len
    r64 = _iota((c_len, c_len), 0)
    c64 = _iota((c_len, c_len), 1)
    p_mask = _pair_mask()
    expand = ((_iota((GLA_QK, GLA_WIDTH), 0) // GLA_DK) == (_iota((GLA_QK, GLA_WIDTH), 1) // GLA_DV)).astype(BF16)
    t_idx = _iota((c_len, GLA_QK), 0)
    for b in range(n_seq):
        for d in range(2):
            reverse = bool(d)
            tri = ((c64 >= r64) if reverse else (c64 <= r64)).astype(F32)

            def step(i, carry, b=b, d=d, reverse=reverse, tri=tri):
                ci = b * nc + ((nc - 1 - i) if reverse else i)
                base = pl.multiple_of(ci * c_len, c_len)
                rows = pl.ds(base, c_len)
                q = q_ref[rows, :]
                k = k_ref[rows, :]
                vb = v_ref[rows, :]
                g = g_ref[rows, d * GLA_QK:(d + 1) * GLA_QK]
                bcum = jnp.dot(tri, g, precision=lax.Precision.HIGHEST, preferred_element_type=F32)
                b_tot = jnp.sum(g, axis=0, keepdims=True)
                b_ref[...] = bcum
                vf_ref[...] = vb.astype(F32)

                def key_row(s, acc):
                    b_s = b_ref[pl.ds(s, 1), :]
                    k_s = k_ref[pl.ds(base + s, 1), :]
                    v_s = vf_ref[pl.ds(s, 1), :]
                    visible = (t_idx <= s) if reverse else (t_idx >= s)
                    w = jnp.where(visible, q * jnp.exp(jnp.minimum(bcum - b_s, 0.0)), 0.0) * k_s
                    return acc + _dot(_bf(w), expand) * v_s

                o = lax.fori_loop(0, c_len, key_row, jnp.zeros((c_len, GLA_WIDTH), F32))
                e_col = jnp.broadcast_to(jnp.exp(b_tot), (SUBLANES, GLA_QK)).T[:, 0:1]
                o = o + _gla_recurrence_step(_bf(q * jnp.exp(bcum)), _bf(k * jnp.exp(b_tot - bcum)), vb, e_col,
                                             s_ref, b, d, p_mask)
                if reverse:
                    o_ref[rows, :] += o
                else:
                    o_ref[rows, :] = o
                return carry

            lax.fori_loop(0, nc, step, 0)


def _gla_sequences(q_ref, k_ref, v_ref, g_ref, o_ref, gs, init_states, state_out, *, seq_len, n_seq):
    s_ref = gs["s"]
    zero = jnp.zeros((GLA_DK, GLA_DV), F32)
    for b in range(n_seq):
        for d in range(2):
            for p in range(2):
                if init_states is not None:
                    s0 = init_states[b][d]
                    top = jnp.concatenate([s0[2 * p], zero], axis=1)
                    bot = jnp.concatenate([zero, s0[2 * p + 1]], axis=1)
                    s_ref[b, d, p] = jnp.concatenate([top, bot], axis=0)
                else:
                    s_ref[b, d, p] = jnp.zeros((GLA_PAIR_K, GLA_PAIR_V), F32)
    worst = jnp.zeros((1, 2 * GLA_QK), F32)
    for ci in range(n_seq * seq_len // GLA_CHUNK):
        worst = jnp.minimum(worst, jnp.sum(g_ref[ci * GLA_CHUNK:(ci + 1) * GLA_CHUNK, :], axis=0, keepdims=True))
    lax.cond(jnp.min(worst) >= GLA_SAFE_LOG_DECAY,
             functools.partial(_gla_fast, q_ref, k_ref, v_ref, g_ref, o_ref, s_ref, gs["qi"], gs["kl"], gs["et"],
                               seq_len=seq_len, n_seq=n_seq),
             functools.partial(_gla_slow, q_ref, k_ref, v_ref, g_ref, o_ref, s_ref, gs["b"], gs["vf"],
                               seq_len=seq_len, n_seq=n_seq))
    if state_out is not None:
        for b in range(n_seq):
            for d in range(2):
                for h in range(GLA_HEADS):
                    p, j = divmod(h, 2)
                    state_out[b][d][h] = s_ref[b, d, p, j * GLA_DK:(j + 1) * GLA_DK, j * GLA_DV:(j + 1) * GLA_DV]


def _diff_group(group, lam):
    scores = []
    for q, kts, _, _ in group:
        first = _iota(q.shape, 1) < DIFF_DH
        zb = jnp.zeros((), BF16)
        qs = jnp.concatenate([jnp.where(first, q, zb), jnp.where(first, zb, q)], axis=0)
        scores.append([_dot(qs, kt) for kt in kts])
    maxes = [functools.reduce(jnp.maximum, [jnp.max(s, axis=-1, keepdims=True) for s in ss]) for ss in scores]
    es = [[_bf(jnp.exp2(s - m)) for s in ss] for ss, m in zip(scores, maxes)]
    for (q, _, vas, store), e_list in zip(group, es):
        tq = q.shape[0]
        r = functools.reduce(jnp.add, [_dot(e, va) for e, va in zip(e_list, vas)])
        n = r[:, :DIFF_DV] / r[:, DIFF_DV:]
        store(n[:tq] - lam * n[tq:])


def _diff_phase(heads, lam, n_keys, seq_len):
    group_size = max(1, DIFF_GROUP_KEYS // n_keys)
    blocks = seq_len // DIFF_QB

    def load_kv1(load_kv):
        kts, vs = load_kv()
        return kts, [_with_ones(v) for v in vs]

    if blocks == 1:
        for g0 in range(0, len(heads), group_size):
            _diff_group([(load_q(row0), *load_kv1(load_kv), functools.partial(store, row0))
                         for row0, load_kv, load_q, store in heads[g0:g0 + group_size]], lam)
        return
    group_size = min(group_size, blocks)
    assert blocks % group_size == 0
    for row0, load_kv, load_q, store in heads:
        kts, vas = load_kv1(load_kv)

        def body(gi, carry, row0=row0, load_q=load_q, store=store, kts=kts, vas=vas):
            base = pl.multiple_of(row0 + gi * (group_size * DIFF_QB), DIFF_QB)
            rs = [base + t * DIFF_QB for t in range(group_size)]
            _diff_group([(load_q(r), kts, vas, functools.partial(store, r)) for r in rs], lam)
            return carry

        lax.fori_loop(0, blocks // group_size, body, 0)


def _diff_lambda(lam_ref, lam_init):
    lp = lam_ref[...]
    return (jnp.exp(jnp.sum(lp[0:1] * lp[1:2], axis=-1, keepdims=True))
            - jnp.exp(jnp.sum(lp[2:3] * lp[3:4], axis=-1, keepdims=True)) + lam_init)


def _with_ones(v):
    return jnp.concatenate([v, jnp.ones(v.shape, v.dtype)], axis=1)


def _head_rmsnorm(x, gain):
    ms = jnp.mean(x * x, axis=-1, keepdims=True)
    return x * lax.rsqrt(ms + EPS) * gain


def _merge_rows(og_ref, od_ref, sg_ref, x, gt, gg, dg, wo_ref, fg, rows):
    slabs = []
    for h in range(GLA_HEADS):
        slabs.append(_head_rmsnorm(og_ref[rows, h * GLA_DV:(h + 1) * GLA_DV], gg))
    for h in range(DIFF_HEADS):
        slabs.append(_head_rmsnorm(od_ref[rows, h * DIFF_DV:(h + 1) * DIFF_DV], dg))
    o = jnp.concatenate(slabs, axis=-1) * sg_ref[rows, :]
    xn = x + gt * _dot(_bf(o), wo_ref[...])
    ms = jnp.mean(xn * xn, axis=-1, keepdims=True)
    return xn * lax.rsqrt(ms + EPS) * fg


def _layer_kernel(*refs, seq_len, n_seq, rope, has_cache, lam_init):
    it = iter(refs)
    x_ref, gain_ref, mod_ref, wt_ref, wo_ref, wa_ref, ba_ref, lam_ref, gg_ref, dg_ref, fg_ref = (next(it) for _ in range(11))
    rope_refs = tuple(next(it) for _ in range(4)) if rope else None
    if has_cache:
        ckt_ref, cv_ref, s0f_ref, s0b_ref = (next(it) for _ in range(4))
    y_ref = next(it)
    if not has_cache:
        kt_out, dv_out, sf_out, sb_out = (next(it) for _ in range(4))
    names = ["gq", "gk", "gv", "g", "dq", "dv", "gate", "og", "od"] + (["dkt"] if has_cache else [])
    sc = {n: next(it) for n in names}
    gs = {n: next(it) for n in ["s", "qi", "kl", "et", "b", "vf"]}

    d = D_MODEL
    n_tok = seq_len * n_seq
    mod_row = (1 + pl.program_id(0)) if has_cache else 0
    shift = mod_ref[pl.ds(mod_row, 1), 0:d]
    scale = mod_ref[pl.ds(mod_row, 1), d:2 * d]
    gt = mod_ref[pl.ds(mod_row, 1), 2 * d:3 * d]
    gain = gain_ref[...]

    zero = jnp.zeros((GLA_GATE_RANK, GLA_QK), F32)
    wa = _bf(jnp.concatenate([jnp.concatenate([wa_ref[0], zero], axis=1),
                              jnp.concatenate([zero, wa_ref[1]], axis=1)], axis=0))
    ba = jnp.concatenate([ba_ref[0:1, :], ba_ref[1:2, :]], axis=1)

    tiles = []
    for t0 in range(0, n_tok, ROW_TILE):
        if has_cache:
            def kt_store(sl, val, t0=t0):
                sc["dkt"][sl, t0:t0 + ROW_TILE] = _bf(val)

            def dv_store(val, t0=t0):
                sc["dv"][t0:t0 + ROW_TILE, :] = _bf(val)
        else:
            def kt_store(sl, val, t0=t0):
                for j in range(ROW_TILE // seq_len):
                    kt_out[(t0 // seq_len) + j, sl, :] = val[:, j * seq_len:(j + 1) * seq_len]

            def dv_store(val, t0=t0):
                sc["dv"][t0:t0 + ROW_TILE, :] = _bf(val)
                for j in range(ROW_TILE // seq_len):
                    dv_out[(t0 // seq_len) + j] = val[j * seq_len:(j + 1) * seq_len, :].reshape(
                        seq_len, DIFF_HEADS, DIFF_DV)

        tiles.append((t0, ROW_TILE, lambda t0=t0: x_ref[t0:t0 + ROW_TILE, :], kt_store, dv_store))
    _projection_phase(tiles, gain, shift, scale, wt_ref, wa, ba, sc, rope_refs)

    init = [(s0f_ref.at[0], s0b_ref.at[0])] if has_cache else None
    out = None if has_cache else [(sf_out.at[b], sb_out.at[b]) for b in range(n_seq)]
    _gla_sequences(sc["gq"], sc["gk"], sc["gv"], sc["g"], sc["og"], gs, init, out, seq_len=seq_len, n_seq=n_seq)

    lam = _diff_lambda(lam_ref, lam_init)
    heads = []
    for b in range(n_seq):
        for hd in range(DIFF_HEADS):
            sl = slice(hd * LANES, (hd + 1) * LANES)

            def load_kv(b=b, sl=sl):
                if has_cache:
                    return ([_bf(ckt_ref[0, sl, :]), sc["dkt"][sl, :]], [_bf(cv_ref[0, :, sl]), sc["dv"][:, sl]])
                return [_bf(kt_out[b, sl, :])], [sc["dv"][b * seq_len:(b + 1) * seq_len, sl]]

            def load_q(r, sl=sl):
                return sc["dq"][pl.ds(r, DIFF_QB), sl]

            def store(r, val, sl=sl):
                sc["od"][pl.ds(r, DIFF_QB), sl] = val

            heads.append((b * seq_len, load_kv, load_q, store))
    _diff_phase(heads, lam, seq_len + (ckt_ref.shape[2] if has_cache else 0), seq_len)

    gg = gg_ref[...]
    dg = dg_ref[...] * (1.0 - lam_init)
    fg = fg_ref[...]
    for t0 in range(0, n_tok, ROW_TILE):
        rows = slice(t0, t0 + ROW_TILE)
        y_ref[rows, :] = _merge_rows(sc["og"], sc["od"], sc["gate"], x_ref[rows, :], gt, gg, dg, wo_ref, fg, rows)


def _layer(x2d, seq_len, n_seq, gain, mod, wt, wo, wa, ba, lam_p, gg, dg, fg, lam_init, rope_tabs=None, cache=None):
    t = x2d.shape[0]
    n_tok = seq_len * n_seq
    n_steps = t // n_tok
    rope = rope_tabs is not None
    has_cache = cache is not None
    assert n_tok % ROW_TILE == 0 and ROW_TILE % seq_len in (0, ROW_TILE) and seq_len % GLA_SUPER == 0

    def whole(a, single=False):
        kw = {"pipeline_mode": pl.Buffered(1)} if single else {}
        return pl.BlockSpec(a.shape, lambda i: (0,) * a.ndim, **kw)

    io_kw = {"pipeline_mode": pl.Buffered(1)} if n_steps <= 2 else {}
    args = [x2d, gain, mod, wt, wo, wa, ba, lam_p, gg, dg, fg]
    in_specs = [pl.BlockSpec((n_tok, D_MODEL), lambda i: (i, 0), **io_kw), whole(gain), whole(mod), whole(wt, True),
                whole(wo, True), whole(wa), whole(ba), whole(lam_p), whole(gg), whole(dg), whole(fg)]
    if rope:
        args += list(rope_tabs)
        in_specs += [whole(a, True) for a in rope_tabs]
    sds = jax.ShapeDtypeStruct
    out_shape = [sds((t, D_MODEL), F32)]
    out_specs = [pl.BlockSpec((n_tok, D_MODEL), lambda i: (i, 0), **io_kw)]
    if has_cache:
        ckt, cv, s0f, s0b = cache
        args += [ckt, cv, s0f, s0b]
        st_spec = pl.BlockSpec((1,) + s0f.shape[1:], lambda i: (i, 0, 0, 0))
        in_specs += [pl.BlockSpec((1,) + ckt.shape[1:], lambda i: (i, 0, 0)),
                     pl.BlockSpec((1,) + cv.shape[1:], lambda i: (i, 0, 0)), st_spec, st_spec]
    else:
        n_b = t // seq_len
        out_shape += [sds((n_b, DIFF_QK, seq_len), F32), sds((n_b, seq_len, DIFF_HEADS, DIFF_DV), F32),
                      sds((n_b, GLA_HEADS, GLA_DK, GLA_DV), F32), sds((n_b, GLA_HEADS, GLA_DK, GLA_DV), F32)]
        st_spec = pl.BlockSpec((n_seq, GLA_HEADS, GLA_DK, GLA_DV), lambda i: (i, 0, 0, 0))
        out_specs += [pl.BlockSpec((n_seq, DIFF_QK, seq_len), lambda i: (i, 0, 0)),
                      pl.BlockSpec((n_seq, seq_len, DIFF_HEADS, DIFF_DV), lambda i: (i, 0, 0, 0)), st_spec, st_spec]
    vm = pltpu.VMEM
    scratch = [vm((n_tok, GLA_QK), F32), vm((n_tok, GLA_QK), F32), vm((n_tok, GLA_WIDTH), BF16),
               vm((n_tok, 2 * GLA_QK), F32), vm((n_tok, DIFF_QK), BF16), vm((n_tok, DIFF_WIDTH), BF16),
               vm((n_tok, MIX_WIDTH), F32), vm((n_tok, GLA_WIDTH), F32), vm((n_tok, DIFF_WIDTH), F32)]
    if has_cache:
        scratch += [vm((DIFF_QK, n_tok), BF16)]
    scratch += [vm((n_seq, 2, 2, GLA_PAIR_K, GLA_PAIR_V), F32),
                vm((2, n_tok, GLA_QK), BF16),
                vm((2, n_tok, GLA_QK), BF16),
                vm((2, n_tok // GLA_CHUNK, SUBLANES, GLA_QK), F32),
                vm((GLA_CHUNK, GLA_QK), F32),
                vm((GLA_CHUNK, GLA_WIDTH), F32)]
    return pl.pallas_call(
        functools.partial(_layer_kernel, seq_len=seq_len, n_seq=n_seq, rope=rope, has_cache=has_cache,
                          lam_init=lam_init),
        grid=(n_steps,),
        in_specs=in_specs,
        out_specs=out_specs,
        out_shape=out_shape,
        scratch_shapes=scratch,
        compiler_params=_params(1),
        name="layer_lat" if has_cache else "layer_ctx",
    )(*args)


def _rope_tables(seq_len):
    t = np.arange(seq_len)
    inv_freq = ROPE_BASE ** (-np.arange(ROPE_PAIRS, dtype=np.float64) / ROPE_PAIRS)
    ang_r = (t // GRID_W)[:, None] * inv_freq[None, :]
    ang_c = (t % GRID_W)[:, None] * inv_freq[None, :]
    cr, sr, cc, sc = np.cos(ang_r), np.sin(ang_r), np.cos(ang_c), np.sin(ang_c)
    cos = np.concatenate([cr, cr, cc, cc] * 2, axis=-1).astype(np.float32)
    sin = np.concatenate([-sr, sr, -sc, sc] * 2, axis=-1).astype(np.float32)
    return (jnp.asarray(cos), jnp.asarray(sin),
            jnp.asarray(np.ascontiguousarray(cos.T)), jnp.asarray(np.ascontiguousarray(sin.T)))


def kernel(x_prompt, x_sample, cache_diff_k, cache_diff_v, state_gla_fwd, state_gla_bwd, c, c_ctx,
           norm_gain, w_mod, b_mod, w_in, w_gla_alpha, b_gla_alpha, diff_lambda,
           gla_head_gain, diff_head_gain, w_out, final_gain):
    bp, lp, d = x_prompt.shape
    bs, ls, _ = x_sample.shape
    depth = norm_gain.shape[0]
    assert depth == 1 and d == D_MODEL and bs + 1 <= MOD_ROWS and w_in.shape[2] == _R_END
    l = 0
    lam_init = 0.8 - 0.6 * math.exp(-0.3 * l)

    mod, wt, wo = _setup(c_ctx[None, :], c, w_mod[l], b_mod, jnp.swapaxes(w_in[l], 0, 1), w_out[l])
    shared = (norm_gain, mod, wt, wo, w_gla_alpha[l], b_gla_alpha[l], diff_lambda[l],
              gla_head_gain, diff_head_gain, final_gain[None, :], lam_init)

    y_p, dkt, dv, s_f, s_b = _layer(x_prompt.reshape(bp * lp, d), lp, ROW_TILE // lp, *shared)
    y_prompt = y_p.reshape(bp, lp, d)
    new_diff_k = jnp.transpose(dkt.reshape(bp, DIFF_HEADS, 2, DIFF_DH, lp), (0, 4, 1, 2, 3))[:, None]
    new_diff_v = dv[:, None]
    new_gla_fwd = s_f[:, None]
    new_gla_bwd = s_b[:, None]

    past = cache_diff_k.shape[2]
    ckt = jnp.transpose(cache_diff_k[:, l], (0, 2, 3, 4, 1)).reshape(bs, DIFF_QK, past)
    cv = cache_diff_v[:, l].reshape(bs, past, DIFF_WIDTH)
    y_s = _layer(x_sample.reshape(bs * ls, d), ls, 1, *shared, rope_tabs=_rope_tables(ls),
                 cache=(ckt, cv, state_gla_fwd[:, l], state_gla_bwd[:, l]))[0]
    y_sample = y_s.reshape(bs, ls, d)

    return (y_prompt, y_sample, new_diff_k, new_diff_v, new_gla_fwd, new_gla_bwd)
```

```python
import functools
import math

import numpy as np
import jax
import jax.numpy as jnp
from jax import lax
from jax.experimental import pallas as pl
from jax.experimental.pallas import tpu as pltpu

F32 = jnp.float32
BF16 = jnp.bfloat16

D_MODEL = 1024
GRID_W = 64
GLA_HEADS = 4
GLA_DK = 64
GLA_DV = 128
GLA_QK = GLA_HEADS * GLA_DK
GLA_WIDTH = GLA_HEADS * GLA_DV
GLA_GATE_RANK = 16
GLA_GATE_TEMP = 16.0
GLA_CHUNK = 64
DIFF_HEADS = 4
DIFF_DH = 64
DIFF_DV = 2 * DIFF_DH
DIFF_QK = DIFF_HEADS * 2 * DIFF_DH
DIFF_WIDTH = DIFF_HEADS * DIFF_DV
MIX_WIDTH = GLA_WIDTH + DIFF_WIDTH
ROPE_PAIRS = DIFF_DH // 4
ROPE_BASE = 10000.0
EPS = 1e-6
LOG2E = math.log2(math.e)

LANES = 128
SUBLANES = 8
MOD_ROWS = 8
ROW_TILE = 512
SETUP_STEPS = 8
DIFF_QB = 256
DIFF_GROUP_KEYS = 4096
GLA_SUPER = 256
GLA_GROUP = 2
GLA_PAIR_K = 2 * GLA_DK
GLA_PAIR_V = 2 * GLA_DV
GLA_SAFE_LOG_DECAY = -40.0
VMEM_LIMIT = 58 * 1024 * 1024

_R_GQ = 0
_R_GK = _R_GQ + GLA_QK
_R_GV = _R_GK + GLA_QK
_R_LR = _R_GV + GLA_WIDTH
_R_DQ = _R_LR + 2 * GLA_GATE_RANK
_R_DK = _R_DQ + DIFF_QK
_R_DV = _R_DK + DIFF_QK
_R_GATE = _R_DV + DIFF_WIDTH
_R_END = _R_GATE + MIX_WIDTH
_WEIGHT_PIECES = ((_R_DV, _R_GATE), (_R_GATE, _R_GATE + MIX_WIDTH // 2), (_R_GATE + MIX_WIDTH // 2, _R_END),
                  (_R_DK, _R_DV), (_R_LR, _R_DK), (_R_GV, _R_LR), (_R_GQ, _R_GV))


def _bf(x):
    return x.astype(BF16)


def _dot(a, b):
    return jnp.dot(a, b, preferred_element_type=F32)


def _dot_nt(a, b):
    return lax.dot_general(a, b, (((1,), (1,)), ((), ())), preferred_element_type=F32)


def _dot_tn(a, b):
    return lax.dot_general(a, b, (((0,), (0,)), ((), ())), preferred_element_type=F32)


def _params(n_parallel=0, n_arbitrary=0):
    sem = ("parallel",) * n_parallel + ("arbitrary",) * n_arbitrary
    return pltpu.CompilerParams(dimension_semantics=sem, vmem_limit_bytes=VMEM_LIMIT)


def _iota(shape, axis):
    return lax.broadcasted_iota(jnp.int32, shape, axis)


def _rows(ref, start, size):
    return ref.at[pl.ds(start, size)]


def _setup_kernel(cc_ref, c_ref, wm_ref, bm_ref, wi_ref, wo_ref, mod_ref, wib_ref, wob_ref):
    row = _iota((MOD_ROWS, D_MODEL), 0)
    cvecs = jnp.where(row == 0, cc_ref[...], 0.0)
    for b in range(c_ref.shape[0]):
        cvecs = jnp.where(row == 1 + b, c_ref[b:b + 1, :], cvecs)
    mod_ref[...] = _dot(_bf(_silu(cvecs)), _bf(wm_ref[...])) + bm_ref[...]
    wib_ref[...] = _bf(wi_ref[...])
    wob_ref[...] = _bf(wo_ref[...])


def _setup(c_ctx, c, w_mod, b_mod, w_in_t, w_out):
    n_mod = w_mod.shape[1]
    n_in = w_in_t.shape[0]
    tm = n_mod // SETUP_STEPS
    ti = pl.cdiv(n_in, SETUP_STEPS * 2 * SUBLANES) * 2 * SUBLANES
    to = w_out.shape[0] // SETUP_STEPS
    assert tm % LANES == 0 and tm * SETUP_STEPS == n_mod
    rows = lambda r: pl.BlockSpec((r, D_MODEL), lambda i: (i, 0))
    cols = lambda r: pl.BlockSpec((r, tm), lambda i: (0, i))
    return pl.pallas_call(
        _setup_kernel,
        grid=(SETUP_STEPS,),
        in_specs=[pl.BlockSpec(c_ctx.shape, lambda i: (0, 0)), pl.BlockSpec(c.shape, lambda i: (0, 0)),
                  cols(D_MODEL), cols(1), rows(ti), rows(to)],
        out_specs=[cols(MOD_ROWS), rows(ti), rows(to)],
        out_shape=[jax.ShapeDtypeStruct((MOD_ROWS, n_mod), F32), jax.ShapeDtypeStruct(w_in_t.shape, BF16),
                   jax.ShapeDtypeStruct(w_out.shape, BF16)],
        compiler_params=_params(1),
        name="setup",
    )(c_ctx, c, w_mod, b_mod, w_in_t, w_out)


def _log_sigmoid_scaled(z, scale):
    soft = jnp.log2(1.0 + jnp.exp2(jnp.abs(z) * (-LOG2E)))
    return jnp.minimum(z, 0.0) * scale - soft * (scale * math.log(2.0))


def _silu(x):
    h = 0.5 * x
    return h + h * jnp.tanh(h)


def _rope_lanes(x, cos, sin):
    lane = _iota(x.shape, 1)
    up = pltpu.roll(x, ROPE_PAIRS, axis=1)
    dn = pltpu.roll(x, LANES - ROPE_PAIRS, axis=1)
    partner = jnp.where((lane & ROPE_PAIRS) == 0, dn, up)
    return x * cos + partner * sin


def _rope_rows(x, cos, sin):
    p = ROPE_PAIRS
    parts = []
    for g in range(x.shape[0] // (2 * p)):
        parts += [x[(2 * g + 1) * p:(2 * g + 2) * p], x[2 * g * p:(2 * g + 1) * p]]
    return x * cos + jnp.concatenate(parts, axis=0) * sin


def _projection_phase(tiles, gain, shift, scale, w_rows, wa, ba, sc, rope_refs):
    hbs = {}

    def normalise(i):
        x = tiles[i][2]()
        ms = jnp.mean(x * x, axis=-1, keepdims=True)
        h = x * lax.rsqrt(ms + EPS) * gain
        hbs[i] = _bf(h * (1.0 + scale) + shift)

    pending = []

    def issue(matmul, finish):
        r = matmul()
        while pending:
            pending.pop()()
        pending.append(lambda: finish(r))

    normalise(0)
    for i, (row0, n_rows, _, kt_store, dv_store) in enumerate(tiles):
        rows = pl.ds(row0, n_rows)
        hb = hbs.pop(i)

        def sec(a, b, hb=hb):
            return lambda: _dot_nt(hb, w_rows(a, b))

        def put(name, f, rows=rows):
            def finish(r):
                sc[name][rows, :] = f(r)
            return finish

        def finish_kl(klt, rows=rows, kt_store=kt_store):
            z = _dot_tn(_bf(klt[DIFF_QK:]), wa) + ba
            sc["g"][rows, :] = _log_sigmoid_scaled(z, 1.0 / GLA_GATE_TEMP)
            if rope_refs is None:
                kt_store(slice(0, DIFF_QK), klt[:DIFF_QK])
            else:
                cost, sint = rope_refs[2][:, rows], rope_refs[3][:, rows]
                for hd in range(DIFF_HEADS):
                    sl = slice(hd * LANES, (hd + 1) * LANES)
                    kt_store(sl, _rope_rows(klt[sl, :], cost, sint))

        def finish_dq(dq, rows=rows):
            dq = dq * (DIFF_DH ** -0.5 * LOG2E)
            if rope_refs is None:
                sc["dq"][rows, :] = _bf(dq)
            else:
                cos, sin = rope_refs[0][rows, :], rope_refs[1][rows, :]
                for hd in range(DIFF_HEADS):
                    sl = slice(hd * LANES, (hd + 1) * LANES)
                    sc["dq"][rows, sl] = _bf(_rope_lanes(dq[:, sl], cos, sin))

        issue(sec(_R_DV, _R_GATE), dv_store)
        issue(sec(_R_GATE, _R_END), put("gate", _silu))
        if i + 1 < len(tiles):
            normalise(i + 1)
        issue(lambda hb=hb: _dot_nt(jnp.concatenate([w_rows(_R_DK, _R_DV), w_rows(_R_LR, _R_DQ)], axis=0), hb),
              finish_kl)
        issue(sec(_R_DQ, _R_DK), finish_dq)
        issue(sec(_R_GV, _R_LR), put("gv", _bf))
        issue(sec(_R_GQ, _R_GK), put("gq", lambda r: r * (GLA_DK ** -0.5)))
        issue(sec(_R_GK, _R_GV), put("gk", lambda r: r))
    while pending:
        pending.pop()()


def _tile4(x):
    return jnp.concatenate([x, x, x, x], axis=0)


def _pair_mask():
    return (_iota((GLA_PAIR_K, GLA_PAIR_V), 0) // GLA_DK) == (_iota((GLA_PAIR_K, GLA_PAIR_V), 1) // GLA_DV)


def _gla_recurrence_step(qi, kl, vb, e_col, s_ref, b, d, p_mask):
    inter = []
    for p in range(2):
        ks = slice(p * GLA_PAIR_K, (p + 1) * GLA_PAIR_K)
        s = s_ref[b, d, p]
        inter.append(_dot(qi[:, ks], _bf(s)))
        upd = _dot_tn(kl[:, ks], vb[:, p * GLA_PAIR_V:(p + 1) * GLA_PAIR_V])
        s_ref[b, d, p] = s * e_col[ks] + jnp.where(p_mask, upd, 0.0)
    return jnp.concatenate(inter, axis=1)


def _gla_fast(q_ref, k_ref, v_ref, g_ref, o_ref, s_ref, qi_ref, kl_ref, et_ref, *, seq_len, n_seq):
    c_len, sc_len = GLA_CHUNK, GLA_SUPER
    nc = seq_len // c_len
    cps = sc_len // c_len
    n_super = n_seq * seq_len // sc_len
    r = _iota((sc_len, 2 * sc_len), 0)
    c = _iota((sc_len, 2 * sc_len), 1) & (sc_len - 1)
    same = (r // c_len) == (c // c_len)
    tri2 = [(same & (c <= r)).astype(BF16), (same & (c >= r)).astype(BF16)]
    rt = _iota((c_len, GLA_HEADS * c_len), 0)
    cs = _iota((c_len, GLA_HEADS * c_len), 1) & (c_len - 1)
    keep = [cs <= rt, cs >= rt]
    k_mask = (_iota((GLA_HEADS * c_len, GLA_QK), 0) // c_len) == (_iota((GLA_HEADS * c_len, GLA_QK), 1) // GLA_DK)
    v_mask = (_iota((GLA_HEADS * c_len, GLA_WIDTH), 0) // c_len) == (_iota((GLA_HEADS * c_len, GLA_WIDTH), 1) // GLA_DV)
    p_mask = _pair_mask()
    zb = jnp.zeros((), BF16)
    dirs = range(2)

    for g0 in range(0, n_super, GLA_GROUP):
        group = range(g0, min(g0 + GLA_GROUP, n_super))
        rows = {u: slice(u * sc_len, (u + 1) * sc_len) for u in group}
        bs = {}
        for u in group:
            for d in dirs:
                g = g_ref[rows[u], d * GLA_QK:(d + 1) * GLA_QK]
                hi = _bf(g)
                lo = _bf(g - hi.astype(F32))
                bs[u, d] = _dot(tri2[d], jnp.concatenate([hi, lo], axis=0))
        qib, kob = {}, {}
        for u in group:
            q = q_ref[rows[u], :]
            k = k_ref[rows[u], :]
            for d in dirs:
                b = bs[u, d]
                qi = _bf(q * jnp.exp(b))
                ko = k * jnp.exp(-b)
                kls = []
                for j in range(cps):
                    last = j * c_len + (0 if d else c_len - 1)
                    e_tot = jnp.exp(b[last:last + 1, :])
                    et_ref[d, u * cps + j] = jnp.broadcast_to(e_tot, (SUBLANES, GLA_QK))
                    kls.append(_bf(ko[j * c_len:(j + 1) * c_len] * e_tot))
                qib[u, d], kob[u, d] = qi, _bf(ko)
                qi_ref[d, rows[u], :] = qi
                kl_ref[d, rows[u], :] = jnp.concatenate(kls, axis=0)
        atts = {}
        for u in group:
            for j in range(cps):
                cr = slice(j * c_len, (j + 1) * c_len)
                for d in dirs:
                    atts[u, j, d] = _dot_nt(qib[u, d][cr], jnp.where(k_mask, _tile4(kob[u, d][cr]), zb))
        for u in group:
            vb = v_ref[rows[u], :]
            for j in range(cps):
                a2 = jnp.concatenate([_bf(jnp.where(keep[d], atts[u, j, d], 0.0)) for d in dirs], axis=0)
                v_bd = jnp.where(v_mask, _tile4(vb[j * c_len:(j + 1) * c_len]), zb)
                r2 = _dot(a2, v_bd)
                o_ref[u * sc_len + j * c_len:u * sc_len + (j + 1) * c_len, :] = r2[:c_len] + r2[c_len:]

    for i in range(nc):
        for b in range(n_seq):
            for d in dirs:
                ci = b * nc + ((nc - 1 - i) if d else i)
                rows_c = slice(ci * c_len, (ci + 1) * c_len)
                e_col = et_ref[d, ci].T[:, 0:1]
                o_ref[rows_c, :] += _gla_recurrence_step(qi_ref[d, rows_c, :], kl_ref[d, rows_c, :],
                                                         v_ref[rows_c, :], e_col, s_ref, b, d, p_mask)


def _gla_slow(q_ref, k_ref, v_ref, g_ref, o_ref, s_ref, b_ref, vf_ref, *, seq_len, n_seq):
    c_len = GLA_CHUNK
    nc = seq_len // c_len
    r64 = _iota((c_len, c_len), 0)
    c64 = _iota((c_len, c_len), 1)
    p_mask = _pair_mask()
    expand = ((_iota((GLA_QK, GLA_WIDTH), 0) // GLA_DK) == (_iota((GLA_QK, GLA_WIDTH), 1) // GLA_DV)).astype(BF16)
    t_idx = _iota((c_len, GLA_QK), 0)
    for b in range(n_seq):
        for d in range(2):
            reverse = bool(d)
            tri = ((c64 >= r64) if reverse else (c64 <= r64)).astype(F32)

            def step(i, carry, b=b, d=d, reverse=reverse, tri=tri):
                ci = b * nc + ((nc - 1 - i) if reverse else i)
                base = pl.multiple_of(ci * c_len, c_len)
                rows = pl.ds(base, c_len)
                q = q_ref[rows, :]
                k = k_ref[rows, :]
                vb = v_ref[rows, :]
                g = g_ref[rows, d * GLA_QK:(d + 1) * GLA_QK]
                bcum = jnp.dot(tri, g, precision=lax.Precision.HIGHEST, preferred_element_type=F32)
                b_tot = jnp.sum(g, axis=0, keepdims=True)
                b_ref[...] = bcum
                vf_ref[...] = vb.astype(F32)

                def key_row(s, acc):
                    b_s = b_ref[pl.ds(s, 1), :]
                    k_s = k_ref[pl.ds(base + s, 1), :]
                    v_s = vf_ref[pl.ds(s, 1), :]
                    visible = (t_idx <= s) if reverse else (t_idx >= s)
                    w = jnp.where(visible, q * jnp.exp(jnp.minimum(bcum - b_s, 0.0)), 0.0) * k_s
                    return acc + _dot(_bf(w), expand) * v_s

                o = lax.fori_loop(0, c_len, key_row, jnp.zeros((c_len, GLA_WIDTH), F32))
                e_col = jnp.broadcast_to(jnp.exp(b_tot), (SUBLANES, GLA_QK)).T[:, 0:1]
                o = o + _gla_recurrence_step(_bf(q * jnp.exp(bcum)), _bf(k * jnp.exp(b_tot - bcum)), vb, e_col,
                                             s_ref, b, d, p_mask)
                if reverse:
                    o_ref[rows, :] += o
                else:
                    o_ref[rows, :] = o
                return carry

            lax.fori_loop(0, nc, step, 0)


def _gla_sequences(q_ref, k_ref, v_ref, g_ref, o_ref, gs, init_states, state_out, *, seq_len, n_seq):
    s_ref = gs["s"]
    zero = jnp.zeros((GLA_DK, GLA_DV), F32)
    for b in range(n_seq):
        for d in range(2):
            for p in range(2):
                if init_states is not None:
                    s0 = init_states[b][d]
                    top = jnp.concatenate([s0[2 * p], zero], axis=1)
                    bot = jnp.concatenate([zero, s0[2 * p + 1]], axis=1)
                    s_ref[b, d, p] = jnp.concatenate([top, bot], axis=0)
                else:
                    s_ref[b, d, p] = jnp.zeros((GLA_PAIR_K, GLA_PAIR_V), F32)
    worst = jnp.zeros((1, 2 * GLA_QK), F32)
    for ci in range(n_seq * seq_len // GLA_CHUNK):
        worst = jnp.minimum(worst, jnp.sum(g_ref[ci * GLA_CHUNK:(ci + 1) * GLA_CHUNK, :], axis=0, keepdims=True))
    lax.cond(jnp.min(worst) >= GLA_SAFE_LOG_DECAY,
             functools.partial(_gla_fast, q_ref, k_ref, v_ref, g_ref, o_ref, s_ref, gs["qi"], gs["kl"], gs["et"],
                               seq_len=seq_len, n_seq=n_seq),
             functools.partial(_gla_slow, q_ref, k_ref, v_ref, g_ref, o_ref, s_ref, gs["b"], gs["vf"],
                               seq_len=seq_len, n_seq=n_seq))
    if state_out is not None:
        for b in range(n_seq):
            for d in range(2):
                for h in range(GLA_HEADS):
                    p, j = divmod(h, 2)
                    state_out[b][d][h] = s_ref[b, d, p, j * GLA_DK:(j + 1) * GLA_DK, j * GLA_DV:(j + 1) * GLA_DV]


def _diff_group(group, lam):
    scores = []
    for q, kts, _, _ in group:
        first = _iota(q.shape, 1) < DIFF_DH
        zb = jnp.zeros((), BF16)
        qs = jnp.concatenate([jnp.where(first, q, zb), jnp.where(first, zb, q)], axis=0)
        scores.append([_dot(qs, kt) for kt in kts])
    maxes = [functools.reduce(jnp.maximum, [jnp.max(s, axis=-1, keepdims=True) for s in ss]) for ss in scores]
    es = [[_bf(jnp.exp2(s - m)) for s in ss] for ss, m in zip(scores, maxes)]
    for (q, _, vas, store), e_list in zip(group, es):
        tq = q.shape[0]
        r = functools.reduce(jnp.add, [_dot(e, va) for e, va in zip(e_list, vas)])
        n = r[:, :DIFF_DV] / r[:, DIFF_DV:]
        store(n[:tq] - lam * n[tq:])


def _diff_phase(heads, lam, n_keys, seq_len):
    group_size = max(1, DIFF_GROUP_KEYS // n_keys)
    blocks = seq_len // DIFF_QB

    def load_kv1(load_kv):
        kts, vs = load_kv()
        return kts, [_with_ones(v) for v in vs]

    if blocks == 1:
        for g0 in range(0, len(heads), group_size):
            _diff_group([(load_q(row0), *load_kv1(load_kv), functools.partial(store, row0))
                         for row0, load_kv, load_q, store in heads[g0:g0 + group_size]], lam)
        return
    group_size = min(group_size, blocks)
    assert blocks % group_size == 0
    for row0, load_kv, load_q, store in heads:
        kts, vas = load_kv1(load_kv)

        def body(gi, carry, row0=row0, load_q=load_q, store=store, kts=kts, vas=vas):
            base = pl.multiple_of(row0 + gi * (group_size * DIFF_QB), DIFF_QB)
            rs = [base + t * DIFF_QB for t in range(group_size)]
            _diff_group([(load_q(r), kts, vas, functools.partial(store, r)) for r in rs], lam)
            return carry

        lax.fori_loop(0, blocks // group_size, body, 0)


def _diff_lambda(lam_ref, lam_init):
    lp = lam_ref[...]
    return (jnp.exp(jnp.sum(lp[0:1] * lp[1:2], axis=-1, keepdims=True))
            - jnp.exp(jnp.sum(lp[2:3] * lp[3:4], axis=-1, keepdims=True)) + lam_init)


def _with_ones(v):
    return jnp.concatenate([v, jnp.ones(v.shape, v.dtype)], axis=1)


def _head_rmsnorm(x, gain):
    ms = jnp.mean(x * x, axis=-1, keepdims=True)
    return x * lax.rsqrt(ms + EPS) * gain


def _merge_rows(og_ref, od_ref, sg_ref, x, gt, gg, dg, wo_ref, fg, rows):
    slabs = []
    for h in range(GLA_HEADS):
        slabs.append(_head_rmsnorm(og_ref[rows, h * GLA_DV:(h + 1) * GLA_DV], gg))
    for h in range(DIFF_HEADS):
        slabs.append(_head_rmsnorm(od_ref[rows, h * DIFF_DV:(h + 1) * DIFF_DV], dg))
    o = jnp.concatenate(slabs, axis=-1) * sg_ref[rows, :]
    xn = x + gt * _dot(_bf(o), wo_ref[...])
    ms = jnp.mean(xn * xn, axis=-1, keepdims=True)
    return xn * lax.rsqrt(ms + EPS) * fg


def _layer_kernel(*refs, seq_len, n_seq, rope, has_cache, lam_init):
    it = iter(refs)
    x_ref, gain_ref, mod_ref, wt_ref, wo_ref, wa_ref, ba_ref, lam_ref, gg_ref, dg_ref, fg_ref = (next(it) for _ in range(11))
    rope_refs = tuple(next(it) for _ in range(4)) if rope else None
    if has_cache:
        ckt_ref, cv_ref, s0f_ref, s0b_ref = (next(it) for _ in range(4))
    y_ref = next(it)
    if not has_cache:
        kt_out, dv_out, sf_out, sb_out = (next(it) for _ in range(4))
    names = ["gq", "gk", "gv", "g", "dq", "dv", "gate", "og", "od"] + (["dkt"] if has_cache else [])
    sc = {n: next(it) for n in names}
    gs = {n: next(it) for n in ["s", "qi", "kl", "et", "b", "vf"]}
    wt_buf, wo_buf, w_sem = next(it), next(it), next(it)
    if has_cache:
        cv_buf, cv_sem = next(it), next(it)
        cv_copy = pltpu.make_async_copy(cv_ref.at[pl.program_id(0)], cv_buf, cv_sem)
        cv_copy.start()

    w_copies = [pltpu.make_async_copy(wt_ref.at[a:b], wt_buf.at[a:b], w_sem.at[i])
                for i, (a, b) in enumerate(_WEIGHT_PIECES)]
    wo_copy = pltpu.make_async_copy(wo_ref, wo_buf, w_sem.at[len(_WEIGHT_PIECES)])
    for cp in w_copies + [wo_copy]:
        cp.start()
    arrived = set()

    def w_rows(a, b):
        for i, (pa, pb) in enumerate(_WEIGHT_PIECES):
            if pa < b and a < pb and i not in arrived:
                arrived.add(i)
                w_copies[i].wait()
        return wt_buf[a:b, :]

    d = D_MODEL
    n_tok = seq_len * n_seq
    mod_row = (1 + pl.program_id(0)) if has_cache else 0
    shift = mod_ref[pl.ds(mod_row, 1), 0:d]
    scale = mod_ref[pl.ds(mod_row, 1), d:2 * d]
    gt = mod_ref[pl.ds(mod_row, 1), 2 * d:3 * d]
    gain = gain_ref[...]

    zero = jnp.zeros((GLA_GATE_RANK, GLA_QK), F32)
    wa = _bf(jnp.concatenate([jnp.concatenate([wa_ref[0], zero], axis=1),
                              jnp.concatenate([zero, wa_ref[1]], axis=1)], axis=0))
    ba = jnp.concatenate([ba_ref[0:1, :], ba_ref[1:2, :]], axis=1)

    tiles = []
    for t0 in range(0, n_tok, ROW_TILE):
        if has_cache:
            def kt_store(sl, val, t0=t0):
                sc["dkt"][sl, t0:t0 + ROW_TILE] = _bf(val)

            def dv_store(val, t0=t0):
                sc["dv"][t0:t0 + ROW_TILE, :] = _bf(val)
        else:
            def kt_store(sl, val, t0=t0):
                for j in range(ROW_TILE // seq_len):
                    kt_out[(t0 // seq_len) + j, sl, :] = val[:, j * seq_len:(j + 1) * seq_len]

            def dv_store(val, t0=t0):
                sc["dv"][t0:t0 + ROW_TILE, :] = _bf(val)
                for j in range(ROW_TILE // seq_len):
                    dv_out[(t0 // seq_len) + j] = val[j * seq_len:(j + 1) * seq_len, :].reshape(
                        seq_len, DIFF_HEADS, DIFF_DV)

        tiles.append((t0, ROW_TILE, lambda t0=t0: x_ref[t0:t0 + ROW_TILE, :], kt_store, dv_store))
    _projection_phase(tiles, gain, shift, scale, w_rows, wa, ba, sc, rope_refs)
    assert len(arrived) == len(_WEIGHT_PIECES)

    init = [(s0f_ref.at[0], s0b_ref.at[0])] if has_cache else None
    out = None if has_cache else [(sf_out.at[b], sb_out.at[b]) for b in range(n_seq)]
    _gla_sequences(sc["gq"], sc["gk"], sc["gv"], sc["g"], sc["og"], gs, init, out, seq_len=seq_len, n_seq=n_seq)

    lam = _diff_lambda(lam_ref, lam_init)
    if has_cache:
        cv_copy.wait()
        cvb = _bf(cv_buf[...].reshape(cv_buf.shape[0], DIFF_WIDTH))
    heads = []
    for b in range(n_seq):
        for hd in range(DIFF_HEADS):
            sl = slice(hd * LANES, (hd + 1) * LANES)

            def load_kv(b=b, sl=sl):
                if has_cache:
                    return ([_bf(ckt_ref[0, sl, :]), sc["dkt"][sl, :]], [cvb[:, sl], sc["dv"][:, sl]])
                return [_bf(kt_out[b, sl, :])], [sc["dv"][b * seq_len:(b + 1) * seq_len, sl]]

            def load_q(r, sl=sl):
                return sc["dq"][pl.ds(r, DIFF_QB), sl]

            def store(r, val, sl=sl):
                sc["od"][pl.ds(r, DIFF_QB), sl] = val

            heads.append((b * seq_len, load_kv, load_q, store))
    _diff_phase(heads, lam, seq_len + (ckt_ref.shape[2] if has_cache else 0), seq_len)

    gg = gg_ref[...]
    dg = dg_ref[...] * (1.0 - lam_init)
    fg = fg_ref[...]
    wo_copy.wait()
    for t0 in range(0, n_tok, ROW_TILE):
        rows = slice(t0, t0 + ROW_TILE)
        y_ref[rows, :] = _merge_rows(sc["og"], sc["od"], sc["gate"], x_ref[rows, :], gt, gg, dg, wo_buf, fg, rows)


def _layer(x2d, seq_len, n_seq, gain, mod, wt, wo, wa, ba, lam_p, gg, dg, fg, lam_init, rope_tabs=None, cache=None):
    t = x2d.shape[0]
    n_tok = seq_len * n_seq
    n_steps = t // n_tok
    rope = rope_tabs is not None
    has_cache = cache is not None
    assert n_tok % ROW_TILE == 0 and ROW_TILE % seq_len in (0, ROW_TILE) and seq_len % GLA_SUPER == 0

    def whole(a, single=False):
        kw = {"pipeline_mode": pl.Buffered(1)} if single else {}
        return pl.BlockSpec(a.shape, lambda i: (0,) * a.ndim, **kw)

    io_kw = {"pipeline_mode": pl.Buffered(1)} if n_steps <= 2 else {}
    hbm = pl.BlockSpec(memory_space=pl.ANY)
    args = [x2d, gain, mod, wt, wo, wa, ba, lam_p, gg, dg, fg]
    in_specs = [pl.BlockSpec((n_tok, D_MODEL), lambda i: (i, 0), **io_kw), whole(gain), whole(mod), hbm,
                hbm, whole(wa), whole(ba), whole(lam_p), whole(gg), whole(dg), whole(fg)]
    if rope:
        args += list(rope_tabs)
        in_specs += [whole(a, True) for a in rope_tabs]
    sds = jax.ShapeDtypeStruct
    out_shape = [sds((t, D_MODEL), F32)]
    out_specs = [pl.BlockSpec((n_tok, D_MODEL), lambda i: (i, 0), **io_kw)]
    if has_cache:
        ckt, cv, s0f, s0b = cache
        args += [ckt, cv, s0f, s0b]
        st_spec = pl.BlockSpec((1,) + s0f.shape[1:], lambda i: (i, 0, 0, 0))
        in_specs += [pl.BlockSpec((1,) + ckt.shape[1:], lambda i: (i, 0, 0)),
                     pl.BlockSpec(memory_space=pl.ANY), st_spec, st_spec]
    else:
        n_b = t // seq_len
        out_shape += [sds((n_b, DIFF_QK, seq_len), F32), sds((n_b, seq_len, DIFF_HEADS, DIFF_DV), F32),
                      sds((n_b, GLA_HEADS, GLA_DK, GLA_DV), F32), sds((n_b, GLA_HEADS, GLA_DK, GLA_DV), F32)]
        st_spec = pl.BlockSpec((n_seq, GLA_HEADS, GLA_DK, GLA_DV), lambda i: (i, 0, 0, 0))
        out_specs += [pl.BlockSpec((n_seq, DIFF_QK, seq_len), lambda i: (i, 0, 0)),
                      pl.BlockSpec((n_seq, seq_len, DIFF_HEADS, DIFF_DV), lambda i: (i, 0, 0, 0)), st_spec, st_spec]
    vm = pltpu.VMEM
    scratch = [vm((n_tok, GLA_QK), F32), vm((n_tok, GLA_QK), F32), vm((n_tok, GLA_WIDTH), BF16),
               vm((n_tok, 2 * GLA_QK), F32), vm((n_tok, DIFF_QK), BF16), vm((n_tok, DIFF_WIDTH), BF16),
               vm((n_tok, MIX_WIDTH), F32), vm((n_tok, GLA_WIDTH), F32), vm((n_tok, DIFF_WIDTH), F32)]
    if has_cache:
        scratch += [vm((DIFF_QK, n_tok), BF16)]
    scratch += [vm((n_seq, 2, 2, GLA_PAIR_K, GLA_PAIR_V), F32),
                vm((2, n_tok, GLA_QK), BF16),
                vm((2, n_tok, GLA_QK), BF16),
                vm((2, n_tok // GLA_CHUNK, SUBLANES, GLA_QK), F32),
                vm((GLA_CHUNK, GLA_QK), F32),
                vm((GLA_CHUNK, GLA_WIDTH), F32)]
    scratch += [vm(wt.shape, BF16), vm(wo.shape, BF16),
                pltpu.SemaphoreType.DMA((len(_WEIGHT_PIECES) + 1,))]
    if has_cache:
        scratch += [vm(cv.shape[1:], F32), pltpu.SemaphoreType.DMA(())]
    return pl.pallas_call(
        functools.partial(_layer_kernel, seq_len=seq_len, n_seq=n_seq, rope=rope, has_cache=has_cache,
                          lam_init=lam_init),
        grid=(n_steps,),
        in_specs=in_specs,
        out_specs=out_specs,
        out_shape=out_shape,
        scratch_shapes=scratch,
        compiler_params=_params(1),
        name="layer_lat" if has_cache else "layer_ctx",
    )(*args)


def _rope_tables(seq_len):
    t = np.arange(seq_len)
    inv_freq = ROPE_BASE ** (-np.arange(ROPE_PAIRS, dtype=np.float64) / ROPE_PAIRS)
    ang_r = (t // GRID_W)[:, None] * inv_freq[None, :]
    ang_c = (t % GRID_W)[:, None] * inv_freq[None, :]
    cr, sr, cc, sc = np.cos(ang_r), np.sin(ang_r), np.cos(ang_c), np.sin(ang_c)
    cos = np.concatenate([cr, cr, cc, cc] * 2, axis=-1).astype(np.float32)
    sin = np.concatenate([-sr, sr, -sc, sc] * 2, axis=-1).astype(np.float32)
    return (jnp.asarray(cos), jnp.asarray(sin),
            jnp.asarray(np.ascontiguousarray(cos.T)), jnp.asarray(np.ascontiguousarray(sin.T)))


def kernel(x_prompt, x_sample, cache_diff_k, cache_diff_v, state_gla_fwd, state_gla_bwd, c, c_ctx,
           norm_gain, w_mod, b_mod, w_in, w_gla_alpha, b_gla_alpha, diff_lambda,
           gla_head_gain, diff_head_gain, w_out, final_gain):
    bp, lp, d = x_prompt.shape
    bs, ls, _ = x_sample.shape
    depth = norm_gain.shape[0]
    assert depth == 1 and d == D_MODEL and bs + 1 <= MOD_ROWS and w_in.shape[2] == _R_END
    l = 0
    lam_init = 0.8 - 0.6 * math.exp(-0.3 * l)

    mod, wt, wo = _setup(c_ctx[None, :], c, w_mod[l], b_mod, jnp.swapaxes(w_in[l], 0, 1), w_out[l])
    shared = (norm_gain, mod, wt, wo, w_gla_alpha[l], b_gla_alpha[l], diff_lambda[l],
              gla_head_gain, diff_head_gain, final_gain[None, :], lam_init)

    y_p, dkt, dv, s_f, s_b = _layer(x_prompt.reshape(bp * lp, d), lp, ROW_TILE // lp, *shared)
    y_prompt = y_p.reshape(bp, lp, d)
    new_diff_k = jnp.transpose(dkt.reshape(bp, DIFF_HEADS, 2, DIFF_DH, lp), (0, 4, 1, 2, 3))[:, None]
    new_diff_v = dv[:, None]
    new_gla_fwd = s_f[:, None]
    new_gla_bwd = s_b[:, None]

    past = cache_diff_k.shape[2]
    ckt = jnp.transpose(cache_diff_k[:, l], (0, 2, 3, 4, 1)).reshape(bs, DIFF_QK, past)
    cv = cache_diff_v[:, l]
    y_s = _layer(x_sample.reshape(bs * ls, d), ls, 1, *shared, rope_tabs=_rope_tables(ls),
                 cache=(ckt, cv, state_gla_fwd[:, l], state_gla_bwd[:, l]))[0]
    y_sample = y_s.reshape(bs, ls, d)

    return (y_prompt, y_sample, new_diff_k, new_diff_v, new_gla_fwd, new_gla_bwd)
```

```python
import functools
import math

import numpy as np
import jax
import jax.numpy as jnp
from jax import lax
from jax.experimental import pallas as pl
from jax.experimental.pallas import tpu as pltpu

F32 = jnp.float32
BF16 = jnp.bfloat16

D_MODEL = 1024
GRID_W = 64
GLA_HEADS = 4
GLA_DK = 64
GLA_DV = 128
GLA_QK = GLA_HEADS * GLA_DK
GLA_WIDTH = GLA_HEADS * GLA_DV
GLA_GATE_RANK = 16
GLA_GATE_TEMP = 16.0
GLA_CHUNK = 64
DIFF_HEADS = 4
DIFF_DH = 64
DIFF_DV = 2 * DIFF_DH
DIFF_QK = DIFF_HEADS * 2 * DIFF_DH
DIFF_WIDTH = DIFF_HEADS * DIFF_DV
MIX_WIDTH = GLA_WIDTH + DIFF_WIDTH
ROPE_PAIRS = DIFF_DH // 4
ROPE_BASE = 10000.0
EPS = 1e-6
LOG2E = math.log2(math.e)

LANES = 128
SUBLANES = 8
MOD_ROWS = 8
ROW_TILE = 512
SETUP_STEPS = 8
DIFF_QB = 256
DIFF_GROUP_KEYS = 4096
GLA_SUPER = 256
GLA_GROUP = 4
GLA_PAIR_K = 2 * GLA_DK
GLA_PAIR_V = 2 * GLA_DV
GLA_SAFE_LOG_DECAY = -40.0
VMEM_LIMIT = 58 * 1024 * 1024

_R_GQ = 0
_R_GK = _R_GQ + GLA_QK
_R_GV = _R_GK + GLA_QK
_R_LR = _R_GV + GLA_WIDTH
_R_DQ = _R_LR + 2 * GLA_GATE_RANK
_R_DK = _R_DQ + DIFF_QK
_R_DV = _R_DK + DIFF_QK
_R_GATE = _R_DV + DIFF_WIDTH
_R_END = _R_GATE + MIX_WIDTH


def _bf(x):
    return x.astype(BF16)


def _dot(a, b):
    return jnp.dot(a, b, preferred_element_type=F32)


def _dot_nt(a, b):
    return lax.dot_general(a, b, (((1,), (1,)), ((), ())), preferred_element_type=F32)


def _dot_tn(a, b):
    return lax.dot_general(a, b, (((0,), (0,)), ((), ())), preferred_element_type=F32)


def _params(n_parallel=0, n_arbitrary=0):
    sem = ("parallel",) * n_parallel + ("arbitrary",) * n_arbitrary
    return pltpu.CompilerParams(dimension_semantics=sem, vmem_limit_bytes=VMEM_LIMIT)


def _iota(shape, axis):
    return lax.broadcasted_iota(jnp.int32, shape, axis)


def _rows(ref, start, size):
    return ref.at[pl.ds(start, size)]


def _setup_kernel(cc_ref, c_ref, wm_ref, bm_ref, wi_ref, wo_ref, mod_ref, wib_ref, wob_ref):
    row = _iota((MOD_ROWS, D_MODEL), 0)
    cvecs = jnp.where(row == 0, cc_ref[...], 0.0)
    for b in range(c_ref.shape[0]):
        cvecs = jnp.where(row == 1 + b, c_ref[b:b + 1, :], cvecs)
    mod_ref[...] = _dot(_bf(_silu(cvecs)), _bf(wm_ref[...])) + bm_ref[...]
    wib_ref[...] = _bf(wi_ref[...])
    wob_ref[...] = _bf(wo_ref[...])


def _setup(c_ctx, c, w_mod, b_mod, w_in_t, w_out):
    n_mod = w_mod.shape[1]
    n_in = w_in_t.shape[0]
    tm = n_mod // SETUP_STEPS
    ti = pl.cdiv(n_in, SETUP_STEPS * 2 * SUBLANES) * 2 * SUBLANES
    to = w_out.shape[0] // SETUP_STEPS
    assert tm % LANES == 0 and tm * SETUP_STEPS == n_mod
    rows = lambda r: pl.BlockSpec((r, D_MODEL), lambda i: (i, 0))
    cols = lambda r: pl.BlockSpec((r, tm), lambda i: (0, i))
    return pl.pallas_call(
        _setup_kernel,
        grid=(SETUP_STEPS,),
        in_specs=[pl.BlockSpec(c_ctx.shape, lambda i: (0, 0)), pl.BlockSpec(c.shape, lambda i: (0, 0)),
                  cols(D_MODEL), cols(1), rows(ti), rows(to)],
        out_specs=[cols(MOD_ROWS), rows(ti), rows(to)],
        out_shape=[jax.ShapeDtypeStruct((MOD_ROWS, n_mod), F32), jax.ShapeDtypeStruct(w_in_t.shape, BF16),
                   jax.ShapeDtypeStruct(w_out.shape, BF16)],
        compiler_params=_params(1),
        name="setup",
    )(c_ctx, c, w_mod, b_mod, w_in_t, w_out)


def _log_sigmoid_scaled(z, scale):
    soft = jnp.log2(1.0 + jnp.exp2(jnp.abs(z) * (-LOG2E)))
    return jnp.minimum(z, 0.0) * scale - soft * (scale * math.log(2.0))


def _silu(x):
    h = 0.5 * x
    return h + h * jnp.tanh(h)


def _rope_lanes(x, cos, sin):
    lane = _iota(x.shape, 1)
    up = pltpu.roll(x, ROPE_PAIRS, axis=1)
    dn = pltpu.roll(x, LANES - ROPE_PAIRS, axis=1)
    partner = jnp.where((lane & ROPE_PAIRS) == 0, dn, up)
    return x * cos + partner * sin


def _rope_rows(x, cos, sin):
    p = ROPE_PAIRS
    parts = []
    for g in range(x.shape[0] // (2 * p)):
        parts += [x[(2 * g + 1) * p:(2 * g + 2) * p], x[2 * g * p:(2 * g + 1) * p]]
    return x * cos + jnp.concatenate(parts, axis=0) * sin


def _projection_phase(tiles, gain, shift, scale, wt_ref, wa, ba, sc, rope_refs):
    hbs = {}

    def normalise(i):
        x = tiles[i][2]()
        ms = jnp.mean(x * x, axis=-1, keepdims=True)
        h = x * lax.rsqrt(ms + EPS) * gain
        hbs[i] = _bf(h * (1.0 + scale) + shift)

    pending = []

    def issue(matmul, finish):
        r = matmul()
        while pending:
            pending.pop()()
        pending.append(lambda: finish(r))

    normalise(0)
    for i, (row0, n_rows, _, kt_store, dv_store) in enumerate(tiles):
        rows = pl.ds(row0, n_rows)
        hb = hbs.pop(i)

        def sec(a, b, hb=hb):
            return lambda: _dot_nt(hb, wt_ref[a:b, :])

        def put(name, f, rows=rows):
            def finish(r):
                sc[name][rows, :] = f(r)
            return finish

        def finish_kl(klt, rows=rows, kt_store=kt_store):
            z = _dot_tn(_bf(klt[DIFF_QK:]), wa) + ba
            sc["g"][rows, :] = _log_sigmoid_scaled(z, 1.0 / GLA_GATE_TEMP)
            if rope_refs is None:
                kt_store(slice(0, DIFF_QK), klt[:DIFF_QK])
            else:
                cost, sint = rope_refs[2][:, rows], rope_refs[3][:, rows]
                for hd in range(DIFF_HEADS):
                    sl = slice(hd * LANES, (hd + 1) * LANES)
                    kt_store(sl, _rope_rows(klt[sl, :], cost, sint))

        def finish_dq(dq, rows=rows):
            dq = dq * (DIFF_DH ** -0.5 * LOG2E)
            if rope_refs is None:
                sc["dq"][rows, :] = _bf(dq)
            else:
                cos, sin = rope_refs[0][rows, :], rope_refs[1][rows, :]
                for hd in range(DIFF_HEADS):
                    sl = slice(hd * LANES, (hd + 1) * LANES)
                    sc["dq"][rows, sl] = _bf(_rope_lanes(dq[:, sl], cos, sin))

        issue(sec(_R_DV, _R_GATE), dv_store)
        issue(sec(_R_GATE, _R_END), put("gate", _silu))
        if i + 1 < len(tiles):
            normalise(i + 1)
        issue(lambda hb=hb: _dot_nt(jnp.concatenate([wt_ref[_R_DK:_R_DV, :], wt_ref[_R_LR:_R_DQ, :]], axis=0), hb),
              finish_kl)
        issue(sec(_R_DQ, _R_DK), finish_dq)
        issue(sec(_R_GV, _R_LR), put("gv", _bf))
        issue(sec(_R_GQ, _R_GK), put("gq", lambda r: r * (GLA_DK ** -0.5)))
        issue(sec(_R_GK, _R_GV), put("gk", lambda r: r))
    while pending:
        pending.pop()()


def _tile4(x):
    return jnp.concatenate([x, x, x, x], axis=0)


def _pair_mask():
    return (_iota((GLA_PAIR_K, GLA_PAIR_V), 0) // GLA_DK) == (_iota((GLA_PAIR_K, GLA_PAIR_V), 1) // GLA_DV)


def _gla_recurrence_step(qi, kl, vb, e_col, s_ref, b, d, p_mask):
    inter = []
    for p in range(2):
        ks = slice(p * GLA_PAIR_K, (p + 1) * GLA_PAIR_K)
        s = s_ref[b, d, p]
        inter.append(_dot(qi[:, ks], _bf(s)))
        upd = _dot_tn(kl[:, ks], vb[:, p * GLA_PAIR_V:(p + 1) * GLA_PAIR_V])
        s_ref[b, d, p] = s * e_col[ks] + jnp.where(p_mask, upd, 0.0)
    return jnp.concatenate(inter, axis=1)


def _gla_fast(q_ref, k_ref, v_ref, g_ref, o_ref, s_ref, qi_ref, kl_ref, et_ref, *, seq_len, n_seq):
    c_len, sc_len = GLA_CHUNK, GLA_SUPER
    nc = seq_len // c_len
    cps = sc_len // c_len
    n_super = n_seq * seq_len // sc_len
    r = _iota((sc_len, 2 * sc_len), 0)
    c = _iota((sc_len, 2 * sc_len), 1) & (sc_len - 1)
    same = (r // c_len) == (c // c_len)
    tri2 = [(same & (c <= r)).astype(BF16), (same & (c >= r)).astype(BF16)]
    rt = _iota((c_len, GLA_HEADS * c_len), 0)
    cs = _iota((c_len, GLA_HEADS * c_len), 1) & (c_len - 1)
    keep = [cs <= rt, cs >= rt]
    k_mask = (_iota((GLA_HEADS * c_len, GLA_QK), 0) // c_len) == (_iota((GLA_HEADS * c_len, GLA_QK), 1) // GLA_DK)
    v_mask = (_iota((GLA_HEADS * c_len, GLA_WIDTH), 0) // c_len) == (_iota((GLA_HEADS * c_len, GLA_WIDTH), 1) // GLA_DV)
    p_mask = _pair_mask()
    zb = jnp.zeros((), BF16)
    dirs = range(2)

    for g0 in range(0, n_super, GLA_GROUP):
        group = range(g0, min(g0 + GLA_GROUP, n_super))
        rows = {u: slice(u * sc_len, (u + 1) * sc_len) for u in group}
        bs = {}
        for u in group:
            for d in dirs:
                g = g_ref[rows[u], d * GLA_QK:(d + 1) * GLA_QK]
                hi = _bf(g)
                lo = _bf(g - hi.astype(F32))
                bs[u, d] = _dot(tri2[d], jnp.concatenate([hi, lo], axis=0))
        qib, kob = {}, {}
        for u in group:
            q = q_ref[rows[u], :]
            k = k_ref[rows[u], :]
            for d in dirs:
                b = bs[u, d]
                qi = _bf(q * jnp.exp(b))
                ko = k * jnp.exp(-b)
                kls = []
                for j in range(cps):
                    last = j * c_len + (0 if d else c_len - 1)
                    e_tot = jnp.exp(b[last:last + 1, :])
                    et_ref[d, u * cps + j] = jnp.broadcast_to(e_tot, (SUBLANES, GLA_QK))
                    kls.append(_bf(ko[j * c_len:(j + 1) * c_len] * e_tot))
                qib[u, d], kob[u, d] = qi, _bf(ko)
                qi_ref[d, rows[u], :] = qi
                kl_ref[d, rows[u], :] = jnp.concatenate(kls, axis=0)
        atts = {}
        for u in group:
            for j in range(cps):
                cr = slice(j * c_len, (j + 1) * c_len)
                for d in dirs:
                    atts[u, j, d] = _dot_nt(qib[u, d][cr], jnp.where(k_mask, _tile4(kob[u, d][cr]), zb))
        for u in group:
            vb = v_ref[rows[u], :]
            for j in range(cps):
                a2 = jnp.concatenate([_bf(jnp.where(keep[d], atts[u, j, d], 0.0)) for d in dirs], axis=0)
                v_bd = jnp.where(v_mask, _tile4(vb[j * c_len:(j + 1) * c_len]), zb)
                r2 = _dot(a2, v_bd)
                o_ref[u * sc_len + j * c_len:u * sc_len + (j + 1) * c_len, :] = r2[:c_len] + r2[c_len:]

    for i in range(nc):
        for b in range(n_seq):
            for d in dirs:
                ci = b * nc + ((nc - 1 - i) if d else i)
                rows_c = slice(ci * c_len, (ci + 1) * c_len)
                e_col = et_ref[d, ci].T[:, 0:1]
                o_ref[rows_c, :] += _gla_recurrence_step(qi_ref[d, rows_c, :], kl_ref[d, rows_c, :],
                                                         v_ref[rows_c, :], e_col, s_ref, b, d, p_mask)


def _gla_slow(q_ref, k_ref, v_ref, g_ref, o_ref, s_ref, b_ref, vf_ref, *, seq_len, n_seq):
    c_len = GLA_CHUNK
    nc = seq_len // c_len
    r64 = _iota((c_len, c_len), 0)
    c64 = _iota((c_len, c_len), 1)
    p_mask = _pair_mask()
    expand = ((_iota((GLA_QK, GLA_WIDTH), 0) // GLA_DK) == (_iota((GLA_QK, GLA_WIDTH), 1) // GLA_DV)).astype(BF16)
    t_idx = _iota((c_len, GLA_QK), 0)
    for b in range(n_seq):
        for d in range(2):
            reverse = bool(d)
            tri = ((c64 >= r64) if reverse else (c64 <= r64)).astype(F32)

            def step(i, carry, b=b, d=d, reverse=reverse, tri=tri):
                ci = b * nc + ((nc - 1 - i) if reverse else i)
                base = pl.multiple_of(ci * c_len, c_len)
                rows = pl.ds(base, c_len)
                q = q_ref[rows, :]
                k = k_ref[rows, :]
                vb = v_ref[rows, :]
                g = g_ref[rows, d * GLA_QK:(d + 1) * GLA_QK]
                bcum = jnp.dot(tri, g, precision=lax.Precision.HIGHEST, preferred_element_type=F32)
                b_tot = jnp.sum(g, axis=0, keepdims=True)
                b_ref[...] = bcum
                vf_ref[...] = vb.astype(F32)

                def key_row(s, acc):
                    b_s = b_ref[pl.ds(s, 1), :]
                    k_s = k_ref[pl.ds(base + s, 1), :]
                    v_s = vf_ref[pl.ds(s, 1), :]
                    visible = (t_idx <= s) if reverse else (t_idx >= s)
                    w = jnp.where(visible, q * jnp.exp(jnp.minimum(bcum - b_s, 0.0)), 0.0) * k_s
                    return acc + _dot(_bf(w), expand) * v_s

                o = lax.fori_loop(0, c_len, key_row, jnp.zeros((c_len, GLA_WIDTH), F32))
                e_col = jnp.broadcast_to(jnp.exp(b_tot), (SUBLANES, GLA_QK)).T[:, 0:1]
                o = o + _gla_recurrence_step(_bf(q * jnp.exp(bcum)), _bf(k * jnp.exp(b_tot - bcum)), vb, e_col,
                                             s_ref, b, d, p_mask)
                if reverse:
                    o_ref[rows, :] += o
                else:
                    o_ref[rows, :] = o
                return carry

            lax.fori_loop(0, nc, step, 0)


def _gla_sequences(q_ref, k_ref, v_ref, g_ref, o_ref, gs, init_states, state_out, *, seq_len, n_seq):
    s_ref = gs["s"]
    zero = jnp.zeros((GLA_DK, GLA_DV), F32)
    for b in range(n_seq):
        for d in range(2):
            for p in range(2):
                if init_states is not None:
                    s0 = init_states[b][d]
                    top = jnp.concatenate([s0[2 * p], zero], axis=1)
                    bot = jnp.concatenate([zero, s0[2 * p + 1]], axis=1)
                    s_ref[b, d, p] = jnp.concatenate([top, bot], axis=0)
                else:
                    s_ref[b, d, p] = jnp.zeros((GLA_PAIR_K, GLA_PAIR_V), F32)
    worst = jnp.zeros((1, 2 * GLA_QK), F32)
    for ci in range(n_seq * seq_len // GLA_CHUNK):
        worst = jnp.minimum(worst, jnp.sum(g_ref[ci * GLA_CHUNK:(ci + 1) * GLA_CHUNK, :], axis=0, keepdims=True))
    lax.cond(jnp.min(worst) >= GLA_SAFE_LOG_DECAY,
             functools.partial(_gla_fast, q_ref, k_ref, v_ref, g_ref, o_ref, s_ref, gs["qi"], gs["kl"], gs["et"],
                               seq_len=seq_len, n_seq=n_seq),
             functools.partial(_gla_slow, q_ref, k_ref, v_ref, g_ref, o_ref, s_ref, gs["b"], gs["vf"],
                               seq_len=seq_len, n_seq=n_seq))
    if state_out is not None:
        for b in range(n_seq):
            for d in range(2):
                for h in range(GLA_HEADS):
                    p, j = divmod(h, 2)
                    state_out[b][d][h] = s_ref[b, d, p, j * GLA_DK:(j + 1) * GLA_DK, j * GLA_DV:(j + 1) * GLA_DV]


def _diff_group(group, lam):
    scores = []
    for q, kts, _, _ in group:
        first = _iota(q.shape, 1) < DIFF_DH
        zb = jnp.zeros((), BF16)
        qs = jnp.concatenate([jnp.where(first, q, zb), jnp.where(first, zb, q)], axis=0)
        scores.append([_dot(qs, kt) for kt in kts])
    maxes = [functools.reduce(jnp.maximum, [jnp.max(s, axis=-1, keepdims=True) for s in ss]) for ss in scores]
    es = [[_bf(jnp.exp2(s - m)) for s in ss] for ss, m in zip(scores, maxes)]
    for (q, _, vas, store), e_list in zip(group, es):
        tq = q.shape[0]
        r = functools.reduce(jnp.add, [_dot(e, va) for e, va in zip(e_list, vas)])
        n = r[:, :DIFF_DV] / r[:, DIFF_DV:]
        store(n[:tq] - lam * n[tq:])


def _diff_phase(heads, lam, n_keys, seq_len):
    group_size = max(1, DIFF_GROUP_KEYS // n_keys)
    blocks = seq_len // DIFF_QB

    def load_kv1(load_kv):
        kts, vs = load_kv()
        return kts, [_with_ones(v) for v in vs]

    if blocks == 1:
        for g0 in range(0, len(heads), group_size):
            _diff_group([(load_q(row0), *load_kv1(load_kv), functools.partial(store, row0))
                         for row0, load_kv, load_q, store in heads[g0:g0 + group_size]], lam)
        return
    group_size = min(group_size, blocks)
    assert blocks % group_size == 0
    for row0, load_kv, load_q, store in heads:
        kts, vas = load_kv1(load_kv)

        def body(gi, carry, row0=row0, load_q=load_q, store=store, kts=kts, vas=vas):
            base = pl.multiple_of(row0 + gi * (group_size * DIFF_QB), DIFF_QB)
            rs = [base + t * DIFF_QB for t in range(group_size)]
            _diff_group([(load_q(r), kts, vas, functools.partial(store, r)) for r in rs], lam)
            return carry

        lax.fori_loop(0, blocks // group_size, body, 0)


def _diff_lambda(lam_ref, lam_init):
    lp = lam_ref[...]
    return (jnp.exp(jnp.sum(lp[0:1] * lp[1:2], axis=-1, keepdims=True))
            - jnp.exp(jnp.sum(lp[2:3] * lp[3:4], axis=-1, keepdims=True)) + lam_init)


def _with_ones(v):
    return jnp.concatenate([v, jnp.ones(v.shape, v.dtype)], axis=1)


def _head_rmsnorm(x, gain):
    ms = jnp.mean(x * x, axis=-1, keepdims=True)
    return x * lax.rsqrt(ms + EPS) * gain


def _merge_rows(og_ref, od_ref, sg_ref, x, gt, gg, dg, wo_ref, fg, rows):
    slabs = []
    for h in range(GLA_HEADS):
        slabs.append(_head_rmsnorm(og_ref[rows, h * GLA_DV:(h + 1) * GLA_DV], gg))
    for h in range(DIFF_HEADS):
        slabs.append(_head_rmsnorm(od_ref[rows, h * DIFF_DV:(h + 1) * DIFF_DV], dg))
    o = jnp.concatenate(slabs, axis=-1) * sg_ref[rows, :]
    xn = x + gt * _dot(_bf(o), wo_ref[...])
    ms = jnp.mean(xn * xn, axis=-1, keepdims=True)
    return xn * lax.rsqrt(ms + EPS) * fg


def _layer_kernel(*refs, seq_len, n_seq, rope, has_cache, lam_init):
    it = iter(refs)
    x_ref, gain_ref, mod_ref, wt_ref, wo_ref, wa_ref, ba_ref, lam_ref, gg_ref, dg_ref, fg_ref = (next(it) for _ in range(11))
    rope_refs = tuple(next(it) for _ in range(4)) if rope else None
    if has_cache:
        ckt_ref, cv_ref, s0f_ref, s0b_ref = (next(it) for _ in range(4))
    y_ref = next(it)
    if not has_cache:
        kt_out, dv_out, sf_out, sb_out = (next(it) for _ in range(4))
    names = ["gq", "gk", "gv", "g", "dq", "dv", "gate", "og", "od"] + (["dkt"] if has_cache else [])
    sc = {n: next(it) for n in names}
    gs = {n: next(it) for n in ["s", "qi", "kl", "et", "b", "vf"]}
    if has_cache:
        cv_buf, cv_sem = next(it), next(it)
        cv_copy = pltpu.make_async_copy(cv_ref.at[pl.program_id(0)], cv_buf, cv_sem)
        cv_copy.start()

    d = D_MODEL
    n_tok = seq_len * n_seq
    mod_row = (1 + pl.program_id(0)) if has_cache else 0
    shift = mod_ref[pl.ds(mod_row, 1), 0:d]
    scale = mod_ref[pl.ds(mod_row, 1), d:2 * d]
    gt = mod_ref[pl.ds(mod_row, 1), 2 * d:3 * d]
    gain = gain_ref[...]

    zero = jnp.zeros((GLA_GATE_RANK, GLA_QK), F32)
    wa = _bf(jnp.concatenate([jnp.concatenate([wa_ref[0], zero], axis=1),
                              jnp.concatenate([zero, wa_ref[1]], axis=1)], axis=0))
    ba = jnp.concatenate([ba_ref[0:1, :], ba_ref[1:2, :]], axis=1)

    tiles = []
    for t0 in range(0, n_tok, ROW_TILE):
        if has_cache:
            def kt_store(sl, val, t0=t0):
                sc["dkt"][sl, t0:t0 + ROW_TILE] = _bf(val)

            def dv_store(val, t0=t0):
                sc["dv"][t0:t0 + ROW_TILE, :] = _bf(val)
        else:
            def kt_store(sl, val, t0=t0):
                for j in range(ROW_TILE // seq_len):
                    kt_out[(t0 // seq_len) + j, sl, :] = val[:, j * seq_len:(j + 1) * seq_len]

            def dv_store(val, t0=t0):
                sc["dv"][t0:t0 + ROW_TILE, :] = _bf(val)
                for j in range(ROW_TILE // seq_len):
                    dv_out[(t0 // seq_len) + j] = val[j * seq_len:(j + 1) * seq_len, :].reshape(
                        seq_len, DIFF_HEADS, DIFF_DV)

        tiles.append((t0, ROW_TILE, lambda t0=t0: x_ref[t0:t0 + ROW_TILE, :], kt_store, dv_store))
    _projection_phase(tiles, gain, shift, scale, wt_ref, wa, ba, sc, rope_refs)

    init = [(s0f_ref.at[0], s0b_ref.at[0])] if has_cache else None
    out = None if has_cache else [(sf_out.at[b], sb_out.at[b]) for b in range(n_seq)]
    _gla_sequences(sc["gq"], sc["gk"], sc["gv"], sc["g"], sc["og"], gs, init, out, seq_len=seq_len, n_seq=n_seq)

    lam = _diff_lambda(lam_ref, lam_init)
    if has_cache:
        cv_copy.wait()
        cvb = _bf(cv_buf[...].reshape(cv_buf.shape[0], DIFF_WIDTH))
    heads = []
    for b in range(n_seq):
        for hd in range(DIFF_HEADS):
            sl = slice(hd * LANES, (hd + 1) * LANES)

            def load_kv(b=b, sl=sl):
                if has_cache:
                    return ([_bf(ckt_ref[0, sl, :]), sc["dkt"][sl, :]], [cvb[:, sl], sc["dv"][:, sl]])
                return [_bf(kt_out[b, sl, :])], [sc["dv"][b * seq_len:(b + 1) * seq_len, sl]]

            def load_q(r, sl=sl):
                return sc["dq"][pl.ds(r, DIFF_QB), sl]

            def store(r, val, sl=sl):
                sc["od"][pl.ds(r, DIFF_QB), sl] = val

            heads.append((b * seq_len, load_kv, load_q, store))
    _diff_phase(heads, lam, seq_len + (ckt_ref.shape[2] if has_cache else 0), seq_len)

    gg = gg_ref[...]
    dg = dg_ref[...] * (1.0 - lam_init)
    fg = fg_ref[...]
    for t0 in range(0, n_tok, ROW_TILE):
        rows = slice(t0, t0 + ROW_TILE)
        y_ref[rows, :] = _merge_rows(sc["og"], sc["od"], sc["gate"], x_ref[rows, :], gt, gg, dg, wo_ref, fg, rows)


def _layer(x2d, seq_len, n_seq, gain, mod, wt, wo, wa, ba, lam_p, gg, dg, fg, lam_init, rope_tabs=None, cache=None):
    t = x2d.shape[0]
    n_tok = seq_len * n_seq
    n_steps = t // n_tok
    rope = rope_tabs is not None
    has_cache = cache is not None
    assert n_tok % ROW_TILE == 0 and ROW_TILE % seq_len in (0, ROW_TILE) and seq_len % GLA_SUPER == 0

    def whole(a, single=False):
        kw = {"pipeline_mode": pl.Buffered(1)} if single else {}
        return pl.BlockSpec(a.shape, lambda i: (0,) * a.ndim, **kw)

    io_kw = {"pipeline_mode": pl.Buffered(1)} if n_steps <= 2 else {}
    args = [x2d, gain, mod, wt, wo, wa, ba, lam_p, gg, dg, fg]
    in_specs = [pl.BlockSpec((n_tok, D_MODEL), lambda i: (i, 0)), whole(gain), whole(mod), whole(wt, True),
                whole(wo, True), whole(wa), whole(ba), whole(lam_p), whole(gg), whole(dg), whole(fg)]
    if rope:
        args += list(rope_tabs)
        in_specs += [whole(a, True) for a in rope_tabs]
    sds = jax.ShapeDtypeStruct
    out_shape = [sds((t, D_MODEL), F32)]
    out_specs = [pl.BlockSpec((n_tok, D_MODEL), lambda i: (i, 0), **io_kw)]
    if has_cache:
        ckt, cv, s0f, s0b = cache
        args += [ckt, cv, s0f, s0b]
        st_spec = pl.BlockSpec((1,) + s0f.shape[1:], lambda i: (i, 0, 0, 0))
        in_specs += [pl.BlockSpec((1,) + ckt.shape[1:], lambda i: (i, 0, 0)),
                     pl.BlockSpec(memory_space=pl.ANY), st_spec, st_spec]
    else:
        n_b = t // seq_len
        out_shape += [sds((n_b, DIFF_QK, seq_len), F32), sds((n_b, seq_len, DIFF_HEADS, DIFF_DV), F32),
                      sds((n_b, GLA_HEADS, GLA_DK, GLA_DV), F32), sds((n_b, GLA_HEADS, GLA_DK, GLA_DV), F32)]
        st_spec = pl.BlockSpec((n_seq, GLA_HEADS, GLA_DK, GLA_DV), lambda i: (i, 0, 0, 0))
        out_specs += [pl.BlockSpec((n_seq, DIFF_QK, seq_len), lambda i: (i, 0, 0)),
                      pl.BlockSpec((n_seq, seq_len, DIFF_HEADS, DIFF_DV), lambda i: (i, 0, 0, 0)), st_spec, st_spec]
    vm = pltpu.VMEM
    scratch = [vm((n_tok, GLA_QK), F32), vm((n_tok, GLA_QK), F32), vm((n_tok, GLA_WIDTH), BF16),
               vm((n_tok, 2 * GLA_QK), F32), vm((n_tok, DIFF_QK), BF16), vm((n_tok, DIFF_WIDTH), BF16),
               vm((n_tok, MIX_WIDTH), F32), vm((n_tok, GLA_WIDTH), F32), vm((n_tok, DIFF_WIDTH), F32)]
    if has_cache:
        scratch += [vm((DIFF_QK, n_tok), BF16)]
    scratch += [vm((n_seq, 2, 2, GLA_PAIR_K, GLA_PAIR_V), F32),
                vm((2, n_tok, GLA_QK), BF16),
                vm((2, n_tok, GLA_QK), BF16),
                vm((2, n_tok // GLA_CHUNK, SUBLANES, GLA_QK), F32),
                vm((GLA_CHUNK, GLA_QK), F32),
                vm((GLA_CHUNK, GLA_WIDTH), F32)]
    if has_cache:
        scratch += [vm(cv.shape[1:], F32), pltpu.SemaphoreType.DMA(())]
    return pl.pallas_call(
        functools.partial(_layer_kernel, seq_len=seq_len, n_seq=n_seq, rope=rope, has_cache=has_cache,
                          lam_init=lam_init),
        grid=(n_steps,),
        in_specs=in_specs,
        out_specs=out_specs,
        out_shape=out_shape,
        scratch_shapes=scratch,
        compiler_params=_params(1),
        name="layer_lat" if has_cache else "layer_ctx",
    )(*args)


def _rope_tables(seq_len):
    t = np.arange(seq_len)
    inv_freq = ROPE_BASE ** (-np.arange(ROPE_PAIRS, dtype=np.float64) / ROPE_PAIRS)
    ang_r = (t // GRID_W)[:, None] * inv_freq[None, :]
    ang_c = (t % GRID_W)[:, None] * inv_freq[None, :]
    cr, sr, cc, sc = np.cos(ang_r), np.sin(ang_r), np.cos(ang_c), np.sin(ang_c)
    cos = np.concatenate([cr, cr, cc, cc] * 2, axis=-1).astype(np.float32)
    sin = np.concatenate([-sr, sr, -sc, sc] * 2, axis=-1).astype(np.float32)
    return (jnp.asarray(cos), jnp.asarray(sin),
            jnp.asarray(np.ascontiguousarray(cos.T)), jnp.asarray(np.ascontiguousarray(sin.T)))


def kernel(x_prompt, x_sample, cache_diff_k, cache_diff_v, state_gla_fwd, state_gla_bwd, c, c_ctx,
           norm_gain, w_mod, b_mod, w_in, w_gla_alpha, b_gla_alpha, diff_lambda,
           gla_head_gain, diff_head_gain, w_out, final_gain):
    bp, lp, d = x_prompt.shape
    bs, ls, _ = x_sample.shape
    depth = norm_gain.shape[0]
    assert depth == 1 and d == D_MODEL and bs + 1 <= MOD_ROWS and w_in.shape[2] == _R_END
    l = 0
    lam_init = 0.8 - 0.6 * math.exp(-0.3 * l)

    mod, wt, wo = _setup(c_ctx[None, :], c, w_mod[l], b_mod, jnp.swapaxes(w_in[l], 0, 1), w_out[l])
    shared = (norm_gain, mod, wt, wo, w_gla_alpha[l], b_gla_alpha[l], diff_lambda[l],
              gla_head_gain, diff_head_gain, final_gain[None, :], lam_init)

    y_p, dkt, dv, s_f, s_b = _layer(x_prompt.reshape(bp * lp, d), lp, ROW_TILE // lp, *shared)
    y_prompt = y_p.reshape(bp, lp, d)
    new_diff_k = jnp.transpose(dkt.reshape(bp, DIFF_HEADS, 2, DIFF_DH, lp), (0, 4, 1, 2, 3))[:, None]
    new_diff_v = dv[:, None]
    new_gla_fwd = s_f[:, None]
    new_gla_bwd = s_b[:, None]

    past = cache_diff_k.shape[2]
    ckt = jnp.transpose(cache_diff_k[:, l], (0, 2, 3, 4, 1)).reshape(bs, DIFF_QK, past)
    cv = cache_diff_v[:, l]
    y_s = _layer(x_sample.reshape(bs * ls, d), ls, 1, *shared, rope_tabs=_rope_tables(ls),
                 cache=(ckt, cv, state_gla_fwd[:, l], state_gla_bwd[:, l]))[0]
    y_sample = y_s.reshape(bs, ls, d)

    return (y_prompt, y_sample, new_diff_k, new_diff_v, new_gla_fwd, new_gla_bwd)
```

```python
import functools
import math

import numpy as np
import jax
import jax.numpy as jnp
from jax import lax
from jax.experimental import pallas as pl
from jax.experimental.pallas import tpu as pltpu

F32 = jnp.float32
BF16 = jnp.bfloat16

D_MODEL = 1024
GRID_W = 64
GLA_HEADS = 4
GLA_DK = 64
GLA_DV = 128
GLA_QK = GLA_HEADS * GLA_DK
GLA_WIDTH = GLA_HEADS * GLA_DV
GLA_GATE_RANK = 16
GLA_GATE_TEMP = 16.0
GLA_CHUNK = 64
DIFF_HEADS = 4
DIFF_DH = 64
DIFF_DV = 2 * DIFF_DH
DIFF_QK = DIFF_HEADS * 2 * DIFF_DH
DIFF_WIDTH = DIFF_HEADS * DIFF_DV
MIX_WIDTH = GLA_WIDTH + DIFF_WIDTH
ROPE_PAIRS = DIFF_DH // 4
ROPE_BASE = 10000.0
EPS = 1e-6
LOG2E = math.log2(math.e)

LANES = 128
SUBLANES = 8
MOD_ROWS = 8
ROW_TILE = 512
CTX_ROWS_PER_STEP = 1024
SETUP_STEPS = 8
DIFF_QB = 256
DIFF_GROUP_KEYS = 4096
DIFF_MAX_GROUP = 8
GLA_SUPER = 256
GLA_GROUP = 4
GLA_PAIR_K = 2 * GLA_DK
GLA_PAIR_V = 2 * GLA_DV
GLA_SAFE_LOG_DECAY = -40.0
VMEM_LIMIT = 58 * 1024 * 1024

_R_GQ = 0
_R_GK = _R_GQ + GLA_QK
_R_GV = _R_GK + GLA_QK
_R_LR = _R_GV + GLA_WIDTH
_R_DQ = _R_LR + 2 * GLA_GATE_RANK
_R_DK = _R_DQ + DIFF_QK
_R_DV = _R_DK + DIFF_QK
_R_GATE = _R_DV + DIFF_WIDTH
_R_END = _R_GATE + MIX_WIDTH


def _bf(x):
    return x.astype(BF16)


def _dot(a, b):
    return jnp.dot(a, b, preferred_element_type=F32)


def _dot_nt(a, b):
    return lax.dot_general(a, b, (((1,), (1,)), ((), ())), preferred_element_type=F32)


def _dot_tn(a, b):
    return lax.dot_general(a, b, (((0,), (0,)), ((), ())), preferred_element_type=F32)


def _params(n_parallel=0, n_arbitrary=0):
    sem = ("parallel",) * n_parallel + ("arbitrary",) * n_arbitrary
    return pltpu.CompilerParams(dimension_semantics=sem, vmem_limit_bytes=VMEM_LIMIT)


def _iota(shape, axis):
    return lax.broadcasted_iota(jnp.int32, shape, axis)


def _rows(ref, start, size):
    return ref.at[pl.ds(start, size)]


def _setup_kernel(cc_ref, c_ref, wm_ref, bm_ref, wi_ref, wo_ref, mod_ref, wib_ref, wob_ref):
    row = _iota((MOD_ROWS, D_MODEL), 0)
    cvecs = jnp.where(row == 0, cc_ref[...], 0.0)
    for b in range(c_ref.shape[0]):
        cvecs = jnp.where(row == 1 + b, c_ref[b:b + 1, :], cvecs)
    mod_ref[...] = _dot(_bf(_silu(cvecs)), _bf(wm_ref[...])) + bm_ref[...]
    wib_ref[...] = _bf(wi_ref[...])
    wob_ref[...] = _bf(wo_ref[...])


def _setup(c_ctx, c, w_mod, b_mod, w_in_t, w_out):
    n_mod = w_mod.shape[1]
    n_in = w_in_t.shape[0]
    tm = n_mod // SETUP_STEPS
    ti = pl.cdiv(n_in, SETUP_STEPS * 2 * SUBLANES) * 2 * SUBLANES
    to = w_out.shape[0] // SETUP_STEPS
    assert tm % LANES == 0 and tm * SETUP_STEPS == n_mod
    rows = lambda r: pl.BlockSpec((r, D_MODEL), lambda i: (i, 0))
    cols = lambda r: pl.BlockSpec((r, tm), lambda i: (0, i))
    return pl.pallas_call(
        _setup_kernel,
        grid=(SETUP_STEPS,),
        in_specs=[pl.BlockSpec(c_ctx.shape, lambda i: (0, 0)), pl.BlockSpec(c.shape, lambda i: (0, 0)),
                  cols(D_MODEL), cols(1), rows(ti), rows(to)],
        out_specs=[cols(MOD_ROWS), rows(ti), rows(to)],
        out_shape=[jax.ShapeDtypeStruct((MOD_ROWS, n_mod), F32), jax.ShapeDtypeStruct(w_in_t.shape, BF16),
                   jax.ShapeDtypeStruct(w_out.shape, BF16)],
        compiler_params=_params(1),
        name="setup",
    )(c_ctx, c, w_mod, b_mod, w_in_t, w_out)


def _log_sigmoid_scaled(z, scale):
    soft = jnp.log2(1.0 + jnp.exp2(jnp.abs(z) * (-LOG2E)))
    return jnp.minimum(z, 0.0) * scale - soft * (scale * math.log(2.0))


def _silu(x):
    h = 0.5 * x
    return h + h * jnp.tanh(h)


def _rope_lanes(x, cos, sin):
    lane = _iota(x.shape, 1)
    up = pltpu.roll(x, ROPE_PAIRS, axis=1)
    dn = pltpu.roll(x, LANES - ROPE_PAIRS, axis=1)
    partner = jnp.where((lane & ROPE_PAIRS) == 0, dn, up)
    return x * cos + partner * sin


def _rope_rows(x, cos, sin):
    p = ROPE_PAIRS
    parts = []
    for g in range(x.shape[0] // (2 * p)):
        parts += [x[(2 * g + 1) * p:(2 * g + 2) * p], x[2 * g * p:(2 * g + 1) * p]]
    return x * cos + jnp.concatenate(parts, axis=0) * sin


def _projection_phase(tiles, gain, shift, scale, wt_ref, wa, ba, sc, rope_refs):
    hbs = {}

    def normalise(i):
        x = tiles[i][2]()
        ms = jnp.mean(x * x, axis=-1, keepdims=True)
        h = x * lax.rsqrt(ms + EPS) * gain
        hbs[i] = _bf(h * (1.0 + scale) + shift)

    pending = []

    def issue(matmul, finish):
        r = matmul()
        while pending:
            pending.pop()()
        pending.append(lambda: finish(r))

    normalise(0)
    for i, (row0, n_rows, _, kt_store, dv_store) in enumerate(tiles):
        rows = pl.ds(row0, n_rows)
        hb = hbs.pop(i)

        def sec(a, b, hb=hb):
            return lambda: _dot_nt(hb, wt_ref[a:b, :])

        def put(name, f, rows=rows):
            def finish(r):
                sc[name][rows, :] = f(r)
            return finish

        def finish_kl(klt, rows=rows, kt_store=kt_store):
            z = _dot_tn(_bf(klt[DIFF_QK:]), wa) + ba
            sc["g"][rows, :] = _log_sigmoid_scaled(z, 1.0 / GLA_GATE_TEMP)
            if rope_refs is None:
                kt_store(slice(0, DIFF_QK), klt[:DIFF_QK])
            else:
                cost, sint = rope_refs[2][:, rows], rope_refs[3][:, rows]
                for hd in range(DIFF_HEADS):
                    sl = slice(hd * LANES, (hd + 1) * LANES)
                    kt_store(sl, _rope_rows(klt[sl, :], cost, sint))

        def finish_dq(dq, rows=rows):
            dq = dq * (DIFF_DH ** -0.5 * LOG2E)
            if rope_refs is None:
                sc["dq"][rows, :] = _bf(dq)
            else:
                cos, sin = rope_refs[0][rows, :], rope_refs[1][rows, :]
                for hd in range(DIFF_HEADS):
                    sl = slice(hd * LANES, (hd + 1) * LANES)
                    sc["dq"][rows, sl] = _bf(_rope_lanes(dq[:, sl], cos, sin))

        issue(sec(_R_DV, _R_GATE), dv_store)
        issue(sec(_R_GATE, _R_END), put("gate", _silu))
        if i + 1 < len(tiles):
            normalise(i + 1)
        issue(lambda hb=hb: _dot_nt(jnp.concatenate([wt_ref[_R_DK:_R_DV, :], wt_ref[_R_LR:_R_DQ, :]], axis=0), hb),
              finish_kl)
        issue(sec(_R_DQ, _R_DK), finish_dq)
        issue(sec(_R_GV, _R_LR), put("gv", _bf))
        issue(sec(_R_GQ, _R_GK), put("gq", lambda r: r * (GLA_DK ** -0.5)))
        issue(sec(_R_GK, _R_GV), put("gk", lambda r: r))
    while pending:
        pending.pop()()


def _tile4(x):
    return jnp.concatenate([x, x, x, x], axis=0)


def _pair_mask():
    return (_iota((GLA_PAIR_K, GLA_PAIR_V), 0) // GLA_DK) == (_iota((GLA_PAIR_K, GLA_PAIR_V), 1) // GLA_DV)


def _gla_recurrence_step(qi, kl, vb, e_col, s_ref, b, d, p_mask):
    inter = []
    for p in range(2):
        ks = slice(p * GLA_PAIR_K, (p + 1) * GLA_PAIR_K)
        s = s_ref[b, d, p]
        inter.append(_dot(qi[:, ks], _bf(s)))
        upd = _dot_tn(kl[:, ks], vb[:, p * GLA_PAIR_V:(p + 1) * GLA_PAIR_V])
        s_ref[b, d, p] = s * e_col[ks] + jnp.where(p_mask, upd, 0.0)
    return jnp.concatenate(inter, axis=1)


def _gla_fast(q_ref, k_ref, v_ref, g_ref, o_ref, s_ref, qi_ref, kl_ref, et_ref, *, seq_len, n_seq):
    c_len, sc_len = GLA_CHUNK, GLA_SUPER
    nc = seq_len // c_len
    cps = sc_len // c_len
    n_super = n_seq * seq_len // sc_len
    r = _iota((sc_len, 2 * sc_len), 0)
    c = _iota((sc_len, 2 * sc_len), 1) & (sc_len - 1)
    same = (r // c_len) == (c // c_len)
    tri2 = [(same & (c <= r)).astype(BF16), (same & (c >= r)).astype(BF16)]
    rt = _iota((c_len, GLA_HEADS * c_len), 0)
    cs = _iota((c_len, GLA_HEADS * c_len), 1) & (c_len - 1)
    keep = [cs <= rt, cs >= rt]
    k_mask = (_iota((GLA_HEADS * c_len, GLA_QK), 0) // c_len) == (_iota((GLA_HEADS * c_len, GLA_QK), 1) // GLA_DK)
    v_mask = (_iota((GLA_HEADS * c_len, GLA_WIDTH), 0) // c_len) == (_iota((GLA_HEADS * c_len, GLA_WIDTH), 1) // GLA_DV)
    p_mask = _pair_mask()
    zb = jnp.zeros((), BF16)
    dirs = range(2)

    for g0 in range(0, n_super, GLA_GROUP):
        group = range(g0, min(g0 + GLA_GROUP, n_super))
        rows = {u: slice(u * sc_len, (u + 1) * sc_len) for u in group}
        bs = {}
        for u in group:
            for d in dirs:
                g = g_ref[rows[u], d * GLA_QK:(d + 1) * GLA_QK]
                hi = _bf(g)
                lo = _bf(g - hi.astype(F32))
                bs[u, d] = _dot(tri2[d], jnp.concatenate([hi, lo], axis=0))
        qib, kob = {}, {}
        for u in group:
            q = q_ref[rows[u], :]
            k = k_ref[rows[u], :]
            for d in dirs:
                b = bs[u, d]
                qi = _bf(q * jnp.exp(b))
                ko = k * jnp.exp(-b)
                kls = []
                for j in range(cps):
                    last = j * c_len + (0 if d else c_len - 1)
                    e_tot = jnp.exp(b[last:last + 1, :])
                    et_ref[d, u * cps + j] = jnp.broadcast_to(e_tot, (SUBLANES, GLA_QK))
                    kls.append(_bf(ko[j * c_len:(j + 1) * c_len] * e_tot))
                qib[u, d], kob[u, d] = qi, _bf(ko)
                qi_ref[d, rows[u], :] = qi
                kl_ref[d, rows[u], :] = jnp.concatenate(kls, axis=0)
        atts = {}
        for u in group:
            for j in range(cps):
                cr = slice(j * c_len, (j + 1) * c_len)
                for d in dirs:
                    atts[u, j, d] = _dot_nt(qib[u, d][cr], jnp.where(k_mask, _tile4(kob[u, d][cr]), zb))
        for u in group:
            vb = v_ref[rows[u], :]
            for j in range(cps):
                a2 = jnp.concatenate([_bf(jnp.where(keep[d], atts[u, j, d], 0.0)) for d in dirs], axis=0)
                v_bd = jnp.where(v_mask, _tile4(vb[j * c_len:(j + 1) * c_len]), zb)
                r2 = _dot(a2, v_bd)
                o_ref[u * sc_len + j * c_len:u * sc_len + (j + 1) * c_len, :] = r2[:c_len] + r2[c_len:]

    for i in range(nc):
        for b in range(n_seq):
            for d in dirs:
                ci = b * nc + ((nc - 1 - i) if d else i)
                rows_c = slice(ci * c_len, (ci + 1) * c_len)
                e_col = et_ref[d, ci].T[:, 0:1]
                o_ref[rows_c, :] += _gla_recurrence_step(qi_ref[d, rows_c, :], kl_ref[d, rows_c, :],
                                                         v_ref[rows_c, :], e_col, s_ref, b, d, p_mask)


def _gla_slow(q_ref, k_ref, v_ref, g_ref, o_ref, s_ref, b_ref, vf_ref, *, seq_len, n_seq):
    c_len = GLA_CHUNK
    nc = seq_len // c_len
    r64 = _iota((c_len, c_len), 0)
    c64 = _iota((c_len, c_len), 1)
    p_mask = _pair_mask()
    expand = ((_iota((GLA_QK, GLA_WIDTH), 0) // GLA_DK) == (_iota((GLA_QK, GLA_WIDTH), 1) // GLA_DV)).astype(BF16)
    t_idx = _iota((c_len, GLA_QK), 0)
    for b in range(n_seq):
        for d in range(2):
            reverse = bool(d)
            tri = ((c64 >= r64) if reverse else (c64 <= r64)).astype(F32)

            def step(i, carry, b=b, d=d, reverse=reverse, tri=tri):
                ci = b * nc + ((nc - 1 - i) if reverse else i)
                base = pl.multiple_of(ci * c_len, c_len)
                rows = pl.ds(base, c_len)
                q = q_ref[rows, :]
                k = k_ref[rows, :]
                vb = v_ref[rows, :]
                g = g_ref[rows, d * GLA_QK:(d + 1) * GLA_QK]
                bcum = jnp.dot(tri, g, precision=lax.Precision.HIGHEST, preferred_element_type=F32)
                b_tot = jnp.sum(g, axis=0, keepdims=True)
                b_ref[...] = bcum
                vf_ref[...] = vb.astype(F32)

                def key_row(s, acc):
                    b_s = b_ref[pl.ds(s, 1), :]
                    k_s = k_ref[pl.ds(base + s, 1), :]
                    v_s = vf_ref[pl.ds(s, 1), :]
                    visible = (t_idx <= s) if reverse else (t_idx >= s)
                    w = jnp.where(visible, q * jnp.exp(jnp.minimum(bcum - b_s, 0.0)), 0.0) * k_s
                    return acc + _dot(_bf(w), expand) * v_s

                o = lax.fori_loop(0, c_len, key_row, jnp.zeros((c_len, GLA_WIDTH), F32))
                e_col = jnp.broadcast_to(jnp.exp(b_tot), (SUBLANES, GLA_QK)).T[:, 0:1]
                o = o + _gla_recurrence_step(_bf(q * jnp.exp(bcum)), _bf(k * jnp.exp(b_tot - bcum)), vb, e_col,
                                             s_ref, b, d, p_mask)
                if reverse:
                    o_ref[rows, :] += o
                else:
                    o_ref[rows, :] = o
                return carry

            lax.fori_loop(0, nc, step, 0)


def _gla_sequences(q_ref, k_ref, v_ref, g_ref, o_ref, gs, init_states, state_out, *, seq_len, n_seq):
    s_ref = gs["s"]
    zero = jnp.zeros((GLA_DK, GLA_DV), F32)
    for b in range(n_seq):
        for d in range(2):
            for p in range(2):
                if init_states is not None:
                    s0 = init_states[b][d]
                    top = jnp.concatenate([s0[2 * p], zero], axis=1)
                    bot = jnp.concatenate([zero, s0[2 * p + 1]], axis=1)
                    s_ref[b, d, p] = jnp.concatenate([top, bot], axis=0)
                else:
                    s_ref[b, d, p] = jnp.zeros((GLA_PAIR_K, GLA_PAIR_V), F32)
    worst = jnp.zeros((1, 2 * GLA_QK), F32)
    for ci in range(n_seq * seq_len // GLA_CHUNK):
        worst = jnp.minimum(worst, jnp.sum(g_ref[ci * GLA_CHUNK:(ci + 1) * GLA_CHUNK, :], axis=0, keepdims=True))
    lax.cond(jnp.min(worst) >= GLA_SAFE_LOG_DECAY,
             functools.partial(_gla_fast, q_ref, k_ref, v_ref, g_ref, o_ref, s_ref, gs["qi"], gs["kl"], gs["et"],
                               seq_len=seq_len, n_seq=n_seq),
             functools.partial(_gla_slow, q_ref, k_ref, v_ref, g_ref, o_ref, s_ref, gs["b"], gs["vf"],
                               seq_len=seq_len, n_seq=n_seq))
    if state_out is not None:
        for b in range(n_seq):
            for d in range(2):
                for h in range(GLA_HEADS):
                    p, j = divmod(h, 2)
                    state_out[b][d][h] = s_ref[b, d, p, j * GLA_DK:(j + 1) * GLA_DK, j * GLA_DV:(j + 1) * GLA_DV]


def _diff_group(group, lam):
    scores = []
    for q, kts, _, _ in group:
        first = _iota(q.shape, 1) < DIFF_DH
        zb = jnp.zeros((), BF16)
        qs = jnp.concatenate([jnp.where(first, q, zb), jnp.where(first, zb, q)], axis=0)
        scores.append([_dot(qs, kt) for kt in kts])
    maxes = [functools.reduce(jnp.maximum, [jnp.max(s, axis=-1, keepdims=True) for s in ss]) for ss in scores]
    es = [[_bf(jnp.exp2(s - m)) for s in ss] for ss, m in zip(scores, maxes)]
    for (q, _, vas, store), e_list in zip(group, es):
        tq = q.shape[0]
        r = functools.reduce(jnp.add, [_dot(e, va) for e, va in zip(e_list, vas)])
        n = r[:, :DIFF_DV] / r[:, DIFF_DV:]
        store(n[:tq] - lam * n[tq:])


def _diff_phase(heads, lam, n_keys, seq_len):
    group_size = min(DIFF_MAX_GROUP, max(1, DIFF_GROUP_KEYS // n_keys))
    blocks = seq_len // DIFF_QB

    def load_kv1(load_kv):
        kts, vs = load_kv()
        return kts, [_with_ones(v) for v in vs]

    if blocks == 1:
        for g0 in range(0, len(heads), group_size):
            _diff_group([(load_q(row0), *load_kv1(load_kv), functools.partial(store, row0))
                         for row0, load_kv, load_q, store in heads[g0:g0 + group_size]], lam)
        return
    group_size = min(group_size, blocks)
    assert blocks % group_size == 0
    for row0, load_kv, load_q, store in heads:
        kts, vas = load_kv1(load_kv)

        def body(gi, carry, row0=row0, load_q=load_q, store=store, kts=kts, vas=vas):
            base = pl.multiple_of(row0 + gi * (group_size * DIFF_QB), DIFF_QB)
            rs = [base + t * DIFF_QB for t in range(group_size)]
            _diff_group([(load_q(r), kts, vas, functools.partial(store, r)) for r in rs], lam)
            return carry

        lax.fori_loop(0, blocks // group_size, body, 0)


def _diff_lambda(lam_ref, lam_init):
    lp = lam_ref[...]
    return (jnp.exp(jnp.sum(lp[0:1] * lp[1:2], axis=-1, keepdims=True))
            - jnp.exp(jnp.sum(lp[2:3] * lp[3:4], axis=-1, keepdims=True)) + lam_init)


def _with_ones(v):
    return jnp.concatenate([v, jnp.ones(v.shape, v.dtype)], axis=1)


def _head_rmsnorm(x, gain):
    ms = jnp.mean(x * x, axis=-1, keepdims=True)
    return x * lax.rsqrt(ms + EPS) * gain


def _merge_rows(og_ref, od_ref, sg_ref, x, gt, gg, dg, wo_ref, fg, rows):
    slabs = []
    for h in range(GLA_HEADS):
        slabs.append(_head_rmsnorm(og_ref[rows, h * GLA_DV:(h + 1) * GLA_DV], gg))
    for h in range(DIFF_HEADS):
        slabs.append(_head_rmsnorm(od_ref[rows, h * DIFF_DV:(h + 1) * DIFF_DV], dg))
    o = jnp.concatenate(slabs, axis=-1) * sg_ref[rows, :]
    xn = x + gt * _dot(_bf(o), wo_ref[...])
    ms = jnp.mean(xn * xn, axis=-1, keepdims=True)
    return xn * lax.rsqrt(ms + EPS) * fg


def _layer_kernel(*refs, seq_len, n_seq, rope, has_cache, lam_init):
    it = iter(refs)
    x_ref, gain_ref, mod_ref, wt_ref, wo_ref, wa_ref, ba_ref, lam_ref, gg_ref, dg_ref, fg_ref = (next(it) for _ in range(11))
    rope_refs = tuple(next(it) for _ in range(4)) if rope else None
    if has_cache:
        ckt_ref, cv_ref, s0f_ref, s0b_ref = (next(it) for _ in range(4))
    y_ref = next(it)
    if not has_cache:
        kt_out, dv_out, sf_out, sb_out = (next(it) for _ in range(4))
    names = ["gq", "gk", "gv", "g", "dq", "dv", "gate", "og", "od"] + (["dkt"] if has_cache else [])
    sc = {n: next(it) for n in names}
    gs = {n: next(it) for n in ["s", "qi", "kl", "et", "b", "vf"]}
    if has_cache:
        cv_buf, cv_sem = next(it), next(it)
        cv_copy = pltpu.make_async_copy(cv_ref.at[pl.program_id(0)], cv_buf, cv_sem)
        cv_copy.start()

    d = D_MODEL
    n_tok = seq_len * n_seq
    mod_row = (1 + pl.program_id(0)) if has_cache else 0
    shift = mod_ref[pl.ds(mod_row, 1), 0:d]
    scale = mod_ref[pl.ds(mod_row, 1), d:2 * d]
    gt = mod_ref[pl.ds(mod_row, 1), 2 * d:3 * d]
    gain = gain_ref[...]

    zero = jnp.zeros((GLA_GATE_RANK, GLA_QK), F32)
    wa = _bf(jnp.concatenate([jnp.concatenate([wa_ref[0], zero], axis=1),
                              jnp.concatenate([zero, wa_ref[1]], axis=1)], axis=0))
    ba = jnp.concatenate([ba_ref[0:1, :], ba_ref[1:2, :]], axis=1)

    tiles = []
    for t0 in range(0, n_tok, ROW_TILE):
        if has_cache:
            def kt_store(sl, val, t0=t0):
                sc["dkt"][sl, t0:t0 + ROW_TILE] = _bf(val)

            def dv_store(val, t0=t0):
                sc["dv"][t0:t0 + ROW_TILE, :] = _bf(val)
        else:
            def kt_store(sl, val, t0=t0):
                for j in range(ROW_TILE // seq_len):
                    kt_out[(t0 // seq_len) + j, sl, :] = val[:, j * seq_len:(j + 1) * seq_len]

            def dv_store(val, t0=t0):
                sc["dv"][t0:t0 + ROW_TILE, :] = _bf(val)
                for j in range(ROW_TILE // seq_len):
                    dv_out[(t0 // seq_len) + j] = val[j * seq_len:(j + 1) * seq_len, :].reshape(
                        seq_len, DIFF_HEADS, DIFF_DV)

        tiles.append((t0, ROW_TILE, lambda t0=t0: x_ref[t0:t0 + ROW_TILE, :], kt_store, dv_store))
    _projection_phase(tiles, gain, shift, scale, wt_ref, wa, ba, sc, rope_refs)

    init = [(s0f_ref.at[0], s0b_ref.at[0])] if has_cache else None
    out = None if has_cache else [(sf_out.at[b], sb_out.at[b]) for b in range(n_seq)]
    _gla_sequences(sc["gq"], sc["gk"], sc["gv"], sc["g"], sc["og"], gs, init, out, seq_len=seq_len, n_seq=n_seq)

    lam = _diff_lambda(lam_ref, lam_init)
    if has_cache:
        cv_copy.wait()
        cvb = _bf(cv_buf[...].reshape(cv_buf.shape[0], DIFF_WIDTH))
    heads = []
    for b in range(n_seq):
        for hd in range(DIFF_HEADS):
            sl = slice(hd * LANES, (hd + 1) * LANES)

            def load_kv(b=b, sl=sl):
                if has_cache:
                    return ([_bf(ckt_ref[0, sl, :]), sc["dkt"][sl, :]], [cvb[:, sl], sc["dv"][:, sl]])
                return [_bf(kt_out[b, sl, :])], [sc["dv"][b * seq_len:(b + 1) * seq_len, sl]]

            def load_q(r, sl=sl):
                return sc["dq"][pl.ds(r, DIFF_QB), sl]

            def store(r, val, sl=sl):
                sc["od"][pl.ds(r, DIFF_QB), sl] = val

            heads.append((b * seq_len, load_kv, load_q, store))
    _diff_phase(heads, lam, seq_len + (ckt_ref.shape[2] if has_cache else 0), seq_len)

    gg = gg_ref[...]
    dg = dg_ref[...] * (1.0 - lam_init)
    fg = fg_ref[...]
    for t0 in range(0, n_tok, ROW_TILE):
        rows = slice(t0, t0 + ROW_TILE)
        y_ref[rows, :] = _merge_rows(sc["og"], sc["od"], sc["gate"], x_ref[rows, :], gt, gg, dg, wo_ref, fg, rows)


def _layer(x2d, seq_len, n_seq, gain, mod, wt, wo, wa, ba, lam_p, gg, dg, fg, lam_init, rope_tabs=None, cache=None):
    t = x2d.shape[0]
    n_tok = seq_len * n_seq
    n_steps = t // n_tok
    rope = rope_tabs is not None
    has_cache = cache is not None
    assert n_tok % ROW_TILE == 0 and ROW_TILE % seq_len in (0, ROW_TILE) and seq_len % GLA_SUPER == 0

    def whole(a, single=False):
        kw = {"pipeline_mode": pl.Buffered(1)} if single else {}
        return pl.BlockSpec(a.shape, lambda i: (0,) * a.ndim, **kw)

    io_kw = {"pipeline_mode": pl.Buffered(1)} if n_tok > ROW_TILE else {}
    args = [x2d, gain, mod, wt, wo, wa, ba, lam_p, gg, dg, fg]
    in_specs = [pl.BlockSpec((n_tok, D_MODEL), lambda i: (i, 0)), whole(gain), whole(mod), whole(wt, True),
                whole(wo, True), whole(wa), whole(ba), whole(lam_p), whole(gg), whole(dg), whole(fg)]
    if rope:
        args += list(rope_tabs)
        in_specs += [whole(a, True) for a in rope_tabs]
    sds = jax.ShapeDtypeStruct
    out_shape = [sds((t, D_MODEL), F32)]
    out_specs = [pl.BlockSpec((n_tok, D_MODEL), lambda i: (i, 0), **io_kw)]
    if has_cache:
        ckt, cv, s0f, s0b = cache
        args += [ckt, cv, s0f, s0b]
        st_spec = pl.BlockSpec((1,) + s0f.shape[1:], lambda i: (i, 0, 0, 0))
        in_specs += [pl.BlockSpec((1,) + ckt.shape[1:], lambda i: (i, 0, 0)),
                     pl.BlockSpec(memory_space=pl.ANY), st_spec, st_spec]
    else:
        n_b = t // seq_len
        out_shape += [sds((n_b, DIFF_QK, seq_len), F32), sds((n_b, seq_len, DIFF_HEADS, DIFF_DV), F32),
                      sds((n_b, GLA_HEADS, GLA_DK, GLA_DV), F32), sds((n_b, GLA_HEADS, GLA_DK, GLA_DV), F32)]
        st_spec = pl.BlockSpec((n_seq, GLA_HEADS, GLA_DK, GLA_DV), lambda i: (i, 0, 0, 0))
        out_specs += [pl.BlockSpec((n_seq, DIFF_QK, seq_len), lambda i: (i, 0, 0), **io_kw),
                      pl.BlockSpec((n_seq, seq_len, DIFF_HEADS, DIFF_DV), lambda i: (i, 0, 0, 0), **io_kw),
                      st_spec, st_spec]
    vm = pltpu.VMEM
    scratch = [vm((n_tok, GLA_QK), F32), vm((n_tok, GLA_QK), F32), vm((n_tok, GLA_WIDTH), BF16),
               vm((n_tok, 2 * GLA_QK), F32), vm((n_tok, DIFF_QK), BF16), vm((n_tok, DIFF_WIDTH), BF16),
               vm((n_tok, MIX_WIDTH), F32), vm((n_tok, GLA_WIDTH), F32), vm((n_tok, DIFF_WIDTH), F32)]
    if has_cache:
        scratch += [vm((DIFF_QK, n_tok), BF16)]
    scratch += [vm((n_seq, 2, 2, GLA_PAIR_K, GLA_PAIR_V), F32),
                vm((2, n_tok, GLA_QK), BF16),
                vm((2, n_tok, GLA_QK), BF16),
                vm((2, n_tok // GLA_CHUNK, SUBLANES, GLA_QK), F32),
                vm((GLA_CHUNK, GLA_QK), F32),
                vm((GLA_CHUNK, GLA_WIDTH), F32)]
    if has_cache:
        scratch += [vm(cv.shape[1:], F32), pltpu.SemaphoreType.DMA(())]
    return pl.pallas_call(
        functools.partial(_layer_kernel, seq_len=seq_len, n_seq=n_seq, rope=rope, has_cache=has_cache,
                          lam_init=lam_init),
        grid=(n_steps,),
        in_specs=in_specs,
        out_specs=out_specs,
        out_shape=out_shape,
        scratch_shapes=scratch,
        compiler_params=_params(1),
        name="layer_lat" if has_cache else "layer_ctx",
    )(*args)


def _rope_tables(seq_len):
    t = np.arange(seq_len)
    inv_freq = ROPE_BASE ** (-np.arange(ROPE_PAIRS, dtype=np.float64) / ROPE_PAIRS)
    ang_r = (t // GRID_W)[:, None] * inv_freq[None, :]
    ang_c = (t % GRID_W)[:, None] * inv_freq[None, :]
    cr, sr, cc, sc = np.cos(ang_r), np.sin(ang_r), np.cos(ang_c), np.sin(ang_c)
    cos = np.concatenate([cr, cr, cc, cc] * 2, axis=-1).astype(np.float32)
    sin = np.concatenate([-sr, sr, -sc, sc] * 2, axis=-1).astype(np.float32)
    return (jnp.asarray(cos), jnp.asarray(sin),
            jnp.asarray(np.ascontiguousarray(cos.T)), jnp.asarray(np.ascontiguousarray(sin.T)))


def kernel(x_prompt, x_sample, cache_diff_k, cache_diff_v, state_gla_fwd, state_gla_bwd, c, c_ctx,
           norm_gain, w_mod, b_mod, w_in, w_gla_alpha, b_gla_alpha, diff_lambda,
           gla_head_gain, diff_head_gain, w_out, final_gain):
    bp, lp, d = x_prompt.shape
    bs, ls, _ = x_sample.shape
    depth = norm_gain.shape[0]
    assert depth == 1 and d == D_MODEL and bs + 1 <= MOD_ROWS and w_in.shape[2] == _R_END
    l = 0
    lam_init = 0.8 - 0.6 * math.exp(-0.3 * l)

    mod, wt, wo = _setup(c_ctx[None, :], c, w_mod[l], b_mod, jnp.swapaxes(w_in[l], 0, 1), w_out[l])
    shared = (norm_gain, mod, wt, wo, w_gla_alpha[l], b_gla_alpha[l], diff_lambda[l],
              gla_head_gain, diff_head_gain, final_gain[None, :], lam_init)

    y_p, dkt, dv, s_f, s_b = _layer(x_prompt.reshape(bp * lp, d), lp, CTX_ROWS_PER_STEP // lp, *shared)
    y_prompt = y_p.reshape(bp, lp, d)
    new_diff_k = jnp.transpose(dkt.reshape(bp, DIFF_HEADS, 2, DIFF_DH, lp), (0, 4, 1, 2, 3))[:, None]
    new_diff_v = dv[:, None]
    new_gla_fwd = s_f[:, None]
    new_gla_bwd = s_b[:, None]

    past = cache_diff_k.shape[2]
    ckt = jnp.transpose(cache_diff_k[:, l], (0, 2, 3, 4, 1)).reshape(bs, DIFF_QK, past)
    cv = cache_diff_v[:, l]
    y_s = _layer(x_sample.reshape(bs * ls, d), ls, 1, *shared, rope_tabs=_rope_tables(ls),
                 cache=(ckt, cv, state_gla_fwd[:, l], state_gla_bwd[:, l]))[0]
    y_sample = y_s.reshape(bs, ls, d)

    return (y_prompt, y_sample, new_diff_k, new_diff_v, new_gla_fwd, new_gla_bwd)
```
